```python
import jax, jax.numpy as jnp
from jax import lax
import numpy as np

D_MODEL = 1024
BATCH = 2
SEQ = 8192
DEPTH = 1

N_HEADS = 8
HEAD_DIM = 64
ATTN_WIDTH = N_HEADS * HEAD_DIM
Q_BLOCK = 128
CONV_WIDTH = D_MODEL // 2
CONV_KERNEL = 31
N_EXPERTS = 32
TOP_K = 4
D_EXPERT = D_MODEL
SWIGLU_LIMIT = 7.0
SWIGLU_ALPHA = 1.702
MOE_BLOCK = 128
RMS_EPS = 1e-5
LN_EPS = 1e-5
N_MOD = 6
COL_SIZES = (CONV_WIDTH, CONV_WIDTH,
             ATTN_WIDTH, ATTN_WIDTH, ATTN_WIDTH,
             N_HEADS,
             D_MODEL, D_MODEL)
IN_COLS = sum(COL_SIZES)

kernel_name = "hybrid_conv_fox_moe_block"


def rms_norm(x, g):
    xf = x.astype(jnp.float32)
    y = xf * lax.rsqrt(jnp.mean(xf * xf, axis=-1, keepdims=True) + RMS_EPS)
    return (y * g.astype(jnp.float32)).astype(x.dtype)


def layer_norm(x, g, b):
    xf = x.astype(jnp.float32)
    mu = jnp.mean(xf, axis=-1, keepdims=True)
    var = jnp.mean(jnp.square(xf - mu), axis=-1, keepdims=True)
    y = (xf - mu) * lax.rsqrt(var + LN_EPS)
    return (y * g.astype(jnp.float32) + b.astype(jnp.float32)).astype(x.dtype)


def conv_module(u_val, u_gate, conv_w, conv_b, ln_g, ln_b, w_proj):
    a = u_val * jax.nn.sigmoid(u_gate)
    a = lax.conv_general_dilated(
        a, conv_w[:, None, :].astype(a.dtype), window_strides=(1,),
        padding=[(CONV_KERNEL - 1, 0)],
        dimension_numbers=('NWC', 'WIO', 'NWC'),
        feature_group_count=CONV_WIDTH) + conv_b
    a = jax.nn.silu(layer_norm(a, ln_g, ln_b))
    return a @ w_proj


def forgetting_attention(q, k, v, f_logit):
    B, S = q.shape[0], q.shape[1]
    n_blocks = S // Q_BLOCK
    log_f = jax.nn.log_sigmoid(f_logit.astype(jnp.float32))
    F = jnp.cumsum(log_f, axis=1).transpose(0, 2, 1)
    kh = k.transpose(0, 2, 1, 3)
    vh = v.transpose(0, 2, 1, 3)
    q_blocks = q.transpose(0, 2, 1, 3).reshape(B, N_HEADS, n_blocks, Q_BLOCK, HEAD_DIM).transpose(2, 0, 1, 3, 4)
    F_blocks = F.reshape(B, N_HEADS, n_blocks, Q_BLOCK).transpose(2, 0, 1, 3)
    k_pos = jnp.arange(S)
    scale = HEAD_DIM ** -0.5

    def one_block(args):
        qi, Fi, bi = args
        q_pos = bi * Q_BLOCK + jnp.arange(Q_BLOCK)
        s = jnp.einsum('bhqd,bhkd->bhqk', qi, kh).astype(jnp.float32) * scale
        s = s + Fi[..., :, None] - F[:, :, None, :]
        s = jnp.where(k_pos[None, :] <= q_pos[:, None], s, -jnp.inf)
        p = jax.nn.softmax(s, axis=-1)
        return jnp.einsum('bhqk,bhkd->bhqd', p.astype(vh.dtype), vh)

    out = lax.map(one_block, (q_blocks, F_blocks, jnp.arange(n_blocks)))
    return out.transpose(1, 0, 3, 2, 4).reshape(B, S, ATTN_WIDTH)


def moe_ffn(h, w_router, b_router, w1, b1, w2, b2):
    B, S, D = h.shape
    T = B * S
    A = T * TOP_K
    hf = h.reshape(T, D)
    logits = (hf @ w_router + b_router).astype(jnp.float32)
    top_v, top_e = lax.top_k(logits, TOP_K)
    gates = jax.nn.softmax(top_v, axis=-1)

    flat_e = top_e.reshape(A).astype(jnp.int32)
    flat_tok = (jnp.arange(A, dtype=jnp.int32) // TOP_K)
    flat_gate = gates.reshape(A)
    order = jnp.argsort(flat_e)
    se, stok, sg = flat_e[order], flat_tok[order], flat_gate[order]
    counts = jnp.bincount(flat_e, length=N_EXPERTS)
    padded = (counts + MOE_BLOCK - 1) // MOE_BLOCK * MOE_BLOCK
    pend = jnp.cumsum(padded)
    pstart = pend - padded
    ustart = jnp.cumsum(counts) - counts
    dest = pstart[se] + jnp.arange(A, dtype=jnp.int32) - ustart[se]
    P = A + N_EXPERTS * MOE_BLOCK
    n_blocks = P // MOE_BLOCK
    tok_buf = jnp.zeros((P,), jnp.int32).at[dest].set(stok)
    gate_buf = jnp.zeros((P,), jnp.float32).at[dest].set(sg)
    block_e = jnp.clip(jnp.searchsorted(pend, jnp.arange(n_blocks) * MOE_BLOCK, side='right'), 0, N_EXPERTS - 1)
    xb = hf[tok_buf].reshape(n_blocks, MOE_BLOCK, D)

    def expert_block(args):
        xi, e = args
        hc = xi @ w1[e] + b1[e]
        g, u = hc[:, :D_EXPERT], hc[:, D_EXPERT:]
        g = jnp.minimum(g, SWIGLU_LIMIT)
        u = jnp.clip(u, -SWIGLU_LIMIT, SWIGLU_LIMIT)
        hm = (u + 1.0) * (g * jax.nn.sigmoid(SWIGLU_ALPHA * g))
        return hm @ w2[e] + b2[e]

    yb = lax.map(expert_block, (xb, block_e)).reshape(P, D)
    y = jax.ops.segment_sum(yb.astype(jnp.float32) * gate_buf[:, None], tok_buf, num_segments=T)
    return y.astype(h.dtype).reshape(B, S, D)


def setup_inputs(seed: int = 0) -> dict:
    key = jax.random.key(seed)
    ks = jax.random.split(key, 24)
    f32 = jnp.float32
    L, D = DEPTH, D_MODEL

    def nrm(k, shape, fan_in, mult=1.0):
        return jax.random.normal(k, shape, f32) * (mult * fan_in ** -0.5)

    return {
        "x": jax.random.normal(ks[0], (BATCH, SEQ, D), f32),
        "c": jax.random.normal(ks[1], (BATCH, D), f32),
        "w_ada": nrm(ks[2], (L, D, N_MOD * D), D, 0.5),
        "b_ada": 0.02 * jax.random.normal(ks[3], (L, N_MOD * D), f32),
        "norm1_g": 1.0 + 0.05 * jax.random.normal(ks[4], (L, D), f32),
        "w_in": nrm(ks[5], (L, D, IN_COLS), D),
        "b_forget": jax.random.uniform(ks[6], (L, N_HEADS), f32, 1.0, 4.0),
        "conv_w": nrm(ks[7], (L, CONV_KERNEL, CONV_WIDTH), CONV_KERNEL),
        "conv_b": 0.02 * jax.random.normal(ks[8], (L, CONV_WIDTH), f32),
        "conv_ln_g": 1.0 + 0.05 * jax.random.normal(ks[9], (L, CONV_WIDTH), f32),
        "conv_ln_b": 0.02 * jax.random.normal(ks[10], (L, CONV_WIDTH), f32),
        "w_conv_out": nrm(ks[11], (L, CONV_WIDTH, D), CONV_WIDTH),
        "w_attn_out": nrm(ks[12], (L, ATTN_WIDTH, D), ATTN_WIDTH),
        "w_out": nrm(ks[13], (L, D, D), D),
        "norm2_g": 1.0 + 0.05 * jax.random.normal(ks[14], (L, D), f32),
        "w_router": nrm(ks[15], (L, D, N_EXPERTS), D),
        "b_router": 0.01 * jax.random.normal(ks[16], (L, N_EXPERTS), f32),
        "w_exp_in": nrm(ks[17], (L, N_EXPERTS, D, 2 * D_EXPERT), D),
        "b_exp_in": 0.02 * jax.random.normal(ks[18], (L, N_EXPERTS, 2 * D_EXPERT), f32),
        "w_exp_out": nrm(ks[19], (L, N_EXPERTS, D_EXPERT, D), D_EXPERT),
        "b_exp_out": 0.02 * jax.random.normal(ks[20], (L, N_EXPERTS, D), f32),
        "final_g": 1.0 + 0.05 * jax.random.normal(ks[21], (D,), f32),
    }


def reference(x, c, w_ada, b_ada, norm1_g, w_in, b_forget, conv_w, conv_b, conv_ln_g, conv_ln_b,
              w_conv_out, w_attn_out, w_out, norm2_g, w_router, b_router, w_exp_in, b_exp_in,
              w_exp_out, b_exp_out, final_g):
    B, S, D = x.shape
    offs = np.cumsum(COL_SIZES)[:-1].tolist()
    for l in range(DEPTH):
        mod = (jax.nn.silu(c) @ w_ada[l] + b_ada[l]).reshape(B, N_MOD, D)
        sh1, sc1, g1, sh2, sc2, g2 = (mod[:, i, None, :] for i in range(N_MOD))

        h = rms_norm(x, norm1_g[l]) * (1.0 + sc1) + sh1
        z = h @ w_in[l]
        u_val, u_gate, q, k, v, f_logit, gate_c, gate_a = jnp.split(z, offs, axis=-1)
        conv_out = conv_module(u_val, u_gate, conv_w[l], conv_b[l], conv_ln_g[l], conv_ln_b[l], w_conv_out[l])
        attn = forgetting_attention(q.reshape(B, S, N_HEADS, HEAD_DIM),
                                    k.reshape(B, S, N_HEADS, HEAD_DIM),
                                    v.reshape(B, S, N_HEADS, HEAD_DIM),
                                    f_logit + b_forget[l])
        attn_out = attn @ w_attn_out[l]
        mixed = jax.nn.sigmoid(gate_c) * conv_out + jax.nn.sigmoid(gate_a) * attn_out
        x = x + g1 * (mixed @ w_out[l])

        h2 = rms_norm(x, norm2_g[l]) * (1.0 + sc2) + sh2
        x = x + g2 * moe_ffn(h2, w_router[l], b_router[l], w_exp_in[l], b_exp_in[l], w_exp_out[l], b_exp_out[l])
    return rms_norm(x, final_g)
```

```python
import functools

import jax
import jax.numpy as jnp
from jax import lax
from jax.experimental import pallas as pl
from jax.experimental.pallas import tpu as pltpu

F32 = jnp.float32
BF16 = jnp.bfloat16
I32 = jnp.int32
U32 = jnp.uint32
HIGHEST = lax.Precision.HIGHEST

LANES = 128
SUBLANES = 8
VMEM_LIMIT_BYTES = 56 * 1024 * 1024

N_HEADS = 8
HEAD_DIM = 64
CONV_KERNEL = 31
CONV_HALO = 32
N_EXPERTS = 32
TOP_K = 4
SWIGLU_LIMIT = 7.0
SWIGLU_ALPHA = 1.702
RMS_EPS = 1e-5
LN_EPS = 1e-5
N_MOD = 6
N_SPLIT = 3

ATTN_BLOCK = 512
MERGE_TILE = 512
CONV_TILE = 256
ROUTE_TILE = 512
TOKEN_TILE = 256
EXPERT_BLOCK = 256


def _cparams(*sem):
    return pltpu.CompilerParams(dimension_semantics=sem, vmem_limit_bytes=VMEM_LIMIT_BYTES)


def _resident(shape):
    nd = len(shape)
    return pl.BlockSpec(shape, lambda *_: (0,) * nd, pipeline_mode=pl.Buffered(1))


def _split3(x):
    p1 = x.astype(BF16)
    r1 = x - p1.astype(F32)
    p2 = r1.astype(BF16)
    p3 = (r1 - p2.astype(F32)).astype(BF16)
    return jnp.concatenate([p1, p2, p3], axis=1)


def _rms(x):
    return x * lax.rsqrt(jnp.mean(x * x, axis=-1, keepdims=True) + RMS_EPS)


def _mod_kernel(c_ref, w_ref, b_ref, o_ref):
    c = c_ref[...]
    sc = c * jax.nn.sigmoid(c)
    o_ref[...] = jnp.dot(sc, w_ref[...], preferred_element_type=F32, precision=HIGHEST) + b_ref[...]


def _mod(c, w_ada, b_ada):
    bsz, d = c.shape
    n = w_ada.shape[1]
    tn = 1024
    cp = jnp.zeros((SUBLANES, d), F32).at[:bsz].set(c)
    out = pl.pallas_call(
        _mod_kernel,
        grid=(n // tn,),
        in_specs=[pl.BlockSpec((SUBLANES, d), lambda j: (0, 0)),
                  pl.BlockSpec((d, tn), lambda j: (0, j)),
                  pl.BlockSpec((1, tn), lambda j: (0, j))],
        out_specs=pl.BlockSpec((SUBLANES, tn), lambda j: (0, j)),
        out_shape=jax.ShapeDtypeStruct((SUBLANES, n), F32),
        compiler_params=_cparams("arbitrary"),
        name="mod",
    )(cp, w_ada, b_ada[None])
    mod = out[:bsz].reshape(bsz, N_MOD, d)
    return jnp.pad(mod, ((0, 0), (0, SUBLANES - N_MOD), (0, 0)))


def _inproj_kernel(x_ref, mod_ref, g_ref, wa_ref, wq_ref, wk_ref, wv_ref, wf_ref, wg_ref, bf_ref,
                   tri_ref, eq_ref, ek_ref, oneq_ref, onek_ref,
                   a_ref, q_ref, k_ref, v_ref, tot_ref, sgc_ref, sga_ref, *, conv_width, d_model):
    x = x_ref[...]
    h = _rms(x) * g_ref[...] * (1.0 + mod_ref[1:2, :]) + mod_ref[0:1, :]
    hb = h.astype(BF16)

    u = jnp.dot(hb, wa_ref[...], preferred_element_type=F32)
    a_ref[...] = (u[:, :conv_width] * jax.nn.sigmoid(u[:, conv_width:])).astype(BF16)

    fl = jnp.dot(h, wf_ref[...], preferred_element_type=F32, precision=HIGHEST) + bf_ref[...]
    lane = lax.broadcasted_iota(I32, fl.shape, 1)
    log_f = -(jnp.maximum(-fl, 0.0) + jnp.log1p(jnp.exp(-jnp.abs(fl))))
    log_f = jnp.where(lane < N_HEADS, log_f, 0.0)
    cs = jnp.dot(tri_ref[...], _split3(log_f), preferred_element_type=F32)
    cum = cs[:, :LANES] + cs[:, LANES:2 * LANES] + cs[:, 2 * LANES:]
    tot_ref[0] = cum[cum.shape[0] - 1:, :]
    cum3 = _split3(cum)

    q = jnp.dot(hb, wq_ref[...], preferred_element_type=F32) * (HEAD_DIM ** -0.5)
    q = q + jnp.dot(cum3, eq_ref[...], preferred_element_type=F32) + oneq_ref[...]
    q_ref[...] = q.astype(BF16)
    k = jnp.dot(hb, wk_ref[...], preferred_element_type=F32)
    k = k + jnp.dot(cum3, ek_ref[...], preferred_element_type=F32) + onek_ref[...]
    k_ref[...] = k.astype(BF16)
    v_ref[...] = jnp.dot(hb, wv_ref[...], preferred_element_type=F32).astype(BF16)

    gts = jnp.dot(hb, wg_ref[...], preferred_element_type=F32)
    sgc_ref[...] = jax.nn.sigmoid(gts[:, :d_model]).astype(BF16)
    sga_ref[...] = jax.nn.sigmoid(gts[:, d_model:]).astype(BF16)


def _spread_heads(w):
    d = w.shape[0]
    w = w.reshape(d, N_HEADS, HEAD_DIM)
    w = jnp.pad(w, ((0, 0), (0, 0), (0, LANES - HEAD_DIM)))
    return w.reshape(d, N_HEADS * LANES)


def _aug_constants():
    rows = jnp.arange(N_SPLIT * LANES)
    piece, head = rows // LANES, rows % LANES
    cols = jnp.arange(N_HEADS * LANES)
    chead, coff = cols // LANES, cols % LANES - HEAD_DIM
    valid = (head[:, None] < N_HEADS) & (head[:, None] == chead[None, :])
    eq = jnp.where(valid & (coff[None, :] == piece[:, None]), 1.0, 0.0).astype(BF16)
    ek = jnp.where(valid & (coff[None, :] == piece[:, None] + N_SPLIT), -1.0, 0.0).astype(BF16)
    oneq = jnp.where((coff >= N_SPLIT) & (coff < 2 * N_SPLIT), 1.0, 0.0).astype(F32)[None]
    onek = jnp.where((coff >= 0) & (coff < N_SPLIT), 1.0, 0.0).astype(F32)[None]
    return eq, ek, oneq, onek


def _inproj(x2, mod8, norm_g, w_in, b_forget, seq):
    t, d = x2.shape
    tm = ATTN_BLOCK
    cw = d // 2
    aw = N_HEADS * HEAD_DIM
    o0 = 2 * cw
    wa = w_in[:, :o0].astype(BF16)
    wq = _spread_heads(w_in[:, o0:o0 + aw]).astype(BF16)
    wk = _spread_heads(w_in[:, o0 + aw:o0 + 2 * aw]).astype(BF16)
    wv = _spread_heads(w_in[:, o0 + 2 * aw:o0 + 3 * aw]).astype(BF16)
    o1 = o0 + 3 * aw
    wf = jnp.pad(w_in[:, o1:o1 + N_HEADS], ((0, 0), (0, LANES - N_HEADS)))
    bfp = jnp.pad(b_forget, (0, LANES - N_HEADS))[None]
    wg = w_in[:, o1 + N_HEADS:].astype(BF16)
    r = jnp.arange(tm)
    tri = (r[None, :] <= r[:, None]).astype(BF16)
    eq, ek, oneq, onek = _aug_constants()
    sw = N_HEADS * LANES
    tiles_per_seq = seq // tm
    row = lambda i: (i, 0)
    outs = pl.pallas_call(
        functools.partial(_inproj_kernel, conv_width=cw, d_model=d),
        grid=(t // tm,),
        in_specs=[pl.BlockSpec((tm, d), row),
                  pl.BlockSpec((None, SUBLANES, d), lambda i: (i // tiles_per_seq, 0, 0)),
                  _resident((1, d)),
                  _resident(wa.shape), _resident(wq.shape), _resident(wk.shape), _resident(wv.shape),
                  _resident(wf.shape), _resident(wg.shape), _resident(bfp.shape),
                  _resident(tri.shape), _resident(eq.shape), _resident(ek.shape),
                  _resident(oneq.shape), _resident(onek.shape)],
        out_specs=[pl.BlockSpec((tm, cw), row),
                   pl.BlockSpec((tm, sw), row), pl.BlockSpec((tm, sw), row), pl.BlockSpec((tm, sw), row),
                   pl.BlockSpec((1, 1, LANES), lambda i: (i, 0, 0)),
                   pl.BlockSpec((tm, d), row), pl.BlockSpec((tm, d), row)],
        out_shape=[jax.ShapeDtypeStruct((t, cw), BF16),
                   jax.ShapeDtypeStruct((t, sw), BF16), jax.ShapeDtypeStruct((t, sw), BF16),
                   jax.ShapeDtypeStruct((t, sw), BF16),
                   jax.ShapeDtypeStruct((t // tm, 1, LANES), F32),
                   jax.ShapeDtypeStruct((t, d), BF16), jax.ShapeDtypeStruct((t, d), BF16)],
        compiler_params=_cparams("arbitrary"),
        name="inproj",
    )(x2, mod8, norm_g[None], wa, wq, wk, wv, wf, wg, bfp, tri, eq, ek, oneq, onek)
    return outs


def _conv_kernel(halo_ref, cur_ref, w_ref, b_ref, g_ref, beta_ref, o_ref, xs_ref):
    ts = cur_ref.shape[0]
    first = pl.program_id(1) == 0
    halo = halo_ref[...].astype(F32)
    xs_ref[0:CONV_HALO, :] = jnp.where(first, 0.0, halo)
    xs_ref[CONV_HALO:, :] = cur_ref[...].astype(F32)
    acc = jnp.zeros(o_ref.shape, F32) + b_ref[...]
    base = CONV_HALO - (CONV_KERNEL - 1)
    for j in range(CONV_KERNEL):
        acc = acc + w_ref[j:j + 1, :] * xs_ref[base + j:base + j + ts, :]
    mu = jnp.mean(acc, axis=-1, keepdims=True)
    cen = acc - mu
    var = jnp.mean(cen * cen, axis=-1, keepdims=True)
    y = cen * lax.rsqrt(var + LN_EPS) * g_ref[...] + beta_ref[...]
    o_ref[...] = (y * jax.nn.sigmoid(y)).astype(BF16)


def _conv(a, conv_w, conv_b, ln_g, ln_b, bsz, seq):
    t, cw = a.shape
    ts = CONV_TILE
    nt = seq // ts
    hpt = ts // CONV_HALO
    vec = lambda: pl.BlockSpec((1, cw), lambda b, i: (0, 0))
    return pl.pallas_call(
        _conv_kernel,
        grid=(bsz, nt),
        in_specs=[pl.BlockSpec((CONV_HALO, cw), lambda b, i: (jnp.maximum((b * nt + i) * hpt - 1, 0), 0)),
                  pl.BlockSpec((ts, cw), lambda b, i: (b * nt + i, 0)),
                  pl.BlockSpec((CONV_KERNEL, cw), lambda b, i: (0, 0)),
                  vec(), vec(), vec()],
        out_specs=pl.BlockSpec((ts, cw), lambda b, i: (b * nt + i, 0)),
        out_shape=jax.ShapeDtypeStruct((t, cw), BF16),
        scratch_shapes=[pltpu.VMEM((CONV_HALO + ts, cw), F32)],
        compiler_params=_cparams("arbitrary", "arbitrary"),
        name="conv",
    )(a, a, conv_w, conv_b[None], ln_g[None], ln_b[None])


def _attn_kernel(q_ref, k_ref, v_ref, tot_ref, o_ref, m_scr, l_scr, acc_scr):
    bq = q_ref.shape[0]
    qi = pl.program_id(2)
    q = q_ref[...]
    nt = (((1,), (1,)), ((), ()))

    def scores(kj):
        start = pl.multiple_of(kj * bq, bq)
        k = k_ref[pl.ds(start, bq), :]
        v = v_ref[pl.ds(start, bq), :]
        return lax.dot_general(q, k, nt, preferred_element_type=F32), v

    s, v = scores(qi)
    row = lax.broadcasted_iota(I32, s.shape, 0)
    col = lax.broadcasted_iota(I32, s.shape, 1)
    s = jnp.where(col <= row, s, -jnp.inf)
    m0 = jnp.max(s, axis=1, keepdims=True)
    p = jnp.exp(s - m0)
    m_scr[...] = jnp.broadcast_to(m0, m_scr.shape)
    l_scr[...] = jnp.broadcast_to(jnp.sum(p, axis=1, keepdims=True), l_scr.shape)
    acc_scr[...] = jnp.dot(p.astype(BF16), v, preferred_element_type=F32)

    def body(step, delta):
        kj = qi - 1 - step
        delta = delta + tot_ref[pl.ds(kj, 1), :]
        s, v = scores(kj)
        m_prev = m_scr[...]
        m_new = jnp.maximum(m_prev, jnp.max(s, axis=1, keepdims=True) + delta)
        shift = m_new - delta
        p = jnp.exp(s - shift[:, :1])
        alpha = jnp.exp(m_prev - m_new)
        l_scr[...] = alpha * l_scr[...] + jnp.sum(p, axis=1, keepdims=True)
        acc_scr[...] = alpha * acc_scr[...] + jnp.dot(p.astype(BF16), v, preferred_element_type=F32)
        m_scr[...] = m_new
        return delta

    lax.fori_loop(0, qi, body, jnp.zeros((1, LANES), F32))
    o_ref[...] = (acc_scr[...] / l_scr[...]).astype(BF16)


def _attention(q, k, v, tot, bsz, seq):
    t, sw = q.shape
    bq = ATTN_BLOCK
    nq = seq // bq
    tot = tot.reshape(bsz, nq, LANES)[:, :, :N_HEADS].transpose(0, 2, 1)
    tot = jnp.broadcast_to(tot[..., None], (bsz, N_HEADS, nq, LANES))
    k3 = k.reshape(bsz, seq, sw)
    v3 = v.reshape(bsz, seq, sw)
    return pl.pallas_call(
        _attn_kernel,
        grid=(bsz, N_HEADS, nq),
        in_specs=[pl.BlockSpec((bq, LANES), lambda b, h, i: (b * nq + i, h)),
                  pl.BlockSpec((None, seq, LANES), lambda b, h, i: (b, 0, h)),
                  pl.BlockSpec((None, seq, LANES), lambda b, h, i: (b, 0, h)),
                  pl.BlockSpec((None, None, nq, LANES), lambda b, h, i: (b, h, 0, 0))],
        out_specs=pl.BlockSpec((bq, LANES), lambda b, h, i: (b * nq + i, h)),
        out_shape=jax.ShapeDtypeStruct((t, sw), BF16),
        scratch_shapes=[pltpu.VMEM((bq, LANES), F32), pltpu.VMEM((bq, LANES), F32),
                        pltpu.VMEM((bq, LANES), F32)],
        compiler_params=_cparams("arbitrary", "arbitrary", "arbitrary"),
        name="attn",
    )(q, k3, v3, tot)


def _pack_bf16_pair(lo, hi):
    lo_bits = pltpu.bitcast(lo.astype(BF16).astype(F32), U32)
    hi_bits = pltpu.bitcast(hi.astype(BF16).astype(F32), U32)
    return (hi_bits & jnp.uint32(0xFFFF0000)) | (lo_bits >> 16)


def _unpack_bf16_pair(w):
    lo = pltpu.bitcast(w << 16, F32)
    hi = pltpu.bitcast(w & jnp.uint32(0xFFFF0000), F32)
    return jnp.concatenate([lo, hi], axis=1)


def _merge_kernel(x_ref, act_ref, attn_ref, sgc_ref, sga_ref, mod_ref, g2_ref, wco_ref, wao_ref, wout_ref,
                  wr_ref, br_ref, x1_ref, h2_ref, ids_ref, gates_ref):
    conv_out = jnp.dot(act_ref[...], wco_ref[...], preferred_element_type=F32)
    attn_out = jnp.dot(attn_ref[...], wao_ref[...], preferred_element_type=F32)
    mixed = sgc_ref[...].astype(F32) * conv_out + sga_ref[...].astype(F32) * attn_out
    upd = jnp.dot(mixed.astype(BF16), wout_ref[...], preferred_element_type=F32)
    x1 = x_ref[...] + mod_ref[2:3, :] * upd
    x1_ref[...] = x1
    h2 = _rms(x1) * g2_ref[...] * (1.0 + mod_ref[4:5, :]) + mod_ref[3:4, :]
    half = h2.shape[1] // 2
    h2_ref[...] = _pack_bf16_pair(h2[:, :half], h2[:, half:])

    logits = jnp.dot(h2, wr_ref[...], preferred_element_type=F32, precision=HIGHEST) + br_ref[...]
    lane = lax.broadcasted_iota(I32, logits.shape, 1)
    work = jnp.where(lane < N_EXPERTS, logits, -jnp.inf)
    ids = jnp.zeros(logits.shape, I32)
    vals = []
    for kk in range(TOP_K):
        mx = jnp.max(work, axis=1, keepdims=True)
        idx = jnp.min(jnp.where(work == mx, lane, LANES), axis=1, keepdims=True)
        ids = jnp.where(lane == kk, idx, ids)
        vals.append(mx)
        work = jnp.where(lane == idx, -jnp.inf, work)
    exps = [jnp.exp(vv - vals[0]) for vv in vals]
    den = exps[0] + exps[1] + exps[2] + exps[3]
    gates = jnp.zeros(logits.shape, F32)
    for kk in range(TOP_K):
        gates = jnp.where(lane == kk, exps[kk] / den, gates)
    ids_ref[...] = ids
    gates_ref[...] = gates


def _merge(x2, act, attn, sgc, sga, mod8, norm2_g, w_conv_out, w_attn_out, w_out, w_router, b_router, seq):
    t, d = x2.shape
    tm = MERGE_TILE
    cw = act.shape[1]
    sw = attn.shape[1]
    wco = w_conv_out.astype(BF16)
    wao = jnp.pad(w_attn_out.reshape(N_HEADS, HEAD_DIM, d), ((0, 0), (0, LANES - HEAD_DIM), (0, 0)))
    wao = wao.reshape(sw, d).astype(BF16)
    wout = w_out.astype(BF16)
    wr = jnp.pad(w_router, ((0, 0), (0, LANES - N_EXPERTS)))
    br = jnp.pad(b_router, (0, LANES - N_EXPERTS))[None]
    tiles_per_seq = seq // tm
    row = lambda i: (i, 0)
    return pl.pallas_call(
        _merge_kernel,
        grid=(t // tm,),
        in_specs=[pl.BlockSpec((tm, d), row), pl.BlockSpec((tm, cw), row), pl.BlockSpec((tm, sw), row),
                  pl.BlockSpec((tm, d), row), pl.BlockSpec((tm, d), row),
                  pl.BlockSpec((None, SUBLANES, d), lambda i: (i // tiles_per_seq, 0, 0)),
                  _resident((1, d)), _resident(wco.shape), _resident(wao.shape), _resident(wout.shape),
                  _resident(wr.shape), _resident(br.shape)],
        out_specs=[pl.BlockSpec((tm, d), row), pl.BlockSpec((tm, d // 2), row),
                   pl.BlockSpec((tm, LANES), row), pl.BlockSpec((tm, LANES), row)],
        out_shape=[jax.ShapeDtypeStruct((t, d), F32), jax.ShapeDtypeStruct((t, d // 2), U32),
                   jax.ShapeDtypeStruct((t, LANES), I32), jax.ShapeDtypeStruct((t, LANES), F32)],
        compiler_params=_cparams("arbitrary"),
        name="merge",
    )(x2, act, attn, sgc, sga, mod8, norm2_g[None], wco, wao, wout, wr, br)


def _expert_onehots(ids):
    lane = lax.broadcasted_iota(I32, ids.shape, 1)
    return lane, [lane == ids[:, kk:kk + 1] for kk in range(TOP_K)]


def _route_kernel(ids_ref, tri_ref, rank_ref, cnt_ref, carry_scr):
    @pl.when(pl.program_id(0) == 0)
    def _():
        carry_scr[...] = jnp.zeros(carry_scr.shape, F32)

    lane, hots = _expert_onehots(ids_ref[...])
    multi = jnp.zeros(lane.shape, F32)
    for hot in hots:
        multi = multi + jnp.where(hot, 1.0, 0.0)
    carry = carry_scr[0:1, :]
    before = jnp.dot(tri_ref[...], multi.astype(BF16), preferred_element_type=F32) + carry
    rank = jnp.zeros(lane.shape, F32)
    for kk, hot in enumerate(hots):
        rk = jnp.sum(jnp.where(hot, before, 0.0), axis=1, keepdims=True)
        rank = jnp.where(lane == kk, rk, rank)
    rank_ref[...] = rank.astype(I32)
    carry = carry + jnp.sum(multi, axis=0, keepdims=True)
    carry_scr[...] = jnp.broadcast_to(carry, carry_scr.shape)
    cnt_ref[...] = jnp.broadcast_to(carry, cnt_ref.shape)


def _route(ids):
    t = ids.shape[0]
    tm = ROUTE_TILE
    r = jnp.arange(tm)
    tri = (r[None, :] < r[:, None]).astype(BF16)
    return pl.pallas_call(
        _route_kernel,
        grid=(t // tm,),
        in_specs=[pl.BlockSpec((tm, LANES), lambda i: (i, 0)), _resident(tri.shape)],
        out_specs=[pl.BlockSpec((tm, LANES), lambda i: (i, 0)),
                   pl.BlockSpec((SUBLANES, LANES), lambda i: (0, 0))],
        out_shape=[jax.ShapeDtypeStruct((t, LANES), I32), jax.ShapeDtypeStruct((SUBLANES, LANES), F32)],
        scratch_shapes=[pltpu.VMEM((SUBLANES, LANES), F32)],
        compiler_params=_cparams("arbitrary"),
        name="route",
    )(ids, tri)


def _dest_kernel(ids_ref, rank_ref, cnt_ref, tri_ref, dest_ref, blk_ref):
    cnt = cnt_ref[...]
    padded = jnp.ceil(cnt * (1.0 / EXPERT_BLOCK)) * EXPERT_BLOCK
    pend = jnp.dot(padded, tri_ref[...], preferred_element_type=F32, precision=HIGHEST)
    pstart = (pend - padded)[0:1, :]

    lane, hots = _expert_onehots(ids_ref[...])
    rank = rank_ref[...]
    dest = jnp.zeros(lane.shape, I32)
    for kk, hot in enumerate(hots):
        ps = jnp.sum(jnp.where(hot, pstart, 0.0), axis=1, keepdims=True).astype(I32)
        dest = jnp.where(lane == kk, ps + rank[:, kk:kk + 1], dest)
    dest_ref[...] = dest

    @pl.when(pl.program_id(0) == 0)
    def _():
        shape = blk_ref.shape[1:]
        pos = (lax.broadcasted_iota(I32, shape, 0) * LANES + lax.broadcasted_iota(I32, shape, 1))
        pos = pos.astype(F32) * EXPERT_BLOCK
        be = jnp.zeros(shape, I32)
        for e in range(N_EXPERTS):
            be = be + jnp.where(pos >= pend[0:1, e:e + 1], 1, 0)
        blk_ref[0] = jnp.minimum(be, N_EXPERTS - 1)
        nact = pend[0:1, N_EXPERTS - 1:N_EXPERTS] * (1.0 / EXPERT_BLOCK)
        blk_ref[1] = jnp.broadcast_to(nact, shape).astype(I32)


def _dest(ids, rank, cnt):
    t = ids.shape[0]
    tm = ROUTE_TILE
    r = jnp.arange(LANES)
    tri = (r[:, None] <= r[None, :]).astype(F32)
    tile = pl.BlockSpec((tm, LANES), lambda i: (i, 0))
    return pl.pallas_call(
        _dest_kernel,
        grid=(t // tm,),
        in_specs=[tile, tile, pl.BlockSpec((SUBLANES, LANES), lambda i: (0, 0)),
                  pl.BlockSpec((LANES, LANES), lambda i: (0, 0))],
        out_specs=[tile, pl.BlockSpec((2, SUBLANES, LANES), lambda i: (0, 0, 0))],
        out_shape=[jax.ShapeDtypeStruct((t, LANES), I32), jax.ShapeDtypeStruct((2, SUBLANES, LANES), I32)],
        compiler_params=_cparams("arbitrary"),
        name="dest",
    )(ids, rank, cnt, tri)


def _row_copy(src, dst, sem):
    return pltpu.make_async_copy(src, dst, sem)


def _dispatch_kernel(dest_ref, h2_ref, xs_in_ref, xs_ref, sem):
    del xs_in_ref
    tm = h2_ref.shape[0]

    def issue(r, carry):
        for kk in range(TOP_K):
            d = dest_ref[r * TOP_K + kk]
            _row_copy(h2_ref.at[pl.ds(r, 1)], xs_ref.at[pl.ds(d, 1)], sem).start()
        return carry

    lax.fori_loop(0, tm, issue, 0)
    for kk in range(TOP_K):
        _row_copy(h2_ref, xs_ref.at[pl.ds(0, tm)], sem).wait()


def _dispatch(dest_flat, h2p, n_rows):
    t, w = h2p.shape
    tm = TOKEN_TILE
    xs0 = jnp.zeros((n_rows, w), U32)
    return pl.pallas_call(
        _dispatch_kernel,
        grid=(t // tm,),
        in_specs=[pl.BlockSpec((tm * TOP_K,), lambda i: (i,), memory_space=pltpu.SMEM),
                  pl.BlockSpec((tm, w), lambda i: (i, 0)),
                  pl.BlockSpec(memory_space=pl.ANY)],
        out_specs=pl.BlockSpec(memory_space=pl.ANY),
        out_shape=jax.ShapeDtypeStruct((n_rows, w), U32),
        scratch_shapes=[pltpu.SemaphoreType.DMA(())],
        input_output_aliases={2: 0},
        compiler_params=_cparams("arbitrary"),
        name="dispatch",
    )(dest_flat, h2p, xs0)


def _expert_kernel(be_ref, na_ref, xs_ref, w1_ref, b1_ref, w2_ref, b2_ref, y_ref, w1b_ref, w2b_ref, *, d_expert):
    i = pl.program_id(0)
    active = i < na_ref[0]
    prev = be_ref[jnp.maximum(i - 1, 0)]
    new_expert = jnp.logical_or(i == 0, be_ref[i] != prev)

    @pl.when(jnp.logical_and(active, new_expert))
    def _():
        w1b_ref[...] = w1_ref[...].astype(BF16)
        w2b_ref[...] = w2_ref[...].astype(BF16)

    @pl.when(active)
    def _():
        x = _unpack_bf16_pair(xs_ref[...]).astype(BF16)
        hc = jnp.dot(x, w1b_ref[...], preferred_element_type=F32) + b1_ref[...]
        g = jnp.minimum(hc[:, :d_expert], SWIGLU_LIMIT)
        u = jnp.clip(hc[:, d_expert:], -SWIGLU_LIMIT, SWIGLU_LIMIT)
        hm = (u + 1.0) * (g * jax.nn.sigmoid(SWIGLU_ALPHA * g))
        y_ref[...] = jnp.dot(hm.astype(BF16), w2b_ref[...], preferred_element_type=F32) + b2_ref[...]

    @pl.when(jnp.logical_not(active))
    def _():
        y_ref[...] = jnp.zeros(y_ref.shape, F32)


def _experts(block_e, n_active, xs, w1, b1, w2, b2):
    n_rows, w = xs.shape
    n_exp, d, two_de = w1.shape
    de = two_de // 2
    bm = EXPERT_BLOCK
    nb = n_rows // bm

    def blk(i, be, na):
        return jnp.minimum(i, na[0] - 1)

    grid_spec = pltpu.PrefetchScalarGridSpec(
        num_scalar_prefetch=2,
        grid=(nb,),
        in_specs=[pl.BlockSpec((bm, w), lambda i, be, na: (blk(i, be, na), 0)),
                  pl.BlockSpec((None, d, two_de), lambda i, be, na: (be[blk(i, be, na)], 0, 0)),
                  pl.BlockSpec((None, 1, two_de), lambda i, be, na: (be[blk(i, be, na)], 0, 0)),
                  pl.BlockSpec((None, de, d), lambda i, be, na: (be[blk(i, be, na)], 0, 0)),
                  pl.BlockSpec((None, 1, d), lambda i, be, na: (be[blk(i, be, na)], 0, 0))],
        out_specs=pl.BlockSpec((bm, d), lambda i, be, na: (i, 0)),
        scratch_shapes=[pltpu.VMEM((d, two_de), BF16), pltpu.VMEM((de, d), BF16)],
    )
    return pl.pallas_call(
        functools.partial(_expert_kernel, d_expert=de),
        grid_spec=grid_spec,
        out_shape=jax.ShapeDtypeStruct((n_rows, d), F32),
        compiler_params=_cparams("arbitrary"),
        name="expert",
    )(block_e, n_active, xs, w1, b1[:, None, :], w2, b2[:, None, :])


def _combine_kernel(dest_ref, x1_ref, gates_ref, mod_ref, fg_ref, yb_ref, o_ref, gbuf, sem):
    tm = x1_ref.shape[0]

    def issue(r, carry):
        for kk in range(TOP_K):
            d = dest_ref[r * TOP_K + kk]
            _row_copy(yb_ref.at[pl.ds(d, 1)], gbuf.at[kk, pl.ds(r, 1)], sem).start()
        return carry

    lax.fori_loop(0, tm, issue, 0)
    for kk in range(TOP_K):
        _row_copy(yb_ref.at[pl.ds(0, tm)], gbuf.at[kk], sem).wait()

    gates = gates_ref[...]
    y = gates[:, 0:1] * gbuf[0]
    for kk in range(1, TOP_K):
        y = y + gates[:, kk:kk + 1] * gbuf[kk]
    x2 = x1_ref[...] + mod_ref[5:6, :] * y
    o_ref[...] = _rms(x2) * fg_ref[...]


def _combine(dest_flat, x1, gates, mod8, final_g, yb, seq):
    t, d = x1.shape
    tm = TOKEN_TILE
    tiles_per_seq = seq // tm
    return pl.pallas_call(
        _combine_kernel,
        grid=(t // tm,),
        in_specs=[pl.BlockSpec((tm * TOP_K,), lambda i: (i,), memory_space=pltpu.SMEM),
                  pl.BlockSpec((tm, d), lambda i: (i, 0)),
                  pl.BlockSpec((tm, LANES), lambda i: (i, 0)),
                  pl.BlockSpec((None, SUBLANES, d), lambda i: (i // tiles_per_seq, 0, 0)),
                  pl.BlockSpec((1, d), lambda i: (0, 0)),
                  pl.BlockSpec(memory_space=pl.ANY)],
        out_specs=pl.BlockSpec((tm, d), lambda i: (i, 0)),
        out_shape=jax.ShapeDtypeStruct((t, d), F32),
        scratch_shapes=[pltpu.VMEM((TOP_K, tm, d), F32), pltpu.SemaphoreType.DMA(())],
        compiler_params=_cparams("arbitrary"),
        name="combine",
    )(dest_flat, x1, gates, mod8, final_g[None], yb)


def kernel(x, c, w_ada, b_ada, norm1_g, w_in, b_forget, conv_w, conv_b, conv_ln_g, conv_ln_b, w_conv_out,
           w_attn_out, w_out, norm2_g, w_router, b_router, w_exp_in, b_exp_in, w_exp_out, b_exp_out, final_g):
    bsz, seq, d = x.shape
    t = bsz * seq
    depth = w_ada.shape[0]
    xf = x.reshape(t, d)
    n_rows = t * TOP_K + N_EXPERTS * EXPERT_BLOCK
    n_blocks = n_rows // EXPERT_BLOCK
    for l in range(depth):
        mod8 = _mod(c, w_ada[l], b_ada[l])
        a, q, k, v, tot, sgc, sga = _inproj(xf, mod8, norm1_g[l], w_in[l], b_forget[l], seq)
        act = _conv(a, conv_w[l], conv_b[l], conv_ln_g[l], conv_ln_b[l], bsz, seq)
        attn = _attention(q, k, v, tot, bsz, seq)
        x1, h2p, ids, gates = _merge(xf, act, attn, sgc, sga, mod8, norm2_g[l], w_conv_out[l], w_attn_out[l],
                                     w_out[l], w_router[l], b_router[l], seq)
        rank, cnt = _route(ids)
        dest, blk = _dest(ids, rank, cnt)
        dest_flat = dest[:, :TOP_K].reshape(t * TOP_K)
        block_e = blk[0].reshape(-1)[:n_blocks]
        n_active = blk[1, 0, :1]
        xs = _dispatch(dest_flat, h2p, n_rows)
        yb = _experts(block_e, n_active, xs, w_exp_in[l], b_exp_in[l], w_exp_out[l], b_exp_out[l])
        assert depth == 1
        xf = _combine(dest_flat, x1, gates, mod8, final_g, yb, seq)
    return xf.reshape(bsz, seq, d)
```

```python
import functools

import jax
import jax.numpy as jnp
from jax import lax
from jax.experimental import pallas as pl
from jax.experimental.pallas import tpu as pltpu

F32 = jnp.float32
BF16 = jnp.bfloat16
I32 = jnp.int32
U32 = jnp.uint32
HIGHEST = lax.Precision.HIGHEST

LANES = 128
SUBLANES = 8
VMEM_LIMIT_BYTES = 56 * 1024 * 1024

N_HEADS = 8
HEAD_DIM = 64
CONV_KERNEL = 31
CONV_HALO = 32
N_EXPERTS = 32
TOP_K = 4
SWIGLU_LIMIT = 7.0
SWIGLU_ALPHA = 1.702
RMS_EPS = 1e-5
LN_EPS = 1e-5
N_MOD = 6
N_SPLIT = 3
LOG2_E = 1.4426950408889634
HEADS_PER_STEP = 2

INPROJ_TILE = 512
ATTN_BQ = 2048
ATTN_BK = 512
MERGE_TILE = 512
CONV_TILE = 256
ROUTE_TILE = 512
TOKEN_TILE = 256
EXPERT_BLOCK = 256


def _cparams(*sem):
    return pltpu.CompilerParams(dimension_semantics=sem, vmem_limit_bytes=VMEM_LIMIT_BYTES)


def _resident(shape):
    nd = len(shape)
    return pl.BlockSpec(shape, lambda *_: (0,) * nd, pipeline_mode=pl.Buffered(1))


def _split3(x):
    p1 = x.astype(BF16)
    r1 = x - p1.astype(F32)
    p2 = r1.astype(BF16)
    p3 = (r1 - p2.astype(F32)).astype(BF16)
    return jnp.concatenate([p1, p2, p3], axis=1)


def _rms(x):
    return x * lax.rsqrt(jnp.mean(x * x, axis=-1, keepdims=True) + RMS_EPS)


def _mod_kernel(c_ref, w_ref, b_ref, o_ref):
    c = c_ref[...]
    sc = c * jax.nn.sigmoid(c)
    o_ref[...] = jnp.dot(sc, w_ref[...], preferred_element_type=F32, precision=HIGHEST) + b_ref[...]


def _mod(c, w_ada, b_ada):
    bsz, d = c.shape
    n = w_ada.shape[1]
    tn = 1024
    cp = jnp.zeros((SUBLANES, d), F32).at[:bsz].set(c)
    out = pl.pallas_call(
        _mod_kernel,
        grid=(n // tn,),
        in_specs=[pl.BlockSpec((SUBLANES, d), lambda j: (0, 0)),
                  pl.BlockSpec((d, tn), lambda j: (0, j)),
                  pl.BlockSpec((1, tn), lambda j: (0, j))],
        out_specs=pl.BlockSpec((SUBLANES, tn), lambda j: (0, j)),
        out_shape=jax.ShapeDtypeStruct((SUBLANES, n), F32),
        compiler_params=_cparams("arbitrary"),
        name="mod",
    )(cp, w_ada, b_ada[None])
    mod = out[:bsz].reshape(bsz, N_MOD, d)
    return jnp.pad(mod, ((0, 0), (0, SUBLANES - N_MOD), (0, 0)))


def _inproj_kernel(x_ref, mod_ref, g_ref, wa_ref, wq_ref, wk_ref, wv_ref, wf_ref, wg_ref, bf_ref,
                   tri_ref, eq_ref, ek_ref, oneq_ref, onek_ref, onev_ref,
                   a_ref, q_ref, k_ref, v_ref, tot_ref, sgc_ref, sga_ref, carry_scr, *, conv_width, d_model):
    x = x_ref[...]
    h = _rms(x) * g_ref[...] * (1.0 + mod_ref[1:2, :]) + mod_ref[0:1, :]
    hb = h.astype(BF16)

    u = jnp.dot(hb, wa_ref[...], preferred_element_type=F32)
    a_ref[...] = (u[:, :conv_width] * jax.nn.sigmoid(u[:, conv_width:])).astype(BF16)

    fl = jnp.dot(h, wf_ref[...], preferred_element_type=F32, precision=HIGHEST) + bf_ref[...]
    lane = lax.broadcasted_iota(I32, fl.shape, 1)
    log_f = -(jnp.maximum(-fl, 0.0) + jnp.log1p(jnp.exp(-jnp.abs(fl))))
    log_f = jnp.where(lane < N_HEADS, log_f, 0.0)
    cs = jnp.dot(tri_ref[...], _split3(log_f), preferred_element_type=F32)
    cum = (cs[:, :LANES] + cs[:, LANES:2 * LANES] + cs[:, 2 * LANES:]) * LOG2_E
    tm = cum.shape[0]

    pieces = []
    for sb in range(tm // ATTN_BK):
        blk = cum[sb * ATTN_BK:(sb + 1) * ATTN_BK, :]
        if sb > 0:
            blk = blk - cum[sb * ATTN_BK - 1:sb * ATTN_BK, :]
        tot_ref[sb] = blk[ATTN_BK - 1:, :]
        pieces.append(blk)
    cum_k = pieces[0] if len(pieces) == 1 else jnp.concatenate(pieces, axis=0)

    @pl.when(pl.program_id(0) % (ATTN_BQ // tm) == 0)
    def _():
        carry_scr[...] = jnp.zeros(carry_scr.shape, F32)

    cum_q = cum + carry_scr[0:1, :]
    carry_scr[...] = jnp.broadcast_to(cum_q[tm - 1:, :], carry_scr.shape)

    q = jnp.dot(hb, wq_ref[...], preferred_element_type=F32) * (HEAD_DIM ** -0.5 * LOG2_E)
    q = q + jnp.dot(_split3(cum_q), eq_ref[...], preferred_element_type=F32) + oneq_ref[...]
    q_ref[...] = q.astype(BF16)
    k = jnp.dot(hb, wk_ref[...], preferred_element_type=F32)
    k = k + jnp.dot(_split3(cum_k), ek_ref[...], preferred_element_type=F32) + onek_ref[...]
    k_ref[...] = k.astype(BF16)
    v_ref[...] = (jnp.dot(hb, wv_ref[...], preferred_element_type=F32) + onev_ref[...]).astype(BF16)

    gts = jnp.dot(hb, wg_ref[...], preferred_element_type=F32)
    sgc_ref[...] = jax.nn.sigmoid(gts[:, :d_model]).astype(BF16)
    sga_ref[...] = jax.nn.sigmoid(gts[:, d_model:]).astype(BF16)


def _spread_heads(w):
    d = w.shape[0]
    w = w.reshape(d, N_HEADS, HEAD_DIM)
    w = jnp.pad(w, ((0, 0), (0, 0), (0, LANES - HEAD_DIM)))
    return w.reshape(d, N_HEADS * LANES)


def _aug_constants():
    rows = jnp.arange(N_SPLIT * LANES)
    piece, head = rows // LANES, rows % LANES
    cols = jnp.arange(N_HEADS * LANES)
    chead, coff = cols // LANES, cols % LANES - HEAD_DIM
    valid = (head[:, None] < N_HEADS) & (head[:, None] == chead[None, :])
    eq = jnp.where(valid & (coff[None, :] == piece[:, None]), 1.0, 0.0).astype(BF16)
    ek = jnp.where(valid & (coff[None, :] == piece[:, None] + N_SPLIT), -1.0, 0.0).astype(BF16)
    oneq = jnp.where((coff >= N_SPLIT) & (coff < 2 * N_SPLIT), 1.0, 0.0).astype(F32)[None]
    onek = jnp.where((coff >= 0) & (coff < N_SPLIT), 1.0, 0.0).astype(F32)[None]
    onev = jnp.where(coff == 0, 1.0, 0.0).astype(F32)[None]
    return eq, ek, oneq, onek, onev


def _inproj(x2, mod8, norm_g, w_in, b_forget, seq):
    t, d = x2.shape
    tm = INPROJ_TILE
    nsub = tm // ATTN_BK
    cw = d // 2
    aw = N_HEADS * HEAD_DIM
    o0 = 2 * cw
    wa = w_in[:, :o0].astype(BF16)
    wq = _spread_heads(w_in[:, o0:o0 + aw]).astype(BF16)
    wk = _spread_heads(w_in[:, o0 + aw:o0 + 2 * aw]).astype(BF16)
    wv = _spread_heads(w_in[:, o0 + 2 * aw:o0 + 3 * aw]).astype(BF16)
    o1 = o0 + 3 * aw
    wf = jnp.pad(w_in[:, o1:o1 + N_HEADS], ((0, 0), (0, LANES - N_HEADS)))
    bfp = jnp.pad(b_forget, (0, LANES - N_HEADS))[None]
    wg = w_in[:, o1 + N_HEADS:].astype(BF16)
    r = jnp.arange(tm)
    tri = (r[None, :] <= r[:, None]).astype(BF16)
    eq, ek, oneq, onek, onev = _aug_constants()
    sw = N_HEADS * LANES
    tiles_per_seq = seq // tm
    row = lambda i: (i, 0)
    outs = pl.pallas_call(
        functools.partial(_inproj_kernel, conv_width=cw, d_model=d),
        grid=(t // tm,),
        in_specs=[pl.BlockSpec((tm, d), row),
                  pl.BlockSpec((None, SUBLANES, d), lambda i: (i // tiles_per_seq, 0, 0)),
                  _resident((1, d)),
                  _resident(wa.shape), _resident(wq.shape), _resident(wk.shape), _resident(wv.shape),
                  _resident(wf.shape), _resident(wg.shape), _resident(bfp.shape),
                  _resident(tri.shape), _resident(eq.shape), _resident(ek.shape),
                  _resident(oneq.shape), _resident(onek.shape), _resident(onev.shape)],
        out_specs=[pl.BlockSpec((tm, cw), row),
                   pl.BlockSpec((tm, sw), row), pl.BlockSpec((tm, sw), row), pl.BlockSpec((tm, sw), row),
                   pl.BlockSpec((nsub, 1, LANES), lambda i: (i, 0, 0)),
                   pl.BlockSpec((tm, d), row), pl.BlockSpec((tm, d), row)],
        out_shape=[jax.ShapeDtypeStruct((t, cw), BF16),
                   jax.ShapeDtypeStruct((t, sw), BF16), jax.ShapeDtypeStruct((t, sw), BF16),
                   jax.ShapeDtypeStruct((t, sw), BF16),
                   jax.ShapeDtypeStruct((t // ATTN_BK, 1, LANES), F32),
                   jax.ShapeDtypeStruct((t, d), BF16), jax.ShapeDtypeStruct((t, d), BF16)],
        scratch_shapes=[pltpu.VMEM((SUBLANES, LANES), F32)],
        compiler_params=_cparams("arbitrary"),
        name="inproj",
    )(x2, mod8, norm_g[None], wa, wq, wk, wv, wf, wg, bfp, tri, eq, ek, oneq, onek, onev)
    return outs


def _conv_kernel(halo_ref, cur_ref, w_ref, b_ref, g_ref, beta_ref, o_ref, xs_ref):
    ts = cur_ref.shape[0]
    first = pl.program_id(1) == 0
    halo = halo_ref[...].astype(F32)
    xs_ref[0:CONV_HALO, :] = jnp.where(first, 0.0, halo)
    xs_ref[CONV_HALO:, :] = cur_ref[...].astype(F32)
    acc = jnp.zeros(o_ref.shape, F32) + b_ref[...]
    base = CONV_HALO - (CONV_KERNEL - 1)
    for j in range(CONV_KERNEL):
        acc = acc + w_ref[j:j + 1, :] * xs_ref[base + j:base + j + ts, :]
    mu = jnp.mean(acc, axis=-1, keepdims=True)
    cen = acc - mu
    var = jnp.mean(cen * cen, axis=-1, keepdims=True)
    y = cen * lax.rsqrt(var + LN_EPS) * g_ref[...] + beta_ref[...]
    o_ref[...] = (y * jax.nn.sigmoid(y)).astype(BF16)


def _conv(a, conv_w, conv_b, ln_g, ln_b, bsz, seq):
    t, cw = a.shape
    ts = CONV_TILE
    nt = seq // ts
    hpt = ts // CONV_HALO
    vec = lambda: pl.BlockSpec((1, cw), lambda b, i: (0, 0))
    return pl.pallas_call(
        _conv_kernel,
        grid=(bsz, nt),
        in_specs=[pl.BlockSpec((CONV_HALO, cw), lambda b, i: (jnp.maximum((b * nt + i) * hpt - 1, 0), 0)),
                  pl.BlockSpec((ts, cw), lambda b, i: (b * nt + i, 0)),
                  pl.BlockSpec((CONV_KERNEL, cw), lambda b, i: (0, 0)),
                  vec(), vec(), vec()],
        out_specs=pl.BlockSpec((ts, cw), lambda b, i: (b * nt + i, 0)),
        out_shape=jax.ShapeDtypeStruct((t, cw), BF16),
        scratch_shapes=[pltpu.VMEM((CONV_HALO + ts, cw), F32)],
        compiler_params=_cparams("arbitrary", "arbitrary"),
        name="conv",
    )(a, a, conv_w, conv_b[None], ln_g[None], ln_b[None])


def _attn_kernel(q_ref, k_ref, v_ref, tot_ref, o_ref, m_scr, acc_scr):
    bq = q_ref.shape[0]
    bk = ATTN_BK
    ratio = bq // bk
    qi = pl.program_id(2)
    nt = (((1,), (1,)), ((), ()))
    heads = range(HEADS_PER_STEP)
    zero = jnp.zeros((1, LANES), F32)

    def scores(hh, kb, row0):
        lanes = slice(hh * LANES, (hh + 1) * LANES)
        start = pl.multiple_of(kb * bk, bk)
        s = lax.dot_general(q_ref[row0:, lanes], k_ref[pl.ds(start, bk), lanes], nt, preferred_element_type=F32)
        return s, v_ref[pl.ds(start, bk), lanes]

    def update(hh, s, v, delta, row0):
        m_prev = m_scr[hh, row0:, :]
        m_new = jnp.maximum(m_prev, jnp.max(s, axis=1, keepdims=True) + delta)
        shift = m_new - delta
        p = jnp.exp2(s - shift[:, :1])
        alpha = jnp.exp2(m_prev - m_new)
        pv = jnp.dot(p.astype(BF16), v, preferred_element_type=F32)
        acc_scr[hh, row0:, :] = alpha * acc_scr[hh, row0:, :] + pv
        m_scr[hh, row0:, :] = m_new

    for hh in heads:
        delta = zero
        for sb in range(ratio):
            kb = qi * ratio + sb
            row0 = sb * bk
            s, v = scores(hh, kb, row0)
            row = lax.broadcasted_iota(I32, s.shape, 0)
            col = lax.broadcasted_iota(I32, s.shape, 1)
            s = jnp.where(col <= row, s, -jnp.inf)
            if sb == 0:
                m0 = jnp.max(s, axis=1, keepdims=True)
                p = jnp.exp2(s - m0)
                m_scr[hh] = jnp.broadcast_to(m0, m_scr.shape[1:])
                acc_scr[hh] = jnp.dot(p.astype(BF16), v, preferred_element_type=F32)
            else:
                update(hh, s, v, delta, row0)
            delta = delta - tot_ref[hh, pl.ds(kb, 1), :]

    def body(step, deltas):
        kb = qi * ratio - 1 - step
        new_deltas = []
        for hh in heads:
            delta = deltas[hh] + tot_ref[hh, pl.ds(kb, 1), :]
            s, v = scores(hh, kb, 0)
            update(hh, s, v, delta, 0)
            new_deltas.append(delta)
        return tuple(new_deltas)

    lax.fori_loop(0, qi * ratio, body, tuple(zero for _ in heads))
    for hh in heads:
        acc = acc_scr[hh]
        lane = lax.broadcasted_iota(I32, acc.shape, 1)
        out = jnp.where(lane < HEAD_DIM, acc / acc[:, HEAD_DIM:HEAD_DIM + 1], 0.0)
        o_ref[:, hh * LANES:(hh + 1) * LANES] = out.astype(BF16)


def _attention(q, k, v, tot, bsz, seq):
    t, sw = q.shape
    bq = ATTN_BQ
    nq = seq // bq
    nk = seq // ATTN_BK
    hw = HEADS_PER_STEP * LANES
    tot = tot.reshape(bsz, nk, LANES)[:, :, :N_HEADS].transpose(0, 2, 1)
    tot = jnp.broadcast_to(tot[..., None], (bsz, N_HEADS, nk, LANES))
    k3 = k.reshape(bsz, seq, sw)
    v3 = v.reshape(bsz, seq, sw)
    return pl.pallas_call(
        _attn_kernel,
        grid=(bsz, N_HEADS // HEADS_PER_STEP, nq),
        in_specs=[pl.BlockSpec((bq, hw), lambda b, h, i: (b * nq + i, h)),
                  pl.BlockSpec((None, seq, hw), lambda b, h, i: (b, 0, h), pipeline_mode=pl.Buffered(1)),
                  pl.BlockSpec((None, seq, hw), lambda b, h, i: (b, 0, h), pipeline_mode=pl.Buffered(1)),
                  pl.BlockSpec((None, HEADS_PER_STEP, nk, LANES), lambda b, h, i: (b, h, 0, 0))],
        out_specs=pl.BlockSpec((bq, hw), lambda b, h, i: (b * nq + i, h)),
        out_shape=jax.ShapeDtypeStruct((t, sw), BF16),
        scratch_shapes=[pltpu.VMEM((HEADS_PER_STEP, bq, LANES), F32),
                        pltpu.VMEM((HEADS_PER_STEP, bq, LANES), F32)],
        compiler_params=_cparams("arbitrary", "arbitrary", "arbitrary"),
        name="attn",
    )(q, k3, v3, tot)


def _pack_bf16_pair(lo, hi):
    lo_bits = pltpu.bitcast(lo.astype(BF16).astype(F32), U32)
    hi_bits = pltpu.bitcast(hi.astype(BF16).astype(F32), U32)
    return (hi_bits & jnp.uint32(0xFFFF0000)) | (lo_bits >> 16)


def _unpack_bf16_pair(w):
    lo = pltpu.bitcast(w << 16, F32)
    hi = pltpu.bitcast(w & jnp.uint32(0xFFFF0000), F32)
    return jnp.concatenate([lo, hi], axis=1)


def _merge_kernel(x_ref, act_ref, attn_ref, sgc_ref, sga_ref, mod_ref, g2_ref, wco_ref, wao_ref, wout_ref,
                  wr_ref, br_ref, x1_ref, h2_ref, ids_ref, gates_ref):
    conv_out = jnp.dot(act_ref[...], wco_ref[...], preferred_element_type=F32)
    attn_out = jnp.dot(attn_ref[...], wao_ref[...], preferred_element_type=F32)
    mixed = sgc_ref[...].astype(F32) * conv_out + sga_ref[...].astype(F32) * attn_out
    upd = jnp.dot(mixed.astype(BF16), wout_ref[...], preferred_element_type=F32)
    x1 = x_ref[...] + mod_ref[2:3, :] * upd
    x1_ref[...] = x1
    h2 = _rms(x1) * g2_ref[...] * (1.0 + mod_ref[4:5, :]) + mod_ref[3:4, :]
    half = h2.shape[1] // 2
    h2_ref[...] = _pack_bf16_pair(h2[:, :half], h2[:, half:])

    logits = jnp.dot(h2, wr_ref[...], preferred_element_type=F32, precision=HIGHEST) + br_ref[...]
    lane = lax.broadcasted_iota(I32, logits.shape, 1)
    work = jnp.where(lane < N_EXPERTS, logits, -jnp.inf)
    ids = jnp.zeros(logits.shape, I32)
    vals = []
    for kk in range(TOP_K):
        mx = jnp.max(work, axis=1, keepdims=True)
        idx = jnp.min(jnp.where(work == mx, lane, LANES), axis=1, keepdims=True)
        ids = jnp.where(lane == kk, idx, ids)
        vals.append(mx)
        work = jnp.where(lane == idx, -jnp.inf, work)
    exps = [jnp.exp(vv - vals[0]) for vv in vals]
    den = exps[0] + exps[1] + exps[2] + exps[3]
    gates = jnp.zeros(logits.shape, F32)
    for kk in range(TOP_K):
        gates = jnp.where(lane == kk, exps[kk] / den, gates)
    ids_ref[...] = ids
    gates_ref[...] = gates


def _merge(x2, act, attn, sgc, sga, mod8, norm2_g, w_conv_out, w_attn_out, w_out, w_router, b_router, seq):
    t, d = x2.shape
    tm = MERGE_TILE
    cw = act.shape[1]
    sw = attn.shape[1]
    wco = w_conv_out.astype(BF16)
    wao = jnp.pad(w_attn_out.reshape(N_HEADS, HEAD_DIM, d), ((0, 0), (0, LANES - HEAD_DIM), (0, 0)))
    wao = wao.reshape(sw, d).astype(BF16)
    wout = w_out.astype(BF16)
    wr = jnp.pad(w_router, ((0, 0), (0, LANES - N_EXPERTS)))
    br = jnp.pad(b_router, (0, LANES - N_EXPERTS))[None]
    tiles_per_seq = seq // tm
    row = lambda i: (i, 0)
    return pl.pallas_call(
        _merge_kernel,
        grid=(t // tm,),
        in_specs=[pl.BlockSpec((tm, d), row), pl.BlockSpec((tm, cw), row), pl.BlockSpec((tm, sw), row),
                  pl.BlockSpec((tm, d), row), pl.BlockSpec((tm, d), row),
                  pl.BlockSpec((None, SUBLANES, d), lambda i: (i // tiles_per_seq, 0, 0)),
                  _resident((1, d)), _resident(wco.shape), _resident(wao.shape), _resident(wout.shape),
                  _resident(wr.shape), _resident(br.shape)],
        out_specs=[pl.BlockSpec((tm, d), row), pl.BlockSpec((tm, d // 2), row),
                   pl.BlockSpec((tm, LANES), row), pl.BlockSpec((tm, LANES), row)],
        out_shape=[jax.ShapeDtypeStruct((t, d), F32), jax.ShapeDtypeStruct((t, d // 2), U32),
                   jax.ShapeDtypeStruct((t, LANES), I32), jax.ShapeDtypeStruct((t, LANES), F32)],
        compiler_params=_cparams("arbitrary"),
        name="merge",
    )(x2, act, attn, sgc, sga, mod8, norm2_g[None], wco, wao, wout, wr, br)


def _expert_onehots(ids):
    lane = lax.broadcasted_iota(I32, ids.shape, 1)
    return lane, [lane == ids[:, kk:kk + 1] for kk in range(TOP_K)]


def _route_kernel(ids_ref, tri_ref, rank_ref, cnt_ref, carry_scr):
    @pl.when(pl.program_id(0) == 0)
    def _():
        carry_scr[...] = jnp.zeros(carry_scr.shape, F32)

    lane, hots = _expert_onehots(ids_ref[...])
    multi = jnp.zeros(lane.shape, F32)
    for hot in hots:
        multi = multi + jnp.where(hot, 1.0, 0.0)
    carry = carry_scr[0:1, :]
    before = jnp.dot(tri_ref[...], multi.astype(BF16), preferred_element_type=F32) + carry
    rank = jnp.zeros(lane.shape, F32)
    for kk, hot in enumerate(hots):
        rk = jnp.sum(jnp.where(hot, before, 0.0), axis=1, keepdims=True)
        rank = jnp.where(lane == kk, rk, rank)
    rank_ref[...] = rank.astype(I32)
    carry = carry + jnp.sum(multi, axis=0, keepdims=True)
    carry_scr[...] = jnp.broadcast_to(carry, carry_scr.shape)
    cnt_ref[...] = jnp.broadcast_to(carry, cnt_ref.shape)


def _route(ids):
    t = ids.shape[0]
    tm = ROUTE_TILE
    r = jnp.arange(tm)
    tri = (r[None, :] < r[:, None]).astype(BF16)
    return pl.pallas_call(
        _route_kernel,
        grid=(t // tm,),
        in_specs=[pl.BlockSpec((tm, LANES), lambda i: (i, 0)), _resident(tri.shape)],
        out_specs=[pl.BlockSpec((tm, LANES), lambda i: (i, 0)),
                   pl.BlockSpec((SUBLANES, LANES), lambda i: (0, 0))],
        out_shape=[jax.ShapeDtypeStruct((t, LANES), I32), jax.ShapeDtypeStruct((SUBLANES, LANES), F32)],
        scratch_shapes=[pltpu.VMEM((SUBLANES, LANES), F32)],
        compiler_params=_cparams("arbitrary"),
        name="route",
    )(ids, tri)


def _dest_kernel(ids_ref, rank_ref, cnt_ref, tri_ref, dest_ref, blk_ref):
    cnt = cnt_ref[...]
    padded = jnp.ceil(cnt * (1.0 / EXPERT_BLOCK)) * EXPERT_BLOCK
    pend = jnp.dot(padded, tri_ref[...], preferred_element_type=F32, precision=HIGHEST)
    pstart = (pend - padded)[0:1, :]

    lane, hots = _expert_onehots(ids_ref[...])
    rank = rank_ref[...]
    dest = jnp.zeros(lane.shape, I32)
    for kk, hot in enumerate(hots):
        ps = jnp.sum(jnp.where(hot, pstart, 0.0), axis=1, keepdims=True).astype(I32)
        dest = jnp.where(lane == kk, ps + rank[:, kk:kk + 1], dest)
    dest_ref[...] = dest

    @pl.when(pl.program_id(0) == 0)
    def _():
        shape = blk_ref.shape[1:]
        pos = (lax.broadcasted_iota(I32, shape, 0) * LANES + lax.broadcasted_iota(I32, shape, 1))
        pos = pos.astype(F32) * EXPERT_BLOCK
        be = jnp.zeros(shape, I32)
        for e in range(N_EXPERTS):
            be = be + jnp.where(pos >= pend[0:1, e:e + 1], 1, 0)
        blk_ref[0] = jnp.minimum(be, N_EXPERTS - 1)
        nact = pend[0:1, N_EXPERTS - 1:N_EXPERTS] * (1.0 / EXPERT_BLOCK)
        blk_ref[1] = jnp.broadcast_to(nact, shape).astype(I32)


def _dest(ids, rank, cnt):
    t = ids.shape[0]
    tm = ROUTE_TILE
    r = jnp.arange(LANES)
    tri = (r[:, None] <= r[None, :]).astype(F32)
    tile = pl.BlockSpec((tm, LANES), lambda i: (i, 0))
    return pl.pallas_call(
        _dest_kernel,
        grid=(t // tm,),
        in_specs=[tile, tile, pl.BlockSpec((SUBLANES, LANES), lambda i: (0, 0)),
                  pl.BlockSpec((LANES, LANES), lambda i: (0, 0))],
        out_specs=[tile, pl.BlockSpec((2, SUBLANES, LANES), lambda i: (0, 0, 0))],
        out_shape=[jax.ShapeDtypeStruct((t, LANES), I32), jax.ShapeDtypeStruct((2, SUBLANES, LANES), I32)],
        compiler_params=_cparams("arbitrary"),
        name="dest",
    )(ids, rank, cnt, tri)


def _row_copy(src, dst, sem):
    return pltpu.make_async_copy(src, dst, sem)


def _dispatch_kernel(dest_ref, h2_ref, xs_in_ref, xs_ref, sem):
    del xs_in_ref
    tm = h2_ref.shape[0]

    def issue(r, carry):
        for kk in range(TOP_K):
            d = dest_ref[r * TOP_K + kk]
            _row_copy(h2_ref.at[pl.ds(r, 1)], xs_ref.at[pl.ds(d, 1)], sem).start()
        return carry

    lax.fori_loop(0, tm, issue, 0)
    for kk in range(TOP_K):
        _row_copy(h2_ref, xs_ref.at[pl.ds(0, tm)], sem).wait()


def _dispatch(dest_flat, h2p, n_rows):
    t, w = h2p.shape
    tm = TOKEN_TILE
    xs0 = jnp.zeros((n_rows, w), U32)
    return pl.pallas_call(
        _dispatch_kernel,
        grid=(t // tm,),
        in_specs=[pl.BlockSpec((tm * TOP_K,), lambda i: (i,), memory_space=pltpu.SMEM),
                  pl.BlockSpec((tm, w), lambda i: (i, 0)),
                  pl.BlockSpec(memory_space=pl.ANY)],
        out_specs=pl.BlockSpec(memory_space=pl.ANY),
        out_shape=jax.ShapeDtypeStruct((n_rows, w), U32),
        scratch_shapes=[pltpu.SemaphoreType.DMA(())],
        input_output_aliases={2: 0},
        compiler_params=_cparams("arbitrary"),
        name="dispatch",
    )(dest_flat, h2p, xs0)


def _expert_kernel(be_ref, na_ref, xs_ref, w1_ref, b1_ref, w2_ref, b2_ref, y_ref, w1b_ref, w2b_ref, *, d_expert):
    i = pl.program_id(0)
    active = i < na_ref[0]
    prev = be_ref[jnp.maximum(i - 1, 0)]
    new_expert = jnp.logical_or(i == 0, be_ref[i] != prev)

    @pl.when(jnp.logical_and(active, new_expert))
    def _():
        w1b_ref[...] = w1_ref[...].astype(BF16)
        w2b_ref[...] = w2_ref[...].astype(BF16)

    @pl.when(active)
    def _():
        x = _unpack_bf16_pair(xs_ref[...]).astype(BF16)
        hc = jnp.dot(x, w1b_ref[...], preferred_element_type=F32) + b1_ref[...]
        g = jnp.minimum(hc[:, :d_expert], SWIGLU_LIMIT)
        u = jnp.clip(hc[:, d_expert:], -SWIGLU_LIMIT, SWIGLU_LIMIT)
        hm = (u + 1.0) * (g * jax.nn.sigmoid(SWIGLU_ALPHA * g))
        y_ref[...] = jnp.dot(hm.astype(BF16), w2b_ref[...], preferred_element_type=F32) + b2_ref[...]

    @pl.when(jnp.logical_not(active))
    def _():
        y_ref[...] = jnp.zeros(y_ref.shape, F32)


def _experts(block_e, n_active, xs, w1, b1, w2, b2):
    n_rows, w = xs.shape
    n_exp, d, two_de = w1.shape
    de = two_de // 2
    bm = EXPERT_BLOCK
    nb = n_rows // bm

    def blk(i, be, na):
        return jnp.minimum(i, na[0] - 1)

    grid_spec = pltpu.PrefetchScalarGridSpec(
        num_scalar_prefetch=2,
        grid=(nb,),
        in_specs=[pl.BlockSpec((bm, w), lambda i, be, na: (blk(i, be, na), 0)),
                  pl.BlockSpec((None, d, two_de), lambda i, be, na: (be[blk(i, be, na)], 0, 0)),
                  pl.BlockSpec((None, 1, two_de), lambda i, be, na: (be[blk(i, be, na)], 0, 0)),
                  pl.BlockSpec((None, de, d), lambda i, be, na: (be[blk(i, be, na)], 0, 0)),
                  pl.BlockSpec((None, 1, d), lambda i, be, na: (be[blk(i, be, na)], 0, 0))],
        out_specs=pl.BlockSpec((bm, d), lambda i, be, na: (i, 0)),
        scratch_shapes=[pltpu.VMEM((d, two_de), BF16), pltpu.VMEM((de, d), BF16)],
    )
    return pl.pallas_call(
        functools.partial(_expert_kernel, d_expert=de),
        grid_spec=grid_spec,
        out_shape=jax.ShapeDtypeStruct((n_rows, d), F32),
        compiler_params=_cparams("arbitrary"),
        name="expert",
    )(block_e, n_active, xs, w1, b1[:, None, :], w2, b2[:, None, :])


def _combine_kernel(dest_ref, x1_ref, gates_ref, mod_ref, fg_ref, yb_ref, o_ref, gbuf, sem):
    tm = x1_ref.shape[0]

    def issue(r, carry):
        for kk in range(TOP_K):
            d = dest_ref[r * TOP_K + kk]
            _row_copy(yb_ref.at[pl.ds(d, 1)], gbuf.at[kk, pl.ds(r, 1)], sem).start()
        return carry

    lax.fori_loop(0, tm, issue, 0)
    for kk in range(TOP_K):
        _row_copy(yb_ref.at[pl.ds(0, tm)], gbuf.at[kk], sem).wait()

    gates = gates_ref[...]
    y = gates[:, 0:1] * gbuf[0]
    for kk in range(1, TOP_K):
        y = y + gates[:, kk:kk + 1] * gbuf[kk]
    x2 = x1_ref[...] + mod_ref[5:6, :] * y
    o_ref[...] = _rms(x2) * fg_ref[...]


def _combine(dest_flat, x1, gates, mod8, final_g, yb, seq):
    t, d = x1.shape
    tm = TOKEN_TILE
    tiles_per_seq = seq // tm
    return pl.pallas_call(
        _combine_kernel,
        grid=(t // tm,),
        in_specs=[pl.BlockSpec((tm * TOP_K,), lambda i: (i,), memory_space=pltpu.SMEM),
                  pl.BlockSpec((tm, d), lambda i: (i, 0)),
                  pl.BlockSpec((tm, LANES), lambda i: (i, 0)),
                  pl.BlockSpec((None, SUBLANES, d), lambda i: (i // tiles_per_seq, 0, 0)),
                  pl.BlockSpec((1, d), lambda i: (0, 0)),
                  pl.BlockSpec(memory_space=pl.ANY)],
        out_specs=pl.BlockSpec((tm, d), lambda i: (i, 0)),
        out_shape=jax.ShapeDtypeStruct((t, d), F32),
        scratch_shapes=[pltpu.VMEM((TOP_K, tm, d), F32), pltpu.SemaphoreType.DMA(())],
        compiler_params=_cparams("arbitrary"),
        name="combine",
    )(dest_flat, x1, gates, mod8, final_g[None], yb)


def kernel(x, c, w_ada, b_ada, norm1_g, w_in, b_forget, conv_w, conv_b, conv_ln_g, conv_ln_b, w_conv_out,
           w_attn_out, w_out, norm2_g, w_router, b_router, w_exp_in, b_exp_in, w_exp_out, b_exp_out, final_g):
    bsz, seq, d = x.shape
    t = bsz * seq
    depth = w_ada.shape[0]
    xf = x.reshape(t, d)
    n_rows = t * TOP_K + N_EXPERTS * EXPERT_BLOCK
    n_blocks = n_rows // EXPERT_BLOCK
    for l in range(depth):
        mod8 = _mod(c, w_ada[l], b_ada[l])
        a, q, k, v, tot, sgc, sga = _inproj(xf, mod8, norm1_g[l], w_in[l], b_forget[l], seq)
        act = _conv(a, conv_w[l], conv_b[l], conv_ln_g[l], conv_ln_b[l], bsz, seq)
        attn = _attention(q, k, v, tot, bsz, seq)
        x1, h2p, ids, gates = _merge(xf, act, attn, sgc, sga, mod8, norm2_g[l], w_conv_out[l], w_attn_out[l],
                                     w_out[l], w_router[l], b_router[l], seq)
        rank, cnt = _route(ids)
        dest, blk = _dest(ids, rank, cnt)
        dest_flat = dest[:, :TOP_K].reshape(t * TOP_K)
        block_e = blk[0].reshape(-1)[:n_blocks]
        n_active = blk[1, 0, :1]
        xs = _dispatch(dest_flat, h2p, n_rows)
        yb = _experts(block_e, n_active, xs, w_exp_in[l], b_exp_in[l], w_exp_out[l], b_exp_out[l])
        assert depth == 1
        xf = _combine(dest_flat, x1, gates, mod8, final_g, yb, seq)
    return xf.reshape(bsz, seq, d)
```

```python
import functools

import jax
import jax.numpy as jnp
from jax import lax
from jax.experimental import pallas as pl
from jax.experimental.pallas import tpu as pltpu

F32 = jnp.float32
BF16 = jnp.bfloat16
I32 = jnp.int32
U32 = jnp.uint32
HIGHEST = lax.Precision.HIGHEST

LANES = 128
SUBLANES = 8
VMEM_LIMIT_BYTES = 56 * 1024 * 1024

N_HEADS = 8
HEAD_DIM = 64
CONV_KERNEL = 31
CONV_HALO = 32
N_EXPERTS = 32
TOP_K = 4
SWIGLU_LIMIT = 7.0
SWIGLU_ALPHA = 1.702
RMS_EPS = 1e-5
LN_EPS = 1e-5
N_MOD = 6
N_SPLIT = 3
LOG2_E = 1.4426950408889634
HEADS_PER_STEP = 2

INPROJ_TILE = 512
ATTN_BQ = 2048
ATTN_BK = 512
MERGE_TILE = 512
CONV_TILE = 256
ROUTE_TILE = 512
TOKEN_TILE = 256
EXPERT_BLOCK = 512


def _cparams(*sem):
    return pltpu.CompilerParams(dimension_semantics=sem, vmem_limit_bytes=VMEM_LIMIT_BYTES)


def _resident(shape):
    nd = len(shape)
    return pl.BlockSpec(shape, lambda *_: (0,) * nd, pipeline_mode=pl.Buffered(1))


def _split3(x):
    p1 = x.astype(BF16)
    r1 = x - p1.astype(F32)
    p2 = r1.astype(BF16)
    p3 = (r1 - p2.astype(F32)).astype(BF16)
    return jnp.concatenate([p1, p2, p3], axis=1)


def _rms(x):
    return x * lax.rsqrt(jnp.mean(x * x, axis=-1, keepdims=True) + RMS_EPS)


def _mod_kernel(c_ref, w_ref, b_ref, o_ref):
    c = c_ref[...]
    sc = c * jax.nn.sigmoid(c)
    o_ref[...] = jnp.dot(sc, w_ref[...], preferred_element_type=F32, precision=HIGHEST) + b_ref[...]


def _mod(c, w_ada, b_ada):
    bsz, d = c.shape
    n = w_ada.shape[1]
    tn = 1024
    cp = jnp.zeros((SUBLANES, d), F32).at[:bsz].set(c)
    out = pl.pallas_call(
        _mod_kernel,
        grid=(n // tn,),
        in_specs=[pl.BlockSpec((SUBLANES, d), lambda j: (0, 0)),
                  pl.BlockSpec((d, tn), lambda j: (0, j)),
                  pl.BlockSpec((1, tn), lambda j: (0, j))],
        out_specs=pl.BlockSpec((SUBLANES, tn), lambda j: (0, j)),
        out_shape=jax.ShapeDtypeStruct((SUBLANES, n), F32),
        compiler_params=_cparams("arbitrary"),
        name="mod",
    )(cp, w_ada, b_ada[None])
    mod = out[:bsz].reshape(bsz, N_MOD, d)
    return jnp.pad(mod, ((0, 0), (0, SUBLANES - N_MOD), (0, 0)))


def _inproj_kernel(x_ref, mod_ref, g_ref, wa_ref, wq_ref, wk_ref, wv_ref, wf_ref, wg_ref, bf_ref,
                   tri_ref, eq_ref, ek_ref, oneq_ref, onek_ref, onev_ref,
                   a_ref, q_ref, k_ref, v_ref, tot_ref, sgc_ref, sga_ref, carry_scr, *, conv_width, d_model):
    x = x_ref[...]
    h = _rms(x) * g_ref[...] * (1.0 + mod_ref[1:2, :]) + mod_ref[0:1, :]
    hb = h.astype(BF16)

    u = jnp.dot(hb, wa_ref[...], preferred_element_type=F32)
    a_ref[...] = (u[:, :conv_width] * jax.nn.sigmoid(u[:, conv_width:])).astype(BF16)

    fl = jnp.dot(hb, wf_ref[...], preferred_element_type=F32) + bf_ref[...]
    lane = lax.broadcasted_iota(I32, fl.shape, 1)
    log_f = -(jnp.maximum(-fl, 0.0) + jnp.log1p(jnp.exp(-jnp.abs(fl))))
    log_f = jnp.where(lane < N_HEADS, log_f, 0.0)
    cs = jnp.dot(tri_ref[...], _split3(log_f), preferred_element_type=F32)
    cum = (cs[:, :LANES] + cs[:, LANES:2 * LANES] + cs[:, 2 * LANES:]) * LOG2_E
    tm = cum.shape[0]

    pieces = []
    for sb in range(tm // ATTN_BK):
        blk = cum[sb * ATTN_BK:(sb + 1) * ATTN_BK, :]
        if sb > 0:
            blk = blk - cum[sb * ATTN_BK - 1:sb * ATTN_BK, :]
        tot_ref[sb] = blk[ATTN_BK - 1:, :]
        pieces.append(blk)
    cum_k = pieces[0] if len(pieces) == 1 else jnp.concatenate(pieces, axis=0)

    @pl.when(pl.program_id(0) % (ATTN_BQ // tm) == 0)
    def _():
        carry_scr[...] = jnp.zeros(carry_scr.shape, F32)

    cum_q = cum + carry_scr[0:1, :]
    carry_scr[...] = jnp.broadcast_to(cum_q[tm - 1:, :], carry_scr.shape)

    q = jnp.dot(hb, wq_ref[...], preferred_element_type=F32) * (HEAD_DIM ** -0.5 * LOG2_E)
    q = q + jnp.dot(_split3(cum_q), eq_ref[...], preferred_element_type=F32) + oneq_ref[...]
    q_ref[...] = q.astype(BF16)
    k = jnp.dot(hb, wk_ref[...], preferred_element_type=F32)
    k = k + jnp.dot(_split3(cum_k), ek_ref[...], preferred_element_type=F32) + onek_ref[...]
    k_ref[...] = k.astype(BF16)
    v_ref[...] = (jnp.dot(hb, wv_ref[...], preferred_element_type=F32) + onev_ref[...]).astype(BF16)

    gts = jnp.dot(hb, wg_ref[...], preferred_element_type=F32)
    sgc_ref[...] = jax.nn.sigmoid(gts[:, :d_model]).astype(BF16)
    sga_ref[...] = jax.nn.sigmoid(gts[:, d_model:]).astype(BF16)


def _spread_heads(w):
    d = w.shape[0]
    w = w.reshape(d, N_HEADS, HEAD_DIM)
    w = jnp.pad(w, ((0, 0), (0, 0), (0, LANES - HEAD_DIM)))
    return w.reshape(d, N_HEADS * LANES)


def _aug_constants():
    rows = jnp.arange(N_SPLIT * LANES)
    piece, head = rows // LANES, rows % LANES
    cols = jnp.arange(N_HEADS * LANES)
    chead, coff = cols // LANES, cols % LANES - HEAD_DIM
    valid = (head[:, None] < N_HEADS) & (head[:, None] == chead[None, :])
    eq = jnp.where(valid & (coff[None, :] == piece[:, None]), 1.0, 0.0).astype(BF16)
    ek = jnp.where(valid & (coff[None, :] == piece[:, None] + N_SPLIT), -1.0, 0.0).astype(BF16)
    oneq = jnp.where((coff >= N_SPLIT) & (coff < 2 * N_SPLIT), 1.0, 0.0).astype(F32)[None]
    onek = jnp.where((coff >= 0) & (coff < N_SPLIT), 1.0, 0.0).astype(F32)[None]
    onev = jnp.where(coff == 0, 1.0, 0.0).astype(F32)[None]
    return eq, ek, oneq, onek, onev


def _inproj(x2, mod8, norm_g, w_in, b_forget, seq):
    t, d = x2.shape
    tm = INPROJ_TILE
    nsub = tm // ATTN_BK
    cw = d // 2
    aw = N_HEADS * HEAD_DIM
    o0 = 2 * cw
    wa = w_in[:, :o0].astype(BF16)
    wq = _spread_heads(w_in[:, o0:o0 + aw]).astype(BF16)
    wk = _spread_heads(w_in[:, o0 + aw:o0 + 2 * aw]).astype(BF16)
    wv = _spread_heads(w_in[:, o0 + 2 * aw:o0 + 3 * aw]).astype(BF16)
    o1 = o0 + 3 * aw
    wf = jnp.pad(w_in[:, o1:o1 + N_HEADS], ((0, 0), (0, LANES - N_HEADS))).astype(BF16)
    bfp = jnp.pad(b_forget, (0, LANES - N_HEADS))[None]
    wg = w_in[:, o1 + N_HEADS:].astype(BF16)
    r = jnp.arange(tm)
    tri = (r[None, :] <= r[:, None]).astype(BF16)
    eq, ek, oneq, onek, onev = _aug_constants()
    sw = N_HEADS * LANES
    tiles_per_seq = seq // tm
    row = lambda i: (i, 0)
    outs = pl.pallas_call(
        functools.partial(_inproj_kernel, conv_width=cw, d_model=d),
        grid=(t // tm,),
        in_specs=[pl.BlockSpec((tm, d), row),
                  pl.BlockSpec((None, SUBLANES, d), lambda i: (i // tiles_per_seq, 0, 0)),
                  _resident((1, d)),
                  _resident(wa.shape), _resident(wq.shape), _resident(wk.shape), _resident(wv.shape),
                  _resident(wf.shape), _resident(wg.shape), _resident(bfp.shape),
                  _resident(tri.shape), _resident(eq.shape), _resident(ek.shape),
                  _resident(oneq.shape), _resident(onek.shape), _resident(onev.shape)],
        out_specs=[pl.BlockSpec((tm, cw), row),
                   pl.BlockSpec((tm, sw), row), pl.BlockSpec((tm, sw), row), pl.BlockSpec((tm, sw), row),
                   pl.BlockSpec((nsub, 1, LANES), lambda i: (i, 0, 0)),
                   pl.BlockSpec((tm, d), row), pl.BlockSpec((tm, d), row)],
        out_shape=[jax.ShapeDtypeStruct((t, cw), BF16),
                   jax.ShapeDtypeStruct((t, sw), BF16), jax.ShapeDtypeStruct((t, sw), BF16),
                   jax.ShapeDtypeStruct((t, sw), BF16),
                   jax.ShapeDtypeStruct((t // ATTN_BK, 1, LANES), F32),
                   jax.ShapeDtypeStruct((t, d), BF16), jax.ShapeDtypeStruct((t, d), BF16)],
        scratch_shapes=[pltpu.VMEM((SUBLANES, LANES), F32)],
        compiler_params=_cparams("arbitrary"),
        name="inproj",
    )(x2, mod8, norm_g[None], wa, wq, wk, wv, wf, wg, bfp, tri, eq, ek, oneq, onek, onev)
    return outs


def _conv_kernel(halo_ref, cur_ref, w_ref, b_ref, g_ref, beta_ref, o_ref, xs_ref, ys_ref):
    ts = cur_ref.shape[0]
    first = pl.program_id(1) == 0
    halo = halo_ref[...].astype(F32)
    xs_ref[0:CONV_HALO, :] = jnp.where(first, 0.0, halo)
    xs_ref[CONV_HALO:, :] = cur_ref[...].astype(F32)
    acc = jnp.zeros(o_ref.shape, F32) + b_ref[...]
    base = CONV_HALO - (CONV_KERNEL - 1)
    for rho in range(SUBLANES):
        offs = [o for o in range(base, base + CONV_KERNEL) if o % SUBLANES == rho]
        if not offs:
            continue
        rows = offs[-1] + ts - rho
        ys_ref[0:rows, :] = xs_ref[rho:rho + rows, :]
        for o in offs:
            lo = o - rho
            acc = acc + w_ref[o - base:o - base + 1, :] * ys_ref[lo:lo + ts, :]
    mu = jnp.mean(acc, axis=-1, keepdims=True)
    cen = acc - mu
    var = jnp.mean(cen * cen, axis=-1, keepdims=True)
    y = cen * lax.rsqrt(var + LN_EPS) * g_ref[...] + beta_ref[...]
    o_ref[...] = (y * jax.nn.sigmoid(y)).astype(BF16)


def _conv(a, conv_w, conv_b, ln_g, ln_b, bsz, seq):
    t, cw = a.shape
    ts = CONV_TILE
    nt = seq // ts
    hpt = ts // CONV_HALO
    vec = lambda: pl.BlockSpec((1, cw), lambda b, i: (0, 0))
    return pl.pallas_call(
        _conv_kernel,
        grid=(bsz, nt),
        in_specs=[pl.BlockSpec((CONV_HALO, cw), lambda b, i: (jnp.maximum((b * nt + i) * hpt - 1, 0), 0)),
                  pl.BlockSpec((ts, cw), lambda b, i: (b * nt + i, 0)),
                  pl.BlockSpec((CONV_KERNEL, cw), lambda b, i: (0, 0)),
                  vec(), vec(), vec()],
        out_specs=pl.BlockSpec((ts, cw), lambda b, i: (b * nt + i, 0)),
        out_shape=jax.ShapeDtypeStruct((t, cw), BF16),
        scratch_shapes=[pltpu.VMEM((CONV_HALO + ts, cw), F32), pltpu.VMEM((CONV_HALO + ts, cw), F32)],
        compiler_params=_cparams("arbitrary", "arbitrary"),
        name="conv",
    )(a, a, conv_w, conv_b[None], ln_g[None], ln_b[None])


def _attn_kernel(q_ref, k_ref, v_ref, tot_ref, o_ref, m_scr, acc_scr):
    bq = q_ref.shape[0]
    bk = ATTN_BK
    ratio = bq // bk
    qi = pl.program_id(2)
    nt = (((1,), (1,)), ((), ()))
    heads = range(HEADS_PER_STEP)
    zero = jnp.zeros((1, LANES), F32)

    def scores(hh, kb, row0):
        lanes = slice(hh * LANES, (hh + 1) * LANES)
        start = pl.multiple_of(kb * bk, bk)
        s = lax.dot_general(q_ref[row0:, lanes], k_ref[pl.ds(start, bk), lanes], nt, preferred_element_type=F32)
        return s, v_ref[pl.ds(start, bk), lanes]

    def update(hh, s, v, delta, row0):
        m_prev = m_scr[hh, row0:, :]
        m_new = jnp.maximum(m_prev, jnp.max(s, axis=1, keepdims=True) + delta)
        shift = m_new - delta
        p = jnp.exp2(s - shift[:, :1])
        alpha = jnp.exp2(m_prev - m_new)
        pv = jnp.dot(p.astype(BF16), v, preferred_element_type=F32)
        acc_scr[hh, row0:, :] = alpha * acc_scr[hh, row0:, :] + pv
        m_scr[hh, row0:, :] = m_new

    for hh in heads:
        delta = zero
        for sb in range(ratio):
            kb = qi * ratio + sb
            row0 = sb * bk
            s, v = scores(hh, kb, row0)
            row = lax.broadcasted_iota(I32, s.shape, 0)
            col = lax.broadcasted_iota(I32, s.shape, 1)
            s = jnp.where(col <= row, s, -jnp.inf)
            if sb == 0:
                m0 = jnp.max(s, axis=1, keepdims=True)
                p = jnp.exp2(s - m0)
                m_scr[hh] = jnp.broadcast_to(m0, m_scr.shape[1:])
                acc_scr[hh] = jnp.dot(p.astype(BF16), v, preferred_element_type=F32)
            else:
                update(hh, s, v, delta, row0)
            delta = delta - tot_ref[hh, pl.ds(kb, 1), :]

    def body(step, deltas):
        kb = qi * ratio - 1 - step
        new_deltas = []
        for hh in heads:
            delta = deltas[hh] + tot_ref[hh, pl.ds(kb, 1), :]
            s, v = scores(hh, kb, 0)
            update(hh, s, v, delta, 0)
            new_deltas.append(delta)
        return tuple(new_deltas)

    lax.fori_loop(0, qi * ratio, body, tuple(zero for _ in heads))
    for hh in heads:
        acc = acc_scr[hh]
        lane = lax.broadcasted_iota(I32, acc.shape, 1)
        out = jnp.where(lane < HEAD_DIM, acc / acc[:, HEAD_DIM:HEAD_DIM + 1], 0.0)
        o_ref[:, hh * LANES:(hh + 1) * LANES] = out.astype(BF16)


def _attention(q, k, v, tot, bsz, seq):
    t, sw = q.shape
    bq = ATTN_BQ
    nq = seq // bq
    nk = seq // ATTN_BK
    hw = HEADS_PER_STEP * LANES
    tot = tot.reshape(bsz, nk, LANES)[:, :, :N_HEADS].transpose(0, 2, 1)
    tot = jnp.broadcast_to(tot[..., None], (bsz, N_HEADS, nk, LANES))
    k3 = k.reshape(bsz, seq, sw)
    v3 = v.reshape(bsz, seq, sw)
    return pl.pallas_call(
        _attn_kernel,
        grid=(bsz, N_HEADS // HEADS_PER_STEP, nq),
        in_specs=[pl.BlockSpec((bq, hw), lambda b, h, i: (b * nq + i, h)),
                  pl.BlockSpec((None, seq, hw), lambda b, h, i: (b, 0, h), pipeline_mode=pl.Buffered(1)),
                  pl.BlockSpec((None, seq, hw), lambda b, h, i: (b, 0, h), pipeline_mode=pl.Buffered(1)),
                  pl.BlockSpec((None, HEADS_PER_STEP, nk, LANES), lambda b, h, i: (b, h, 0, 0))],
        out_specs=pl.BlockSpec((bq, hw), lambda b, h, i: (b * nq + i, h)),
        out_shape=jax.ShapeDtypeStruct((t, sw), BF16),
        scratch_shapes=[pltpu.VMEM((HEADS_PER_STEP, bq, LANES), F32),
                        pltpu.VMEM((HEADS_PER_STEP, bq, LANES), F32)],
        compiler_params=_cparams("arbitrary", "arbitrary", "arbitrary"),
        name="attn",
    )(q, k3, v3, tot)


def _pack_bf16_pair(lo, hi):
    lo_bits = pltpu.bitcast(lo.astype(BF16).astype(F32), U32)
    hi_bits = pltpu.bitcast(hi.astype(BF16).astype(F32), U32)
    return (hi_bits & jnp.uint32(0xFFFF0000)) | (lo_bits >> 16)


def _unpack_bf16_pair(w):
    lo = pltpu.bitcast(w << 16, F32)
    hi = pltpu.bitcast(w & jnp.uint32(0xFFFF0000), F32)
    return jnp.concatenate([lo, hi], axis=1)


def _merge_kernel(x_ref, act_ref, attn_ref, sgc_ref, sga_ref, mod_ref, g2_ref, wco_ref, wao_ref, wout_ref,
                  wr_ref, br_ref, x1_ref, h2_ref, ids_ref, gates_ref):
    conv_out = jnp.dot(act_ref[...], wco_ref[...], preferred_element_type=F32)
    attn_out = jnp.dot(attn_ref[...], wao_ref[...], preferred_element_type=F32)
    mixed = sgc_ref[...].astype(F32) * conv_out + sga_ref[...].astype(F32) * attn_out
    upd = jnp.dot(mixed.astype(BF16), wout_ref[...], preferred_element_type=F32)
    x1 = x_ref[...] + mod_ref[2:3, :] * upd
    x1_ref[...] = x1
    h2 = _rms(x1) * g2_ref[...] * (1.0 + mod_ref[4:5, :]) + mod_ref[3:4, :]
    half = h2.shape[1] // 2
    h2_ref[...] = _pack_bf16_pair(h2[:, :half], h2[:, half:])

    h2_hi = h2.astype(BF16)
    h2_lo = (h2 - h2_hi.astype(F32)).astype(BF16)
    h2_cat = jnp.concatenate([h2_hi, h2_lo, h2_hi], axis=1)
    logits = jnp.dot(h2_cat, wr_ref[...], preferred_element_type=F32) + br_ref[...]
    lane = lax.broadcasted_iota(I32, logits.shape, 1)
    work = jnp.where(lane < N_EXPERTS, logits, -jnp.inf)
    ids = jnp.zeros(logits.shape, I32)
    vals = []
    for kk in range(TOP_K):
        mx = jnp.max(work, axis=1, keepdims=True)
        idx = jnp.min(jnp.where(work == mx, lane, LANES), axis=1, keepdims=True)
        ids = jnp.where(lane == kk, idx, ids)
        vals.append(mx)
        work = jnp.where(lane == idx, -jnp.inf, work)
    exps = [jnp.exp(vv - vals[0]) for vv in vals]
    den = exps[0] + exps[1] + exps[2] + exps[3]
    gates = jnp.zeros(logits.shape, F32)
    for kk in range(TOP_K):
        gates = jnp.where(lane == kk, exps[kk] / den, gates)
    ids_ref[...] = ids
    gates_ref[...] = gates


def _merge(x2, act, attn, sgc, sga, mod8, norm2_g, w_conv_out, w_attn_out, w_out, w_router, b_router, seq):
    t, d = x2.shape
    tm = MERGE_TILE
    cw = act.shape[1]
    sw = attn.shape[1]
    wco = w_conv_out.astype(BF16)
    wao = jnp.pad(w_attn_out.reshape(N_HEADS, HEAD_DIM, d), ((0, 0), (0, LANES - HEAD_DIM), (0, 0)))
    wao = wao.reshape(sw, d).astype(BF16)
    wout = w_out.astype(BF16)
    wr = jnp.pad(w_router, ((0, 0), (0, LANES - N_EXPERTS)))
    wr_hi = wr.astype(BF16)
    wr_lo = (wr - wr_hi.astype(F32)).astype(BF16)
    wr = jnp.concatenate([wr_hi, wr_hi, wr_lo], axis=0)
    br = jnp.pad(b_router, (0, LANES - N_EXPERTS))[None]
    tiles_per_seq = seq // tm
    row = lambda i: (i, 0)
    return pl.pallas_call(
        _merge_kernel,
        grid=(t // tm,),
        in_specs=[pl.BlockSpec((tm, d), row), pl.BlockSpec((tm, cw), row), pl.BlockSpec((tm, sw), row),
                  pl.BlockSpec((tm, d), row), pl.BlockSpec((tm, d), row),
                  pl.BlockSpec((None, SUBLANES, d), lambda i: (i // tiles_per_seq, 0, 0)),
                  _resident((1, d)), _resident(wco.shape), _resident(wao.shape), _resident(wout.shape),
                  _resident(wr.shape), _resident(br.shape)],
        out_specs=[pl.BlockSpec((tm, d), row), pl.BlockSpec((tm, d // 2), row),
                   pl.BlockSpec((tm, LANES), row), pl.BlockSpec((tm, LANES), row)],
        out_shape=[jax.ShapeDtypeStruct((t, d), F32), jax.ShapeDtypeStruct((t, d // 2), U32),
                   jax.ShapeDtypeStruct((t, LANES), I32), jax.ShapeDtypeStruct((t, LANES), F32)],
        compiler_params=_cparams("arbitrary"),
        name="merge",
    )(x2, act, attn, sgc, sga, mod8, norm2_g[None], wco, wao, wout, wr, br)


def _expert_onehots(ids):
    lane = lax.broadcasted_iota(I32, ids.shape, 1)
    return lane, [lane == ids[:, kk:kk + 1] for kk in range(TOP_K)]


def _route_kernel(ids_ref, tri_ref, rank_ref, cnt_ref, carry_scr):
    @pl.when(pl.program_id(0) == 0)
    def _():
        carry_scr[...] = jnp.zeros(carry_scr.shape, F32)

    lane, hots = _expert_onehots(ids_ref[...])
    multi = jnp.zeros(lane.shape, F32)
    for hot in hots:
        multi = multi + jnp.where(hot, 1.0, 0.0)
    carry = carry_scr[0:1, :]
    before = jnp.dot(tri_ref[...], multi.astype(BF16), preferred_element_type=F32) + carry
    rank = jnp.zeros(lane.shape, F32)
    for kk, hot in enumerate(hots):
        rk = jnp.sum(jnp.where(hot, before, 0.0), axis=1, keepdims=True)
        rank = jnp.where(lane == kk, rk, rank)
    rank_ref[...] = rank.astype(I32)
    carry = carry + jnp.sum(multi, axis=0, keepdims=True)
    carry_scr[...] = jnp.broadcast_to(carry, carry_scr.shape)
    cnt_ref[...] = jnp.broadcast_to(carry, cnt_ref.shape)


def _route(ids):
    t = ids.shape[0]
    tm = ROUTE_TILE
    r = jnp.arange(tm)
    tri = (r[None, :] < r[:, None]).astype(BF16)
    return pl.pallas_call(
        _route_kernel,
        grid=(t // tm,),
        in_specs=[pl.BlockSpec((tm, LANES), lambda i: (i, 0)), _resident(tri.shape)],
        out_specs=[pl.BlockSpec((tm, LANES), lambda i: (i, 0)),
                   pl.BlockSpec((SUBLANES, LANES), lambda i: (0, 0))],
        out_shape=[jax.ShapeDtypeStruct((t, LANES), I32), jax.ShapeDtypeStruct((SUBLANES, LANES), F32)],
        scratch_shapes=[pltpu.VMEM((SUBLANES, LANES), F32)],
        compiler_params=_cparams("arbitrary"),
        name="route",
    )(ids, tri)


def _dest_kernel(ids_ref, rank_ref, cnt_ref, tri_ref, dest_ref, blk_ref):
    cnt = cnt_ref[...]
    padded = jnp.ceil(cnt * (1.0 / EXPERT_BLOCK)) * EXPERT_BLOCK
    pend = jnp.dot(padded, tri_ref[...], preferred_element_type=F32, precision=HIGHEST)
    pstart = (pend - padded)[0:1, :]

    lane, hots = _expert_onehots(ids_ref[...])
    rank = rank_ref[...]
    dest = jnp.zeros(lane.shape, I32)
    for kk, hot in enumerate(hots):
        ps = jnp.sum(jnp.where(hot, pstart, 0.0), axis=1, keepdims=True).astype(I32)
        dest = jnp.where(lane == kk, ps + rank[:, kk:kk + 1], dest)
    dest_ref[...] = dest

    @pl.when(pl.program_id(0) == 0)
    def _():
        shape = blk_ref.shape[1:]
        pos = (lax.broadcasted_iota(I32, shape, 0) * LANES + lax.broadcasted_iota(I32, shape, 1))
        pos = pos.astype(F32) * EXPERT_BLOCK
        be = jnp.zeros(shape, I32)
        for e in range(N_EXPERTS):
            be = be + jnp.where(pos >= pend[0:1, e:e + 1], 1, 0)
        blk_ref[0] = jnp.minimum(be, N_EXPERTS - 1)
        nact = pend[0:1, N_EXPERTS - 1:N_EXPERTS] * (1.0 / EXPERT_BLOCK)
        blk_ref[1] = jnp.broadcast_to(nact, shape).astype(I32)


def _dest(ids, rank, cnt):
    t = ids.shape[0]
    tm = ROUTE_TILE
    r = jnp.arange(LANES)
    tri = (r[:, None] <= r[None, :]).astype(F32)
    tile = pl.BlockSpec((tm, LANES), lambda i: (i, 0))
    return pl.pallas_call(
        _dest_kernel,
        grid=(t // tm,),
        in_specs=[tile, tile, pl.BlockSpec((SUBLANES, LANES), lambda i: (0, 0)),
                  pl.BlockSpec((LANES, LANES), lambda i: (0, 0))],
        out_specs=[tile, pl.BlockSpec((2, SUBLANES, LANES), lambda i: (0, 0, 0))],
        out_shape=[jax.ShapeDtypeStruct((t, LANES), I32), jax.ShapeDtypeStruct((2, SUBLANES, LANES), I32)],
        compiler_params=_cparams("arbitrary"),
        name="dest",
    )(ids, rank, cnt, tri)


def _row_copy(src, dst, sem):
    return pltpu.make_async_copy(src, dst, sem)


def _dispatch_kernel(dest_ref, h2_ref, xs_in_ref, xs_ref, sem):
    del xs_in_ref
    tm = h2_ref.shape[0]

    def issue(r, carry):
        for kk in range(TOP_K):
            d = dest_ref[r * TOP_K + kk]
            _row_copy(h2_ref.at[pl.ds(r, 1)], xs_ref.at[pl.ds(d, 1)], sem).start()
        return carry

    lax.fori_loop(0, tm, issue, 0)
    for kk in range(TOP_K):
        _row_copy(h2_ref, xs_ref.at[pl.ds(0, tm)], sem).wait()


def _dispatch(dest_flat, h2p, n_rows):
    t, w = h2p.shape
    tm = TOKEN_TILE
    xs0 = jnp.zeros((n_rows, w), U32)
    return pl.pallas_call(
        _dispatch_kernel,
        grid=(t // tm,),
        in_specs=[pl.BlockSpec((tm * TOP_K,), lambda i: (i,), memory_space=pltpu.SMEM),
                  pl.BlockSpec((tm, w), lambda i: (i, 0)),
                  pl.BlockSpec(memory_space=pl.ANY)],
        out_specs=pl.BlockSpec(memory_space=pl.ANY),
        out_shape=jax.ShapeDtypeStruct((n_rows, w), U32),
        scratch_shapes=[pltpu.SemaphoreType.DMA(())],
        input_output_aliases={2: 0},
        compiler_params=_cparams("arbitrary"),
        name="dispatch",
    )(dest_flat, h2p, xs0)


def _expert_kernel(be_ref, na_ref, xs_ref, w1_ref, b1_ref, w2_ref, b2_ref, y_ref, w1t_ref, w2t_ref, b1c_ref,
                   *, d_expert):
    i = pl.program_id(0)
    active = i < na_ref[0]
    prev = be_ref[jnp.maximum(i - 1, 0)]
    new_expert = jnp.logical_or(i == 0, be_ref[i] != prev)
    nt = (((1,), (1,)), ((), ()))

    @pl.when(jnp.logical_and(active, new_expert))
    def _():
        w1t_ref[...] = w1_ref[...].T.astype(BF16)
        w2t_ref[...] = w2_ref[...].T.astype(BF16)
        b1c_ref[...] = jnp.broadcast_to(b1_ref[...], (LANES, b1_ref.shape[1])).T

    @pl.when(active)
    def _():
        x = _unpack_bf16_pair(xs_ref[...]).astype(BF16)
        bias = jnp.tile(b1c_ref[...], (1, x.shape[0] // LANES))
        hct = lax.dot_general(w1t_ref[...], x, nt, preferred_element_type=F32) + bias
        g = jnp.minimum(hct[:d_expert, :], SWIGLU_LIMIT)
        u = jnp.clip(hct[d_expert:, :], -SWIGLU_LIMIT, SWIGLU_LIMIT)
        hmt = (u + 1.0) * (g * jax.nn.sigmoid(SWIGLU_ALPHA * g))
        yt = jnp.dot(w2t_ref[...], hmt.astype(BF16), preferred_element_type=F32)
        y_ref[...] = yt.T + b2_ref[...]

    @pl.when(jnp.logical_not(active))
    def _():
        y_ref[...] = jnp.zeros(y_ref.shape, F32)


def _experts(block_e, n_active, xs, w1, b1, w2, b2):
    n_rows, w = xs.shape
    n_exp, d, two_de = w1.shape
    de = two_de // 2
    bm = EXPERT_BLOCK
    nb = n_rows // bm

    def blk(i, be, na):
        return jnp.maximum(jnp.minimum(i, na[0] - 1), 0)

    grid_spec = pltpu.PrefetchScalarGridSpec(
        num_scalar_prefetch=2,
        grid=(nb,),
        in_specs=[pl.BlockSpec((bm, w), lambda i, be, na: (blk(i, be, na), 0)),
                  pl.BlockSpec((None, d, two_de), lambda i, be, na: (be[blk(i, be, na)], 0, 0)),
                  pl.BlockSpec((None, 1, two_de), lambda i, be, na: (be[blk(i, be, na)], 0, 0)),
                  pl.BlockSpec((None, de, d), lambda i, be, na: (be[blk(i, be, na)], 0, 0)),
                  pl.BlockSpec((None, 1, d), lambda i, be, na: (be[blk(i, be, na)], 0, 0))],
        out_specs=pl.BlockSpec((bm, d), lambda i, be, na: (i, 0)),
        scratch_shapes=[pltpu.VMEM((two_de, d), BF16), pltpu.VMEM((d, de), BF16),
                        pltpu.VMEM((two_de, LANES), F32)],
    )
    return pl.pallas_call(
        functools.partial(_expert_kernel, d_expert=de),
        grid_spec=grid_spec,
        out_shape=jax.ShapeDtypeStruct((n_rows, d), F32),
        compiler_params=_cparams("arbitrary"),
        name="expert",
    )(block_e, n_active, xs, w1, b1[:, None, :], w2, b2[:, None, :])


def _combine_kernel(dest_ref, x1_ref, gates_ref, mod_ref, fg_ref, yb_ref, o_ref, gbuf, sem):
    tm = x1_ref.shape[0]

    def issue(r, carry):
        for kk in range(TOP_K):
            d = dest_ref[r * TOP_K + kk]
            _row_copy(yb_ref.at[pl.ds(d, 1)], gbuf.at[kk, pl.ds(r, 1)], sem).start()
        return carry

    lax.fori_loop(0, tm, issue, 0)
    for kk in range(TOP_K):
        _row_copy(yb_ref.at[pl.ds(0, tm)], gbuf.at[kk], sem).wait()

    gates = gates_ref[...]
    y = gates[:, 0:1] * gbuf[0]
    for kk in range(1, TOP_K):
        y = y + gates[:, kk:kk + 1] * gbuf[kk]
    x2 = x1_ref[...] + mod_ref[5:6, :] * y
    o_ref[...] = _rms(x2) * fg_ref[...]


def _combine(dest_flat, x1, gates, mod8, final_g, yb, seq):
    t, d = x1.shape
    tm = TOKEN_TILE
    tiles_per_seq = seq // tm
    return pl.pallas_call(
        _combine_kernel,
        grid=(t // tm,),
        in_specs=[pl.BlockSpec((tm * TOP_K,), lambda i: (i,), memory_space=pltpu.SMEM),
                  pl.BlockSpec((tm, d), lambda i: (i, 0)),
                  pl.BlockSpec((tm, LANES), lambda i: (i, 0)),
                  pl.BlockSpec((None, SUBLANES, d), lambda i: (i // tiles_per_seq, 0, 0)),
                  pl.BlockSpec((1, d), lambda i: (0, 0)),
                  pl.BlockSpec(memory_space=pl.ANY)],
        out_specs=pl.BlockSpec((tm, d), lambda i: (i, 0)),
        out_shape=jax.ShapeDtypeStruct((t, d), F32),
        scratch_shapes=[pltpu.VMEM((TOP_K, tm, d), F32), pltpu.SemaphoreType.DMA(())],
        compiler_params=_cparams("arbitrary"),
        name="combine",
    )(dest_flat, x1, gates, mod8, final_g[None], yb)


def kernel(x, c, w_ada, b_ada, norm1_g, w_in, b_forget, conv_w, conv_b, conv_ln_g, conv_ln_b, w_conv_out,
           w_attn_out, w_out, norm2_g, w_router, b_router, w_exp_in, b_exp_in, w_exp_out, b_exp_out, final_g):
    bsz, seq, d = x.shape
    t = bsz * seq
    depth = w_ada.shape[0]
    xf = x.reshape(t, d)
    n_rows = t * TOP_K + N_EXPERTS * EXPERT_BLOCK
    n_blocks = n_rows // EXPERT_BLOCK
    for l in range(depth):
        mod8 = _mod(c, w_ada[l], b_ada[l])
        a, q, k, v, tot, sgc, sga = _inproj(xf, mod8, norm1_g[l], w_in[l], b_forget[l], seq)
        act = _conv(a, conv_w[l], conv_b[l], conv_ln_g[l], conv_ln_b[l], bsz, seq)
        attn = _attention(q, k, v, tot, bsz, seq)
        x1, h2p, ids, gates = _merge(xf, act, attn, sgc, sga, mod8, norm2_g[l], w_conv_out[l], w_attn_out[l],
                                     w_out[l], w_router[l], b_router[l], seq)
        rank, cnt = _route(ids)
        dest, blk = _dest(ids, rank, cnt)
        dest_flat = dest[:, :TOP_K].reshape(t * TOP_K)
        block_e = blk[0].reshape(-1)[:n_blocks]
        n_active = blk[1, 0, :1]
        xs = _dispatch(dest_flat, h2p, n_rows)
        yb = _experts(block_e, n_active, xs, w_exp_in[l], b_exp_in[l], w_exp_out[l], b_exp_out[l])
        assert depth == 1
        xf = _combine(dest_flat, x1, gates, mod8, final_g, yb, seq)
    return xf.reshape(bsz, seq, d)
```

```python
import functools

import jax
import jax.numpy as jnp
from jax import lax
from jax.experimental import pallas as pl
from jax.experimental.pallas import tpu as pltpu

F32 = jnp.float32
BF16 = jnp.bfloat16
I32 = jnp.int32
U32 = jnp.uint32
HIGHEST = lax.Precision.HIGHEST

LANES = 128
SUBLANES = 8
VMEM_LIMIT_BYTES = 56 * 1024 * 1024

N_HEADS = 8
HEAD_DIM = 64
CONV_KERNEL = 31
CONV_HALO = 32
N_EXPERTS = 32
TOP_K = 4
SWIGLU_LIMIT = 7.0
SWIGLU_ALPHA = 1.702
RMS_EPS = 1e-5
LN_EPS = 1e-5
N_MOD = 6
N_SPLIT = 3
LOG2_E = 1.4426950408889634
HEADS_PER_STEP = 2

INPROJ_TILE = 512
ATTN_BQ = 2048
ATTN_BK = 512
MERGE_TILE = 512
CONV_TILE = 256
ROUTE_TILE = 512
TOKEN_TILE = 256
EXPERT_BLOCK = 512


def _cparams(*sem):
    return pltpu.CompilerParams(dimension_semantics=sem, vmem_limit_bytes=VMEM_LIMIT_BYTES)


def _resident(shape):
    nd = len(shape)
    return pl.BlockSpec(shape, lambda *_: (0,) * nd, pipeline_mode=pl.Buffered(1))


def _split3(x):
    p1 = x.astype(BF16)
    r1 = x - p1.astype(F32)
    p2 = r1.astype(BF16)
    p3 = (r1 - p2.astype(F32)).astype(BF16)
    return jnp.concatenate([p1, p2, p3], axis=1)


def _rms(x):
    return x * lax.rsqrt(jnp.mean(x * x, axis=-1, keepdims=True) + RMS_EPS)


def _mod_kernel(c_ref, w_ref, b_ref, o_ref):
    c = c_ref[...]
    sc = c * jax.nn.sigmoid(c)
    o_ref[...] = jnp.dot(sc, w_ref[...], preferred_element_type=F32, precision=HIGHEST) + b_ref[...]


def _mod(c, w_ada, b_ada):
    bsz, d = c.shape
    n = w_ada.shape[1]
    tn = 1024
    cp = jnp.zeros((SUBLANES, d), F32).at[:bsz].set(c)
    out = pl.pallas_call(
        _mod_kernel,
        grid=(n // tn,),
        in_specs=[pl.BlockSpec((SUBLANES, d), lambda j: (0, 0)),
                  pl.BlockSpec((d, tn), lambda j: (0, j)),
                  pl.BlockSpec((1, tn), lambda j: (0, j))],
        out_specs=pl.BlockSpec((SUBLANES, tn), lambda j: (0, j)),
        out_shape=jax.ShapeDtypeStruct((SUBLANES, n), F32),
        compiler_params=_cparams("arbitrary"),
        name="mod",
    )(cp, w_ada, b_ada[None])
    mod = out[:bsz].reshape(bsz, N_MOD, d)
    return jnp.pad(mod, ((0, 0), (0, SUBLANES - N_MOD), (0, 0)))


def _inproj_kernel(x_ref, mod_ref, g_ref, wa_ref, wq_ref, wk_ref, wv_ref, wf_ref, wg_ref, bf_ref,
                   tri_ref, eq_ref, ek_ref, oneq_ref, onek_ref, onev_ref,
                   a_ref, q_ref, k_ref, v_ref, tot_ref, sgc_ref, sga_ref, carry_scr, *, conv_width, d_model):
    x = x_ref[...]
    h = _rms(x) * g_ref[...] * (1.0 + mod_ref[1:2, :]) + mod_ref[0:1, :]
    hb = h.astype(BF16)

    u = jnp.dot(hb, wa_ref[...], preferred_element_type=F32)
    a_ref[...] = (u[:, :conv_width] * jax.nn.sigmoid(u[:, conv_width:])).astype(BF16)

    fl = jnp.dot(hb, wf_ref[...], preferred_element_type=F32) + bf_ref[...]
    lane = lax.broadcasted_iota(I32, fl.shape, 1)
    log_f = -(jnp.maximum(-fl, 0.0) + jnp.log1p(jnp.exp(-jnp.abs(fl))))
    log_f = jnp.where(lane < N_HEADS, log_f, 0.0)
    cs = jnp.dot(tri_ref[...], _split3(log_f), preferred_element_type=F32)
    cum = (cs[:, :LANES] + cs[:, LANES:2 * LANES] + cs[:, 2 * LANES:]) * LOG2_E
    tm = cum.shape[0]

    pieces = []
    for sb in range(tm // ATTN_BK):
        blk = cum[sb * ATTN_BK:(sb + 1) * ATTN_BK, :]
        if sb > 0:
            blk = blk - cum[sb * ATTN_BK - 1:sb * ATTN_BK, :]
        tot_ref[sb] = blk[ATTN_BK - 1:, :]
        pieces.append(blk)
    cum_k = pieces[0] if len(pieces) == 1 else jnp.concatenate(pieces, axis=0)

    @pl.when(pl.program_id(0) % (ATTN_BQ // tm) == 0)
    def _():
        carry_scr[...] = jnp.zeros(carry_scr.shape, F32)

    cum_q = cum + carry_scr[0:1, :]
    carry_scr[...] = jnp.broadcast_to(cum_q[tm - 1:, :], carry_scr.shape)

    q = jnp.dot(hb, wq_ref[...], preferred_element_type=F32) * (HEAD_DIM ** -0.5 * LOG2_E)
    q = q + jnp.dot(_split3(cum_q), eq_ref[...], preferred_element_type=F32) + oneq_ref[...]
    q_ref[...] = q.astype(BF16)
    k = jnp.dot(hb, wk_ref[...], preferred_element_type=F32)
    k = k + jnp.dot(_split3(cum_k), ek_ref[...], preferred_element_type=F32) + onek_ref[...]
    k_ref[...] = k.astype(BF16)
    v_ref[...] = (jnp.dot(hb, wv_ref[...], preferred_element_type=F32) + onev_ref[...]).astype(BF16)

    gts = jnp.dot(hb, wg_ref[...], preferred_element_type=F32)
    sgc_ref[...] = jax.nn.sigmoid(gts[:, :d_model]).astype(BF16)
    sga_ref[...] = jax.nn.sigmoid(gts[:, d_model:]).astype(BF16)


def _spread_heads(w):
    d = w.shape[0]
    w = w.reshape(d, N_HEADS, HEAD_DIM)
    w = jnp.pad(w, ((0, 0), (0, 0), (0, LANES - HEAD_DIM)))
    return w.reshape(d, N_HEADS * LANES)


def _aug_constants():
    rows = jnp.arange(N_SPLIT * LANES)
    piece, head = rows // LANES, rows % LANES
    cols = jnp.arange(N_HEADS * LANES)
    chead, coff = cols // LANES, cols % LANES - HEAD_DIM
    valid = (head[:, None] < N_HEADS) & (head[:, None] == chead[None, :])
    eq = jnp.where(valid & (coff[None, :] == piece[:, None]), 1.0, 0.0).astype(BF16)
    ek = jnp.where(valid & (coff[None, :] == piece[:, None] + N_SPLIT), -1.0, 0.0).astype(BF16)
    oneq = jnp.where((coff >= N_SPLIT) & (coff < 2 * N_SPLIT), 1.0, 0.0).astype(F32)[None]
    onek = jnp.where((coff >= 0) & (coff < N_SPLIT), 1.0, 0.0).astype(F32)[None]
    onev = jnp.where(coff == 0, 1.0, 0.0).astype(F32)[None]
    return eq, ek, oneq, onek, onev


def _inproj(x2, mod8, norm_g, w_in, b_forget, seq):
    t, d = x2.shape
    tm = INPROJ_TILE
    nsub = tm // ATTN_BK
    cw = d // 2
    aw = N_HEADS * HEAD_DIM
    o0 = 2 * cw
    wa = w_in[:, :o0].astype(BF16)
    wq = _spread_heads(w_in[:, o0:o0 + aw]).astype(BF16)
    wk = _spread_heads(w_in[:, o0 + aw:o0 + 2 * aw]).astype(BF16)
    wv = _spread_heads(w_in[:, o0 + 2 * aw:o0 + 3 * aw]).astype(BF16)
    o1 = o0 + 3 * aw
    wf = jnp.pad(w_in[:, o1:o1 + N_HEADS], ((0, 0), (0, LANES - N_HEADS))).astype(BF16)
    bfp = jnp.pad(b_forget, (0, LANES - N_HEADS))[None]
    wg = w_in[:, o1 + N_HEADS:].astype(BF16)
    r = jnp.arange(tm)
    tri = (r[None, :] <= r[:, None]).astype(BF16)
    eq, ek, oneq, onek, onev = _aug_constants()
    sw = N_HEADS * LANES
    tiles_per_seq = seq // tm
    row = lambda i: (i, 0)
    outs = pl.pallas_call(
        functools.partial(_inproj_kernel, conv_width=cw, d_model=d),
        grid=(t // tm,),
        in_specs=[pl.BlockSpec((tm, d), row),
                  pl.BlockSpec((None, SUBLANES, d), lambda i: (i // tiles_per_seq, 0, 0)),
                  _resident((1, d)),
                  _resident(wa.shape), _resident(wq.shape), _resident(wk.shape), _resident(wv.shape),
                  _resident(wf.shape), _resident(wg.shape), _resident(bfp.shape),
                  _resident(tri.shape), _resident(eq.shape), _resident(ek.shape),
                  _resident(oneq.shape), _resident(onek.shape), _resident(onev.shape)],
        out_specs=[pl.BlockSpec((tm, cw), row),
                   pl.BlockSpec((tm, sw), row), pl.BlockSpec((tm, sw), row), pl.BlockSpec((tm, sw), row),
                   pl.BlockSpec((nsub, 1, LANES), lambda i: (i, 0, 0)),
                   pl.BlockSpec((tm, d), row), pl.BlockSpec((tm, d), row)],
        out_shape=[jax.ShapeDtypeStruct((t, cw), BF16),
                   jax.ShapeDtypeStruct((t, sw), BF16), jax.ShapeDtypeStruct((t, sw), BF16),
                   jax.ShapeDtypeStruct((t, sw), BF16),
                   jax.ShapeDtypeStruct((t // ATTN_BK, 1, LANES), F32),
                   jax.ShapeDtypeStruct((t, d), BF16), jax.ShapeDtypeStruct((t, d), BF16)],
        scratch_shapes=[pltpu.VMEM((SUBLANES, LANES), F32)],
        compiler_params=_cparams("arbitrary"),
        name="inproj",
    )(x2, mod8, norm_g[None], wa, wq, wk, wv, wf, wg, bfp, tri, eq, ek, oneq, onek, onev)
    return outs


def _conv_kernel(halo_ref, cur_ref, w_ref, b_ref, g_ref, beta_ref, o_ref, xs_ref, ys_ref):
    ts = cur_ref.shape[0]
    first = pl.program_id(1) == 0
    halo = halo_ref[...].astype(F32)
    xs_ref[0:CONV_HALO, :] = jnp.where(first, 0.0, halo)
    xs_ref[CONV_HALO:, :] = cur_ref[...].astype(F32)
    acc = jnp.zeros(o_ref.shape, F32) + b_ref[...]
    base = CONV_HALO - (CONV_KERNEL - 1)
    for rho in range(SUBLANES):
        offs = [o for o in range(base, base + CONV_KERNEL) if o % SUBLANES == rho]
        if not offs:
            continue
        rows = offs[-1] + ts - rho
        ys_ref[0:rows, :] = xs_ref[rho:rho + rows, :]
        for o in offs:
            lo = o - rho
            acc = acc + w_ref[o - base:o - base + 1, :] * ys_ref[lo:lo + ts, :]
    mu = jnp.mean(acc, axis=-1, keepdims=True)
    cen = acc - mu
    var = jnp.mean(cen * cen, axis=-1, keepdims=True)
    y = cen * lax.rsqrt(var + LN_EPS) * g_ref[...] + beta_ref[...]
    o_ref[...] = (y * jax.nn.sigmoid(y)).astype(BF16)


def _conv(a, conv_w, conv_b, ln_g, ln_b, bsz, seq):
    t, cw = a.shape
    ts = CONV_TILE
    nt = seq // ts
    hpt = ts // CONV_HALO
    vec = lambda: pl.BlockSpec((1, cw), lambda b, i: (0, 0))
    return pl.pallas_call(
        _conv_kernel,
        grid=(bsz, nt),
        in_specs=[pl.BlockSpec((CONV_HALO, cw), lambda b, i: (jnp.maximum((b * nt + i) * hpt - 1, 0), 0)),
                  pl.BlockSpec((ts, cw), lambda b, i: (b * nt + i, 0)),
                  pl.BlockSpec((CONV_KERNEL, cw), lambda b, i: (0, 0)),
                  vec(), vec(), vec()],
        out_specs=pl.BlockSpec((ts, cw), lambda b, i: (b * nt + i, 0)),
        out_shape=jax.ShapeDtypeStruct((t, cw), BF16),
        scratch_shapes=[pltpu.VMEM((CONV_HALO + ts, cw), F32), pltpu.VMEM((CONV_HALO + ts, cw), F32)],
        compiler_params=_cparams("arbitrary", "arbitrary"),
        name="conv",
    )(a, a, conv_w, conv_b[None], ln_g[None], ln_b[None])


def _attn_kernel(q_ref, k_ref, v_ref, tot_ref, o_ref, m_scr, acc_scr):
    bq = q_ref.shape[0]
    bk = ATTN_BK
    ratio = bq // bk
    qi = pl.program_id(2)
    nt = (((1,), (1,)), ((), ()))
    heads = range(HEADS_PER_STEP)
    zero = jnp.zeros((1, LANES), F32)

    def scores(hh, kb, row0):
        lanes = slice(hh * LANES, (hh + 1) * LANES)
        start = pl.multiple_of(kb * bk, bk)
        s = lax.dot_general(q_ref[row0:, lanes], k_ref[pl.ds(start, bk), lanes], nt, preferred_element_type=F32)
        return s, v_ref[pl.ds(start, bk), lanes]

    def update(hh, s, v, delta, row0):
        m_prev = m_scr[hh, row0:, :]
        m_new = jnp.maximum(m_prev, jnp.max(s, axis=1, keepdims=True) + delta)
        shift = m_new - delta
        p = jnp.exp2(s - shift[:, :1])
        alpha = jnp.exp2(m_prev - m_new)
        pv = jnp.dot(p.astype(BF16), v, preferred_element_type=F32)
        acc_scr[hh, row0:, :] = alpha * acc_scr[hh, row0:, :] + pv
        m_scr[hh, row0:, :] = m_new

    for hh in heads:
        delta = zero
        for sb in range(ratio):
            kb = qi * ratio + sb
            row0 = sb * bk
            s, v = scores(hh, kb, row0)
            row = lax.broadcasted_iota(I32, s.shape, 0)
            col = lax.broadcasted_iota(I32, s.shape, 1)
            s = jnp.where(col <= row, s, -jnp.inf)
            if sb == 0:
                m0 = jnp.max(s, axis=1, keepdims=True)
                p = jnp.exp2(s - m0)
                m_scr[hh] = jnp.broadcast_to(m0, m_scr.shape[1:])
                acc_scr[hh] = jnp.dot(p.astype(BF16), v, preferred_element_type=F32)
            else:
                update(hh, s, v, delta, row0)
            delta = delta - tot_ref[hh, pl.ds(kb, 1), :]

    def body(step, deltas):
        kb = qi * ratio - 1 - step
        new_deltas = []
        for hh in heads:
            delta = deltas[hh] + tot_ref[hh, pl.ds(kb, 1), :]
            s, v = scores(hh, kb, 0)
            update(hh, s, v, delta, 0)
            new_deltas.append(delta)
        return tuple(new_deltas)

    lax.fori_loop(0, qi * ratio, body, tuple(zero for _ in heads))
    for hh in heads:
        acc = acc_scr[hh]
        lane = lax.broadcasted_iota(I32, acc.shape, 1)
        out = jnp.where(lane < HEAD_DIM, acc / acc[:, HEAD_DIM:HEAD_DIM + 1], 0.0)
        o_ref[:, hh * LANES:(hh + 1) * LANES] = out.astype(BF16)


def _attention(q, k, v, tot, bsz, seq):
    t, sw = q.shape
    bq = ATTN_BQ
    nq = seq // bq
    nk = seq // ATTN_BK
    hw = HEADS_PER_STEP * LANES
    tot = tot.reshape(bsz, nk, LANES)[:, :, :N_HEADS].transpose(0, 2, 1)
    tot = jnp.broadcast_to(tot[..., None], (bsz, N_HEADS, nk, LANES))
    k3 = k.reshape(bsz, seq, sw)
    v3 = v.reshape(bsz, seq, sw)
    return pl.pallas_call(
        _attn_kernel,
        grid=(bsz, N_HEADS // HEADS_PER_STEP, nq),
        in_specs=[pl.BlockSpec((bq, hw), lambda b, h, i: (b * nq + i, h)),
                  pl.BlockSpec((None, seq, hw), lambda b, h, i: (b, 0, h), pipeline_mode=pl.Buffered(1)),
                  pl.BlockSpec((None, seq, hw), lambda b, h, i: (b, 0, h), pipeline_mode=pl.Buffered(1)),
                  pl.BlockSpec((None, HEADS_PER_STEP, nk, LANES), lambda b, h, i: (b, h, 0, 0))],
        out_specs=pl.BlockSpec((bq, hw), lambda b, h, i: (b * nq + i, h)),
        out_shape=jax.ShapeDtypeStruct((t, sw), BF16),
        scratch_shapes=[pltpu.VMEM((HEADS_PER_STEP, bq, LANES), F32),
                        pltpu.VMEM((HEADS_PER_STEP, bq, LANES), F32)],
        compiler_params=_cparams("arbitrary", "arbitrary", "arbitrary"),
        name="attn",
    )(q, k3, v3, tot)


def _pack_bf16_pair(lo, hi):
    lo_bits = pltpu.bitcast(lo.astype(BF16).astype(F32), U32)
    hi_bits = pltpu.bitcast(hi.astype(BF16).astype(F32), U32)
    return (hi_bits & jnp.uint32(0xFFFF0000)) | (lo_bits >> 16)


def _unpack_bf16_pair(w):
    lo = pltpu.bitcast(w << 16, F32)
    hi = pltpu.bitcast(w & jnp.uint32(0xFFFF0000), F32)
    return jnp.concatenate([lo, hi], axis=1)


def _merge_kernel(x_ref, act_ref, attn_ref, sgc_ref, sga_ref, mod_ref, g2_ref, wco_ref, wao_ref, wout_ref,
                  wr_ref, br_ref, x1_ref, h2_ref, ids_ref, gates_ref):
    conv_out = jnp.dot(act_ref[...], wco_ref[...], preferred_element_type=F32)
    attn_out = jnp.dot(attn_ref[...], wao_ref[...], preferred_element_type=F32)
    mixed = sgc_ref[...].astype(F32) * conv_out + sga_ref[...].astype(F32) * attn_out
    upd = jnp.dot(mixed.astype(BF16), wout_ref[...], preferred_element_type=F32)
    x1 = x_ref[...] + mod_ref[2:3, :] * upd
    x1_ref[...] = x1
    h2 = _rms(x1) * g2_ref[...] * (1.0 + mod_ref[4:5, :]) + mod_ref[3:4, :]
    half = h2.shape[1] // 2
    h2_ref[...] = _pack_bf16_pair(h2[:, :half], h2[:, half:])

    h2_hi = h2.astype(BF16)
    h2_lo = (h2 - h2_hi.astype(F32)).astype(BF16)
    h2_cat = jnp.concatenate([h2_hi, h2_lo, h2_hi], axis=1)
    logits = jnp.dot(h2_cat, wr_ref[...], preferred_element_type=F32) + br_ref[...]
    lane = lax.broadcasted_iota(I32, logits.shape, 1)
    work = jnp.where(lane < N_EXPERTS, logits, -jnp.inf)
    ids = jnp.zeros(logits.shape, I32)
    vals = []
    for kk in range(TOP_K):
        mx = jnp.max(work, axis=1, keepdims=True)
        idx = jnp.min(jnp.where(work == mx, lane, LANES), axis=1, keepdims=True)
        ids = jnp.where(lane == kk, idx, ids)
        vals.append(mx)
        work = jnp.where(lane == idx, -jnp.inf, work)
    exps = [jnp.exp(vv - vals[0]) for vv in vals]
    den = exps[0] + exps[1] + exps[2] + exps[3]
    gates = jnp.zeros(logits.shape, F32)
    for kk in range(TOP_K):
        gates = jnp.where(lane == kk, exps[kk] / den, gates)
    ids_ref[...] = ids
    gates_ref[...] = gates


def _merge(x2, act, attn, sgc, sga, mod8, norm2_g, w_conv_out, w_attn_out, w_out, w_router, b_router, seq):
    t, d = x2.shape
    tm = MERGE_TILE
    cw = act.shape[1]
    sw = attn.shape[1]
    wco = w_conv_out.astype(BF16)
    wao = jnp.pad(w_attn_out.reshape(N_HEADS, HEAD_DIM, d), ((0, 0), (0, LANES - HEAD_DIM), (0, 0)))
    wao = wao.reshape(sw, d).astype(BF16)
    wout = w_out.astype(BF16)
    wr = jnp.pad(w_router, ((0, 0), (0, LANES - N_EXPERTS)))
    wr_hi = wr.astype(BF16)
    wr_lo = (wr - wr_hi.astype(F32)).astype(BF16)
    wr = jnp.concatenate([wr_hi, wr_hi, wr_lo], axis=0)
    br = jnp.pad(b_router, (0, LANES - N_EXPERTS))[None]
    tiles_per_seq = seq // tm
    row = lambda i: (i, 0)
    return pl.pallas_call(
        _merge_kernel,
        grid=(t // tm,),
        in_specs=[pl.BlockSpec((tm, d), row), pl.BlockSpec((tm, cw), row), pl.BlockSpec((tm, sw), row),
                  pl.BlockSpec((tm, d), row), pl.BlockSpec((tm, d), row),
                  pl.BlockSpec((None, SUBLANES, d), lambda i: (i // tiles_per_seq, 0, 0)),
                  _resident((1, d)), _resident(wco.shape), _resident(wao.shape), _resident(wout.shape),
                  _resident(wr.shape), _resident(br.shape)],
        out_specs=[pl.BlockSpec((tm, d), row), pl.BlockSpec((tm, d // 2), row),
                   pl.BlockSpec((tm, LANES), row), pl.BlockSpec((tm, LANES), row)],
        out_shape=[jax.ShapeDtypeStruct((t, d), F32), jax.ShapeDtypeStruct((t, d // 2), U32),
                   jax.ShapeDtypeStruct((t, LANES), I32), jax.ShapeDtypeStruct((t, LANES), F32)],
        compiler_params=_cparams("arbitrary"),
        name="merge",
    )(x2, act, attn, sgc, sga, mod8, norm2_g[None], wco, wao, wout, wr, br)


def _expert_onehots(ids):
    lane = lax.broadcasted_iota(I32, ids.shape, 1)
    return lane, [lane == ids[:, kk:kk + 1] for kk in range(TOP_K)]


def _route_kernel(ids_ref, tri_ref, rank_ref, cnt_ref, carry_scr):
    @pl.when(pl.program_id(0) == 0)
    def _():
        carry_scr[...] = jnp.zeros(carry_scr.shape, F32)

    lane, hots = _expert_onehots(ids_ref[...])
    multi = jnp.zeros(lane.shape, F32)
    for hot in hots:
        multi = multi + jnp.where(hot, 1.0, 0.0)
    carry = carry_scr[0:1, :]
    before = jnp.dot(tri_ref[...], multi.astype(BF16), preferred_element_type=F32) + carry
    rank = jnp.zeros(lane.shape, F32)
    for kk, hot in enumerate(hots):
        rk = jnp.sum(jnp.where(hot, before, 0.0), axis=1, keepdims=True)
        rank = jnp.where(lane == kk, rk, rank)
    rank_ref[...] = rank.astype(I32)
    carry = carry + jnp.sum(multi, axis=0, keepdims=True)
    carry_scr[...] = jnp.broadcast_to(carry, carry_scr.shape)
    cnt_ref[...] = jnp.broadcast_to(carry, cnt_ref.shape)


def _route(ids):
    t = ids.shape[0]
    tm = ROUTE_TILE
    r = jnp.arange(tm)
    tri = (r[None, :] < r[:, None]).astype(BF16)
    return pl.pallas_call(
        _route_kernel,
        grid=(t // tm,),
        in_specs=[pl.BlockSpec((tm, LANES), lambda i: (i, 0)), _resident(tri.shape)],
        out_specs=[pl.BlockSpec((tm, LANES), lambda i: (i, 0)),
                   pl.BlockSpec((SUBLANES, LANES), lambda i: (0, 0))],
        out_shape=[jax.ShapeDtypeStruct((t, LANES), I32), jax.ShapeDtypeStruct((SUBLANES, LANES), F32)],
        scratch_shapes=[pltpu.VMEM((SUBLANES, LANES), F32)],
        compiler_params=_cparams("arbitrary"),
        name="route",
    )(ids, tri)


def _dest_kernel(ids_ref, rank_ref, cnt_ref, tri_ref, dest_ref, blk_ref):
    cnt = cnt_ref[...]
    padded = jnp.ceil(cnt * (1.0 / EXPERT_BLOCK)) * EXPERT_BLOCK
    pend = jnp.dot(padded, tri_ref[...], preferred_element_type=F32, precision=HIGHEST)
    pstart = (pend - padded)[0:1, :]

    lane, hots = _expert_onehots(ids_ref[...])
    rank = rank_ref[...]
    dest = jnp.zeros(lane.shape, I32)
    for kk, hot in enumerate(hots):
        ps = jnp.sum(jnp.where(hot, pstart, 0.0), axis=1, keepdims=True).astype(I32)
        dest = jnp.where(lane == kk, ps + rank[:, kk:kk + 1], dest)
    dest_ref[...] = dest

    @pl.when(pl.program_id(0) == 0)
    def _():
        shape = blk_ref.shape[1:]
        pos = (lax.broadcasted_iota(I32, shape, 0) * LANES + lax.broadcasted_iota(I32, shape, 1))
        pos = pos.astype(F32) * EXPERT_BLOCK
        be = jnp.zeros(shape, I32)
        for e in range(N_EXPERTS):
            be = be + jnp.where(pos >= pend[0:1, e:e + 1], 1, 0)
        blk_ref[0] = jnp.minimum(be, N_EXPERTS - 1)
        nact = pend[0:1, N_EXPERTS - 1:N_EXPERTS] * (1.0 / EXPERT_BLOCK)
        blk_ref[1] = jnp.broadcast_to(nact, shape).astype(I32)
        blk_ref[2] = jnp.broadcast_to(pend[0:1, :], shape).astype(I32)


def _dest(ids, rank, cnt):
    t = ids.shape[0]
    tm = ROUTE_TILE
    r = jnp.arange(LANES)
    tri = (r[:, None] <= r[None, :]).astype(F32)
    tile = pl.BlockSpec((tm, LANES), lambda i: (i, 0))
    return pl.pallas_call(
        _dest_kernel,
        grid=(t // tm,),
        in_specs=[tile, tile, pl.BlockSpec((SUBLANES, LANES), lambda i: (0, 0)),
                  pl.BlockSpec((LANES, LANES), lambda i: (0, 0))],
        out_specs=[tile, pl.BlockSpec((3, SUBLANES, LANES), lambda i: (0, 0, 0))],
        out_shape=[jax.ShapeDtypeStruct((t, LANES), I32), jax.ShapeDtypeStruct((3, SUBLANES, LANES), I32)],
        compiler_params=_cparams("arbitrary"),
        name="dest",
    )(ids, rank, cnt, tri)


def _row_copy(src, dst, sem):
    return pltpu.make_async_copy(src, dst, sem)


def _dispatch_kernel(pend_ref, na_ref, dest_ref, h2_ref, xs_ref, zero_ref, sem):
    tm = h2_ref.shape[0]
    bm = zero_ref.shape[0]
    nb = xs_ref.shape[0] // bm

    @pl.when(pl.program_id(0) == 0)
    def _():
        zero_ref[...] = jnp.zeros(zero_ref.shape, zero_ref.dtype)

        def zero_block(start):
            return _row_copy(zero_ref, xs_ref.at[pl.ds(pl.multiple_of(start, bm), bm)], sem)

        def nonempty(e):
            return pend_ref[e] > (pend_ref[e - 1] if e > 0 else 0)

        for e in range(N_EXPERTS):
            @pl.when(nonempty(e))
            def _():
                zero_block(pend_ref[e] - bm).start()

        def tail(b, carry):
            zero_block(b * bm).start()
            return carry

        lax.fori_loop(na_ref[0], nb, tail, 0)
        for e in range(N_EXPERTS):
            @pl.when(nonempty(e))
            def _():
                zero_block(0).wait()

        def tail_wait(b, carry):
            zero_block(0).wait()
            return carry

        lax.fori_loop(na_ref[0], nb, tail_wait, 0)

    def issue(r, carry):
        for kk in range(TOP_K):
            d = dest_ref[r * TOP_K + kk]
            _row_copy(h2_ref.at[pl.ds(r, 1)], xs_ref.at[pl.ds(d, 1)], sem).start(priority=kk % 2)
        return carry

    lax.fori_loop(0, tm, issue, 0)
    for kk in range(TOP_K):
        _row_copy(h2_ref, xs_ref.at[pl.ds(0, tm)], sem).wait()


def _dispatch(pend, n_active, dest_flat, h2p, n_rows):
    t, w = h2p.shape
    tm = TOKEN_TILE
    grid_spec = pltpu.PrefetchScalarGridSpec(
        num_scalar_prefetch=2,
        grid=(t // tm,),
        in_specs=[pl.BlockSpec((tm * TOP_K,), lambda i, pe, na: (i,), memory_space=pltpu.SMEM),
                  pl.BlockSpec((tm, w), lambda i, pe, na: (i, 0))],
        out_specs=pl.BlockSpec(memory_space=pl.ANY),
        scratch_shapes=[pltpu.VMEM((EXPERT_BLOCK, w), U32), pltpu.SemaphoreType.DMA(())],
    )
    return pl.pallas_call(
        _dispatch_kernel,
        grid_spec=grid_spec,
        out_shape=jax.ShapeDtypeStruct((n_rows, w), U32),
        compiler_params=_cparams("arbitrary"),
        name="dispatch",
    )(pend, n_active, dest_flat, h2p)


def _expert_kernel(be_ref, na_ref, xs_ref, w1_ref, b1_ref, w2_ref, b2_ref, y_ref, w1b_ref, w2b_ref, *, d_expert):
    i = pl.program_id(0)
    active = i < na_ref[0]
    prev = be_ref[jnp.maximum(i - 1, 0)]
    new_expert = jnp.logical_or(i == 0, be_ref[i] != prev)

    @pl.when(jnp.logical_and(active, new_expert))
    def _():
        w1b_ref[...] = w1_ref[...].astype(BF16)
        w2b_ref[...] = w2_ref[...].astype(BF16)

    @pl.when(active)
    def _():
        x = _unpack_bf16_pair(xs_ref[...]).astype(BF16)
        hc = jnp.dot(x, w1b_ref[...], preferred_element_type=F32) + b1_ref[...]
        g = jnp.minimum(hc[:, :d_expert], SWIGLU_LIMIT)
        u = jnp.clip(hc[:, d_expert:], -SWIGLU_LIMIT, SWIGLU_LIMIT)
        hm = (u + 1.0) * (g * jax.nn.sigmoid(SWIGLU_ALPHA * g))
        y_ref[...] = jnp.dot(hm.astype(BF16), w2b_ref[...], preferred_element_type=F32) + b2_ref[...]

    @pl.when(jnp.logical_not(active))
    def _():
        y_ref[...] = jnp.zeros(y_ref.shape, F32)


def _experts(block_e, n_active, xs, w1, b1, w2, b2):
    n_rows, w = xs.shape
    n_exp, d, two_de = w1.shape
    de = two_de // 2
    bm = EXPERT_BLOCK
    nb = n_rows // bm

    def blk(i, be, na):
        return jnp.maximum(jnp.minimum(i, na[0] - 1), 0)

    grid_spec = pltpu.PrefetchScalarGridSpec(
        num_scalar_prefetch=2,
        grid=(nb,),
        in_specs=[pl.BlockSpec((bm, w), lambda i, be, na: (blk(i, be, na), 0)),
                  pl.BlockSpec((None, d, two_de), lambda i, be, na: (be[blk(i, be, na)], 0, 0)),
                  pl.BlockSpec((None, 1, two_de), lambda i, be, na: (be[blk(i, be, na)], 0, 0)),
                  pl.BlockSpec((None, de, d), lambda i, be, na: (be[blk(i, be, na)], 0, 0)),
                  pl.BlockSpec((None, 1, d), lambda i, be, na: (be[blk(i, be, na)], 0, 0))],
        out_specs=pl.BlockSpec((bm, d), lambda i, be, na: (i, 0)),
        scratch_shapes=[pltpu.VMEM((d, two_de), BF16), pltpu.VMEM((de, d), BF16)],
    )
    return pl.pallas_call(
        functools.partial(_expert_kernel, d_expert=de),
        grid_spec=grid_spec,
        out_shape=jax.ShapeDtypeStruct((n_rows, d), F32),
        compiler_params=_cparams("arbitrary"),
        name="expert",
    )(block_e, n_active, xs, w1, b1[:, None, :], w2, b2[:, None, :])


def _combine_kernel(dest_ref, x1_ref, gates_ref, mod_ref, fg_ref, yb_ref, o_ref, gbuf, sem):
    tm = x1_ref.shape[0]

    def issue(r, carry):
        for kk in range(TOP_K):
            d = dest_ref[r * TOP_K + kk]
            _row_copy(yb_ref.at[pl.ds(d, 1)], gbuf.at[kk, pl.ds(r, 1)], sem).start(priority=kk % 2)
        return carry

    lax.fori_loop(0, tm, issue, 0)
    for kk in range(TOP_K):
        _row_copy(yb_ref.at[pl.ds(0, tm)], gbuf.at[kk], sem).wait()

    gates = gates_ref[...]
    y = gates[:, 0:1] * gbuf[0]
    for kk in range(1, TOP_K):
        y = y + gates[:, kk:kk + 1] * gbuf[kk]
    x2 = x1_ref[...] + mod_ref[5:6, :] * y
    o_ref[...] = _rms(x2) * fg_ref[...]


def _combine(dest_flat, x1, gates, mod8, final_g, yb, seq):
    t, d = x1.shape
    tm = TOKEN_TILE
    tiles_per_seq = seq // tm
    return pl.pallas_call(
        _combine_kernel,
        grid=(t // tm,),
        in_specs=[pl.BlockSpec((tm * TOP_K,), lambda i: (i,), memory_space=pltpu.SMEM),
                  pl.BlockSpec((tm, d), lambda i: (i, 0)),
                  pl.BlockSpec((tm, LANES), lambda i: (i, 0)),
                  pl.BlockSpec((None, SUBLANES, d), lambda i: (i // tiles_per_seq, 0, 0)),
                  pl.BlockSpec((1, d), lambda i: (0, 0)),
                  pl.BlockSpec(memory_space=pl.ANY)],
        out_specs=pl.BlockSpec((tm, d), lambda i: (i, 0)),
        out_shape=jax.ShapeDtypeStruct((t, d), F32),
        scratch_shapes=[pltpu.VMEM((TOP_K, tm, d), F32), pltpu.SemaphoreType.DMA(())],
        compiler_params=_cparams("arbitrary"),
        name="combine",
    )(dest_flat, x1, gates, mod8, final_g[None], yb)


def kernel(x, c, w_ada, b_ada, norm1_g, w_in, b_forget, conv_w, conv_b, conv_ln_g, conv_ln_b, w_conv_out,
           w_attn_out, w_out, norm2_g, w_router, b_router, w_exp_in, b_exp_in, w_exp_out, b_exp_out, final_g):
    bsz, seq, d = x.shape
    t = bsz * seq
    depth = w_ada.shape[0]
    xf = x.reshape(t, d)
    n_rows = t * TOP_K + N_EXPERTS * EXPERT_BLOCK
    n_blocks = n_rows // EXPERT_BLOCK
    for l in range(depth):
        mod8 = _mod(c, w_ada[l], b_ada[l])
        a, q, k, v, tot, sgc, sga = _inproj(xf, mod8, norm1_g[l], w_in[l], b_forget[l], seq)
        act = _conv(a, conv_w[l], conv_b[l], conv_ln_g[l], conv_ln_b[l], bsz, seq)
        attn = _attention(q, k, v, tot, bsz, seq)
        x1, h2p, ids, gates = _merge(xf, act, attn, sgc, sga, mod8, norm2_g[l], w_conv_out[l], w_attn_out[l],
                                     w_out[l], w_router[l], b_router[l], seq)
        rank, cnt = _route(ids)
        dest, blk = _dest(ids, rank, cnt)
        dest_flat = dest[:, :TOP_K].reshape(t * TOP_K)
        block_e = blk[0].reshape(-1)[:n_blocks]
        n_active = blk[1, 0, :1]
        pend = blk[2, 0, :N_EXPERTS]
        xs = _dispatch(pend, n_active, dest_flat, h2p, n_rows)
        yb = _experts(block_e, n_active, xs, w_exp_in[l], b_exp_in[l], w_exp_out[l], b_exp_out[l])
        assert depth == 1
        xf = _combine(dest_flat, x1, gates, mod8, final_g, yb, seq)
    return xf.reshape(bsz, seq, d)
```

```python
import functools

import jax
import jax.numpy as jnp
from jax import lax
from jax.experimental import pallas as pl
from jax.experimental.pallas import tpu as pltpu

F32 = jnp.float32
BF16 = jnp.bfloat16
I32 = jnp.int32
U32 = jnp.uint32
HIGHEST = lax.Precision.HIGHEST

LANES = 128
SUBLANES = 8
VMEM_LIMIT_BYTES = 56 * 1024 * 1024

N_HEADS = 8
HEAD_DIM = 64
CONV_KERNEL = 31
CONV_HALO = 32
N_EXPERTS = 32
TOP_K = 4
SWIGLU_LIMIT = 7.0
SWIGLU_ALPHA = 1.702
RMS_EPS = 1e-5
LN_EPS = 1e-5
N_MOD = 6
N_SPLIT = 3
LOG2_E = 1.4426950408889634
HEADS_PER_STEP = 2

INPROJ_TILE = 512
ATTN_BQ = 2048
ATTN_BK = 512
MERGE_TILE = 512
CONV_TILE = 256
TOKEN_TILE = 256
EXPERT_BLOCK = 512
CHUNK = SUBLANES
SLOT_ROWS = 1280
GROUP_TABLE = SUBLANES * LANES
assert SLOT_ROWS >= TOKEN_TILE * TOP_K + N_EXPERTS * (CHUNK - 1) and SLOT_ROWS // CHUNK < GROUP_TABLE


def _cparams(*sem):
    return pltpu.CompilerParams(dimension_semantics=sem, vmem_limit_bytes=VMEM_LIMIT_BYTES)


def _resident(shape):
    nd = len(shape)
    return pl.BlockSpec(shape, lambda *_: (0,) * nd, pipeline_mode=pl.Buffered(1))


def _split3(x):
    p1 = x.astype(BF16)
    r1 = x - p1.astype(F32)
    p2 = r1.astype(BF16)
    p3 = (r1 - p2.astype(F32)).astype(BF16)
    return jnp.concatenate([p1, p2, p3], axis=1)


def _rms(x):
    return x * lax.rsqrt(jnp.mean(x * x, axis=-1, keepdims=True) + RMS_EPS)


def _mod_kernel(c_ref, w_ref, b_ref, o_ref):
    c = c_ref[...]
    sc = c * jax.nn.sigmoid(c)
    o_ref[...] = jnp.dot(sc, w_ref[...], preferred_element_type=F32, precision=HIGHEST) + b_ref[...]


def _mod(c, w_ada, b_ada):
    bsz, d = c.shape
    n = w_ada.shape[1]
    tn = 1024
    cp = jnp.zeros((SUBLANES, d), F32).at[:bsz].set(c)
    out = pl.pallas_call(
        _mod_kernel,
        grid=(n // tn,),
        in_specs=[pl.BlockSpec((SUBLANES, d), lambda j: (0, 0)),
                  pl.BlockSpec((d, tn), lambda j: (0, j)),
                  pl.BlockSpec((1, tn), lambda j: (0, j))],
        out_specs=pl.BlockSpec((SUBLANES, tn), lambda j: (0, j)),
        out_shape=jax.ShapeDtypeStruct((SUBLANES, n), F32),
        compiler_params=_cparams("arbitrary"),
        name="mod",
    )(cp, w_ada, b_ada[None])
    mod = out[:bsz].reshape(bsz, N_MOD, d)
    return jnp.pad(mod, ((0, 0), (0, SUBLANES - N_MOD), (0, 0)))


def _inproj_kernel(x_ref, mod_ref, g_ref, wa_ref, wq_ref, wk_ref, wv_ref, wf_ref, wg_ref, bf_ref,
                   tri_ref, eq_ref, ek_ref, oneq_ref, onek_ref, onev_ref,
                   a_ref, q_ref, k_ref, v_ref, tot_ref, sgc_ref, sga_ref, carry_scr, *, conv_width, d_model):
    x = x_ref[...]
    h = _rms(x) * g_ref[...] * (1.0 + mod_ref[1:2, :]) + mod_ref[0:1, :]
    hb = h.astype(BF16)

    u = jnp.dot(hb, wa_ref[...], preferred_element_type=F32)
    a_ref[...] = (u[:, :conv_width] * jax.nn.sigmoid(u[:, conv_width:])).astype(BF16)

    fl = jnp.dot(hb, wf_ref[...], preferred_element_type=F32) + bf_ref[...]
    lane = lax.broadcasted_iota(I32, fl.shape, 1)
    log_f = -(jnp.maximum(-fl, 0.0) + jnp.log1p(jnp.exp(-jnp.abs(fl))))
    log_f = jnp.where(lane < N_HEADS, log_f, 0.0)
    cs = jnp.dot(tri_ref[...], _split3(log_f), preferred_element_type=F32)
    cum = (cs[:, :LANES] + cs[:, LANES:2 * LANES] + cs[:, 2 * LANES:]) * LOG2_E
    tm = cum.shape[0]

    pieces = []
    for sb in range(tm // ATTN_BK):
        blk = cum[sb * ATTN_BK:(sb + 1) * ATTN_BK, :]
        if sb > 0:
            blk = blk - cum[sb * ATTN_BK - 1:sb * ATTN_BK, :]
        tot_ref[sb] = blk[ATTN_BK - 1:, :]
        pieces.append(blk)
    cum_k = pieces[0] if len(pieces) == 1 else jnp.concatenate(pieces, axis=0)

    @pl.when(pl.program_id(0) % (ATTN_BQ // tm) == 0)
    def _():
        carry_scr[...] = jnp.zeros(carry_scr.shape, F32)

    cum_q = cum + carry_scr[0:1, :]
    carry_scr[...] = jnp.broadcast_to(cum_q[tm - 1:, :], carry_scr.shape)

    q = jnp.dot(hb, wq_ref[...], preferred_element_type=F32) * (HEAD_DIM ** -0.5 * LOG2_E)
    q = q + jnp.dot(_split3(cum_q), eq_ref[...], preferred_element_type=F32) + oneq_ref[...]
    q_ref[...] = q.astype(BF16)
    k = jnp.dot(hb, wk_ref[...], preferred_element_type=F32)
    k = k + jnp.dot(_split3(cum_k), ek_ref[...], preferred_element_type=F32) + onek_ref[...]
    k_ref[...] = k.astype(BF16)
    v_ref[...] = (jnp.dot(hb, wv_ref[...], preferred_element_type=F32) + onev_ref[...]).astype(BF16)

    gts = jnp.dot(hb, wg_ref[...], preferred_element_type=F32)
    sgc_ref[...] = jax.nn.sigmoid(gts[:, :d_model]).astype(BF16)
    sga_ref[...] = jax.nn.sigmoid(gts[:, d_model:]).astype(BF16)


def _spread_heads(w):
    d = w.shape[0]
    w = w.reshape(d, N_HEADS, HEAD_DIM)
    w = jnp.pad(w, ((0, 0), (0, 0), (0, LANES - HEAD_DIM)))
    return w.reshape(d, N_HEADS * LANES)


def _aug_constants():
    rows = jnp.arange(N_SPLIT * LANES)
    piece, head = rows // LANES, rows % LANES
    cols = jnp.arange(N_HEADS * LANES)
    chead, coff = cols // LANES, cols % LANES - HEAD_DIM
    valid = (head[:, None] < N_HEADS) & (head[:, None] == chead[None, :])
    eq = jnp.where(valid & (coff[None, :] == piece[:, None]), 1.0, 0.0).astype(BF16)
    ek = jnp.where(valid & (coff[None, :] == piece[:, None] + N_SPLIT), -1.0, 0.0).astype(BF16)
    oneq = jnp.where((coff >= N_SPLIT) & (coff < 2 * N_SPLIT), 1.0, 0.0).astype(F32)[None]
    onek = jnp.where((coff >= 0) & (coff < N_SPLIT), 1.0, 0.0).astype(F32)[None]
    onev = jnp.where(coff == 0, 1.0, 0.0).astype(F32)[None]
    return eq, ek, oneq, onek, onev


def _inproj(x2, mod8, norm_g, w_in, b_forget, seq):
    t, d = x2.shape
    tm = INPROJ_TILE
    nsub = tm // ATTN_BK
    cw = d // 2
    aw = N_HEADS * HEAD_DIM
    o0 = 2 * cw
    wa = w_in[:, :o0].astype(BF16)
    wq = _spread_heads(w_in[:, o0:o0 + aw]).astype(BF16)
    wk = _spread_heads(w_in[:, o0 + aw:o0 + 2 * aw]).astype(BF16)
    wv = _spread_heads(w_in[:, o0 + 2 * aw:o0 + 3 * aw]).astype(BF16)
    o1 = o0 + 3 * aw
    wf = jnp.pad(w_in[:, o1:o1 + N_HEADS], ((0, 0), (0, LANES - N_HEADS))).astype(BF16)
    bfp = jnp.pad(b_forget, (0, LANES - N_HEADS))[None]
    wg = w_in[:, o1 + N_HEADS:].astype(BF16)
    r = jnp.arange(tm)
    tri = (r[None, :] <= r[:, None]).astype(BF16)
    eq, ek, oneq, onek, onev = _aug_constants()
    sw = N_HEADS * LANES
    tiles_per_seq = seq // tm
    row = lambda i: (i, 0)
    outs = pl.pallas_call(
        functools.partial(_inproj_kernel, conv_width=cw, d_model=d),
        grid=(t // tm,),
        in_specs=[pl.BlockSpec((tm, d), row),
                  pl.BlockSpec((None, SUBLANES, d), lambda i: (i // tiles_per_seq, 0, 0)),
                  _resident((1, d)),
                  _resident(wa.shape), _resident(wq.shape), _resident(wk.shape), _resident(wv.shape),
                  _resident(wf.shape), _resident(wg.shape), _resident(bfp.shape),
                  _resident(tri.shape), _resident(eq.shape), _resident(ek.shape),
                  _resident(oneq.shape), _resident(onek.shape), _resident(onev.shape)],
        out_specs=[pl.BlockSpec((tm, cw), row),
                   pl.BlockSpec((tm, sw), row), pl.BlockSpec((tm, sw), row), pl.BlockSpec((tm, sw), row),
                   pl.BlockSpec((nsub, 1, LANES), lambda i: (i, 0, 0)),
                   pl.BlockSpec((tm, d), row), pl.BlockSpec((tm, d), row)],
        out_shape=[jax.ShapeDtypeStruct((t, cw), BF16),
                   jax.ShapeDtypeStruct((t, sw), BF16), jax.ShapeDtypeStruct((t, sw), BF16),
                   jax.ShapeDtypeStruct((t, sw), BF16),
                   jax.ShapeDtypeStruct((t // ATTN_BK, 1, LANES), F32),
                   jax.ShapeDtypeStruct((t, d), BF16), jax.ShapeDtypeStruct((t, d), BF16)],
        scratch_shapes=[pltpu.VMEM((SUBLANES, LANES), F32)],
        compiler_params=_cparams("arbitrary"),
        name="inproj",
    )(x2, mod8, norm_g[None], wa, wq, wk, wv, wf, wg, bfp, tri, eq, ek, oneq, onek, onev)
    return outs


def _conv_kernel(halo_ref, cur_ref, w_ref, b_ref, g_ref, beta_ref, o_ref, xs_ref, ys_ref):
    ts = cur_ref.shape[0]
    first = pl.program_id(1) == 0
    halo = halo_ref[...].astype(F32)
    xs_ref[0:CONV_HALO, :] = jnp.where(first, 0.0, halo)
    xs_ref[CONV_HALO:, :] = cur_ref[...].astype(F32)
    acc = jnp.zeros(o_ref.shape, F32) + b_ref[...]
    base = CONV_HALO - (CONV_KERNEL - 1)
    for rho in range(SUBLANES):
        offs = [o for o in range(base, base + CONV_KERNEL) if o % SUBLANES == rho]
        if not offs:
            continue
        rows = offs[-1] + ts - rho
        ys_ref[0:rows, :] = xs_ref[rho:rho + rows, :]
        for o in offs:
            lo = o - rho
            acc = acc + w_ref[o - base:o - base + 1, :] * ys_ref[lo:lo + ts, :]
    mu = jnp.mean(acc, axis=-1, keepdims=True)
    cen = acc - mu
    var = jnp.mean(cen * cen, axis=-1, keepdims=True)
    y = cen * lax.rsqrt(var + LN_EPS) * g_ref[...] + beta_ref[...]
    o_ref[...] = (y * jax.nn.sigmoid(y)).astype(BF16)


def _conv(a, conv_w, conv_b, ln_g, ln_b, bsz, seq):
    t, cw = a.shape
    ts = CONV_TILE
    nt = seq // ts
    hpt = ts // CONV_HALO
    vec = lambda: pl.BlockSpec((1, cw), lambda b, i: (0, 0))
    return pl.pallas_call(
        _conv_kernel,
        grid=(bsz, nt),
        in_specs=[pl.BlockSpec((CONV_HALO, cw), lambda b, i: (jnp.maximum((b * nt + i) * hpt - 1, 0), 0)),
                  pl.BlockSpec((ts, cw), lambda b, i: (b * nt + i, 0)),
                  pl.BlockSpec((CONV_KERNEL, cw), lambda b, i: (0, 0)),
                  vec(), vec(), vec()],
        out_specs=pl.BlockSpec((ts, cw), lambda b, i: (b * nt + i, 0)),
        out_shape=jax.ShapeDtypeStruct((t, cw), BF16),
        scratch_shapes=[pltpu.VMEM((CONV_HALO + ts, cw), F32), pltpu.VMEM((CONV_HALO + ts, cw), F32)],
        compiler_params=_cparams("arbitrary", "arbitrary"),
        name="conv",
    )(a, a, conv_w, conv_b[None], ln_g[None], ln_b[None])


def _attn_kernel(q_ref, k_ref, v_ref, tot_ref, o_ref, m_scr, acc_scr):
    bq = q_ref.shape[0]
    bk = ATTN_BK
    ratio = bq // bk
    qi = pl.program_id(2)
    nt = (((1,), (1,)), ((), ()))
    heads = range(HEADS_PER_STEP)
    zero = jnp.zeros((1, LANES), F32)

    def scores(hh, kb, row0):
        lanes = slice(hh * LANES, (hh + 1) * LANES)
        start = pl.multiple_of(kb * bk, bk)
        s = lax.dot_general(q_ref[row0:, lanes], k_ref[pl.ds(start, bk), lanes], nt, preferred_element_type=F32)
        return s, v_ref[pl.ds(start, bk), lanes]

    def update(hh, s, v, delta, row0):
        m_prev = m_scr[hh, row0:, :]
        m_new = jnp.maximum(m_prev, jnp.max(s, axis=1, keepdims=True) + delta)
        shift = m_new - delta
        p = jnp.exp2(s - shift[:, :1])
        alpha = jnp.exp2(m_prev - m_new)
        pv = jnp.dot(p.astype(BF16), v, preferred_element_type=F32)
        acc_scr[hh, row0:, :] = alpha * acc_scr[hh, row0:, :] + pv
        m_scr[hh, row0:, :] = m_new

    for hh in heads:
        delta = zero
        for sb in range(ratio):
            kb = qi * ratio + sb
            row0 = sb * bk
            s, v = scores(hh, kb, row0)
            row = lax.broadcasted_iota(I32, s.shape, 0)
            col = lax.broadcasted_iota(I32, s.shape, 1)
            s = jnp.where(col <= row, s, -jnp.inf)
            if sb == 0:
                m0 = jnp.max(s, axis=1, keepdims=True)
                p = jnp.exp2(s - m0)
                m_scr[hh] = jnp.broadcast_to(m0, m_scr.shape[1:])
                acc_scr[hh] = jnp.dot(p.astype(BF16), v, preferred_element_type=F32)
            else:
                update(hh, s, v, delta, row0)
            delta = delta - tot_ref[hh, pl.ds(kb, 1), :]

    def body(step, deltas):
        kb = qi * ratio - 1 - step
        new_deltas = []
        for hh in heads:
            delta = deltas[hh] + tot_ref[hh, pl.ds(kb, 1), :]
            s, v = scores(hh, kb, 0)
            update(hh, s, v, delta, 0)
            new_deltas.append(delta)
        return tuple(new_deltas)

    lax.fori_loop(0, qi * ratio, body, tuple(zero for _ in heads))
    for hh in heads:
        acc = acc_scr[hh]
        lane = lax.broadcasted_iota(I32, acc.shape, 1)
        out = jnp.where(lane < HEAD_DIM, acc / acc[:, HEAD_DIM:HEAD_DIM + 1], 0.0)
        o_ref[:, hh * LANES:(hh + 1) * LANES] = out.astype(BF16)


def _attention(q, k, v, tot, bsz, seq):
    t, sw = q.shape
    bq = ATTN_BQ
    nq = seq // bq
    nk = seq // ATTN_BK
    hw = HEADS_PER_STEP * LANES
    tot = tot.reshape(bsz, nk, LANES)[:, :, :N_HEADS].transpose(0, 2, 1)
    tot = jnp.broadcast_to(tot[..., None], (bsz, N_HEADS, nk, LANES))
    k3 = k.reshape(bsz, seq, sw)
    v3 = v.reshape(bsz, seq, sw)
    return pl.pallas_call(
        _attn_kernel,
        grid=(bsz, N_HEADS // HEADS_PER_STEP, nq),
        in_specs=[pl.BlockSpec((bq, hw), lambda b, h, i: (b * nq + i, h)),
                  pl.BlockSpec((None, seq, hw), lambda b, h, i: (b, 0, h), pipeline_mode=pl.Buffered(1)),
                  pl.BlockSpec((None, seq, hw), lambda b, h, i: (b, 0, h), pipeline_mode=pl.Buffered(1)),
                  pl.BlockSpec((None, HEADS_PER_STEP, nk, LANES), lambda b, h, i: (b, h, 0, 0))],
        out_specs=pl.BlockSpec((bq, hw), lambda b, h, i: (b * nq + i, h)),
        out_shape=jax.ShapeDtypeStruct((t, sw), BF16),
        scratch_shapes=[pltpu.VMEM((HEADS_PER_STEP, bq, LANES), F32),
                        pltpu.VMEM((HEADS_PER_STEP, bq, LANES), F32)],
        compiler_params=_cparams("arbitrary", "arbitrary", "arbitrary"),
        name="attn",
    )(q, k3, v3, tot)


def _pack_bf16_pair(lo, hi):
    lo_bits = pltpu.bitcast(lo.astype(BF16).astype(F32), U32)
    hi_bits = pltpu.bitcast(hi.astype(BF16).astype(F32), U32)
    return (hi_bits & jnp.uint32(0xFFFF0000)) | (lo_bits >> 16)


def _unpack_bf16_pair(w):
    lo = pltpu.bitcast(w << 16, F32)
    hi = pltpu.bitcast(w & jnp.uint32(0xFFFF0000), F32)
    return jnp.concatenate([lo, hi], axis=1)


def _merge_kernel(x_ref, act_ref, attn_ref, sgc_ref, sga_ref, mod_ref, g2_ref, wco_ref, wao_ref, wout_ref,
                  wr_ref, br_ref, x1_ref, h2_ref, ids_ref, gates_ref):
    conv_out = jnp.dot(act_ref[...], wco_ref[...], preferred_element_type=F32)
    attn_out = jnp.dot(attn_ref[...], wao_ref[...], preferred_element_type=F32)
    mixed = sgc_ref[...].astype(F32) * conv_out + sga_ref[...].astype(F32) * attn_out
    upd = jnp.dot(mixed.astype(BF16), wout_ref[...], preferred_element_type=F32)
    x1 = x_ref[...] + mod_ref[2:3, :] * upd
    x1_ref[...] = x1
    h2 = _rms(x1) * g2_ref[...] * (1.0 + mod_ref[4:5, :]) + mod_ref[3:4, :]
    h2_hi = h2.astype(BF16)
    h2_ref[...] = h2_hi
    h2_lo = (h2 - h2_hi.astype(F32)).astype(BF16)
    h2_cat = jnp.concatenate([h2_hi, h2_lo, h2_hi], axis=1)
    logits = jnp.dot(h2_cat, wr_ref[...], preferred_element_type=F32) + br_ref[...]
    lane = lax.broadcasted_iota(I32, logits.shape, 1)
    work = jnp.where(lane < N_EXPERTS, logits, -jnp.inf)
    ids = jnp.zeros(logits.shape, I32)
    vals = []
    for kk in range(TOP_K):
        mx = jnp.max(work, axis=1, keepdims=True)
        idx = jnp.min(jnp.where(work == mx, lane, LANES), axis=1, keepdims=True)
        ids = jnp.where(lane == kk, idx, ids)
        vals.append(mx)
        work = jnp.where(lane == idx, -jnp.inf, work)
    exps = [jnp.exp(vv - vals[0]) for vv in vals]
    den = exps[0] + exps[1] + exps[2] + exps[3]
    gates = jnp.zeros(logits.shape, F32)
    for kk in range(TOP_K):
        gates = jnp.where(lane == kk, exps[kk] / den, gates)
    ids_ref[...] = ids
    gates_ref[...] = gates


def _merge(x2, act, attn, sgc, sga, mod8, norm2_g, w_conv_out, w_attn_out, w_out, w_router, b_router, seq):
    t, d = x2.shape
    tm = MERGE_TILE
    cw = act.shape[1]
    sw = attn.shape[1]
    wco = w_conv_out.astype(BF16)
    wao = jnp.pad(w_attn_out.reshape(N_HEADS, HEAD_DIM, d), ((0, 0), (0, LANES - HEAD_DIM), (0, 0)))
    wao = wao.reshape(sw, d).astype(BF16)
    wout = w_out.astype(BF16)
    wr = jnp.pad(w_router, ((0, 0), (0, LANES - N_EXPERTS)))
    wr_hi = wr.astype(BF16)
    wr_lo = (wr - wr_hi.astype(F32)).astype(BF16)
    wr = jnp.concatenate([wr_hi, wr_hi, wr_lo], axis=0)
    br = jnp.pad(b_router, (0, LANES - N_EXPERTS))[None]
    tiles_per_seq = seq // tm
    row = lambda i: (i, 0)
    return pl.pallas_call(
        _merge_kernel,
        grid=(t // tm,),
        in_specs=[pl.BlockSpec((tm, d), row), pl.BlockSpec((tm, cw), row), pl.BlockSpec((tm, sw), row),
                  pl.BlockSpec((tm, d), row), pl.BlockSpec((tm, d), row),
                  pl.BlockSpec((None, SUBLANES, d), lambda i: (i // tiles_per_seq, 0, 0)),
                  _resident((1, d)), _resident(wco.shape), _resident(wao.shape), _resident(wout.shape),
                  _resident(wr.shape), _resident(br.shape)],
        out_specs=[pl.BlockSpec((tm, d), row), pl.BlockSpec((tm, d), row),
                   pl.BlockSpec((tm, LANES), row), pl.BlockSpec((tm, LANES), row)],
        out_shape=[jax.ShapeDtypeStruct((t, d), F32), jax.ShapeDtypeStruct((t, d), BF16),
                   jax.ShapeDtypeStruct((t, LANES), I32), jax.ShapeDtypeStruct((t, LANES), F32)],
        compiler_params=_cparams("arbitrary"),
        name="merge",
    )(x2, act, attn, sgc, sga, mod8, norm2_g[None], wco, wao, wout, wr, br)


def _expert_onehots(ids):
    lane = lax.broadcasted_iota(I32, ids.shape, 1)
    return lane, [lane == ids[:, kk:kk + 1] for kk in range(TOP_K)]


def _route_kernel(ids_ref, gates_ref, tri_ref, utri_ref, slot_ref, slott_ref, gatest_ref, meta_ref,
                  cnt_ref, carry_scr):
    @pl.when(pl.program_id(0) == 0)
    def _():
        carry_scr[...] = jnp.zeros(carry_scr.shape, F32)

    lane, hots = _expert_onehots(ids_ref[...])
    multi = jnp.zeros(lane.shape, F32)
    for hot in hots:
        multi = multi + jnp.where(hot, 1.0, 0.0)
    carry = carry_scr[0:1, :]
    before = jnp.dot(tri_ref[...], multi.astype(BF16), preferred_element_type=F32)
    units = jnp.ceil(jnp.sum(multi, axis=0, keepdims=True) * (1.0 / CHUNK))
    units8 = jnp.broadcast_to(units, (SUBLANES, LANES)).astype(BF16)
    lstart = jnp.dot(units8, utri_ref[...], preferred_element_type=F32)[0:1, :] * CHUNK
    csize = units * CHUNK
    slot = jnp.zeros(lane.shape, F32)
    for kk, hot in enumerate(hots):
        sl = jnp.sum(jnp.where(hot, lstart + before, 0.0), axis=1, keepdims=True)
        slot = jnp.where(lane == kk, sl, slot)
    slot_ref[...] = slot.astype(I32)
    slott_ref[...] = slot.T[0:SUBLANES, :].astype(I32)
    gatest_ref[...] = gates_ref[...].T[0:SUBLANES, :]
    row = lax.broadcasted_iota(I32, (SUBLANES, LANES), 0)
    meta = jnp.where(row == 0, csize, jnp.where(row == 1, lstart, jnp.where(row == 2, carry, 0.0)))
    meta_ref[0] = meta
    carry = carry + csize
    carry_scr[...] = jnp.broadcast_to(carry, carry_scr.shape)
    cnt_ref[...] = jnp.broadcast_to(carry, cnt_ref.shape)


def _route(ids, gates):
    t = ids.shape[0]
    tm = TOKEN_TILE
    nt = t // tm
    r = jnp.arange(tm)
    tri = (r[None, :] < r[:, None]).astype(BF16)
    e = jnp.arange(LANES)
    utri = (e[:, None] < e[None, :]).astype(BF16)
    tile = pl.BlockSpec((tm, LANES), lambda i: (i, 0))
    return pl.pallas_call(
        _route_kernel,
        grid=(nt,),
        in_specs=[tile, tile, _resident(tri.shape), _resident(utri.shape)],
        out_specs=[tile, pl.BlockSpec((SUBLANES, tm), lambda i: (i, 0)),
                   pl.BlockSpec((SUBLANES, tm), lambda i: (i, 0)),
                   pl.BlockSpec((1, SUBLANES, LANES), lambda i: (i, 0, 0)),
                   pl.BlockSpec((SUBLANES, LANES), lambda i: (0, 0))],
        out_shape=[jax.ShapeDtypeStruct((t, LANES), I32), jax.ShapeDtypeStruct((nt * SUBLANES, tm), I32),
                   jax.ShapeDtypeStruct((nt * SUBLANES, tm), F32),
                   jax.ShapeDtypeStruct((nt, SUBLANES, LANES), F32),
                   jax.ShapeDtypeStruct((SUBLANES, LANES), F32)],
        scratch_shapes=[pltpu.VMEM((SUBLANES, LANES), F32)],
        compiler_params=_cparams("arbitrary"),
        name="route",
    )(ids, gates, tri, utri)


def _dest_kernel(cnt_ref, meta_ref, tri_ref, gd_ref, blk_ref):
    cnt = cnt_ref[...]
    padded = jnp.ceil(cnt * (1.0 / EXPERT_BLOCK)) * EXPERT_BLOCK
    pend = jnp.dot(padded, tri_ref[...], preferred_element_type=F32, precision=HIGHEST)
    pstart = (pend - padded)[0:1, :]

    meta = meta_ref[0]
    csize, lstart, seg_off = meta[0:1, :], meta[1:2, :], meta[2:3, :]
    lend = lstart + csize
    base = pstart + seg_off - lstart
    grow = lax.broadcasted_iota(I32, (SUBLANES, LANES), 0)
    glane = lax.broadcasted_iota(I32, (SUBLANES, LANES), 1)
    gpos = ((grow * LANES + glane) * CHUNK).astype(F32)
    owner = jnp.zeros(gpos.shape, I32)
    for e in range(N_EXPERTS):
        owner = owner + jnp.where(gpos >= lend[0:1, e:e + 1], 1, 0)
    gdst = gpos
    for e in range(N_EXPERTS):
        gdst = gdst + jnp.where(owner == e, base[0:1, e:e + 1], 0.0)
    n_groups = lend[0:1, N_EXPERTS - 1:N_EXPERTS] * (1.0 / CHUNK)
    last = jnp.logical_and(grow == SUBLANES - 1, glane == LANES - 1)
    gd_ref[0] = jnp.where(last, n_groups, gdst).astype(I32)

    @pl.when(pl.program_id(0) == 0)
    def _():
        shape = blk_ref.shape[1:]
        pos = (lax.broadcasted_iota(I32, shape, 0) * LANES + lax.broadcasted_iota(I32, shape, 1))
        pos = pos.astype(F32) * EXPERT_BLOCK
        be = jnp.zeros(shape, I32)
        for e in range(N_EXPERTS):
            be = be + jnp.where(pos >= pend[0:1, e:e + 1], 1, 0)
        blk_ref[0] = jnp.minimum(be, N_EXPERTS - 1)
        nact = pend[0:1, N_EXPERTS - 1:N_EXPERTS] * (1.0 / EXPERT_BLOCK)
        blk_ref[1] = jnp.broadcast_to(nact, shape).astype(I32)
        blk_ref[2] = jnp.broadcast_to(pend[0:1, :], shape).astype(I32)


def _dest(cnt, meta):
    nt = meta.shape[0]
    r = jnp.arange(LANES)
    tri = (r[:, None] <= r[None, :]).astype(F32)
    return pl.pallas_call(
        _dest_kernel,
        grid=(nt,),
        in_specs=[pl.BlockSpec((SUBLANES, LANES), lambda i: (0, 0)),
                  pl.BlockSpec((1, SUBLANES, LANES), lambda i: (i, 0, 0)),
                  pl.BlockSpec((LANES, LANES), lambda i: (0, 0))],
        out_specs=[pl.BlockSpec((1, SUBLANES, LANES), lambda i: (i, 0, 0)),
                   pl.BlockSpec((3, SUBLANES, LANES), lambda i: (0, 0, 0))],
        out_shape=[jax.ShapeDtypeStruct((nt, SUBLANES, LANES), I32),
                   jax.ShapeDtypeStruct((3, SUBLANES, LANES), I32)],
        compiler_params=_cparams("arbitrary"),
        name="dest",
    )(cnt, meta, tri)


def _row_copy(src, dst, sem):
    return pltpu.make_async_copy(src, dst, sem)


def _dispatch_kernel(pend_ref, na_ref, gd_ref, slott_ref, gatest_ref, h2_ref, xs_ref, zero_ref, sorted_scr, sem):
    tm = h2_ref.shape[0]
    bm = zero_ref.shape[0]
    nb = xs_ref.shape[0] // bm

    @pl.when(pl.program_id(0) == 0)
    def _():
        zero_ref[...] = jnp.zeros(zero_ref.shape, zero_ref.dtype)

        def zero_block(start):
            return _row_copy(zero_ref, xs_ref.at[pl.ds(pl.multiple_of(start, bm), bm)], sem)

        def nonempty(e):
            return pend_ref[e] > (pend_ref[e - 1] if e > 0 else 0)

        for e in range(N_EXPERTS):
            @pl.when(nonempty(e))
            def _():
                zero_block(pend_ref[e] - bm).start()

        def tail(b, carry):
            zero_block(b * bm).start()
            return carry

        lax.fori_loop(na_ref[0], nb, tail, 0)
        for e in range(N_EXPERTS):
            @pl.when(nonempty(e))
            def _():
                zero_block(0).wait()

        def tail_wait(b, carry):
            zero_block(0).wait()
            return carry

        lax.fori_loop(na_ref[0], nb, tail_wait, 0)

    srow = lax.broadcasted_iota(I32, (SLOT_ROWS, tm), 0)
    perm = jnp.zeros((SLOT_ROWS, tm), F32)
    gate = jnp.zeros((SLOT_ROWS, 1), F32)
    for kk in range(TOP_K):
        hit = srow == slott_ref[kk:kk + 1, :]
        perm = perm + jnp.where(hit, 1.0, 0.0)
        gate = gate + jnp.sum(jnp.where(hit, gatest_ref[kk:kk + 1, :], 0.0), axis=1, keepdims=True)
    perm = perm.astype(BF16)
    half = h2_ref.shape[1] // 2
    lo = jnp.dot(perm, h2_ref[:, :half], preferred_element_type=F32)
    hi = jnp.dot(perm, h2_ref[:, half:], preferred_element_type=F32)
    sorted_scr[:, :half] = _pack_bf16_pair(lo, hi)
    sorted_scr[:, half:] = pltpu.bitcast(jnp.broadcast_to(gate, (SLOT_ROWS, LANES)), U32)

    n_groups = gd_ref[GROUP_TABLE - 1]

    def group_copy(j, d):
        src = sorted_scr.at[pl.ds(pl.multiple_of(j * CHUNK, CHUNK), CHUNK)]
        return _row_copy(src, xs_ref.at[pl.ds(pl.multiple_of(d, CHUNK), CHUNK)], sem)

    def issue(j, carry):
        group_copy(j, gd_ref[j]).start()
        return carry

    def retire(j, carry):
        group_copy(0, 0).wait()
        return carry

    lax.fori_loop(0, n_groups, issue, 0)
    lax.fori_loop(0, n_groups, retire, 0)


def _dispatch(pend, n_active, gd_flat, slott, gatest, h2b, n_rows):
    t, d = h2b.shape
    w = d // 2 + LANES
    tm = TOKEN_TILE
    grid_spec = pltpu.PrefetchScalarGridSpec(
        num_scalar_prefetch=2,
        grid=(t // tm,),
        in_specs=[pl.BlockSpec((GROUP_TABLE,), lambda i, pe, na: (i,), memory_space=pltpu.SMEM),
                  pl.BlockSpec((SUBLANES, tm), lambda i, pe, na: (i, 0)),
                  pl.BlockSpec((SUBLANES, tm), lambda i, pe, na: (i, 0)),
                  pl.BlockSpec((tm, d), lambda i, pe, na: (i, 0))],
        out_specs=pl.BlockSpec(memory_space=pl.ANY),
        scratch_shapes=[pltpu.VMEM((EXPERT_BLOCK, w), U32), pltpu.VMEM((SLOT_ROWS, w), U32),
                        pltpu.SemaphoreType.DMA(())],
    )
    return pl.pallas_call(
        _dispatch_kernel,
        grid_spec=grid_spec,
        out_shape=jax.ShapeDtypeStruct((n_rows, w), U32),
        compiler_params=_cparams("arbitrary"),
        name="dispatch",
    )(pend, n_active, gd_flat, slott, gatest, h2b)


def _expert_kernel(be_ref, na_ref, xs_ref, w1_ref, b1_ref, w2_ref, b2_ref, y_ref, w1b_ref, w2b_ref, *, d_expert):
    i = pl.program_id(0)
    active = i < na_ref[0]
    prev = be_ref[jnp.maximum(i - 1, 0)]
    new_expert = jnp.logical_or(i == 0, be_ref[i] != prev)
    half = y_ref.shape[1]

    @pl.when(jnp.logical_and(active, new_expert))
    def _():
        w1b_ref[...] = w1_ref[...].astype(BF16)
        w2b_ref[...] = w2_ref[...].astype(BF16)

    @pl.when(active)
    def _():
        x = _unpack_bf16_pair(xs_ref[:, :half]).astype(BF16)
        gate = pltpu.bitcast(xs_ref[:, half:], F32)[:, 0:1]
        hc = jnp.dot(x, w1b_ref[...], preferred_element_type=F32) + b1_ref[...]
        g = jnp.minimum(hc[:, :d_expert], SWIGLU_LIMIT)
        u = jnp.clip(hc[:, d_expert:], -SWIGLU_LIMIT, SWIGLU_LIMIT)
        hm = (u + 1.0) * (g * jax.nn.sigmoid(SWIGLU_ALPHA * g))
        y = (jnp.dot(hm.astype(BF16), w2b_ref[...], preferred_element_type=F32) + b2_ref[...]) * gate
        y_ref[...] = _pack_bf16_pair(y[:, :half], y[:, half:])

    @pl.when(jnp.logical_not(active))
    def _():
        y_ref[...] = jnp.zeros(y_ref.shape, U32)


def _experts(block_e, n_active, xs, w1, b1, w2, b2):
    n_rows, w = xs.shape
    n_exp, d, two_de = w1.shape
    de = two_de // 2
    bm = EXPERT_BLOCK
    nb = n_rows // bm

    def blk(i, be, na):
        return jnp.maximum(jnp.minimum(i, na[0] - 1), 0)

    grid_spec = pltpu.PrefetchScalarGridSpec(
        num_scalar_prefetch=2,
        grid=(nb,),
        in_specs=[pl.BlockSpec((bm, w), lambda i, be, na: (blk(i, be, na), 0)),
                  pl.BlockSpec((None, d, two_de), lambda i, be, na: (be[blk(i, be, na)], 0, 0)),
                  pl.BlockSpec((None, 1, two_de), lambda i, be, na: (be[blk(i, be, na)], 0, 0)),
                  pl.BlockSpec((None, de, d), lambda i, be, na: (be[blk(i, be, na)], 0, 0)),
                  pl.BlockSpec((None, 1, d), lambda i, be, na: (be[blk(i, be, na)], 0, 0))],
        out_specs=pl.BlockSpec((bm, d // 2), lambda i, be, na: (i, 0)),
        scratch_shapes=[pltpu.VMEM((d, two_de), BF16), pltpu.VMEM((de, d), BF16)],
    )
    return pl.pallas_call(
        functools.partial(_expert_kernel, d_expert=de),
        grid_spec=grid_spec,
        out_shape=jax.ShapeDtypeStruct((n_rows, d // 2), U32),
        compiler_params=_cparams("arbitrary"),
        name="expert",
    )(block_e, n_active, xs, w1, b1[:, None, :], w2, b2[:, None, :])


def _combine_kernel(gd_ref, slot_ref, x1_ref, mod_ref, fg_ref, yb_ref, o_ref, sorted_scr, sem):
    tm = x1_ref.shape[0]

    @pl.when(pl.program_id(0) == 0)
    def _():
        sorted_scr[...] = jnp.zeros(sorted_scr.shape, sorted_scr.dtype)

    n_groups = gd_ref[GROUP_TABLE - 1]

    def group_copy(j, d):
        dst = sorted_scr.at[pl.ds(pl.multiple_of(j * CHUNK, CHUNK), CHUNK)]
        return _row_copy(yb_ref.at[pl.ds(pl.multiple_of(d, CHUNK), CHUNK)], dst, sem)

    def issue(j, carry):
        group_copy(j, gd_ref[j]).start()
        return carry

    def retire(j, carry):
        group_copy(0, 0).wait()
        return carry

    lax.fori_loop(0, n_groups, issue, 0)
    lax.fori_loop(0, n_groups, retire, 0)

    slot = slot_ref[...]
    scol = lax.broadcasted_iota(I32, (tm, SLOT_ROWS), 1)
    pick = jnp.zeros((tm, SLOT_ROWS), F32)
    for kk in range(TOP_K):
        pick = pick + jnp.where(scol == slot[:, kk:kk + 1], 1.0, 0.0)
    pick = pick.astype(BF16)
    words = sorted_scr[...]
    lo = pltpu.bitcast(words << 16, F32).astype(BF16)
    hi = pltpu.bitcast(words & jnp.uint32(0xFFFF0000), F32).astype(BF16)
    y = jnp.concatenate([jnp.dot(pick, lo, preferred_element_type=F32),
                         jnp.dot(pick, hi, preferred_element_type=F32)], axis=1)
    x2 = x1_ref[...] + mod_ref[5:6, :] * y
    o_ref[...] = _rms(x2) * fg_ref[...]


def _combine(gd_flat, slot, x1, mod8, final_g, yb, seq):
    t, d = x1.shape
    tm = TOKEN_TILE
    tiles_per_seq = seq // tm
    return pl.pallas_call(
        _combine_kernel,
        grid=(t // tm,),
        in_specs=[pl.BlockSpec((GROUP_TABLE,), lambda i: (i,), memory_space=pltpu.SMEM),
                  pl.BlockSpec((tm, LANES), lambda i: (i, 0)),
                  pl.BlockSpec((tm, d), lambda i: (i, 0)),
                  pl.BlockSpec((None, SUBLANES, d), lambda i: (i // tiles_per_seq, 0, 0)),
                  pl.BlockSpec((1, d), lambda i: (0, 0)),
                  pl.BlockSpec(memory_space=pl.ANY)],
        out_specs=pl.BlockSpec((tm, d), lambda i: (i, 0)),
        out_shape=jax.ShapeDtypeStruct((t, d), F32),
        scratch_shapes=[pltpu.VMEM((SLOT_ROWS, d // 2), U32), pltpu.SemaphoreType.DMA(())],
        compiler_params=_cparams("arbitrary"),
        name="combine",
    )(gd_flat, slot, x1, mod8, final_g[None], yb)


def kernel(x, c, w_ada, b_ada, norm1_g, w_in, b_forget, conv_w, conv_b, conv_ln_g, conv_ln_b, w_conv_out,
           w_attn_out, w_out, norm2_g, w_router, b_router, w_exp_in, b_exp_in, w_exp_out, b_exp_out, final_g):
    bsz, seq, d = x.shape
    t = bsz * seq
    depth = w_ada.shape[0]
    xf = x.reshape(t, d)
    n_rows = t * TOP_K + (t // TOKEN_TILE) * N_EXPERTS * (CHUNK - 1) + N_EXPERTS * (EXPERT_BLOCK - CHUNK)
    n_rows = -(-n_rows // EXPERT_BLOCK) * EXPERT_BLOCK
    n_blocks = n_rows // EXPERT_BLOCK
    for l in range(depth):
        mod8 = _mod(c, w_ada[l], b_ada[l])
        a, q, k, v, tot, sgc, sga = _inproj(xf, mod8, norm1_g[l], w_in[l], b_forget[l], seq)
        act = _conv(a, conv_w[l], conv_b[l], conv_ln_g[l], conv_ln_b[l], bsz, seq)
        attn = _attention(q, k, v, tot, bsz, seq)
        x1, h2b, ids, gates = _merge(xf, act, attn, sgc, sga, mod8, norm2_g[l], w_conv_out[l], w_attn_out[l],
                                     w_out[l], w_router[l], b_router[l], seq)
        slot, slott, gatest, meta, cnt = _route(ids, gates)
        gd, blk = _dest(cnt, meta)
        gd_flat = gd.reshape(-1)
        block_e = blk[0].reshape(-1)[:n_blocks]
        n_active = blk[1, 0, :1]
        pend = blk[2, 0, :N_EXPERTS]
        xs = _dispatch(pend, n_active, gd_flat, slott, gatest, h2b, n_rows)
        yb = _experts(block_e, n_active, xs, w_exp_in[l], b_exp_in[l], w_exp_out[l], b_exp_out[l])
        assert depth == 1
        xf = _combine(gd_flat, slot, x1, mod8, final_g, yb, seq)
    return xf.reshape(bsz, seq, d)
```

```python
import functools

import jax
import jax.numpy as jnp
from jax import lax
from jax.experimental import pallas as pl
from jax.experimental.pallas import tpu as pltpu

F32 = jnp.float32
BF16 = jnp.bfloat16
I32 = jnp.int32
U32 = jnp.uint32
HIGHEST = lax.Precision.HIGHEST

LANES = 128
SUBLANES = 8
VMEM_LIMIT_BYTES = 56 * 1024 * 1024

N_HEADS = 8
HEAD_DIM = 64
CONV_KERNEL = 31
CONV_HALO = 32
N_EXPERTS = 32
TOP_K = 4
SWIGLU_LIMIT = 7.0
SWIGLU_ALPHA = 1.702
RMS_EPS = 1e-5
LN_EPS = 1e-5
N_MOD = 6
N_SPLIT = 3
LOG2_E = 1.4426950408889634
HEADS_PER_STEP = 2

INPROJ_TILE = 512
ATTN_BQ = 2048
ATTN_BK = 512
MERGE_TILE = 512
CONV_TILE = 256
TOKEN_TILE = 256
EXPERT_BLOCK = 512
CHUNK = SUBLANES
SLOT_ROWS = 1280
GROUP_TABLE = SUBLANES * LANES
assert SLOT_ROWS >= TOKEN_TILE * TOP_K + N_EXPERTS * (CHUNK - 1) and SLOT_ROWS // CHUNK < GROUP_TABLE


def _cparams(*sem):
    return pltpu.CompilerParams(dimension_semantics=sem, vmem_limit_bytes=VMEM_LIMIT_BYTES)


def _resident(shape):
    nd = len(shape)
    return pl.BlockSpec(shape, lambda *_: (0,) * nd, pipeline_mode=pl.Buffered(1))


def _split3(x):
    p1 = x.astype(BF16)
    r1 = x - p1.astype(F32)
    p2 = r1.astype(BF16)
    p3 = (r1 - p2.astype(F32)).astype(BF16)
    return jnp.concatenate([p1, p2, p3], axis=1)


def _rms(x):
    return x * lax.rsqrt(jnp.mean(x * x, axis=-1, keepdims=True) + RMS_EPS)


def _mod_kernel(c_ref, w_ref, b_ref, o_ref):
    c = c_ref[...]
    sc = c * jax.nn.sigmoid(c)
    o_ref[...] = jnp.dot(sc, w_ref[...], preferred_element_type=F32, precision=HIGHEST) + b_ref[...]


def _mod(c, w_ada, b_ada):
    bsz, d = c.shape
    n = w_ada.shape[1]
    tn = 1024
    cp = jnp.zeros((SUBLANES, d), F32).at[:bsz].set(c)
    out = pl.pallas_call(
        _mod_kernel,
        grid=(n // tn,),
        in_specs=[pl.BlockSpec((SUBLANES, d), lambda j: (0, 0)),
                  pl.BlockSpec((d, tn), lambda j: (0, j)),
                  pl.BlockSpec((1, tn), lambda j: (0, j))],
        out_specs=pl.BlockSpec((SUBLANES, tn), lambda j: (0, j)),
        out_shape=jax.ShapeDtypeStruct((SUBLANES, n), F32),
        compiler_params=_cparams("arbitrary"),
        name="mod",
    )(cp, w_ada, b_ada[None])
    mod = out[:bsz].reshape(bsz, N_MOD, d)
    return jnp.pad(mod, ((0, 0), (0, SUBLANES - N_MOD), (0, 0)))


def _inproj_kernel(x_ref, mod_ref, g_ref, wa_ref, wq_ref, wk_ref, wv_ref, wf_ref, wg_ref, bf_ref,
                   tri_ref, eq_ref, ek_ref, oneq_ref, onek_ref, onev_ref,
                   a_ref, q_ref, k_ref, v_ref, tot_ref, sgc_ref, sga_ref, carry_scr, *, conv_width, d_model):
    x = x_ref[...]
    h = _rms(x) * g_ref[...] * (1.0 + mod_ref[1:2, :]) + mod_ref[0:1, :]
    hb = h.astype(BF16)

    u = jnp.dot(hb, wa_ref[...], preferred_element_type=F32)
    a_ref[...] = (u[:, :conv_width] * jax.nn.sigmoid(u[:, conv_width:])).astype(BF16)

    fl = jnp.dot(hb, wf_ref[...], preferred_element_type=F32) + bf_ref[...]
    lane = lax.broadcasted_iota(I32, fl.shape, 1)
    log_f = -(jnp.maximum(-fl, 0.0) + jnp.log1p(jnp.exp(-jnp.abs(fl))))
    log_f = jnp.where(lane < N_HEADS, log_f, 0.0)
    cs = jnp.dot(tri_ref[...], _split3(log_f), preferred_element_type=F32)
    cum = (cs[:, :LANES] + cs[:, LANES:2 * LANES] + cs[:, 2 * LANES:]) * LOG2_E
    tm = cum.shape[0]

    pieces = []
    for sb in range(tm // ATTN_BK):
        blk = cum[sb * ATTN_BK:(sb + 1) * ATTN_BK, :]
        if sb > 0:
            blk = blk - cum[sb * ATTN_BK - 1:sb * ATTN_BK, :]
        tot_ref[sb] = blk[ATTN_BK - 1:, :]
        pieces.append(blk)
    cum_k = pieces[0] if len(pieces) == 1 else jnp.concatenate(pieces, axis=0)

    @pl.when(pl.program_id(0) % (ATTN_BQ // tm) == 0)
    def _():
        carry_scr[...] = jnp.zeros(carry_scr.shape, F32)

    cum_q = cum + carry_scr[0:1, :]
    carry_scr[...] = jnp.broadcast_to(cum_q[tm - 1:, :], carry_scr.shape)

    q = jnp.dot(hb, wq_ref[...], preferred_element_type=F32) * (HEAD_DIM ** -0.5 * LOG2_E)
    q = q + jnp.dot(_split3(cum_q), eq_ref[...], preferred_element_type=F32) + oneq_ref[...]
    q_ref[...] = q.astype(BF16)
    k = jnp.dot(hb, wk_ref[...], preferred_element_type=F32)
    k = k + jnp.dot(_split3(cum_k), ek_ref[...], preferred_element_type=F32) + onek_ref[...]
    k_ref[...] = k.astype(BF16)
    v_ref[...] = (jnp.dot(hb, wv_ref[...], preferred_element_type=F32) + onev_ref[...]).astype(BF16)

    gts = jnp.dot(hb, wg_ref[...], preferred_element_type=F32)
    sgc_ref[...] = jax.nn.sigmoid(gts[:, :d_model]).astype(BF16)
    sga_ref[...] = jax.nn.sigmoid(gts[:, d_model:]).astype(BF16)


def _spread_heads(w):
    d = w.shape[0]
    w = w.reshape(d, N_HEADS, HEAD_DIM)
    w = jnp.pad(w, ((0, 0), (0, 0), (0, LANES - HEAD_DIM)))
    return w.reshape(d, N_HEADS * LANES)


def _aug_constants():
    rows = jnp.arange(N_SPLIT * LANES)
    piece, head = rows // LANES, rows % LANES
    cols = jnp.arange(N_HEADS * LANES)
    chead, coff = cols // LANES, cols % LANES - HEAD_DIM
    valid = (head[:, None] < N_HEADS) & (head[:, None] == chead[None, :])
    eq = jnp.where(valid & (coff[None, :] == piece[:, None]), 1.0, 0.0).astype(BF16)
    ek = jnp.where(valid & (coff[None, :] == piece[:, None] + N_SPLIT), -1.0, 0.0).astype(BF16)
    oneq = jnp.where((coff >= N_SPLIT) & (coff < 2 * N_SPLIT), 1.0, 0.0).astype(F32)[None]
    onek = jnp.where((coff >= 0) & (coff < N_SPLIT), 1.0, 0.0).astype(F32)[None]
    onev = jnp.where(coff == 0, 1.0, 0.0).astype(F32)[None]
    return eq, ek, oneq, onek, onev


def _inproj(x2, mod8, norm_g, w_in, b_forget, seq):
    t, d = x2.shape
    tm = INPROJ_TILE
    nsub = tm // ATTN_BK
    cw = d // 2
    aw = N_HEADS * HEAD_DIM
    o0 = 2 * cw
    wa = w_in[:, :o0].astype(BF16)
    wq = _spread_heads(w_in[:, o0:o0 + aw]).astype(BF16)
    wk = _spread_heads(w_in[:, o0 + aw:o0 + 2 * aw]).astype(BF16)
    wv = _spread_heads(w_in[:, o0 + 2 * aw:o0 + 3 * aw]).astype(BF16)
    o1 = o0 + 3 * aw
    wf = jnp.pad(w_in[:, o1:o1 + N_HEADS], ((0, 0), (0, LANES - N_HEADS))).astype(BF16)
    bfp = jnp.pad(b_forget, (0, LANES - N_HEADS))[None]
    wg = w_in[:, o1 + N_HEADS:].astype(BF16)
    r = jnp.arange(tm)
    tri = (r[None, :] <= r[:, None]).astype(BF16)
    eq, ek, oneq, onek, onev = _aug_constants()
    sw = N_HEADS * LANES
    tiles_per_seq = seq // tm
    row = lambda i: (i, 0)
    outs = pl.pallas_call(
        functools.partial(_inproj_kernel, conv_width=cw, d_model=d),
        grid=(t // tm,),
        in_specs=[pl.BlockSpec((tm, d), row),
                  pl.BlockSpec((None, SUBLANES, d), lambda i: (i // tiles_per_seq, 0, 0)),
                  _resident((1, d)),
                  _resident(wa.shape), _resident(wq.shape), _resident(wk.shape), _resident(wv.shape),
                  _resident(wf.shape), _resident(wg.shape), _resident(bfp.shape),
                  _resident(tri.shape), _resident(eq.shape), _resident(ek.shape),
                  _resident(oneq.shape), _resident(onek.shape), _resident(onev.shape)],
        out_specs=[pl.BlockSpec((tm, cw), row),
                   pl.BlockSpec((tm, sw), row), pl.BlockSpec((tm, sw), row), pl.BlockSpec((tm, sw), row),
                   pl.BlockSpec((nsub, 1, LANES), lambda i: (i, 0, 0)),
                   pl.BlockSpec((tm, d), row), pl.BlockSpec((tm, d), row)],
        out_shape=[jax.ShapeDtypeStruct((t, cw), BF16),
                   jax.ShapeDtypeStruct((t, sw), BF16), jax.ShapeDtypeStruct((t, sw), BF16),
                   jax.ShapeDtypeStruct((t, sw), BF16),
                   jax.ShapeDtypeStruct((t // ATTN_BK, 1, LANES), F32),
                   jax.ShapeDtypeStruct((t, d), BF16), jax.ShapeDtypeStruct((t, d), BF16)],
        scratch_shapes=[pltpu.VMEM((SUBLANES, LANES), F32)],
        compiler_params=_cparams("arbitrary"),
        name="inproj",
    )(x2, mod8, norm_g[None], wa, wq, wk, wv, wf, wg, bfp, tri, eq, ek, oneq, onek, onev)
    return outs


def _conv_kernel(halo_ref, cur_ref, w_ref, b_ref, g_ref, beta_ref, o_ref, xs_ref, ys_ref):
    ts = cur_ref.shape[0]
    first = pl.program_id(1) == 0
    halo = halo_ref[...].astype(F32)
    xs_ref[0:CONV_HALO, :] = jnp.where(first, 0.0, halo)
    xs_ref[CONV_HALO:, :] = cur_ref[...].astype(F32)
    acc = jnp.zeros(o_ref.shape, F32) + b_ref[...]
    base = CONV_HALO - (CONV_KERNEL - 1)
    for rho in range(SUBLANES):
        offs = [o for o in range(base, base + CONV_KERNEL) if o % SUBLANES == rho]
        if not offs:
            continue
        rows = offs[-1] + ts - rho
        ys_ref[0:rows, :] = xs_ref[rho:rho + rows, :]
        for o in offs:
            lo = o - rho
            acc = acc + w_ref[o - base:o - base + 1, :] * ys_ref[lo:lo + ts, :]
    mu = jnp.mean(acc, axis=-1, keepdims=True)
    cen = acc - mu
    var = jnp.mean(cen * cen, axis=-1, keepdims=True)
    y = cen * lax.rsqrt(var + LN_EPS) * g_ref[...] + beta_ref[...]
    o_ref[...] = (y * jax.nn.sigmoid(y)).astype(BF16)


def _conv(a, conv_w, conv_b, ln_g, ln_b, bsz, seq):
    t, cw = a.shape
    ts = CONV_TILE
    nt = seq // ts
    hpt = ts // CONV_HALO
    vec = lambda: pl.BlockSpec((1, cw), lambda b, i: (0, 0))
    return pl.pallas_call(
        _conv_kernel,
        grid=(bsz, nt),
        in_specs=[pl.BlockSpec((CONV_HALO, cw), lambda b, i: (jnp.maximum((b * nt + i) * hpt - 1, 0), 0)),
                  pl.BlockSpec((ts, cw), lambda b, i: (b * nt + i, 0)),
                  pl.BlockSpec((CONV_KERNEL, cw), lambda b, i: (0, 0)),
                  vec(), vec(), vec()],
        out_specs=pl.BlockSpec((ts, cw), lambda b, i: (b * nt + i, 0)),
        out_shape=jax.ShapeDtypeStruct((t, cw), BF16),
        scratch_shapes=[pltpu.VMEM((CONV_HALO + ts, cw), F32), pltpu.VMEM((CONV_HALO + ts, cw), F32)],
        compiler_params=_cparams("arbitrary", "arbitrary"),
        name="conv",
    )(a, a, conv_w, conv_b[None], ln_g[None], ln_b[None])


def _attn_kernel(q_ref, k_ref, v_ref, tot_ref, o_ref, m_scr, acc_scr):
    bq = q_ref.shape[0]
    bk = ATTN_BK
    ratio = bq // bk
    qi = pl.program_id(2)
    nt = (((1,), (1,)), ((), ()))
    heads = range(HEADS_PER_STEP)
    zero = jnp.zeros((1, LANES), F32)

    def scores(hh, kb, row0):
        lanes = slice(hh * LANES, (hh + 1) * LANES)
        start = pl.multiple_of(kb * bk, bk)
        s = lax.dot_general(q_ref[row0:, lanes], k_ref[pl.ds(start, bk), lanes], nt, preferred_element_type=F32)
        return s, v_ref[pl.ds(start, bk), lanes]

    def update(hh, s, v, delta, row0):
        m_prev = m_scr[hh, row0:, :]
        m_new = jnp.maximum(m_prev, jnp.max(s, axis=1, keepdims=True) + delta)
        shift = m_new - delta
        p = jnp.exp2(s - shift[:, :1])
        alpha = jnp.exp2(m_prev - m_new)
        pv = jnp.dot(p.astype(BF16), v, preferred_element_type=F32)
        acc_scr[hh, row0:, :] = alpha * acc_scr[hh, row0:, :] + pv
        m_scr[hh, row0:, :] = m_new

    for hh in heads:
        delta = zero
        for sb in range(ratio):
            kb = qi * ratio + sb
            row0 = sb * bk
            s, v = scores(hh, kb, row0)
            row = lax.broadcasted_iota(I32, s.shape, 0)
            col = lax.broadcasted_iota(I32, s.shape, 1)
            s = jnp.where(col <= row, s, -jnp.inf)
            if sb == 0:
                m0 = jnp.max(s, axis=1, keepdims=True)
                p = jnp.exp2(s - m0)
                m_scr[hh] = jnp.broadcast_to(m0, m_scr.shape[1:])
                acc_scr[hh] = jnp.dot(p.astype(BF16), v, preferred_element_type=F32)
            else:
                update(hh, s, v, delta, row0)
            delta = delta - tot_ref[hh, pl.ds(kb, 1), :]

    def body(step, deltas):
        kb = qi * ratio - 1 - step
        new_deltas = []
        for hh in heads:
            delta = deltas[hh] + tot_ref[hh, pl.ds(kb, 1), :]
            s, v = scores(hh, kb, 0)
            update(hh, s, v, delta, 0)
            new_deltas.append(delta)
        return tuple(new_deltas)

    lax.fori_loop(0, qi * ratio, body, tuple(zero for _ in heads))
    for hh in heads:
        acc = acc_scr[hh]
        lane = lax.broadcasted_iota(I32, acc.shape, 1)
        out = jnp.where(lane < HEAD_DIM, acc / acc[:, HEAD_DIM:HEAD_DIM + 1], 0.0)
        o_ref[:, hh * LANES:(hh + 1) * LANES] = out.astype(BF16)


def _attention(q, k, v, tot, bsz, seq):
    t, sw = q.shape
    bq = ATTN_BQ
    nq = seq // bq
    nk = seq // ATTN_BK
    hw = HEADS_PER_STEP * LANES
    tot = tot.reshape(bsz, nk, LANES)[:, :, :N_HEADS].transpose(0, 2, 1)
    tot = jnp.broadcast_to(tot[..., None], (bsz, N_HEADS, nk, LANES))
    k3 = k.reshape(bsz, seq, sw)
    v3 = v.reshape(bsz, seq, sw)
    return pl.pallas_call(
        _attn_kernel,
        grid=(bsz, N_HEADS // HEADS_PER_STEP, nq),
        in_specs=[pl.BlockSpec((bq, hw), lambda b, h, i: (b * nq + i, h)),
                  pl.BlockSpec((None, seq, hw), lambda b, h, i: (b, 0, h), pipeline_mode=pl.Buffered(1)),
                  pl.BlockSpec((None, seq, hw), lambda b, h, i: (b, 0, h), pipeline_mode=pl.Buffered(1)),
                  pl.BlockSpec((None, HEADS_PER_STEP, nk, LANES), lambda b, h, i: (b, h, 0, 0))],
        out_specs=pl.BlockSpec((bq, hw), lambda b, h, i: (b * nq + i, h)),
        out_shape=jax.ShapeDtypeStruct((t, sw), BF16),
        scratch_shapes=[pltpu.VMEM((HEADS_PER_STEP, bq, LANES), F32),
                        pltpu.VMEM((HEADS_PER_STEP, bq, LANES), F32)],
        compiler_params=_cparams("arbitrary", "arbitrary", "arbitrary"),
        name="attn",
    )(q, k3, v3, tot)


def _pack_bf16_pair(lo, hi):
    lo_bits = pltpu.bitcast(lo.astype(BF16).astype(F32), U32)
    hi_bits = pltpu.bitcast(hi.astype(BF16).astype(F32), U32)
    return (hi_bits & jnp.uint32(0xFFFF0000)) | (lo_bits >> 16)


def _unpack_bf16_pair(w):
    lo = pltpu.bitcast(w << 16, F32)
    hi = pltpu.bitcast(w & jnp.uint32(0xFFFF0000), F32)
    return jnp.concatenate([lo, hi], axis=1)


def _merge_kernel(x_ref, act_ref, attn_ref, sgc_ref, sga_ref, mod_ref, g2_ref, wco_ref, wao_ref, wout_ref,
                  wr_ref, br_ref, x1_ref, h2_ref, ids_ref, gates_ref):
    conv_out = jnp.dot(act_ref[...], wco_ref[...], preferred_element_type=F32)
    attn_out = jnp.dot(attn_ref[...], wao_ref[...], preferred_element_type=F32)
    mixed = sgc_ref[...].astype(F32) * conv_out + sga_ref[...].astype(F32) * attn_out
    upd = jnp.dot(mixed.astype(BF16), wout_ref[...], preferred_element_type=F32)
    x1 = x_ref[...] + mod_ref[2:3, :] * upd
    x1_ref[...] = x1
    h2 = _rms(x1) * g2_ref[...] * (1.0 + mod_ref[4:5, :]) + mod_ref[3:4, :]
    h2_hi = h2.astype(BF16)
    h2_ref[...] = h2_hi
    h2_lo = (h2 - h2_hi.astype(F32)).astype(BF16)
    h2_cat = jnp.concatenate([h2_hi, h2_lo, h2_hi], axis=1)
    logits = jnp.dot(h2_cat, wr_ref[...], preferred_element_type=F32) + br_ref[...]
    lane = lax.broadcasted_iota(I32, logits.shape, 1)
    work = jnp.where(lane < N_EXPERTS, logits, -jnp.inf)
    ids = jnp.zeros(logits.shape, I32)
    vals = []
    for kk in range(TOP_K):
        mx = jnp.max(work, axis=1, keepdims=True)
        idx = jnp.min(jnp.where(work == mx, lane, LANES), axis=1, keepdims=True)
        ids = jnp.where(lane == kk, idx, ids)
        vals.append(mx)
        work = jnp.where(lane == idx, -jnp.inf, work)
    exps = [jnp.exp(vv - vals[0]) for vv in vals]
    den = exps[0] + exps[1] + exps[2] + exps[3]
    gates = jnp.zeros(logits.shape, F32)
    for kk in range(TOP_K):
        gates = jnp.where(lane == kk, exps[kk] / den, gates)
    ids_ref[...] = ids
    gates_ref[...] = gates


def _merge(x2, act, attn, sgc, sga, mod8, norm2_g, w_conv_out, w_attn_out, w_out, w_router, b_router, seq):
    t, d = x2.shape
    tm = MERGE_TILE
    cw = act.shape[1]
    sw = attn.shape[1]
    wco = w_conv_out.astype(BF16)
    wao = jnp.pad(w_attn_out.reshape(N_HEADS, HEAD_DIM, d), ((0, 0), (0, LANES - HEAD_DIM), (0, 0)))
    wao = wao.reshape(sw, d).astype(BF16)
    wout = w_out.astype(BF16)
    wr = jnp.pad(w_router, ((0, 0), (0, LANES - N_EXPERTS)))
    wr_hi = wr.astype(BF16)
    wr_lo = (wr - wr_hi.astype(F32)).astype(BF16)
    wr = jnp.concatenate([wr_hi, wr_hi, wr_lo], axis=0)
    br = jnp.pad(b_router, (0, LANES - N_EXPERTS))[None]
    tiles_per_seq = seq // tm
    row = lambda i: (i, 0)
    return pl.pallas_call(
        _merge_kernel,
        grid=(t // tm,),
        in_specs=[pl.BlockSpec((tm, d), row), pl.BlockSpec((tm, cw), row), pl.BlockSpec((tm, sw), row),
                  pl.BlockSpec((tm, d), row), pl.BlockSpec((tm, d), row),
                  pl.BlockSpec((None, SUBLANES, d), lambda i: (i // tiles_per_seq, 0, 0)),
                  _resident((1, d)), _resident(wco.shape), _resident(wao.shape), _resident(wout.shape),
                  _resident(wr.shape), _resident(br.shape)],
        out_specs=[pl.BlockSpec((tm, d), row), pl.BlockSpec((tm, d), row),
                   pl.BlockSpec((tm, LANES), row), pl.BlockSpec((tm, LANES), row)],
        out_shape=[jax.ShapeDtypeStruct((t, d), F32), jax.ShapeDtypeStruct((t, d), BF16),
                   jax.ShapeDtypeStruct((t, LANES), I32), jax.ShapeDtypeStruct((t, LANES), F32)],
        compiler_params=_cparams("arbitrary"),
        name="merge",
    )(x2, act, attn, sgc, sga, mod8, norm2_g[None], wco, wao, wout, wr, br)


def _expert_onehots(ids):
    lane = lax.broadcasted_iota(I32, ids.shape, 1)
    return lane, [lane == ids[:, kk:kk + 1] for kk in range(TOP_K)]


def _route_kernel(ids_ref, gates_ref, tri_ref, utri_ref, slot_ref, slott_ref, gatest_ref, meta_ref,
                  cnt_ref, carry_scr):
    @pl.when(pl.program_id(0) == 0)
    def _():
        carry_scr[...] = jnp.zeros(carry_scr.shape, F32)

    lane, hots = _expert_onehots(ids_ref[...])
    multi = jnp.zeros(lane.shape, F32)
    for hot in hots:
        multi = multi + jnp.where(hot, 1.0, 0.0)
    carry = carry_scr[0:1, :]
    before = jnp.dot(tri_ref[...], multi.astype(BF16), preferred_element_type=F32)
    units = jnp.ceil(jnp.sum(multi, axis=0, keepdims=True) * (1.0 / CHUNK))
    units8 = jnp.broadcast_to(units, (SUBLANES, LANES)).astype(BF16)
    lstart = jnp.dot(units8, utri_ref[...], preferred_element_type=F32)[0:1, :] * CHUNK
    csize = units * CHUNK
    slot = jnp.zeros(lane.shape, F32)
    for kk, hot in enumerate(hots):
        sl = jnp.sum(jnp.where(hot, lstart + before, 0.0), axis=1, keepdims=True)
        slot = jnp.where(lane == kk, sl, slot)
    slot_ref[...] = slot.astype(I32)
    slott_ref[...] = slot.T[0:SUBLANES, :].astype(I32)
    gatest_ref[...] = gates_ref[...].T[0:SUBLANES, :]
    row = lax.broadcasted_iota(I32, (SUBLANES, LANES), 0)
    meta = jnp.where(row == 0, csize, jnp.where(row == 1, lstart, jnp.where(row == 2, carry, 0.0)))
    meta_ref[0] = meta
    carry = carry + csize
    carry_scr[...] = jnp.broadcast_to(carry, carry_scr.shape)
    cnt_ref[...] = jnp.broadcast_to(carry, cnt_ref.shape)


def _route(ids, gates):
    t = ids.shape[0]
    tm = TOKEN_TILE
    nt = t // tm
    r = jnp.arange(tm)
    tri = (r[None, :] < r[:, None]).astype(BF16)
    e = jnp.arange(LANES)
    utri = (e[:, None] < e[None, :]).astype(BF16)
    tile = pl.BlockSpec((tm, LANES), lambda i: (i, 0))
    return pl.pallas_call(
        _route_kernel,
        grid=(nt,),
        in_specs=[tile, tile, _resident(tri.shape), _resident(utri.shape)],
        out_specs=[tile, pl.BlockSpec((SUBLANES, tm), lambda i: (i, 0)),
                   pl.BlockSpec((SUBLANES, tm), lambda i: (i, 0)),
                   pl.BlockSpec((1, SUBLANES, LANES), lambda i: (i, 0, 0)),
                   pl.BlockSpec((SUBLANES, LANES), lambda i: (0, 0))],
        out_shape=[jax.ShapeDtypeStruct((t, LANES), I32), jax.ShapeDtypeStruct((nt * SUBLANES, tm), I32),
                   jax.ShapeDtypeStruct((nt * SUBLANES, tm), F32),
                   jax.ShapeDtypeStruct((nt, SUBLANES, LANES), F32),
                   jax.ShapeDtypeStruct((SUBLANES, LANES), F32)],
        scratch_shapes=[pltpu.VMEM((SUBLANES, LANES), F32)],
        compiler_params=_cparams("arbitrary"),
        name="route",
    )(ids, gates, tri, utri)


def _dest_kernel(cnt_ref, meta_ref, tri_ref, gd_ref, blk_ref):
    cnt = cnt_ref[...]
    padded = jnp.ceil(cnt * (1.0 / EXPERT_BLOCK)) * EXPERT_BLOCK
    pend = jnp.dot(padded, tri_ref[...], preferred_element_type=F32, precision=HIGHEST)
    pstart = (pend - padded)[0:1, :]

    meta = meta_ref[0]
    csize, lstart, seg_off = meta[0:1, :], meta[1:2, :], meta[2:3, :]
    lend = lstart + csize
    base = pstart + seg_off - lstart
    grow = lax.broadcasted_iota(I32, (SUBLANES, LANES), 0)
    glane = lax.broadcasted_iota(I32, (SUBLANES, LANES), 1)
    gpos = ((grow * LANES + glane) * CHUNK).astype(F32)
    owner = jnp.zeros(gpos.shape, I32)
    for e in range(N_EXPERTS):
        owner = owner + jnp.where(gpos >= lend[0:1, e:e + 1], 1, 0)
    gdst = gpos
    for e in range(N_EXPERTS):
        gdst = gdst + jnp.where(owner == e, base[0:1, e:e + 1], 0.0)
    n_groups = lend[0:1, N_EXPERTS - 1:N_EXPERTS] * (1.0 / CHUNK)
    last = jnp.logical_and(grow == SUBLANES - 1, glane == LANES - 1)
    gd_ref[0] = jnp.where(last, n_groups, gdst).astype(I32)

    @pl.when(pl.program_id(0) == 0)
    def _():
        shape = blk_ref.shape[1:]
        pos = (lax.broadcasted_iota(I32, shape, 0) * LANES + lax.broadcasted_iota(I32, shape, 1))
        pos = pos.astype(F32) * EXPERT_BLOCK
        be = jnp.zeros(shape, I32)
        for e in range(N_EXPERTS):
            be = be + jnp.where(pos >= pend[0:1, e:e + 1], 1, 0)
        blk_ref[0] = jnp.minimum(be, N_EXPERTS - 1)
        nact = pend[0:1, N_EXPERTS - 1:N_EXPERTS] * (1.0 / EXPERT_BLOCK)
        blk_ref[1] = jnp.broadcast_to(nact, shape).astype(I32)
        blk_ref[2] = jnp.broadcast_to(pend[0:1, :], shape).astype(I32)


def _dest(cnt, meta):
    nt = meta.shape[0]
    r = jnp.arange(LANES)
    tri = (r[:, None] <= r[None, :]).astype(F32)
    return pl.pallas_call(
        _dest_kernel,
        grid=(nt,),
        in_specs=[pl.BlockSpec((SUBLANES, LANES), lambda i: (0, 0)),
                  pl.BlockSpec((1, SUBLANES, LANES), lambda i: (i, 0, 0)),
                  pl.BlockSpec((LANES, LANES), lambda i: (0, 0))],
        out_specs=[pl.BlockSpec((1, SUBLANES, LANES), lambda i: (i, 0, 0)),
                   pl.BlockSpec((3, SUBLANES, LANES), lambda i: (0, 0, 0))],
        out_shape=[jax.ShapeDtypeStruct((nt, SUBLANES, LANES), I32),
                   jax.ShapeDtypeStruct((3, SUBLANES, LANES), I32)],
        compiler_params=_cparams("arbitrary"),
        name="dest",
    )(cnt, meta, tri)


def _row_copy(src, dst, sem):
    return pltpu.make_async_copy(src, dst, sem)


def _dispatch_kernel(pend_ref, na_ref, gd_ref, slott_ref, gatest_ref, h2_ref, xs_ref, zero_ref, sorted_scr,
                     ng_scr, sem):
    tm = h2_ref.shape[0]
    bm = zero_ref.shape[0]
    nb = xs_ref.shape[0] // bm
    step = pl.program_id(0)
    last = pl.num_programs(0) - 1
    cur = step % 2

    @pl.when(pl.program_id(0) == 0)
    def _():
        zero_ref[...] = jnp.zeros(zero_ref.shape, zero_ref.dtype)

        def zero_block(start):
            return _row_copy(zero_ref, xs_ref.at[pl.ds(pl.multiple_of(start, bm), bm)], sem.at[0])

        def nonempty(e):
            return pend_ref[e] > (pend_ref[e - 1] if e > 0 else 0)

        for e in range(N_EXPERTS):
            @pl.when(nonempty(e))
            def _():
                zero_block(pend_ref[e] - bm).start()

        def tail(b, carry):
            zero_block(b * bm).start()
            return carry

        lax.fori_loop(na_ref[0], nb, tail, 0)
        for e in range(N_EXPERTS):
            @pl.when(nonempty(e))
            def _():
                zero_block(0).wait()

        def tail_wait(b, carry):
            zero_block(0).wait()
            return carry

        lax.fori_loop(na_ref[0], nb, tail_wait, 0)

    srow = lax.broadcasted_iota(I32, (SLOT_ROWS, tm), 0)
    perm = jnp.zeros((SLOT_ROWS, tm), F32)
    gate = jnp.zeros((SLOT_ROWS, 1), F32)
    for kk in range(TOP_K):
        hit = srow == slott_ref[kk:kk + 1, :]
        perm = perm + jnp.where(hit, 1.0, 0.0)
        gate = gate + jnp.sum(jnp.where(hit, gatest_ref[kk:kk + 1, :], 0.0), axis=1, keepdims=True)
    perm = perm.astype(BF16)
    half = h2_ref.shape[1] // 2
    lo = jnp.dot(perm, h2_ref[:, :half], preferred_element_type=F32)
    hi = jnp.dot(perm, h2_ref[:, half:], preferred_element_type=F32)
    def group_copy(buf, j, d):
        src = sorted_scr.at[buf, pl.ds(pl.multiple_of(j * CHUNK, CHUNK), CHUNK)]
        return _row_copy(src, xs_ref.at[pl.ds(pl.multiple_of(d, CHUNK), CHUNK)], sem.at[buf])

    def drain(buf):
        def retire(j, carry):
            group_copy(buf, 0, 0).wait()
            return carry
        lax.fori_loop(0, ng_scr[buf], retire, 0)

    @pl.when(step >= 2)
    def _():
        drain(cur)

    sorted_scr[cur, :, :half] = _pack_bf16_pair(lo, hi)
    sorted_scr[cur, :, half:] = pltpu.bitcast(jnp.broadcast_to(gate, (SLOT_ROWS, LANES)), U32)

    n_groups = gd_ref[GROUP_TABLE - 1]
    ng_scr[cur] = n_groups

    def issue(j, carry):
        group_copy(cur, j, gd_ref[j]).start()
        return carry

    lax.fori_loop(0, n_groups, issue, 0)

    @pl.when(step == last)
    def _():
        @pl.when(last >= 1)
        def _():
            drain(1 - cur)
        drain(cur)


def _dispatch(pend, n_active, gd_flat, slott, gatest, h2b, n_rows):
    t, d = h2b.shape
    w = d // 2 + LANES
    tm = TOKEN_TILE
    grid_spec = pltpu.PrefetchScalarGridSpec(
        num_scalar_prefetch=2,
        grid=(t // tm,),
        in_specs=[pl.BlockSpec((GROUP_TABLE,), lambda i, pe, na: (i,), memory_space=pltpu.SMEM),
                  pl.BlockSpec((SUBLANES, tm), lambda i, pe, na: (i, 0)),
                  pl.BlockSpec((SUBLANES, tm), lambda i, pe, na: (i, 0)),
                  pl.BlockSpec((tm, d), lambda i, pe, na: (i, 0))],
        out_specs=pl.BlockSpec(memory_space=pl.ANY),
        scratch_shapes=[pltpu.VMEM((EXPERT_BLOCK, w), U32), pltpu.VMEM((2, SLOT_ROWS, w), U32),
                        pltpu.SMEM((2,), I32), pltpu.SemaphoreType.DMA((2,))],
    )
    return pl.pallas_call(
        _dispatch_kernel,
        grid_spec=grid_spec,
        out_shape=jax.ShapeDtypeStruct((n_rows, w), U32),
        compiler_params=_cparams("arbitrary"),
        name="dispatch",
    )(pend, n_active, gd_flat, slott, gatest, h2b)


def _expert_kernel(be_ref, na_ref, xs_ref, w1_ref, b1_ref, w2_ref, b2_ref, y_ref, w1b_ref, w2b_ref, *, d_expert):
    i = pl.program_id(0)
    active = i < na_ref[0]
    prev = be_ref[jnp.maximum(i - 1, 0)]
    new_expert = jnp.logical_or(i == 0, be_ref[i] != prev)
    half = y_ref.shape[1]

    @pl.when(jnp.logical_and(active, new_expert))
    def _():
        w1b_ref[...] = w1_ref[...].astype(BF16)
        w2b_ref[...] = w2_ref[...].astype(BF16)

    @pl.when(active)
    def _():
        x = _unpack_bf16_pair(xs_ref[:, :half]).astype(BF16)
        gate = pltpu.bitcast(xs_ref[:, half:], F32)[:, 0:1]
        hc = jnp.dot(x, w1b_ref[...], preferred_element_type=F32) + b1_ref[...]
        g = jnp.minimum(hc[:, :d_expert], SWIGLU_LIMIT)
        u = jnp.clip(hc[:, d_expert:], -SWIGLU_LIMIT, SWIGLU_LIMIT)
        hm = (u + 1.0) * (g * jax.nn.sigmoid(SWIGLU_ALPHA * g))
        y = (jnp.dot(hm.astype(BF16), w2b_ref[...], preferred_element_type=F32) + b2_ref[...]) * gate
        y_ref[...] = _pack_bf16_pair(y[:, :half], y[:, half:])

    @pl.when(jnp.logical_not(active))
    def _():
        y_ref[...] = jnp.zeros(y_ref.shape, U32)


def _experts(block_e, n_active, xs, w1, b1, w2, b2):
    n_rows, w = xs.shape
    n_exp, d, two_de = w1.shape
    de = two_de // 2
    bm = EXPERT_BLOCK
    nb = n_rows // bm

    def blk(i, be, na):
        return jnp.maximum(jnp.minimum(i, na[0] - 1), 0)

    grid_spec = pltpu.PrefetchScalarGridSpec(
        num_scalar_prefetch=2,
        grid=(nb,),
        in_specs=[pl.BlockSpec((bm, w), lambda i, be, na: (blk(i, be, na), 0)),
                  pl.BlockSpec((None, d, two_de), lambda i, be, na: (be[blk(i, be, na)], 0, 0)),
                  pl.BlockSpec((None, 1, two_de), lambda i, be, na: (be[blk(i, be, na)], 0, 0)),
                  pl.BlockSpec((None, de, d), lambda i, be, na: (be[blk(i, be, na)], 0, 0)),
                  pl.BlockSpec((None, 1, d), lambda i, be, na: (be[blk(i, be, na)], 0, 0))],
        out_specs=pl.BlockSpec((bm, d // 2), lambda i, be, na: (i, 0)),
        scratch_shapes=[pltpu.VMEM((d, two_de), BF16), pltpu.VMEM((de, d), BF16)],
    )
    return pl.pallas_call(
        functools.partial(_expert_kernel, d_expert=de),
        grid_spec=grid_spec,
        out_shape=jax.ShapeDtypeStruct((n_rows, d // 2), U32),
        compiler_params=_cparams("arbitrary"),
        name="expert",
    )(block_e, n_active, xs, w1, b1[:, None, :], w2, b2[:, None, :])


def _combine_kernel(gd_ref, gd_next_ref, slot_ref, x1_ref, mod_ref, fg_ref, yb_ref, o_ref, sorted_scr, sem):
    tm = x1_ref.shape[0]
    step = pl.program_id(0)
    cur = step % 2

    def group_copy(buf, j, d):
        dst = sorted_scr.at[buf, pl.ds(pl.multiple_of(j * CHUNK, CHUNK), CHUNK)]
        return _row_copy(yb_ref.at[pl.ds(pl.multiple_of(d, CHUNK), CHUNK)], dst, sem.at[buf])

    def fetch(table_ref, buf):
        def issue(j, carry):
            group_copy(buf, j, table_ref[j]).start()
            return carry
        lax.fori_loop(0, table_ref[GROUP_TABLE - 1], issue, 0)

    @pl.when(step == 0)
    def _():
        sorted_scr[...] = jnp.zeros(sorted_scr.shape, sorted_scr.dtype)
        fetch(gd_ref, 0)

    @pl.when(step + 1 < pl.num_programs(0))
    def _():
        fetch(gd_next_ref, 1 - cur)

    def retire(j, carry):
        group_copy(cur, 0, 0).wait()
        return carry

    lax.fori_loop(0, gd_ref[GROUP_TABLE - 1], retire, 0)

    slot = slot_ref[...]
    scol = lax.broadcasted_iota(I32, (tm, SLOT_ROWS), 1)
    pick = jnp.zeros((tm, SLOT_ROWS), F32)
    for kk in range(TOP_K):
        pick = pick + jnp.where(scol == slot[:, kk:kk + 1], 1.0, 0.0)
    pick = pick.astype(BF16)
    words = sorted_scr[cur]
    lo = pltpu.bitcast(words << 16, F32).astype(BF16)
    hi = pltpu.bitcast(words & jnp.uint32(0xFFFF0000), F32).astype(BF16)
    y = jnp.concatenate([jnp.dot(pick, lo, preferred_element_type=F32),
                         jnp.dot(pick, hi, preferred_element_type=F32)], axis=1)
    x2 = x1_ref[...] + mod_ref[5:6, :] * y
    o_ref[...] = _rms(x2) * fg_ref[...]


def _combine(gd_flat, slot, x1, mod8, final_g, yb, seq):
    t, d = x1.shape
    tm = TOKEN_TILE
    nt = t // tm
    tiles_per_seq = seq // tm
    return pl.pallas_call(
        _combine_kernel,
        grid=(nt,),
        in_specs=[pl.BlockSpec((GROUP_TABLE,), lambda i: (i,), memory_space=pltpu.SMEM),
                  pl.BlockSpec((GROUP_TABLE,), lambda i: (jnp.minimum(i + 1, nt - 1),), memory_space=pltpu.SMEM),
                  pl.BlockSpec((tm, LANES), lambda i: (i, 0)),
                  pl.BlockSpec((tm, d), lambda i: (i, 0)),
                  pl.BlockSpec((None, SUBLANES, d), lambda i: (i // tiles_per_seq, 0, 0)),
                  pl.BlockSpec((1, d), lambda i: (0, 0)),
                  pl.BlockSpec(memory_space=pl.ANY)],
        out_specs=pl.BlockSpec((tm, d), lambda i: (i, 0)),
        out_shape=jax.ShapeDtypeStruct((t, d), F32),
        scratch_shapes=[pltpu.VMEM((2, SLOT_ROWS, d // 2), U32), pltpu.SemaphoreType.DMA((2,))],
        compiler_params=_cparams("arbitrary"),
        name="combine",
    )(gd_flat, gd_flat, slot, x1, mod8, final_g[None], yb)


def kernel(x, c, w_ada, b_ada, norm1_g, w_in, b_forget, conv_w, conv_b, conv_ln_g, conv_ln_b, w_conv_out,
           w_attn_out, w_out, norm2_g, w_router, b_router, w_exp_in, b_exp_in, w_exp_out, b_exp_out, final_g):
    bsz, seq, d = x.shape
    t = bsz * seq
    depth = w_ada.shape[0]
    xf = x.reshape(t, d)
    n_rows = t * TOP_K + (t // TOKEN_TILE) * N_EXPERTS * (CHUNK - 1) + N_EXPERTS * (EXPERT_BLOCK - CHUNK)
    n_rows = -(-n_rows // EXPERT_BLOCK) * EXPERT_BLOCK
    n_blocks = n_rows // EXPERT_BLOCK
    for l in range(depth):
        mod8 = _mod(c, w_ada[l], b_ada[l])
        a, q, k, v, tot, sgc, sga = _inproj(xf, mod8, norm1_g[l], w_in[l], b_forget[l], seq)
        act = _conv(a, conv_w[l], conv_b[l], conv_ln_g[l], conv_ln_b[l], bsz, seq)
        attn = _attention(q, k, v, tot, bsz, seq)
        x1, h2b, ids, gates = _merge(xf, act, attn, sgc, sga, mod8, norm2_g[l], w_conv_out[l], w_attn_out[l],
                                     w_out[l], w_router[l], b_router[l], seq)
        slot, slott, gatest, meta, cnt = _route(ids, gates)
        gd, blk = _dest(cnt, meta)
        gd_flat = gd.reshape(-1)
        block_e = blk[0].reshape(-1)[:n_blocks]
        n_active = blk[1, 0, :1]
        pend = blk[2, 0, :N_EXPERTS]
        xs = _dispatch(pend, n_active, gd_flat, slott, gatest, h2b, n_rows)
        yb = _experts(block_e, n_active, xs, w_exp_in[l], b_exp_in[l], w_exp_out[l], b_exp_out[l])
        assert depth == 1
        xf = _combine(gd_flat, slot, x1, mod8, final_g, yb, seq)
    return xf.reshape(bsz, seq, d)
```

```python
import functools

import jax
import jax.numpy as jnp
from jax import lax
from jax.experimental import pallas as pl
from jax.experimental.pallas import tpu as pltpu

F32 = jnp.float32
BF16 = jnp.bfloat16
I32 = jnp.int32
U32 = jnp.uint32
HIGHEST = lax.Precision.HIGHEST

LANES = 128
SUBLANES = 8
VMEM_LIMIT_BYTES = 56 * 1024 * 1024

N_HEADS = 8
HEAD_DIM = 64
CONV_KERNEL = 31
CONV_HALO = 32
N_EXPERTS = 32
TOP_K = 4
SWIGLU_LIMIT = 7.0
SWIGLU_ALPHA = 1.702
RMS_EPS = 1e-5
LN_EPS = 1e-5
N_MOD = 6
N_SPLIT = 3
LOG2_E = 1.4426950408889634
HEADS_PER_STEP = 2

INPROJ_TILE = 512
ATTN_BQ = 2048
ATTN_BK = 512
MERGE_TILE = 512
CONV_TILE = 256
TOKEN_TILE = 256
EXPERT_BLOCK = 1024
EXPERT_HIDDEN_SLICES = 4
CHUNK = SUBLANES
SLOT_ROWS = 1280
GROUP_TABLE = SUBLANES * LANES
assert SLOT_ROWS >= TOKEN_TILE * TOP_K + N_EXPERTS * (CHUNK - 1) and SLOT_ROWS // CHUNK < GROUP_TABLE


def _cparams(*sem):
    return pltpu.CompilerParams(dimension_semantics=sem, vmem_limit_bytes=VMEM_LIMIT_BYTES)


def _resident(shape):
    nd = len(shape)
    return pl.BlockSpec(shape, lambda *_: (0,) * nd, pipeline_mode=pl.Buffered(1))


def _split3(x):
    p1 = x.astype(BF16)
    r1 = x - p1.astype(F32)
    p2 = r1.astype(BF16)
    p3 = (r1 - p2.astype(F32)).astype(BF16)
    return jnp.concatenate([p1, p2, p3], axis=1)


def _rms(x):
    return x * lax.rsqrt(jnp.mean(x * x, axis=-1, keepdims=True) + RMS_EPS)


def _mod_kernel(c_ref, w_ref, b_ref, o_ref):
    c = c_ref[...]
    sc = c * jax.nn.sigmoid(c)
    o_ref[...] = jnp.dot(sc, w_ref[...], preferred_element_type=F32, precision=HIGHEST) + b_ref[...]


def _mod(c, w_ada, b_ada):
    bsz, d = c.shape
    n = w_ada.shape[1]
    tn = 1024
    cp = jnp.zeros((SUBLANES, d), F32).at[:bsz].set(c)
    out = pl.pallas_call(
        _mod_kernel,
        grid=(n // tn,),
        in_specs=[pl.BlockSpec((SUBLANES, d), lambda j: (0, 0)),
                  pl.BlockSpec((d, tn), lambda j: (0, j)),
                  pl.BlockSpec((1, tn), lambda j: (0, j))],
        out_specs=pl.BlockSpec((SUBLANES, tn), lambda j: (0, j)),
        out_shape=jax.ShapeDtypeStruct((SUBLANES, n), F32),
        compiler_params=_cparams("arbitrary"),
        name="mod",
    )(cp, w_ada, b_ada[None])
    mod = out[:bsz].reshape(bsz, N_MOD, d)
    return jnp.pad(mod, ((0, 0), (0, SUBLANES - N_MOD), (0, 0)))


def _inproj_kernel(x_ref, mod_ref, g_ref, wa_ref, wq_ref, wk_ref, wv_ref, wf_ref, wg_ref, bf_ref,
                   tri_ref, eq_ref, ek_ref, oneq_ref, onek_ref, onev_ref,
                   a_ref, q_ref, k_ref, v_ref, tot_ref, sgc_ref, sga_ref, carry_scr, *, conv_width, d_model):
    x = x_ref[...]
    h = _rms(x) * g_ref[...] * (1.0 + mod_ref[1:2, :]) + mod_ref[0:1, :]
    hb = h.astype(BF16)

    u = jnp.dot(hb, wa_ref[...], preferred_element_type=F32)
    a_ref[...] = (u[:, :conv_width] * jax.nn.sigmoid(u[:, conv_width:])).astype(BF16)

    fl = jnp.dot(hb, wf_ref[...], preferred_element_type=F32) + bf_ref[...]
    lane = lax.broadcasted_iota(I32, fl.shape, 1)
    log_f = -(jnp.maximum(-fl, 0.0) + jnp.log1p(jnp.exp(-jnp.abs(fl))))
    log_f = jnp.where(lane < N_HEADS, log_f, 0.0)
    cs = jnp.dot(tri_ref[...], _split3(log_f), preferred_element_type=F32)
    cum = (cs[:, :LANES] + cs[:, LANES:2 * LANES] + cs[:, 2 * LANES:]) * LOG2_E
    tm = cum.shape[0]

    pieces = []
    for sb in range(tm // ATTN_BK):
        blk = cum[sb * ATTN_BK:(sb + 1) * ATTN_BK, :]
        if sb > 0:
            blk = blk - cum[sb * ATTN_BK - 1:sb * ATTN_BK, :]
        tot_ref[sb] = blk[ATTN_BK - 1:, :]
        pieces.append(blk)
    cum_k = pieces[0] if len(pieces) == 1 else jnp.concatenate(pieces, axis=0)

    @pl.when(pl.program_id(0) % (ATTN_BQ // tm) == 0)
    def _():
        carry_scr[...] = jnp.zeros(carry_scr.shape, F32)

    cum_q = cum + carry_scr[0:1, :]
    carry_scr[...] = jnp.broadcast_to(cum_q[tm - 1:, :], carry_scr.shape)

    q = jnp.dot(hb, wq_ref[...], preferred_element_type=F32) * (HEAD_DIM ** -0.5 * LOG2_E)
    q = q + jnp.dot(_split3(cum_q), eq_ref[...], preferred_element_type=F32) + oneq_ref[...]
    q_ref[...] = q.astype(BF16)
    k = jnp.dot(hb, wk_ref[...], preferred_element_type=F32)
    k = k + jnp.dot(_split3(cum_k), ek_ref[...], preferred_element_type=F32) + onek_ref[...]
    k_ref[...] = k.astype(BF16)
    v_ref[...] = (jnp.dot(hb, wv_ref[...], preferred_element_type=F32) + onev_ref[...]).astype(BF16)

    gts = jnp.dot(hb, wg_ref[...], preferred_element_type=F32)
    sgc_ref[...] = jax.nn.sigmoid(gts[:, :d_model]).astype(BF16)
    sga_ref[...] = jax.nn.sigmoid(gts[:, d_model:]).astype(BF16)


def _spread_heads(w):
    d = w.shape[0]
    w = w.reshape(d, N_HEADS, HEAD_DIM)
    w = jnp.pad(w, ((0, 0), (0, 0), (0, LANES - HEAD_DIM)))
    return w.reshape(d, N_HEADS * LANES)


def _aug_constants():
    rows = jnp.arange(N_SPLIT * LANES)
    piece, head = rows // LANES, rows % LANES
    cols = jnp.arange(N_HEADS * LANES)
    chead, coff = cols // LANES, cols % LANES - HEAD_DIM
    valid = (head[:, None] < N_HEADS) & (head[:, None] == chead[None, :])
    eq = jnp.where(valid & (coff[None, :] == piece[:, None]), 1.0, 0.0).astype(BF16)
    ek = jnp.where(valid & (coff[None, :] == piece[:, None] + N_SPLIT), -1.0, 0.0).astype(BF16)
    oneq = jnp.where((coff >= N_SPLIT) & (coff < 2 * N_SPLIT), 1.0, 0.0).astype(F32)[None]
    onek = jnp.where((coff >= 0) & (coff < N_SPLIT), 1.0, 0.0).astype(F32)[None]
    onev = jnp.where(coff == 0, 1.0, 0.0).astype(F32)[None]
    return eq, ek, oneq, onek, onev


def _inproj(x2, mod8, norm_g, w_in, b_forget, seq):
    t, d = x2.shape
    tm = INPROJ_TILE
    nsub = tm // ATTN_BK
    cw = d // 2
    aw = N_HEADS * HEAD_DIM
    o0 = 2 * cw
    wa = w_in[:, :o0].astype(BF16)
    wq = _spread_heads(w_in[:, o0:o0 + aw]).astype(BF16)
    wk = _spread_heads(w_in[:, o0 + aw:o0 + 2 * aw]).astype(BF16)
    wv = _spread_heads(w_in[:, o0 + 2 * aw:o0 + 3 * aw]).astype(BF16)
    o1 = o0 + 3 * aw
    wf = jnp.pad(w_in[:, o1:o1 + N_HEADS], ((0, 0), (0, LANES - N_HEADS))).astype(BF16)
    bfp = jnp.pad(b_forget, (0, LANES - N_HEADS))[None]
    wg = w_in[:, o1 + N_HEADS:].astype(BF16)
    r = jnp.arange(tm)
    tri = (r[None, :] <= r[:, None]).astype(BF16)
    eq, ek, oneq, onek, onev = _aug_constants()
    sw = N_HEADS * LANES
    tiles_per_seq = seq // tm
    row = lambda i: (i, 0)
    outs = pl.pallas_call(
        functools.partial(_inproj_kernel, conv_width=cw, d_model=d),
        grid=(t // tm,),
        in_specs=[pl.BlockSpec((tm, d), row),
                  pl.BlockSpec((None, SUBLANES, d), lambda i: (i // tiles_per_seq, 0, 0)),
                  _resident((1, d)),
                  _resident(wa.shape), _resident(wq.shape), _resident(wk.shape), _resident(wv.shape),
                  _resident(wf.shape), _resident(wg.shape), _resident(bfp.shape),
                  _resident(tri.shape), _resident(eq.shape), _resident(ek.shape),
                  _resident(oneq.shape), _resident(onek.shape), _resident(onev.shape)],
        out_specs=[pl.BlockSpec((tm, cw), row),
                   pl.BlockSpec((tm, sw), row), pl.BlockSpec((tm, sw), row), pl.BlockSpec((tm, sw), row),
                   pl.BlockSpec((nsub, 1, LANES), lambda i: (i, 0, 0)),
                   pl.BlockSpec((tm, d), row), pl.BlockSpec((tm, d), row)],
        out_shape=[jax.ShapeDtypeStruct((t, cw), BF16),
                   jax.ShapeDtypeStruct((t, sw), BF16), jax.ShapeDtypeStruct((t, sw), BF16),
                   jax.ShapeDtypeStruct((t, sw), BF16),
                   jax.ShapeDtypeStruct((t // ATTN_BK, 1, LANES), F32),
                   jax.ShapeDtypeStruct((t, d), BF16), jax.ShapeDtypeStruct((t, d), BF16)],
        scratch_shapes=[pltpu.VMEM((SUBLANES, LANES), F32)],
        compiler_params=_cparams("arbitrary"),
        name="inproj",
    )(x2, mod8, norm_g[None], wa, wq, wk, wv, wf, wg, bfp, tri, eq, ek, oneq, onek, onev)
    return outs


def _conv_kernel(halo_ref, cur_ref, w_ref, b_ref, g_ref, beta_ref, o_ref, xs_ref, ys_ref):
    ts = cur_ref.shape[0]
    first = pl.program_id(1) == 0
    halo = halo_ref[...].astype(F32)
    xs_ref[0:CONV_HALO, :] = jnp.where(first, 0.0, halo)
    xs_ref[CONV_HALO:, :] = cur_ref[...].astype(F32)
    acc = jnp.zeros(o_ref.shape, F32) + b_ref[...]
    base = CONV_HALO - (CONV_KERNEL - 1)
    for rho in range(SUBLANES):
        offs = [o for o in range(base, base + CONV_KERNEL) if o % SUBLANES == rho]
        if not offs:
            continue
        rows = offs[-1] + ts - rho
        ys_ref[0:rows, :] = xs_ref[rho:rho + rows, :]
        for o in offs:
            lo = o - rho
            acc = acc + w_ref[o - base:o - base + 1, :] * ys_ref[lo:lo + ts, :]
    mu = jnp.mean(acc, axis=-1, keepdims=True)
    cen = acc - mu
    var = jnp.mean(cen * cen, axis=-1, keepdims=True)
    y = cen * lax.rsqrt(var + LN_EPS) * g_ref[...] + beta_ref[...]
    o_ref[...] = (y * jax.nn.sigmoid(y)).astype(BF16)


def _conv(a, conv_w, conv_b, ln_g, ln_b, bsz, seq):
    t, cw = a.shape
    ts = CONV_TILE
    nt = seq // ts
    hpt = ts // CONV_HALO
    vec = lambda: pl.BlockSpec((1, cw), lambda b, i: (0, 0))
    return pl.pallas_call(
        _conv_kernel,
        grid=(bsz, nt),
        in_specs=[pl.BlockSpec((CONV_HALO, cw), lambda b, i: (jnp.maximum((b * nt + i) * hpt - 1, 0), 0)),
                  pl.BlockSpec((ts, cw), lambda b, i: (b * nt + i, 0)),
                  pl.BlockSpec((CONV_KERNEL, cw), lambda b, i: (0, 0)),
                  vec(), vec(), vec()],
        out_specs=pl.BlockSpec((ts, cw), lambda b, i: (b * nt + i, 0)),
        out_shape=jax.ShapeDtypeStruct((t, cw), BF16),
        scratch_shapes=[pltpu.VMEM((CONV_HALO + ts, cw), F32), pltpu.VMEM((CONV_HALO + ts, cw), F32)],
        compiler_params=_cparams("arbitrary", "arbitrary"),
        name="conv",
    )(a, a, conv_w, conv_b[None], ln_g[None], ln_b[None])


def _attn_kernel(q_ref, k_ref, v_ref, tot_ref, o_ref, m_scr, acc_scr):
    bq = q_ref.shape[0]
    bk = ATTN_BK
    ratio = bq // bk
    qi = pl.program_id(2)
    nt = (((1,), (1,)), ((), ()))
    heads = range(HEADS_PER_STEP)
    zero = jnp.zeros((1, LANES), F32)

    def scores(hh, kb, row0):
        lanes = slice(hh * LANES, (hh + 1) * LANES)
        start = pl.multiple_of(kb * bk, bk)
        s = lax.dot_general(q_ref[row0:, lanes], k_ref[pl.ds(start, bk), lanes], nt, preferred_element_type=F32)
        return s, v_ref[pl.ds(start, bk), lanes]

    def update(hh, s, v, delta, row0):
        m_prev = m_scr[hh, row0:, :]
        m_new = jnp.maximum(m_prev, jnp.max(s, axis=1, keepdims=True) + delta)
        shift = m_new - delta
        p = jnp.exp2(s - shift[:, :1])
        alpha = jnp.exp2(m_prev - m_new)
        pv = jnp.dot(p.astype(BF16), v, preferred_element_type=F32)
        acc_scr[hh, row0:, :] = alpha * acc_scr[hh, row0:, :] + pv
        m_scr[hh, row0:, :] = m_new

    for hh in heads:
        delta = zero
        for sb in range(ratio):
            kb = qi * ratio + sb
            row0 = sb * bk
            s, v = scores(hh, kb, row0)
            row = lax.broadcasted_iota(I32, s.shape, 0)
            col = lax.broadcasted_iota(I32, s.shape, 1)
            s = jnp.where(col <= row, s, -jnp.inf)
            if sb == 0:
                m0 = jnp.max(s, axis=1, keepdims=True)
                p = jnp.exp2(s - m0)
                m_scr[hh] = jnp.broadcast_to(m0, m_scr.shape[1:])
                acc_scr[hh] = jnp.dot(p.astype(BF16), v, preferred_element_type=F32)
            else:
                update(hh, s, v, delta, row0)
            delta = delta - tot_ref[hh, pl.ds(kb, 1), :]

    def body(step, deltas):
        kb = qi * ratio - 1 - step
        new_deltas = []
        for hh in heads:
            delta = deltas[hh] + tot_ref[hh, pl.ds(kb, 1), :]
            s, v = scores(hh, kb, 0)
            update(hh, s, v, delta, 0)
            new_deltas.append(delta)
        return tuple(new_deltas)

    lax.fori_loop(0, qi * ratio, body, tuple(zero for _ in heads))
    for hh in heads:
        acc = acc_scr[hh]
        lane = lax.broadcasted_iota(I32, acc.shape, 1)
        out = jnp.where(lane < HEAD_DIM, acc / acc[:, HEAD_DIM:HEAD_DIM + 1], 0.0)
        o_ref[:, hh * LANES:(hh + 1) * LANES] = out.astype(BF16)


def _attention(q, k, v, tot, bsz, seq):
    t, sw = q.shape
    bq = ATTN_BQ
    nq = seq // bq
    nk = seq // ATTN_BK
    hw = HEADS_PER_STEP * LANES
    tot = tot.reshape(bsz, nk, LANES)[:, :, :N_HEADS].transpose(0, 2, 1)
    tot = jnp.broadcast_to(tot[..., None], (bsz, N_HEADS, nk, LANES))
    k3 = k.reshape(bsz, seq, sw)
    v3 = v.reshape(bsz, seq, sw)
    return pl.pallas_call(
        _attn_kernel,
        grid=(bsz, N_HEADS // HEADS_PER_STEP, nq),
        in_specs=[pl.BlockSpec((bq, hw), lambda b, h, i: (b * nq + i, h)),
                  pl.BlockSpec((None, seq, hw), lambda b, h, i: (b, 0, h), pipeline_mode=pl.Buffered(1)),
                  pl.BlockSpec((None, seq, hw), lambda b, h, i: (b, 0, h), pipeline_mode=pl.Buffered(1)),
                  pl.BlockSpec((None, HEADS_PER_STEP, nk, LANES), lambda b, h, i: (b, h, 0, 0))],
        out_specs=pl.BlockSpec((bq, hw), lambda b, h, i: (b * nq + i, h)),
        out_shape=jax.ShapeDtypeStruct((t, sw), BF16),
        scratch_shapes=[pltpu.VMEM((HEADS_PER_STEP, bq, LANES), F32),
                        pltpu.VMEM((HEADS_PER_STEP, bq, LANES), F32)],
        compiler_params=_cparams("arbitrary", "arbitrary", "arbitrary"),
        name="attn",
    )(q, k3, v3, tot)


def _pack_bf16_pair(lo, hi):
    lo_bits = pltpu.bitcast(lo.astype(BF16).astype(F32), U32)
    hi_bits = pltpu.bitcast(hi.astype(BF16).astype(F32), U32)
    return (hi_bits & jnp.uint32(0xFFFF0000)) | (lo_bits >> 16)


def _unpack_bf16_pair(w):
    lo = pltpu.bitcast(w << 16, F32)
    hi = pltpu.bitcast(w & jnp.uint32(0xFFFF0000), F32)
    return jnp.concatenate([lo, hi], axis=1)


def _merge_kernel(x_ref, act_ref, attn_ref, sgc_ref, sga_ref, mod_ref, g2_ref, wco_ref, wao_ref, wout_ref,
                  wr_ref, br_ref, x1_ref, h2_ref, ids_ref, gates_ref):
    conv_out = jnp.dot(act_ref[...], wco_ref[...], preferred_element_type=F32)
    attn_out = jnp.dot(attn_ref[...], wao_ref[...], preferred_element_type=F32)
    mixed = sgc_ref[...].astype(F32) * conv_out + sga_ref[...].astype(F32) * attn_out
    upd = jnp.dot(mixed.astype(BF16), wout_ref[...], preferred_element_type=F32)
    x1 = x_ref[...] + mod_ref[2:3, :] * upd
    x1_ref[...] = x1
    h2 = _rms(x1) * g2_ref[...] * (1.0 + mod_ref[4:5, :]) + mod_ref[3:4, :]
    h2_hi = h2.astype(BF16)
    h2_ref[...] = h2_hi
    h2_lo = (h2 - h2_hi.astype(F32)).astype(BF16)
    h2_cat = jnp.concatenate([h2_hi, h2_lo, h2_hi], axis=1)
    logits = jnp.dot(h2_cat, wr_ref[...], preferred_element_type=F32) + br_ref[...]
    lane = lax.broadcasted_iota(I32, logits.shape, 1)
    work = jnp.where(lane < N_EXPERTS, logits, -jnp.inf)
    ids = jnp.zeros(logits.shape, I32)
    vals = []
    for kk in range(TOP_K):
        mx = jnp.max(work, axis=1, keepdims=True)
        idx = jnp.min(jnp.where(work == mx, lane, LANES), axis=1, keepdims=True)
        ids = jnp.where(lane == kk, idx, ids)
        vals.append(mx)
        work = jnp.where(lane == idx, -jnp.inf, work)
    exps = [jnp.exp(vv - vals[0]) for vv in vals]
    den = exps[0] + exps[1] + exps[2] + exps[3]
    gates = jnp.zeros(logits.shape, F32)
    for kk in range(TOP_K):
        gates = jnp.where(lane == kk, exps[kk] / den, gates)
    ids_ref[...] = ids
    gates_ref[...] = gates


def _merge(x2, act, attn, sgc, sga, mod8, norm2_g, w_conv_out, w_attn_out, w_out, w_router, b_router, seq):
    t, d = x2.shape
    tm = MERGE_TILE
    cw = act.shape[1]
    sw = attn.shape[1]
    wco = w_conv_out.astype(BF16)
    wao = jnp.pad(w_attn_out.reshape(N_HEADS, HEAD_DIM, d), ((0, 0), (0, LANES - HEAD_DIM), (0, 0)))
    wao = wao.reshape(sw, d).astype(BF16)
    wout = w_out.astype(BF16)
    wr = jnp.pad(w_router, ((0, 0), (0, LANES - N_EXPERTS)))
    wr_hi = wr.astype(BF16)
    wr_lo = (wr - wr_hi.astype(F32)).astype(BF16)
    wr = jnp.concatenate([wr_hi, wr_hi, wr_lo], axis=0)
    br = jnp.pad(b_router, (0, LANES - N_EXPERTS))[None]
    tiles_per_seq = seq // tm
    row = lambda i: (i, 0)
    return pl.pallas_call(
        _merge_kernel,
        grid=(t // tm,),
        in_specs=[pl.BlockSpec((tm, d), row), pl.BlockSpec((tm, cw), row), pl.BlockSpec((tm, sw), row),
                  pl.BlockSpec((tm, d), row), pl.BlockSpec((tm, d), row),
                  pl.BlockSpec((None, SUBLANES, d), lambda i: (i // tiles_per_seq, 0, 0)),
                  _resident((1, d)), _resident(wco.shape), _resident(wao.shape), _resident(wout.shape),
                  _resident(wr.shape), _resident(br.shape)],
        out_specs=[pl.BlockSpec((tm, d), row), pl.BlockSpec((tm, d), row),
                   pl.BlockSpec((tm, LANES), row), pl.BlockSpec((tm, LANES), row)],
        out_shape=[jax.ShapeDtypeStruct((t, d), F32), jax.ShapeDtypeStruct((t, d), BF16),
                   jax.ShapeDtypeStruct((t, LANES), I32), jax.ShapeDtypeStruct((t, LANES), F32)],
        compiler_params=_cparams("arbitrary"),
        name="merge",
    )(x2, act, attn, sgc, sga, mod8, norm2_g[None], wco, wao, wout, wr, br)


def _expert_onehots(ids):
    lane = lax.broadcasted_iota(I32, ids.shape, 1)
    return lane, [lane == ids[:, kk:kk + 1] for kk in range(TOP_K)]


def _route_kernel(ids_ref, gates_ref, tri_ref, utri_ref, slot_ref, slott_ref, gatest_ref, meta_ref,
                  cnt_ref, carry_scr):
    @pl.when(pl.program_id(0) == 0)
    def _():
        carry_scr[...] = jnp.zeros(carry_scr.shape, F32)

    lane, hots = _expert_onehots(ids_ref[...])
    multi = jnp.zeros(lane.shape, F32)
    for hot in hots:
        multi = multi + jnp.where(hot, 1.0, 0.0)
    carry = carry_scr[0:1, :]
    before = jnp.dot(tri_ref[...], multi.astype(BF16), preferred_element_type=F32)
    units = jnp.ceil(jnp.sum(multi, axis=0, keepdims=True) * (1.0 / CHUNK))
    units8 = jnp.broadcast_to(units, (SUBLANES, LANES)).astype(BF16)
    lstart = jnp.dot(units8, utri_ref[...], preferred_element_type=F32)[0:1, :] * CHUNK
    csize = units * CHUNK
    slot = jnp.zeros(lane.shape, F32)
    for kk, hot in enumerate(hots):
        sl = jnp.sum(jnp.where(hot, lstart + before, 0.0), axis=1, keepdims=True)
        slot = jnp.where(lane == kk, sl, slot)
    slot_ref[...] = slot.astype(I32)
    slott_ref[...] = slot.T[0:SUBLANES, :].astype(I32)
    gatest_ref[...] = gates_ref[...].T[0:SUBLANES, :]
    row = lax.broadcasted_iota(I32, (SUBLANES, LANES), 0)
    meta = jnp.where(row == 0, csize, jnp.where(row == 1, lstart, jnp.where(row == 2, carry, 0.0)))
    meta_ref[0] = meta
    carry = carry + csize
    carry_scr[...] = jnp.broadcast_to(carry, carry_scr.shape)
    cnt_ref[...] = jnp.broadcast_to(carry, cnt_ref.shape)


def _route(ids, gates):
    t = ids.shape[0]
    tm = TOKEN_TILE
    nt = t // tm
    r = jnp.arange(tm)
    tri = (r[None, :] < r[:, None]).astype(BF16)
    e = jnp.arange(LANES)
    utri = (e[:, None] < e[None, :]).astype(BF16)
    tile = pl.BlockSpec((tm, LANES), lambda i: (i, 0))
    return pl.pallas_call(
        _route_kernel,
        grid=(nt,),
        in_specs=[tile, tile, _resident(tri.shape), _resident(utri.shape)],
        out_specs=[tile, pl.BlockSpec((SUBLANES, tm), lambda i: (i, 0)),
                   pl.BlockSpec((SUBLANES, tm), lambda i: (i, 0)),
                   pl.BlockSpec((1, SUBLANES, LANES), lambda i: (i, 0, 0)),
                   pl.BlockSpec((SUBLANES, LANES), lambda i: (0, 0))],
        out_shape=[jax.ShapeDtypeStruct((t, LANES), I32), jax.ShapeDtypeStruct((nt * SUBLANES, tm), I32),
                   jax.ShapeDtypeStruct((nt * SUBLANES, tm), F32),
                   jax.ShapeDtypeStruct((nt, SUBLANES, LANES), F32),
                   jax.ShapeDtypeStruct((SUBLANES, LANES), F32)],
        scratch_shapes=[pltpu.VMEM((SUBLANES, LANES), F32)],
        compiler_params=_cparams("arbitrary"),
        name="route",
    )(ids, gates, tri, utri)


def _dest_kernel(cnt_ref, meta_ref, tri_ref, gd_ref, blk_ref):
    cnt = cnt_ref[...]
    padded = jnp.ceil(cnt * (1.0 / EXPERT_BLOCK)) * EXPERT_BLOCK
    pend = jnp.dot(padded, tri_ref[...], preferred_element_type=F32, precision=HIGHEST)
    pstart = (pend - padded)[0:1, :]

    meta = meta_ref[0]
    csize, lstart, seg_off = meta[0:1, :], meta[1:2, :], meta[2:3, :]
    lend = lstart + csize
    base = pstart + seg_off - lstart
    grow = lax.broadcasted_iota(I32, (SUBLANES, LANES), 0)
    glane = lax.broadcasted_iota(I32, (SUBLANES, LANES), 1)
    gpos = ((grow * LANES + glane) * CHUNK).astype(F32)
    owner = jnp.zeros(gpos.shape, I32)
    for e in range(N_EXPERTS):
        owner = owner + jnp.where(gpos >= lend[0:1, e:e + 1], 1, 0)
    gdst = gpos
    for e in range(N_EXPERTS):
        gdst = gdst + jnp.where(owner == e, base[0:1, e:e + 1], 0.0)
    n_groups = lend[0:1, N_EXPERTS - 1:N_EXPERTS] * (1.0 / CHUNK)
    last = jnp.logical_and(grow == SUBLANES - 1, glane == LANES - 1)
    gd_ref[0] = jnp.where(last, n_groups, gdst).astype(I32)

    @pl.when(pl.program_id(0) == 0)
    def _():
        shape = blk_ref.shape[1:]
        pos = (lax.broadcasted_iota(I32, shape, 0) * LANES + lax.broadcasted_iota(I32, shape, 1))
        pos = pos.astype(F32) * EXPERT_BLOCK
        be = jnp.zeros(shape, I32)
        for e in range(N_EXPERTS):
            be = be + jnp.where(pos >= pend[0:1, e:e + 1], 1, 0)
        blk_ref[0] = jnp.minimum(be, N_EXPERTS - 1)
        nact = pend[0:1, N_EXPERTS - 1:N_EXPERTS] * (1.0 / EXPERT_BLOCK)
        blk_ref[1] = jnp.broadcast_to(nact, shape).astype(I32)
        blk_ref[2] = jnp.broadcast_to(pend[0:1, :], shape).astype(I32)


def _dest(cnt, meta):
    nt = meta.shape[0]
    r = jnp.arange(LANES)
    tri = (r[:, None] <= r[None, :]).astype(F32)
    return pl.pallas_call(
        _dest_kernel,
        grid=(nt,),
        in_specs=[pl.BlockSpec((SUBLANES, LANES), lambda i: (0, 0)),
                  pl.BlockSpec((1, SUBLANES, LANES), lambda i: (i, 0, 0)),
                  pl.BlockSpec((LANES, LANES), lambda i: (0, 0))],
        out_specs=[pl.BlockSpec((1, SUBLANES, LANES), lambda i: (i, 0, 0)),
                   pl.BlockSpec((3, SUBLANES, LANES), lambda i: (0, 0, 0))],
        out_shape=[jax.ShapeDtypeStruct((nt, SUBLANES, LANES), I32),
                   jax.ShapeDtypeStruct((3, SUBLANES, LANES), I32)],
        compiler_params=_cparams("arbitrary"),
        name="dest",
    )(cnt, meta, tri)


def _row_copy(src, dst, sem):
    return pltpu.make_async_copy(src, dst, sem)


def _dispatch_kernel(pend_ref, na_ref, gd_ref, slott_ref, gatest_ref, h2_ref, xs_ref, zero_ref, sorted_scr, sem):
    tm = h2_ref.shape[0]
    bm = zero_ref.shape[0]
    nb = xs_ref.shape[0] // bm

    @pl.when(pl.program_id(0) == 0)
    def _():
        zero_ref[...] = jnp.zeros(zero_ref.shape, zero_ref.dtype)

        def zero_block(start):
            return _row_copy(zero_ref, xs_ref.at[pl.ds(pl.multiple_of(start, bm), bm)], sem)

        def nonempty(e):
            return pend_ref[e] > (pend_ref[e - 1] if e > 0 else 0)

        for e in range(N_EXPERTS):
            @pl.when(nonempty(e))
            def _():
                zero_block(pend_ref[e] - bm).start()

        def tail(b, carry):
            zero_block(b * bm).start()
            return carry

        lax.fori_loop(na_ref[0], nb, tail, 0)
        for e in range(N_EXPERTS):
            @pl.when(nonempty(e))
            def _():
                zero_block(0).wait()

        def tail_wait(b, carry):
            zero_block(0).wait()
            return carry

        lax.fori_loop(na_ref[0], nb, tail_wait, 0)

    srow = lax.broadcasted_iota(I32, (SLOT_ROWS, tm), 0)
    perm = jnp.zeros((SLOT_ROWS, tm), F32)
    gate = jnp.zeros((SLOT_ROWS, 1), F32)
    for kk in range(TOP_K):
        hit = srow == slott_ref[kk:kk + 1, :]
        perm = perm + jnp.where(hit, 1.0, 0.0)
        gate = gate + jnp.sum(jnp.where(hit, gatest_ref[kk:kk + 1, :], 0.0), axis=1, keepdims=True)
    perm = perm.astype(BF16)
    half = h2_ref.shape[1] // 2
    lo = jnp.dot(perm, h2_ref[:, :half], preferred_element_type=F32)
    hi = jnp.dot(perm, h2_ref[:, half:], preferred_element_type=F32)
    sorted_scr[:, :half] = _pack_bf16_pair(lo, hi)
    sorted_scr[:, half:] = pltpu.bitcast(jnp.broadcast_to(gate, (SLOT_ROWS, LANES)), U32)

    n_groups = gd_ref[GROUP_TABLE - 1]

    def group_copy(j, d):
        src = sorted_scr.at[pl.ds(pl.multiple_of(j * CHUNK, CHUNK), CHUNK)]
        return _row_copy(src, xs_ref.at[pl.ds(pl.multiple_of(d, CHUNK), CHUNK)], sem)

    def issue(j, carry):
        group_copy(j, gd_ref[j]).start()
        return carry

    def retire(j, carry):
        group_copy(0, 0).wait()
        return carry

    lax.fori_loop(0, n_groups, issue, 0)
    lax.fori_loop(0, n_groups, retire, 0)


def _dispatch(pend, n_active, gd_flat, slott, gatest, h2b, n_rows):
    t, d = h2b.shape
    w = d // 2 + LANES
    tm = TOKEN_TILE
    grid_spec = pltpu.PrefetchScalarGridSpec(
        num_scalar_prefetch=2,
        grid=(t // tm,),
        in_specs=[pl.BlockSpec((GROUP_TABLE,), lambda i, pe, na: (i,), memory_space=pltpu.SMEM),
                  pl.BlockSpec((SUBLANES, tm), lambda i, pe, na: (i, 0)),
                  pl.BlockSpec((SUBLANES, tm), lambda i, pe, na: (i, 0)),
                  pl.BlockSpec((tm, d), lambda i, pe, na: (i, 0))],
        out_specs=pl.BlockSpec(memory_space=pl.ANY),
        scratch_shapes=[pltpu.VMEM((EXPERT_BLOCK, w), U32), pltpu.VMEM((SLOT_ROWS, w), U32),
                        pltpu.SemaphoreType.DMA(())],
    )
    return pl.pallas_call(
        _dispatch_kernel,
        grid_spec=grid_spec,
        out_shape=jax.ShapeDtypeStruct((n_rows, w), U32),
        compiler_params=_cparams("arbitrary"),
        name="dispatch",
    )(pend, n_active, gd_flat, slott, gatest, h2b)


def _expert_kernel(be_ref, na_ref, xs_ref, w1_ref, b1_ref, w2_ref, b2_ref, y_ref, w1b_ref, w2b_ref, *, d_expert):
    i = pl.program_id(0)
    active = i < na_ref[0]
    prev = be_ref[jnp.maximum(i - 1, 0)]
    new_expert = jnp.logical_or(i == 0, be_ref[i] != prev)
    half = y_ref.shape[1]

    @pl.when(jnp.logical_and(active, new_expert))
    def _():
        w1b_ref[...] = w1_ref[...].astype(BF16)
        w2b_ref[...] = w2_ref[...].astype(BF16)

    @pl.when(active)
    def _():
        x = _unpack_bf16_pair(xs_ref[:, :half]).astype(BF16)
        gate = pltpu.bitcast(xs_ref[:, half:], F32)[:, 0:1]
        cw = d_expert // EXPERT_HIDDEN_SLICES
        y = None
        for c in range(EXPERT_HIDDEN_SLICES):
            gs = slice(c * cw, (c + 1) * cw)
            us = slice(d_expert + c * cw, d_expert + (c + 1) * cw)
            g = jnp.dot(x, w1b_ref[:, gs], preferred_element_type=F32) + b1_ref[:, gs]
            u = jnp.dot(x, w1b_ref[:, us], preferred_element_type=F32) + b1_ref[:, us]
            g = jnp.minimum(g, SWIGLU_LIMIT)
            u = jnp.clip(u, -SWIGLU_LIMIT, SWIGLU_LIMIT)
            hm = (u + 1.0) * (g * jax.nn.sigmoid(SWIGLU_ALPHA * g))
            part = jnp.dot(hm.astype(BF16), w2b_ref[gs, :], preferred_element_type=F32)
            y = part if y is None else y + part
        y = (y + b2_ref[...]) * gate
        y_ref[...] = _pack_bf16_pair(y[:, :half], y[:, half:])

    @pl.when(jnp.logical_not(active))
    def _():
        y_ref[...] = jnp.zeros(y_ref.shape, U32)


def _experts(block_e, n_active, xs, w1, b1, w2, b2):
    n_rows, w = xs.shape
    n_exp, d, two_de = w1.shape
    de = two_de // 2
    bm = EXPERT_BLOCK
    nb = n_rows // bm

    def blk(i, be, na):
        return jnp.maximum(jnp.minimum(i, na[0] - 1), 0)

    grid_spec = pltpu.PrefetchScalarGridSpec(
        num_scalar_prefetch=2,
        grid=(nb,),
        in_specs=[pl.BlockSpec((bm, w), lambda i, be, na: (blk(i, be, na), 0)),
                  pl.BlockSpec((None, d, two_de), lambda i, be, na: (be[blk(i, be, na)], 0, 0)),
                  pl.BlockSpec((None, 1, two_de), lambda i, be, na: (be[blk(i, be, na)], 0, 0)),
                  pl.BlockSpec((None, de, d), lambda i, be, na: (be[blk(i, be, na)], 0, 0)),
                  pl.BlockSpec((None, 1, d), lambda i, be, na: (be[blk(i, be, na)], 0, 0))],
        out_specs=pl.BlockSpec((bm, d // 2), lambda i, be, na: (i, 0)),
        scratch_shapes=[pltpu.VMEM((d, two_de), BF16), pltpu.VMEM((de, d), BF16)],
    )
    return pl.pallas_call(
        functools.partial(_expert_kernel, d_expert=de),
        grid_spec=grid_spec,
        out_shape=jax.ShapeDtypeStruct((n_rows, d // 2), U32),
        compiler_params=_cparams("arbitrary"),
        name="expert",
    )(block_e, n_active, xs, w1, b1[:, None, :], w2, b2[:, None, :])


def _combine_kernel(gd_ref, gd_next_ref, slot_ref, x1_ref, mod_ref, fg_ref, yb_ref, o_ref, sorted_scr, sem):
    tm = x1_ref.shape[0]
    step = pl.program_id(0)
    cur = step % 2

    def group_copy(buf, j, d):
        dst = sorted_scr.at[buf, pl.ds(pl.multiple_of(j * CHUNK, CHUNK), CHUNK)]
        return _row_copy(yb_ref.at[pl.ds(pl.multiple_of(d, CHUNK), CHUNK)], dst, sem.at[buf])

    def fetch(table_ref, buf):
        def issue(j, carry):
            group_copy(buf, j, table_ref[j]).start()
            return carry
        lax.fori_loop(0, table_ref[GROUP_TABLE - 1], issue, 0)

    @pl.when(step == 0)
    def _():
        sorted_scr[...] = jnp.zeros(sorted_scr.shape, sorted_scr.dtype)
        fetch(gd_ref, 0)

    @pl.when(step + 1 < pl.num_programs(0))
    def _():
        fetch(gd_next_ref, 1 - cur)

    def retire(j, carry):
        group_copy(cur, 0, 0).wait()
        return carry

    lax.fori_loop(0, gd_ref[GROUP_TABLE - 1], retire, 0)

    slot = slot_ref[...]
    scol = lax.broadcasted_iota(I32, (tm, SLOT_ROWS), 1)
    pick = jnp.zeros((tm, SLOT_ROWS), F32)
    for kk in range(TOP_K):
        pick = pick + jnp.where(scol == slot[:, kk:kk + 1], 1.0, 0.0)
    pick = pick.astype(BF16)
    words = sorted_scr[cur]
    lo = pltpu.bitcast(words << 16, F32).astype(BF16)
    hi = pltpu.bitcast(words & jnp.uint32(0xFFFF0000), F32).astype(BF16)
    y = jnp.concatenate([jnp.dot(pick, lo, preferred_element_type=F32),
                         jnp.dot(pick, hi, preferred_element_type=F32)], axis=1)
    x2 = x1_ref[...] + mod_ref[5:6, :] * y
    o_ref[...] = _rms(x2) * fg_ref[...]


def _combine(gd_flat, slot, x1, mod8, final_g, yb, seq):
    t, d = x1.shape
    tm = TOKEN_TILE
    nt = t // tm
    tiles_per_seq = seq // tm
    return pl.pallas_call(
        _combine_kernel,
        grid=(nt,),
        in_specs=[pl.BlockSpec((GROUP_TABLE,), lambda i: (i,), memory_space=pltpu.SMEM),
                  pl.BlockSpec((GROUP_TABLE,), lambda i: (jnp.minimum(i + 1, nt - 1),), memory_space=pltpu.SMEM),
                  pl.BlockSpec((tm, LANES), lambda i: (i, 0)),
                  pl.BlockSpec((tm, d), lambda i: (i, 0)),
                  pl.BlockSpec((None, SUBLANES, d), lambda i: (i // tiles_per_seq, 0, 0)),
                  pl.BlockSpec((1, d), lambda i: (0, 0)),
                  pl.BlockSpec(memory_space=pl.ANY)],
        out_specs=pl.BlockSpec((tm, d), lambda i: (i, 0)),
        out_shape=jax.ShapeDtypeStruct((t, d), F32),
        scratch_shapes=[pltpu.VMEM((2, SLOT_ROWS, d // 2), U32), pltpu.SemaphoreType.DMA((2,))],
        compiler_params=_cparams("arbitrary"),
        name="combine",
    )(gd_flat, gd_flat, slot, x1, mod8, final_g[None], yb)


def kernel(x, c, w_ada, b_ada, norm1_g, w_in, b_forget, conv_w, conv_b, conv_ln_g, conv_ln_b, w_conv_out,
           w_attn_out, w_out, norm2_g, w_router, b_router, w_exp_in, b_exp_in, w_exp_out, b_exp_out, final_g):
    bsz, seq, d = x.shape
    t = bsz * seq
    depth = w_ada.shape[0]
    xf = x.reshape(t, d)
    n_rows = t * TOP_K + (t // TOKEN_TILE) * N_EXPERTS * (CHUNK - 1) + N_EXPERTS * (EXPERT_BLOCK - CHUNK)
    n_rows = -(-n_rows // EXPERT_BLOCK) * EXPERT_BLOCK
    n_blocks = n_rows // EXPERT_BLOCK
    for l in range(depth):
        mod8 = _mod(c, w_ada[l], b_ada[l])
        a, q, k, v, tot, sgc, sga = _inproj(xf, mod8, norm1_g[l], w_in[l], b_forget[l], seq)
        act = _conv(a, conv_w[l], conv_b[l], conv_ln_g[l], conv_ln_b[l], bsz, seq)
        attn = _attention(q, k, v, tot, bsz, seq)
        x1, h2b, ids, gates = _merge(xf, act, attn, sgc, sga, mod8, norm2_g[l], w_conv_out[l], w_attn_out[l],
                                     w_out[l], w_router[l], b_router[l], seq)
        slot, slott, gatest, meta, cnt = _route(ids, gates)
        gd, blk = _dest(cnt, meta)
        gd_flat = gd.reshape(-1)
        block_e = blk[0].reshape(-1)[:n_blocks]
        n_active = blk[1, 0, :1]
        pend = blk[2, 0, :N_EXPERTS]
        xs = _dispatch(pend, n_active, gd_flat, slott, gatest, h2b, n_rows)
        yb = _experts(block_e, n_active, xs, w_exp_in[l], b_exp_in[l], w_exp_out[l], b_exp_out[l])
        assert depth == 1
        xf = _combine(gd_flat, slot, x1, mod8, final_g, yb, seq)
    return xf.reshape(bsz, seq, d)
```

```python
import functools

import jax
import jax.numpy as jnp
from jax import lax
from jax.experimental import pallas as pl
from jax.experimental.pallas import tpu as pltpu

F32 = jnp.float32
BF16 = jnp.bfloat16
I32 = jnp.int32
U32 = jnp.uint32
HIGHEST = lax.Precision.HIGHEST

LANES = 128
SUBLANES = 8
VMEM_LIMIT_BYTES = 56 * 1024 * 1024

N_HEADS = 8
HEAD_DIM = 64
CONV_KERNEL = 31
CONV_HALO = 32
N_EXPERTS = 32
TOP_K = 4
SWIGLU_LIMIT = 7.0
SWIGLU_ALPHA = 1.702
RMS_EPS = 1e-5
LN_EPS = 1e-5
N_MOD = 6
N_SPLIT = 3
LOG2_E = 1.4426950408889634
HEADS_PER_STEP = 2

INPROJ_TILE = 512
ATTN_BQ = 2048
ATTN_BK = 512
MERGE_TILE = 512
CONV_TILE = 256
TOKEN_TILE = 256
EXPERT_BLOCK = 512
CHUNK = SUBLANES
SLOT_ROWS = 1280
GROUP_TABLE = SUBLANES * LANES
assert SLOT_ROWS >= TOKEN_TILE * TOP_K + N_EXPERTS * (CHUNK - 1) and SLOT_ROWS // CHUNK < GROUP_TABLE


def _cparams(*sem):
    return pltpu.CompilerParams(dimension_semantics=sem, vmem_limit_bytes=VMEM_LIMIT_BYTES)


def _resident(shape):
    nd = len(shape)
    return pl.BlockSpec(shape, lambda *_: (0,) * nd, pipeline_mode=pl.Buffered(1))


def _split3(x):
    p1 = x.astype(BF16)
    r1 = x - p1.astype(F32)
    p2 = r1.astype(BF16)
    p3 = (r1 - p2.astype(F32)).astype(BF16)
    return jnp.concatenate([p1, p2, p3], axis=1)


def _rms(x):
    return x * lax.rsqrt(jnp.mean(x * x, axis=-1, keepdims=True) + RMS_EPS)


def _mod_kernel(c_ref, w_ref, b_ref, o_ref):
    c = c_ref[...]
    sc = c * jax.nn.sigmoid(c)
    o_ref[...] = jnp.dot(sc, w_ref[...], preferred_element_type=F32, precision=HIGHEST) + b_ref[...]


def _mod(c, w_ada, b_ada):
    bsz, d = c.shape
    n = w_ada.shape[1]
    tn = 1024
    cp = jnp.zeros((SUBLANES, d), F32).at[:bsz].set(c)
    out = pl.pallas_call(
        _mod_kernel,
        grid=(n // tn,),
        in_specs=[pl.BlockSpec((SUBLANES, d), lambda j: (0, 0)),
                  pl.BlockSpec((d, tn), lambda j: (0, j)),
                  pl.BlockSpec((1, tn), lambda j: (0, j))],
        out_specs=pl.BlockSpec((SUBLANES, tn), lambda j: (0, j)),
        out_shape=jax.ShapeDtypeStruct((SUBLANES, n), F32),
        compiler_params=_cparams("arbitrary"),
        name="mod",
    )(cp, w_ada, b_ada[None])
    mod = out[:bsz].reshape(bsz, N_MOD, d)
    return jnp.pad(mod, ((0, 0), (0, SUBLANES - N_MOD), (0, 0)))


def _inproj_kernel(x_ref, mod_ref, g_ref, wa_ref, wqkv_ref, wf_ref, wg_ref, bf_ref,
                   tri_ref, eq_ref, ek_ref, oneq_ref, onek_ref,
                   a_ref, q_ref, k_ref, v_ref, aq_ref, ak_ref, tot_ref, sgc_ref, sga_ref, carry_scr,
                   *, conv_width, d_model):
    x = x_ref[...]
    h = _rms(x) * g_ref[...] * (1.0 + mod_ref[1:2, :]) + mod_ref[0:1, :]
    hb = h.astype(BF16)

    u = jnp.dot(hb, wa_ref[...], preferred_element_type=F32)
    a_ref[...] = (u[:, :conv_width] * jax.nn.sigmoid(u[:, conv_width:])).astype(BF16)

    fl = jnp.dot(hb, wf_ref[...], preferred_element_type=F32) + bf_ref[...]
    lane = lax.broadcasted_iota(I32, fl.shape, 1)
    log_f = -(jnp.maximum(-fl, 0.0) + jnp.log1p(jnp.exp(-jnp.abs(fl))))
    log_f = jnp.where(lane < N_HEADS, log_f, 0.0)
    cs = jnp.dot(tri_ref[...], _split3(log_f), preferred_element_type=F32)
    cum = (cs[:, :LANES] + cs[:, LANES:2 * LANES] + cs[:, 2 * LANES:]) * LOG2_E
    tm = cum.shape[0]

    pieces = []
    for sb in range(tm // ATTN_BK):
        blk = cum[sb * ATTN_BK:(sb + 1) * ATTN_BK, :]
        if sb > 0:
            blk = blk - cum[sb * ATTN_BK - 1:sb * ATTN_BK, :]
        tot_ref[sb] = blk[ATTN_BK - 1:, :]
        pieces.append(blk)
    cum_k = pieces[0] if len(pieces) == 1 else jnp.concatenate(pieces, axis=0)

    @pl.when(pl.program_id(0) % (ATTN_BQ // tm) == 0)
    def _():
        carry_scr[...] = jnp.zeros(carry_scr.shape, F32)

    cum_q = cum + carry_scr[0:1, :]
    carry_scr[...] = jnp.broadcast_to(cum_q[tm - 1:, :], carry_scr.shape)

    qkv = jnp.dot(hb, wqkv_ref[...], preferred_element_type=F32)
    aw = qkv.shape[1] // 3
    q_ref[...] = (qkv[:, :aw] * (HEAD_DIM ** -0.5 * LOG2_E)).astype(BF16)
    k_ref[...] = qkv[:, aw:2 * aw].astype(BF16)
    v_ref[...] = qkv[:, 2 * aw:].astype(BF16)
    aq_ref[...] = (jnp.dot(_split3(cum_q), eq_ref[...], preferred_element_type=F32) + oneq_ref[...]).astype(BF16)
    ak_ref[...] = (jnp.dot(_split3(cum_k), ek_ref[...], preferred_element_type=F32) + onek_ref[...]).astype(BF16)

    gts = jnp.dot(hb, wg_ref[...], preferred_element_type=F32)
    sgc_ref[...] = jax.nn.sigmoid(gts[:, :d_model]).astype(BF16)
    sga_ref[...] = jax.nn.sigmoid(gts[:, d_model:]).astype(BF16)


def _aug_constants():
    rows = jnp.arange(N_SPLIT * LANES)
    piece, head = rows // LANES, rows % LANES
    cols = jnp.arange(N_HEADS * HEAD_DIM)
    pair, off = cols // LANES, cols % LANES
    chead = 2 * pair + jnp.where(off >= HEAD_DIM, 0, 1)
    pos = off % HEAD_DIM
    valid = (head[:, None] < N_HEADS) & (head[:, None] == chead[None, :])
    eq = jnp.where(valid & (pos[None, :] == piece[:, None]), 1.0, 0.0).astype(BF16)
    ek = jnp.where(valid & (pos[None, :] == piece[:, None] + N_SPLIT), -1.0, 0.0).astype(BF16)
    oneq = jnp.where((pos >= N_SPLIT) & (pos < 2 * N_SPLIT), 1.0, 0.0).astype(F32)[None]
    onek = jnp.where(pos < N_SPLIT, 1.0, 0.0).astype(F32)[None]
    return eq, ek, oneq, onek


def _inproj(x2, mod8, norm_g, w_in, b_forget, seq):
    t, d = x2.shape
    tm = INPROJ_TILE
    nsub = tm // ATTN_BK
    cw = d // 2
    aw = N_HEADS * HEAD_DIM
    o0 = 2 * cw
    wa = w_in[:, :o0].astype(BF16)
    o1 = o0 + 3 * aw
    wqkv = w_in[:, o0:o1].astype(BF16)
    wf = jnp.pad(w_in[:, o1:o1 + N_HEADS], ((0, 0), (0, LANES - N_HEADS))).astype(BF16)
    bfp = jnp.pad(b_forget, (0, LANES - N_HEADS))[None]
    wg = w_in[:, o1 + N_HEADS:].astype(BF16)
    r = jnp.arange(tm)
    tri = (r[None, :] <= r[:, None]).astype(BF16)
    eq, ek, oneq, onek = _aug_constants()
    tiles_per_seq = seq // tm
    row = lambda i: (i, 0)
    outs = pl.pallas_call(
        functools.partial(_inproj_kernel, conv_width=cw, d_model=d),
        grid=(t // tm,),
        in_specs=[pl.BlockSpec((tm, d), row),
                  pl.BlockSpec((None, SUBLANES, d), lambda i: (i // tiles_per_seq, 0, 0)),
                  _resident((1, d)),
                  _resident(wa.shape), _resident(wqkv.shape),
                  _resident(wf.shape), _resident(wg.shape), _resident(bfp.shape),
                  _resident(tri.shape), _resident(eq.shape), _resident(ek.shape),
                  _resident(oneq.shape), _resident(onek.shape)],
        out_specs=[pl.BlockSpec((tm, cw), row)] + [pl.BlockSpec((tm, aw), row)] * 5 + [
                   pl.BlockSpec((nsub, 1, LANES), lambda i: (i, 0, 0)),
                   pl.BlockSpec((tm, d), row), pl.BlockSpec((tm, d), row)],
        out_shape=[jax.ShapeDtypeStruct((t, cw), BF16)] + [jax.ShapeDtypeStruct((t, aw), BF16)] * 5 + [
                   jax.ShapeDtypeStruct((t // ATTN_BK, 1, LANES), F32),
                   jax.ShapeDtypeStruct((t, d), BF16), jax.ShapeDtypeStruct((t, d), BF16)],
        scratch_shapes=[pltpu.VMEM((SUBLANES, LANES), F32)],
        compiler_params=_cparams("arbitrary"),
        name="inproj",
    )(x2, mod8, norm_g[None], wa, wqkv, wf, wg, bfp, tri, eq, ek, oneq, onek)
    return outs


def _conv_kernel(halo_ref, cur_ref, w_ref, b_ref, g_ref, beta_ref, o_ref, xs_ref, ys_ref):
    ts = cur_ref.shape[0]
    first = pl.program_id(1) == 0
    halo = halo_ref[...].astype(F32)
    xs_ref[0:CONV_HALO, :] = jnp.where(first, 0.0, halo)
    xs_ref[CONV_HALO:, :] = cur_ref[...].astype(F32)
    acc = jnp.zeros(o_ref.shape, F32) + b_ref[...]
    base = CONV_HALO - (CONV_KERNEL - 1)
    for rho in range(SUBLANES):
        offs = [o for o in range(base, base + CONV_KERNEL) if o % SUBLANES == rho]
        if not offs:
            continue
        rows = offs[-1] + ts - rho
        ys_ref[0:rows, :] = xs_ref[rho:rho + rows, :]
        for o in offs:
            lo = o - rho
            acc = acc + w_ref[o - base:o - base + 1, :] * ys_ref[lo:lo + ts, :]
    mu = jnp.mean(acc, axis=-1, keepdims=True)
    cen = acc - mu
    var = jnp.mean(cen * cen, axis=-1, keepdims=True)
    y = cen * lax.rsqrt(var + LN_EPS) * g_ref[...] + beta_ref[...]
    o_ref[...] = (y * jax.nn.sigmoid(y)).astype(BF16)


def _conv(a, conv_w, conv_b, ln_g, ln_b, bsz, seq):
    t, cw = a.shape
    ts = CONV_TILE
    nt = seq // ts
    hpt = ts // CONV_HALO
    vec = lambda: pl.BlockSpec((1, cw), lambda b, i: (0, 0))
    return pl.pallas_call(
        _conv_kernel,
        grid=(bsz, nt),
        in_specs=[pl.BlockSpec((CONV_HALO, cw), lambda b, i: (jnp.maximum((b * nt + i) * hpt - 1, 0), 0)),
                  pl.BlockSpec((ts, cw), lambda b, i: (b * nt + i, 0)),
                  pl.BlockSpec((CONV_KERNEL, cw), lambda b, i: (0, 0)),
                  vec(), vec(), vec()],
        out_specs=pl.BlockSpec((ts, cw), lambda b, i: (b * nt + i, 0)),
        out_shape=jax.ShapeDtypeStruct((t, cw), BF16),
        scratch_shapes=[pltpu.VMEM((CONV_HALO + ts, cw), F32), pltpu.VMEM((CONV_HALO + ts, cw), F32)],
        compiler_params=_cparams("arbitrary", "arbitrary"),
        name="conv",
    )(a, a, conv_w, conv_b[None], ln_g[None], ln_b[None])


def _attn_kernel(q_ref, aq_ref, k_ref, ak_ref, v_ref, tot_ref, o_ref, qop_scr, m_scr, acc_scr):
    bq = q_ref.shape[0]
    bk = ATTN_BK
    ratio = bq // bk
    qi = pl.program_id(2)
    nt = (((1,), (1,)), ((), ()))
    heads = range(HEADS_PER_STEP)
    zero = jnp.zeros((1, LANES), F32)

    def splice(hh, main, extra):
        first = lax.broadcasted_iota(I32, main.shape, 1) < HEAD_DIM
        return jnp.where(first, main, extra) if hh == 0 else jnp.where(first, extra, main)

    for hh in heads:
        qop_scr[hh] = splice(hh, q_ref[...], aq_ref[...])

    def scores(hh, kb, row0):
        start = pl.multiple_of(kb * bk, bk)
        k = splice(hh, k_ref[pl.ds(start, bk), :], ak_ref[pl.ds(start, bk), :])
        v = v_ref[pl.ds(start, bk), :]
        v = splice(hh, v, jnp.ones(v.shape, v.dtype))
        s = lax.dot_general(qop_scr[hh, row0:, :], k, nt, preferred_element_type=F32)
        return s, v

    def update(hh, s, v, delta, row0):
        m_prev = m_scr[hh, row0:, :]
        m_new = jnp.maximum(m_prev, jnp.max(s, axis=1, keepdims=True) + delta)
        shift = m_new - delta
        p = jnp.exp2(s - shift[:, :1])
        alpha = jnp.exp2(m_prev - m_new)
        pv = jnp.dot(p.astype(BF16), v, preferred_element_type=F32)
        acc_scr[hh, row0:, :] = alpha * acc_scr[hh, row0:, :] + pv
        m_scr[hh, row0:, :] = m_new

    for hh in heads:
        delta = zero
        for sb in range(ratio):
            kb = qi * ratio + sb
            row0 = sb * bk
            s, v = scores(hh, kb, row0)
            row = lax.broadcasted_iota(I32, s.shape, 0)
            col = lax.broadcasted_iota(I32, s.shape, 1)
            s = jnp.where(col <= row, s, -jnp.inf)
            if sb == 0:
                m0 = jnp.max(s, axis=1, keepdims=True)
                p = jnp.exp2(s - m0)
                m_scr[hh] = jnp.broadcast_to(m0, m_scr.shape[1:])
                acc_scr[hh] = jnp.dot(p.astype(BF16), v, preferred_element_type=F32)
            else:
                update(hh, s, v, delta, row0)
            delta = delta - tot_ref[hh, pl.ds(kb, 1), :]

    def body(step, deltas):
        kb = qi * ratio - 1 - step
        new_deltas = []
        for hh in heads:
            delta = deltas[hh] + tot_ref[hh, pl.ds(kb, 1), :]
            s, v = scores(hh, kb, 0)
            update(hh, s, v, delta, 0)
            new_deltas.append(delta)
        return tuple(new_deltas)

    lax.fori_loop(0, qi * ratio, body, tuple(zero for _ in heads))
    acc0, acc1 = acc_scr[0], acc_scr[1]
    out0 = acc0 / acc0[:, HEAD_DIM:HEAD_DIM + 1]
    out1 = acc1 / acc1[:, 0:1]
    o_ref[...] = splice(0, out0, out1).astype(BF16)


def _attention(q, aq, k, ak, v, tot, bsz, seq):
    t, aw = q.shape
    bq = ATTN_BQ
    nq = seq // bq
    nk = seq // ATTN_BK
    assert HEADS_PER_STEP * HEAD_DIM == LANES
    tot = tot.reshape(bsz, nk, LANES)[:, :, :N_HEADS].transpose(0, 2, 1)
    tot = jnp.broadcast_to(tot[..., None], (bsz, N_HEADS, nk, LANES))
    qblk = pl.BlockSpec((bq, LANES), lambda b, h, i: (b * nq + i, h))
    kblk = pl.BlockSpec((None, seq, LANES), lambda b, h, i: (b, 0, h), pipeline_mode=pl.Buffered(1))
    seq3 = lambda a: a.reshape(bsz, seq, aw)
    return pl.pallas_call(
        _attn_kernel,
        grid=(bsz, N_HEADS // HEADS_PER_STEP, nq),
        in_specs=[qblk, qblk, kblk, kblk, kblk,
                  pl.BlockSpec((None, HEADS_PER_STEP, nk, LANES), lambda b, h, i: (b, h, 0, 0))],
        out_specs=qblk,
        out_shape=jax.ShapeDtypeStruct((t, aw), BF16),
        scratch_shapes=[pltpu.VMEM((HEADS_PER_STEP, bq, LANES), BF16),
                        pltpu.VMEM((HEADS_PER_STEP, bq, LANES), F32),
                        pltpu.VMEM((HEADS_PER_STEP, bq, LANES), F32)],
        compiler_params=_cparams("arbitrary", "arbitrary", "arbitrary"),
        name="attn",
    )(q, aq, seq3(k), seq3(ak), seq3(v), tot)


def _pack_bf16_pair(lo, hi):
    lo_bits = pltpu.bitcast(lo.astype(BF16).astype(F32), U32)
    hi_bits = pltpu.bitcast(hi.astype(BF16).astype(F32), U32)
    return (hi_bits & jnp.uint32(0xFFFF0000)) | (lo_bits >> 16)


def _unpack_bf16_pair(w):
    lo = pltpu.bitcast(w << 16, F32)
    hi = pltpu.bitcast(w & jnp.uint32(0xFFFF0000), F32)
    return jnp.concatenate([lo, hi], axis=1)


def _merge_kernel(x_ref, act_ref, attn_ref, sgc_ref, sga_ref, mod_ref, g2_ref, wco_ref, wao_ref, wout_ref,
                  wr_ref, br_ref, x1_ref, h2_ref, ids_ref, gates_ref):
    conv_out = jnp.dot(act_ref[...], wco_ref[...], preferred_element_type=F32)
    attn_out = jnp.dot(attn_ref[...], wao_ref[...], preferred_element_type=F32)
    mixed = sgc_ref[...].astype(F32) * conv_out + sga_ref[...].astype(F32) * attn_out
    upd = jnp.dot(mixed.astype(BF16), wout_ref[...], preferred_element_type=F32)
    x1 = x_ref[...] + mod_ref[2:3, :] * upd
    x1_ref[...] = x1
    h2 = _rms(x1) * g2_ref[...] * (1.0 + mod_ref[4:5, :]) + mod_ref[3:4, :]
    h2_hi = h2.astype(BF16)
    h2_ref[...] = h2_hi
    h2_lo = (h2 - h2_hi.astype(F32)).astype(BF16)
    h2_cat = jnp.concatenate([h2_hi, h2_lo, h2_hi], axis=1)
    logits = jnp.dot(h2_cat, wr_ref[...], preferred_element_type=F32) + br_ref[...]
    lane = lax.broadcasted_iota(I32, logits.shape, 1)
    work = jnp.where(lane < N_EXPERTS, logits, -jnp.inf)
    ids = jnp.zeros(logits.shape, I32)
    vals = []
    for kk in range(TOP_K):
        mx = jnp.max(work, axis=1, keepdims=True)
        idx = jnp.min(jnp.where(work == mx, lane, LANES), axis=1, keepdims=True)
        ids = jnp.where(lane == kk, idx, ids)
        vals.append(mx)
        work = jnp.where(lane == idx, -jnp.inf, work)
    exps = [jnp.exp(vv - vals[0]) for vv in vals]
    den = exps[0] + exps[1] + exps[2] + exps[3]
    gates = jnp.zeros(logits.shape, F32)
    for kk in range(TOP_K):
        gates = jnp.where(lane == kk, exps[kk] / den, gates)
    ids_ref[...] = ids
    gates_ref[...] = gates


def _merge(x2, act, attn, sgc, sga, mod8, norm2_g, w_conv_out, w_attn_out, w_out, w_router, b_router, seq):
    t, d = x2.shape
    tm = MERGE_TILE
    cw = act.shape[1]
    sw = attn.shape[1]
    wco = w_conv_out.astype(BF16)
    wao = w_attn_out.astype(BF16)
    wout = w_out.astype(BF16)
    wr = jnp.pad(w_router, ((0, 0), (0, LANES - N_EXPERTS)))
    wr_hi = wr.astype(BF16)
    wr_lo = (wr - wr_hi.astype(F32)).astype(BF16)
    wr = jnp.concatenate([wr_hi, wr_hi, wr_lo], axis=0)
    br = jnp.pad(b_router, (0, LANES - N_EXPERTS))[None]
    tiles_per_seq = seq // tm
    row = lambda i: (i, 0)
    return pl.pallas_call(
        _merge_kernel,
        grid=(t // tm,),
        in_specs=[pl.BlockSpec((tm, d), row), pl.BlockSpec((tm, cw), row), pl.BlockSpec((tm, sw), row),
                  pl.BlockSpec((tm, d), row), pl.BlockSpec((tm, d), row),
                  pl.BlockSpec((None, SUBLANES, d), lambda i: (i // tiles_per_seq, 0, 0)),
                  _resident((1, d)), _resident(wco.shape), _resident(wao.shape), _resident(wout.shape),
                  _resident(wr.shape), _resident(br.shape)],
        out_specs=[pl.BlockSpec((tm, d), row), pl.BlockSpec((tm, d), row),
                   pl.BlockSpec((tm, LANES), row), pl.BlockSpec((tm, LANES), row)],
        out_shape=[jax.ShapeDtypeStruct((t, d), F32), jax.ShapeDtypeStruct((t, d), BF16),
                   jax.ShapeDtypeStruct((t, LANES), I32), jax.ShapeDtypeStruct((t, LANES), F32)],
        compiler_params=_cparams("arbitrary"),
        name="merge",
    )(x2, act, attn, sgc, sga, mod8, norm2_g[None], wco, wao, wout, wr, br)


def _expert_onehots(ids):
    lane = lax.broadcasted_iota(I32, ids.shape, 1)
    return lane, [lane == ids[:, kk:kk + 1] for kk in range(TOP_K)]


def _route_kernel(ids_ref, gates_ref, tri_ref, utri_ref, slot_ref, slott_ref, gatest_ref, meta_ref,
                  cnt_ref, carry_scr):
    @pl.when(pl.program_id(0) == 0)
    def _():
        carry_scr[...] = jnp.zeros(carry_scr.shape, F32)

    lane, hots = _expert_onehots(ids_ref[...])
    multi = jnp.zeros(lane.shape, F32)
    for hot in hots:
        multi = multi + jnp.where(hot, 1.0, 0.0)
    carry = carry_scr[0:1, :]
    before = jnp.dot(tri_ref[...], multi.astype(BF16), preferred_element_type=F32)
    units = jnp.ceil(jnp.sum(multi, axis=0, keepdims=True) * (1.0 / CHUNK))
    units8 = jnp.broadcast_to(units, (SUBLANES, LANES)).astype(BF16)
    lstart = jnp.dot(units8, utri_ref[...], preferred_element_type=F32)[0:1, :] * CHUNK
    csize = units * CHUNK
    slot = jnp.zeros(lane.shape, F32)
    for kk, hot in enumerate(hots):
        sl = jnp.sum(jnp.where(hot, lstart + before, 0.0), axis=1, keepdims=True)
        slot = jnp.where(lane == kk, sl, slot)
    slot_ref[...] = slot.astype(I32)
    slott_ref[...] = slot.T[0:SUBLANES, :].astype(I32)
    gatest_ref[...] = gates_ref[...].T[0:SUBLANES, :]
    row = lax.broadcasted_iota(I32, (SUBLANES, LANES), 0)
    meta = jnp.where(row == 0, csize, jnp.where(row == 1, lstart, jnp.where(row == 2, carry, 0.0)))
    meta_ref[0] = meta
    carry = carry + csize
    carry_scr[...] = jnp.broadcast_to(carry, carry_scr.shape)
    cnt_ref[...] = jnp.broadcast_to(carry, cnt_ref.shape)


def _route(ids, gates):
    t = ids.shape[0]
    tm = TOKEN_TILE
    nt = t // tm
    r = jnp.arange(tm)
    tri = (r[None, :] < r[:, None]).astype(BF16)
    e = jnp.arange(LANES)
    utri = (e[:, None] < e[None, :]).astype(BF16)
    tile = pl.BlockSpec((tm, LANES), lambda i: (i, 0))
    return pl.pallas_call(
        _route_kernel,
        grid=(nt,),
        in_specs=[tile, tile, _resident(tri.shape), _resident(utri.shape)],
        out_specs=[tile, pl.BlockSpec((SUBLANES, tm), lambda i: (i, 0)),
                   pl.BlockSpec((SUBLANES, tm), lambda i: (i, 0)),
                   pl.BlockSpec((1, SUBLANES, LANES), lambda i: (i, 0, 0)),
                   pl.BlockSpec((SUBLANES, LANES), lambda i: (0, 0))],
        out_shape=[jax.ShapeDtypeStruct((t, LANES), I32), jax.ShapeDtypeStruct((nt * SUBLANES, tm), I32),
                   jax.ShapeDtypeStruct((nt * SUBLANES, tm), F32),
                   jax.ShapeDtypeStruct((nt, SUBLANES, LANES), F32),
                   jax.ShapeDtypeStruct((SUBLANES, LANES), F32)],
        scratch_shapes=[pltpu.VMEM((SUBLANES, LANES), F32)],
        compiler_params=_cparams("arbitrary"),
        name="route",
    )(ids, gates, tri, utri)


def _dest_kernel(cnt_ref, meta_ref, tri_ref, gd_ref, blk_ref):
    cnt = cnt_ref[...]
    padded = jnp.ceil(cnt * (1.0 / EXPERT_BLOCK)) * EXPERT_BLOCK
    pend = jnp.dot(padded, tri_ref[...], preferred_element_type=F32, precision=HIGHEST)
    pstart = (pend - padded)[0:1, :]

    meta = meta_ref[0]
    csize, lstart, seg_off = meta[0:1, :], meta[1:2, :], meta[2:3, :]
    lend = lstart + csize
    base = pstart + seg_off - lstart
    grow = lax.broadcasted_iota(I32, (SUBLANES, LANES), 0)
    glane = lax.broadcasted_iota(I32, (SUBLANES, LANES), 1)
    gpos = ((grow * LANES + glane) * CHUNK).astype(F32)
    owner = jnp.zeros(gpos.shape, I32)
    for e in range(N_EXPERTS):
        owner = owner + jnp.where(gpos >= lend[0:1, e:e + 1], 1, 0)
    gdst = gpos
    for e in range(N_EXPERTS):
        gdst = gdst + jnp.where(owner == e, base[0:1, e:e + 1], 0.0)
    n_groups = lend[0:1, N_EXPERTS - 1:N_EXPERTS] * (1.0 / CHUNK)
    last = jnp.logical_and(grow == SUBLANES - 1, glane == LANES - 1)
    gd_ref[0] = jnp.where(last, n_groups, gdst).astype(I32)

    @pl.when(pl.program_id(0) == 0)
    def _():
        shape = blk_ref.shape[1:]
        pos = (lax.broadcasted_iota(I32, shape, 0) * LANES + lax.broadcasted_iota(I32, shape, 1))
        pos = pos.astype(F32) * EXPERT_BLOCK
        be = jnp.zeros(shape, I32)
        for e in range(N_EXPERTS):
            be = be + jnp.where(pos >= pend[0:1, e:e + 1], 1, 0)
        blk_ref[0] = jnp.minimum(be, N_EXPERTS - 1)
        nact = pend[0:1, N_EXPERTS - 1:N_EXPERTS] * (1.0 / EXPERT_BLOCK)
        blk_ref[1] = jnp.broadcast_to(nact, shape).astype(I32)
        blk_ref[2] = jnp.broadcast_to(pend[0:1, :], shape).astype(I32)


def _dest(cnt, meta):
    nt = meta.shape[0]
    r = jnp.arange(LANES)
    tri = (r[:, None] <= r[None, :]).astype(F32)
    return pl.pallas_call(
        _dest_kernel,
        grid=(nt,),
        in_specs=[pl.BlockSpec((SUBLANES, LANES), lambda i: (0, 0)),
                  pl.BlockSpec((1, SUBLANES, LANES), lambda i: (i, 0, 0)),
                  pl.BlockSpec((LANES, LANES), lambda i: (0, 0))],
        out_specs=[pl.BlockSpec((1, SUBLANES, LANES), lambda i: (i, 0, 0)),
                   pl.BlockSpec((3, SUBLANES, LANES), lambda i: (0, 0, 0))],
        out_shape=[jax.ShapeDtypeStruct((nt, SUBLANES, LANES), I32),
                   jax.ShapeDtypeStruct((3, SUBLANES, LANES), I32)],
        compiler_params=_cparams("arbitrary"),
        name="dest",
    )(cnt, meta, tri)


def _row_copy(src, dst, sem):
    return pltpu.make_async_copy(src, dst, sem)


def _dispatch_kernel(pend_ref, na_ref, gd_ref, slott_ref, gatest_ref, h2_ref, xs_ref, zero_ref, sorted_scr, sem):
    tm = h2_ref.shape[0]
    bm = zero_ref.shape[0]
    nb = xs_ref.shape[0] // bm

    @pl.when(pl.program_id(0) == 0)
    def _():
        zero_ref[...] = jnp.zeros(zero_ref.shape, zero_ref.dtype)

        def zero_block(start):
            return _row_copy(zero_ref, xs_ref.at[pl.ds(pl.multiple_of(start, bm), bm)], sem)

        def nonempty(e):
            return pend_ref[e] > (pend_ref[e - 1] if e > 0 else 0)

        for e in range(N_EXPERTS):
            @pl.when(nonempty(e))
            def _():
                zero_block(pend_ref[e] - bm).start()

        def tail(b, carry):
            zero_block(b * bm).start()
            return carry

        lax.fori_loop(na_ref[0], nb, tail, 0)
        for e in range(N_EXPERTS):
            @pl.when(nonempty(e))
            def _():
                zero_block(0).wait()

        def tail_wait(b, carry):
            zero_block(0).wait()
            return carry

        lax.fori_loop(na_ref[0], nb, tail_wait, 0)

    srow = lax.broadcasted_iota(I32, (SLOT_ROWS, tm), 0)
    perm = jnp.zeros((SLOT_ROWS, tm), F32)
    gate = jnp.zeros((SLOT_ROWS, 1), F32)
    for kk in range(TOP_K):
        hit = srow == slott_ref[kk:kk + 1, :]
        perm = perm + jnp.where(hit, 1.0, 0.0)
        gate = gate + jnp.sum(jnp.where(hit, gatest_ref[kk:kk + 1, :], 0.0), axis=1, keepdims=True)
    perm = perm.astype(BF16)
    half = h2_ref.shape[1] // 2
    lo = jnp.dot(perm, h2_ref[:, :half], preferred_element_type=F32)
    hi = jnp.dot(perm, h2_ref[:, half:], preferred_element_type=F32)
    sorted_scr[:, :half] = _pack_bf16_pair(lo, hi)
    sorted_scr[:, half:] = pltpu.bitcast(jnp.broadcast_to(gate, (SLOT_ROWS, LANES)), U32)

    n_groups = gd_ref[GROUP_TABLE - 1]

    def group_copy(j, d):
        src = sorted_scr.at[pl.ds(pl.multiple_of(j * CHUNK, CHUNK), CHUNK)]
        return _row_copy(src, xs_ref.at[pl.ds(pl.multiple_of(d, CHUNK), CHUNK)], sem)

    def issue(j, carry):
        group_copy(j, gd_ref[j]).start()
        return carry

    def retire(j, carry):
        group_copy(0, 0).wait()
        return carry

    lax.fori_loop(0, n_groups, issue, 0)
    lax.fori_loop(0, n_groups, retire, 0)


def _dispatch(pend, n_active, gd_flat, slott, gatest, h2b, n_rows):
    t, d = h2b.shape
    w = d // 2 + LANES
    tm = TOKEN_TILE
    grid_spec = pltpu.PrefetchScalarGridSpec(
        num_scalar_prefetch=2,
        grid=(t // tm,),
        in_specs=[pl.BlockSpec((GROUP_TABLE,), lambda i, pe, na: (i,), memory_space=pltpu.SMEM),
                  pl.BlockSpec((SUBLANES, tm), lambda i, pe, na: (i, 0)),
                  pl.BlockSpec((SUBLANES, tm), lambda i, pe, na: (i, 0)),
                  pl.BlockSpec((tm, d), lambda i, pe, na: (i, 0))],
        out_specs=pl.BlockSpec(memory_space=pl.ANY),
        scratch_shapes=[pltpu.VMEM((EXPERT_BLOCK, w), U32), pltpu.VMEM((SLOT_ROWS, w), U32),
                        pltpu.SemaphoreType.DMA(())],
    )
    return pl.pallas_call(
        _dispatch_kernel,
        grid_spec=grid_spec,
        out_shape=jax.ShapeDtypeStruct((n_rows, w), U32),
        compiler_params=_cparams("arbitrary"),
        name="dispatch",
    )(pend, n_active, gd_flat, slott, gatest, h2b)


def _expert_kernel(be_ref, na_ref, xs_ref, w1_ref, b1_ref, w2_ref, b2_ref, y_ref, w1b_ref, w2b_ref, *, d_expert):
    i = pl.program_id(0)
    active = i < na_ref[0]
    prev = be_ref[jnp.maximum(i - 1, 0)]
    new_expert = jnp.logical_or(i == 0, be_ref[i] != prev)
    half = y_ref.shape[1]

    @pl.when(jnp.logical_and(active, new_expert))
    def _():
        w1b_ref[...] = w1_ref[...].astype(BF16)
        w2b_ref[...] = w2_ref[...].astype(BF16)

    @pl.when(active)
    def _():
        x = _unpack_bf16_pair(xs_ref[:, :half]).astype(BF16)
        gate = pltpu.bitcast(xs_ref[:, half:], F32)[:, 0:1]
        hc = jnp.dot(x, w1b_ref[...], preferred_element_type=F32) + b1_ref[...]
        g = jnp.minimum(hc[:, :d_expert], SWIGLU_LIMIT)
        u = jnp.clip(hc[:, d_expert:], -SWIGLU_LIMIT, SWIGLU_LIMIT)
        hm = (u + 1.0) * (g * jax.nn.sigmoid(SWIGLU_ALPHA * g))
        y = (jnp.dot(hm.astype(BF16), w2b_ref[...], preferred_element_type=F32) + b2_ref[...]) * gate
        y_ref[...] = _pack_bf16_pair(y[:, :half], y[:, half:])

    @pl.when(jnp.logical_not(active))
    def _():
        y_ref[...] = jnp.zeros(y_ref.shape, U32)


def _experts(block_e, n_active, xs, w1, b1, w2, b2):
    n_rows, w = xs.shape
    n_exp, d, two_de = w1.shape
    de = two_de // 2
    bm = EXPERT_BLOCK
    nb = n_rows // bm

    def blk(i, be, na):
        return jnp.maximum(jnp.minimum(i, na[0] - 1), 0)

    grid_spec = pltpu.PrefetchScalarGridSpec(
        num_scalar_prefetch=2,
        grid=(nb,),
        in_specs=[pl.BlockSpec((bm, w), lambda i, be, na: (blk(i, be, na), 0)),
                  pl.BlockSpec((None, d, two_de), lambda i, be, na: (be[blk(i, be, na)], 0, 0)),
                  pl.BlockSpec((None, 1, two_de), lambda i, be, na: (be[blk(i, be, na)], 0, 0)),
                  pl.BlockSpec((None, de, d), lambda i, be, na: (be[blk(i, be, na)], 0, 0)),
                  pl.BlockSpec((None, 1, d), lambda i, be, na: (be[blk(i, be, na)], 0, 0))],
        out_specs=pl.BlockSpec((bm, d // 2), lambda i, be, na: (i, 0)),
        scratch_shapes=[pltpu.VMEM((d, two_de), BF16), pltpu.VMEM((de, d), BF16)],
    )
    return pl.pallas_call(
        functools.partial(_expert_kernel, d_expert=de),
        grid_spec=grid_spec,
        out_shape=jax.ShapeDtypeStruct((n_rows, d // 2), U32),
        compiler_params=_cparams("arbitrary"),
        name="expert",
    )(block_e, n_active, xs, w1, b1[:, None, :], w2, b2[:, None, :])


def _combine_kernel(gd_ref, gd_next_ref, slot_ref, x1_ref, mod_ref, fg_ref, yb_ref, o_ref, sorted_scr, sem):
    tm = x1_ref.shape[0]
    step = pl.program_id(0)
    cur = step % 2

    def group_copy(buf, j, d):
        dst = sorted_scr.at[buf, pl.ds(pl.multiple_of(j * CHUNK, CHUNK), CHUNK)]
        return _row_copy(yb_ref.at[pl.ds(pl.multiple_of(d, CHUNK), CHUNK)], dst, sem.at[buf])

    def fetch(table_ref, buf):
        def issue(j, carry):
            group_copy(buf, j, table_ref[j]).start()
            return carry
        lax.fori_loop(0, table_ref[GROUP_TABLE - 1], issue, 0)

    @pl.when(step == 0)
    def _():
        sorted_scr[...] = jnp.zeros(sorted_scr.shape, sorted_scr.dtype)
        fetch(gd_ref, 0)

    @pl.when(step + 1 < pl.num_programs(0))
    def _():
        fetch(gd_next_ref, 1 - cur)

    def retire(j, carry):
        group_copy(cur, 0, 0).wait()
        return carry

    lax.fori_loop(0, gd_ref[GROUP_TABLE - 1], retire, 0)

    slot = slot_ref[...]
    scol = lax.broadcasted_iota(I32, (tm, SLOT_ROWS), 1)
    pick = jnp.zeros((tm, SLOT_ROWS), F32)
    for kk in range(TOP_K):
        pick = pick + jnp.where(scol == slot[:, kk:kk + 1], 1.0, 0.0)
    pick = pick.astype(BF16)
    words = sorted_scr[cur]
    lo = pltpu.bitcast(words << 16, F32).astype(BF16)
    hi = pltpu.bitcast(words & jnp.uint32(0xFFFF0000), F32).astype(BF16)
    y = jnp.concatenate([jnp.dot(pick, lo, preferred_element_type=F32),
                         jnp.dot(pick, hi, preferred_element_type=F32)], axis=1)
    x2 = x1_ref[...] + mod_ref[5:6, :] * y
    o_ref[...] = _rms(x2) * fg_ref[...]


def _combine(gd_flat, slot, x1, mod8, final_g, yb, seq):
    t, d = x1.shape
    tm = TOKEN_TILE
    nt = t // tm
    tiles_per_seq = seq // tm
    return pl.pallas_call(
        _combine_kernel,
        grid=(nt,),
        in_specs=[pl.BlockSpec((GROUP_TABLE,), lambda i: (i,), memory_space=pltpu.SMEM),
                  pl.BlockSpec((GROUP_TABLE,), lambda i: (jnp.minimum(i + 1, nt - 1),), memory_space=pltpu.SMEM),
                  pl.BlockSpec((tm, LANES), lambda i: (i, 0)),
                  pl.BlockSpec((tm, d), lambda i: (i, 0)),
                  pl.BlockSpec((None, SUBLANES, d), lambda i: (i // tiles_per_seq, 0, 0)),
                  pl.BlockSpec((1, d), lambda i: (0, 0)),
                  pl.BlockSpec(memory_space=pl.ANY)],
        out_specs=pl.BlockSpec((tm, d), lambda i: (i, 0)),
        out_shape=jax.ShapeDtypeStruct((t, d), F32),
        scratch_shapes=[pltpu.VMEM((2, SLOT_ROWS, d // 2), U32), pltpu.SemaphoreType.DMA((2,))],
        compiler_params=_cparams("arbitrary"),
        name="combine",
    )(gd_flat, gd_flat, slot, x1, mod8, final_g[None], yb)


def kernel(x, c, w_ada, b_ada, norm1_g, w_in, b_forget, conv_w, conv_b, conv_ln_g, conv_ln_b, w_conv_out,
           w_attn_out, w_out, norm2_g, w_router, b_router, w_exp_in, b_exp_in, w_exp_out, b_exp_out, final_g):
    bsz, seq, d = x.shape
    t = bsz * seq
    depth = w_ada.shape[0]
    xf = x.reshape(t, d)
    n_rows = t * TOP_K + (t // TOKEN_TILE) * N_EXPERTS * (CHUNK - 1) + N_EXPERTS * (EXPERT_BLOCK - CHUNK)
    n_rows = -(-n_rows // EXPERT_BLOCK) * EXPERT_BLOCK
    n_blocks = n_rows // EXPERT_BLOCK
    for l in range(depth):
        mod8 = _mod(c, w_ada[l], b_ada[l])
        a, q, k, v, aq, ak, tot, sgc, sga = _inproj(xf, mod8, norm1_g[l], w_in[l], b_forget[l], seq)
        act = _conv(a, conv_w[l], conv_b[l], conv_ln_g[l], conv_ln_b[l], bsz, seq)
        attn = _attention(q, aq, k, ak, v, tot, bsz, seq)
        x1, h2b, ids, gates = _merge(xf, act, attn, sgc, sga, mod8, norm2_g[l], w_conv_out[l], w_attn_out[l],
                                     w_out[l], w_router[l], b_router[l], seq)
        slot, slott, gatest, meta, cnt = _route(ids, gates)
        gd, blk = _dest(cnt, meta)
        gd_flat = gd.reshape(-1)
        block_e = blk[0].reshape(-1)[:n_blocks]
        n_active = blk[1, 0, :1]
        pend = blk[2, 0, :N_EXPERTS]
        xs = _dispatch(pend, n_active, gd_flat, slott, gatest, h2b, n_rows)
        yb = _experts(block_e, n_active, xs, w_exp_in[l], b_exp_in[l], w_exp_out[l], b_exp_out[l])
        assert depth == 1
        xf = _combine(gd_flat, slot, x1, mod8, final_g, yb, seq)
    return xf.reshape(bsz, seq, d)
```

```python
import functools

import jax
import jax.numpy as jnp
from jax import lax
from jax.experimental import pallas as pl
from jax.experimental.pallas import tpu as pltpu

F32 = jnp.float32
BF16 = jnp.bfloat16
I32 = jnp.int32
U32 = jnp.uint32
HIGHEST = lax.Precision.HIGHEST

LANES = 128
SUBLANES = 8
VMEM_LIMIT_BYTES = 56 * 1024 * 1024

N_HEADS = 8
HEAD_DIM = 64
CONV_KERNEL = 31
CONV_HALO = 32
N_EXPERTS = 32
TOP_K = 4
SWIGLU_LIMIT = 7.0
SWIGLU_ALPHA = 1.702
RMS_EPS = 1e-5
LN_EPS = 1e-5
N_MOD = 6
N_SPLIT = 3
LOG2_E = 1.4426950408889634
HEADS_PER_STEP = 2

INPROJ_TILE = 512
ATTN_BQ = 2048
ATTN_BK = 512
MERGE_TILE = 512
CONV_TILE = 256
TOKEN_TILE = 256
EXPERT_BLOCK = 512
CHUNK = SUBLANES
SLOT_ROWS = 1280
GROUP_TABLE = SUBLANES * LANES
assert SLOT_ROWS >= TOKEN_TILE * TOP_K + N_EXPERTS * (CHUNK - 1) and SLOT_ROWS // CHUNK < GROUP_TABLE


def _cparams(*sem):
    return pltpu.CompilerParams(dimension_semantics=sem, vmem_limit_bytes=VMEM_LIMIT_BYTES)


def _resident(shape):
    nd = len(shape)
    return pl.BlockSpec(shape, lambda *_: (0,) * nd, pipeline_mode=pl.Buffered(1))


def _split3(x):
    p1 = x.astype(BF16)
    r1 = x - p1.astype(F32)
    p2 = r1.astype(BF16)
    p3 = (r1 - p2.astype(F32)).astype(BF16)
    return jnp.concatenate([p1, p2, p3], axis=1)


def _rms(x):
    return x * lax.rsqrt(jnp.mean(x * x, axis=-1, keepdims=True) + RMS_EPS)


def _mod_kernel(ct_ref, w_ref, b_ref, o_ref, *, batch):
    rows = []
    for b in range(batch):
        col = ct_ref[:, b:b + 1]
        rows.append(jnp.sum((col * jax.nn.sigmoid(col)) * w_ref[...], axis=0, keepdims=True) + b_ref[...])
    rows.append(jnp.zeros((SUBLANES - batch, rows[0].shape[1]), F32))
    o_ref[...] = jnp.concatenate(rows, axis=0)


def _mod(c, w_ada, b_ada):
    bsz, d = c.shape
    n = w_ada.shape[1]
    tn = 1024
    out = pl.pallas_call(
        functools.partial(_mod_kernel, batch=bsz),
        grid=(n // tn,),
        in_specs=[pl.BlockSpec((d, bsz), lambda j: (0, 0)),
                  pl.BlockSpec((d, tn), lambda j: (0, j)),
                  pl.BlockSpec((1, tn), lambda j: (0, j))],
        out_specs=pl.BlockSpec((SUBLANES, tn), lambda j: (0, j)),
        out_shape=jax.ShapeDtypeStruct((SUBLANES, n), F32),
        compiler_params=_cparams("arbitrary"),
        name="mod",
    )(c.T, w_ada, b_ada[None])
    mod = out[:bsz].reshape(bsz, N_MOD, d)
    return jnp.pad(mod, ((0, 0), (0, SUBLANES - N_MOD), (0, 0)))


def _conv_module(a, first_of_seq, w_ref, b_ref, g_ref, beta_ref, xs_scr, ys_scr):
    ts = a.shape[0]
    xs_scr[0:CONV_HALO, :] = jnp.where(first_of_seq, 0.0, xs_scr[ts:ts + CONV_HALO, :])
    xs_scr[CONV_HALO:, :] = a
    acc = jnp.zeros(a.shape, F32) + b_ref[...]
    base = CONV_HALO - (CONV_KERNEL - 1)
    for rho in range(SUBLANES):
        offs = [o for o in range(base, base + CONV_KERNEL) if o % SUBLANES == rho]
        if not offs:
            continue
        rows = offs[-1] + ts - rho
        ys_scr[0:rows, :] = xs_scr[rho:rho + rows, :]
        for o in offs:
            lo = o - rho
            acc = acc + w_ref[o - base:o - base + 1, :] * ys_scr[lo:lo + ts, :]
    mu = jnp.mean(acc, axis=-1, keepdims=True)
    cen = acc - mu
    var = jnp.mean(cen * cen, axis=-1, keepdims=True)
    y = cen * lax.rsqrt(var + LN_EPS) * g_ref[...] + beta_ref[...]
    return (y * jax.nn.sigmoid(y)).astype(BF16)


def _inproj_kernel(x_ref, mod_ref, g_ref, wa_ref, wqkv_ref, wf_ref, wg_ref, bf_ref,
                   tri_ref, eq_ref, ek_ref, oneq_ref, onek_ref, cw_ref, cb_ref, cg_ref, cbeta_ref,
                   act_ref, q_ref, k_ref, v_ref, aq_ref, ak_ref, tot_ref, sgc_ref, sga_ref,
                   carry_scr, xs_scr, ys_scr, *, conv_width, d_model, tiles_per_seq):
    first_of_seq = pl.program_id(0) % tiles_per_seq == 0

    @pl.when(pl.program_id(0) == 0)
    def _():
        xs_scr[...] = jnp.zeros(xs_scr.shape, F32)
        carry_scr[...] = jnp.zeros(carry_scr.shape, F32)

    x = x_ref[...]
    h = _rms(x) * g_ref[...] * (1.0 + mod_ref[1:2, :]) + mod_ref[0:1, :]
    hb = h.astype(BF16)

    u = jnp.dot(hb, wa_ref[...], preferred_element_type=F32)
    a = u[:, :conv_width] * jax.nn.sigmoid(u[:, conv_width:])
    act_ref[...] = _conv_module(a, first_of_seq, cw_ref, cb_ref, cg_ref, cbeta_ref, xs_scr, ys_scr)

    fl = jnp.dot(hb, wf_ref[...], preferred_element_type=F32) + bf_ref[...]
    lane = lax.broadcasted_iota(I32, fl.shape, 1)
    log_f = -(jnp.maximum(-fl, 0.0) + jnp.log1p(jnp.exp(-jnp.abs(fl))))
    log_f = jnp.where(lane < N_HEADS, log_f, 0.0)
    cs = jnp.dot(tri_ref[...], _split3(log_f), preferred_element_type=F32)
    cum = (cs[:, :LANES] + cs[:, LANES:2 * LANES] + cs[:, 2 * LANES:]) * LOG2_E
    tm = cum.shape[0]

    pieces = []
    for sb in range(tm // ATTN_BK):
        blk = cum[sb * ATTN_BK:(sb + 1) * ATTN_BK, :]
        if sb > 0:
            blk = blk - cum[sb * ATTN_BK - 1:sb * ATTN_BK, :]
        tot_ref[sb] = blk[ATTN_BK - 1:, :]
        pieces.append(blk)
    cum_k = pieces[0] if len(pieces) == 1 else jnp.concatenate(pieces, axis=0)

    first_of_block = pl.program_id(0) % (ATTN_BQ // tm) == 0
    cum_q = cum + jnp.where(first_of_block, 0.0, carry_scr[0:1, :])
    carry_scr[...] = jnp.broadcast_to(cum_q[tm - 1:, :], carry_scr.shape)

    qkv = jnp.dot(hb, wqkv_ref[...], preferred_element_type=F32)
    aw = qkv.shape[1] // 3
    q_ref[...] = (qkv[:, :aw] * (HEAD_DIM ** -0.5 * LOG2_E)).astype(BF16)
    k_ref[...] = qkv[:, aw:2 * aw].astype(BF16)
    v_ref[...] = qkv[:, 2 * aw:].astype(BF16)
    aq_ref[...] = (jnp.dot(_split3(cum_q), eq_ref[...], preferred_element_type=F32) + oneq_ref[...]).astype(BF16)
    ak_ref[...] = (jnp.dot(_split3(cum_k), ek_ref[...], preferred_element_type=F32) + onek_ref[...]).astype(BF16)

    gts = jnp.dot(hb, wg_ref[...], preferred_element_type=F32)
    sgc_ref[...] = jax.nn.sigmoid(gts[:, :d_model]).astype(BF16)
    sga_ref[...] = jax.nn.sigmoid(gts[:, d_model:]).astype(BF16)


def _aug_constants():
    rows = jnp.arange(N_SPLIT * LANES)
    piece, head = rows // LANES, rows % LANES
    cols = jnp.arange(N_HEADS * HEAD_DIM)
    pair, off = cols // LANES, cols % LANES
    chead = 2 * pair + jnp.where(off >= HEAD_DIM, 0, 1)
    pos = off % HEAD_DIM
    valid = (head[:, None] < N_HEADS) & (head[:, None] == chead[None, :])
    eq = jnp.where(valid & (pos[None, :] == piece[:, None]), 1.0, 0.0).astype(BF16)
    ek = jnp.where(valid & (pos[None, :] == piece[:, None] + N_SPLIT), -1.0, 0.0).astype(BF16)
    oneq = jnp.where((pos >= N_SPLIT) & (pos < 2 * N_SPLIT), 1.0, 0.0).astype(F32)[None]
    onek = jnp.where(pos < N_SPLIT, 1.0, 0.0).astype(F32)[None]
    return eq, ek, oneq, onek


def _inproj(x2, mod8, norm_g, w_in, b_forget, conv_w, conv_b, conv_ln_g, conv_ln_b, seq):
    t, d = x2.shape
    tm = INPROJ_TILE
    nsub = tm // ATTN_BK
    cw = d // 2
    aw = N_HEADS * HEAD_DIM
    o0 = 2 * cw
    wa = w_in[:, :o0].astype(BF16)
    o1 = o0 + 3 * aw
    wqkv = w_in[:, o0:o1].astype(BF16)
    wf = jnp.pad(w_in[:, o1:o1 + N_HEADS], ((0, 0), (0, LANES - N_HEADS))).astype(BF16)
    bfp = jnp.pad(b_forget, (0, LANES - N_HEADS))[None]
    wg = w_in[:, o1 + N_HEADS:].astype(BF16)
    r = jnp.arange(tm)
    tri = (r[None, :] <= r[:, None]).astype(BF16)
    eq, ek, oneq, onek = _aug_constants()
    tiles_per_seq = seq // tm
    row = lambda i: (i, 0)
    outs = pl.pallas_call(
        functools.partial(_inproj_kernel, conv_width=cw, d_model=d, tiles_per_seq=tiles_per_seq),
        grid=(t // tm,),
        in_specs=[pl.BlockSpec((tm, d), row),
                  pl.BlockSpec((None, SUBLANES, d), lambda i: (i // tiles_per_seq, 0, 0)),
                  _resident((1, d)),
                  _resident(wa.shape), _resident(wqkv.shape),
                  _resident(wf.shape), _resident(wg.shape), _resident(bfp.shape),
                  _resident(tri.shape), _resident(eq.shape), _resident(ek.shape),
                  _resident(oneq.shape), _resident(onek.shape),
                  _resident(conv_w.shape), _resident((1, cw)), _resident((1, cw)), _resident((1, cw))],
        out_specs=[pl.BlockSpec((tm, cw), row)] + [pl.BlockSpec((tm, aw), row)] * 5 + [
                   pl.BlockSpec((nsub, 1, LANES), lambda i: (i, 0, 0)),
                   pl.BlockSpec((tm, d), row), pl.BlockSpec((tm, d), row)],
        out_shape=[jax.ShapeDtypeStruct((t, cw), BF16)] + [jax.ShapeDtypeStruct((t, aw), BF16)] * 5 + [
                   jax.ShapeDtypeStruct((t // ATTN_BK, 1, LANES), F32),
                   jax.ShapeDtypeStruct((t, d), BF16), jax.ShapeDtypeStruct((t, d), BF16)],
        scratch_shapes=[pltpu.VMEM((SUBLANES, LANES), F32),
                        pltpu.VMEM((CONV_HALO + tm, cw), F32), pltpu.VMEM((CONV_HALO + tm, cw), F32)],
        compiler_params=_cparams("arbitrary"),
        name="inproj",
    )(x2, mod8, norm_g[None], wa, wqkv, wf, wg, bfp, tri, eq, ek, oneq, onek,
      conv_w, conv_b[None], conv_ln_g[None], conv_ln_b[None])
    return outs


def _attn_kernel(q_ref, aq_ref, k_ref, ak_ref, v_ref, tot_ref, o_ref, qop_scr, m_scr, acc_scr):
    bq = q_ref.shape[0]
    bk = ATTN_BK
    ratio = bq // bk
    qi = pl.program_id(2)
    nt = (((1,), (1,)), ((), ()))
    heads = range(HEADS_PER_STEP)
    zero = jnp.zeros((1, LANES), F32)

    def splice(hh, main, extra):
        first = lax.broadcasted_iota(I32, main.shape, 1) < HEAD_DIM
        return jnp.where(first, main, extra) if hh == 0 else jnp.where(first, extra, main)

    for hh in heads:
        qop_scr[hh] = splice(hh, q_ref[...], aq_ref[...])

    def scores(hh, kb, row0):
        start = pl.multiple_of(kb * bk, bk)
        k = splice(hh, k_ref[pl.ds(start, bk), :], ak_ref[pl.ds(start, bk), :])
        v = v_ref[pl.ds(start, bk), :]
        v = splice(hh, v, jnp.ones(v.shape, v.dtype))
        s = lax.dot_general(qop_scr[hh, row0:, :], k, nt, preferred_element_type=F32)
        return s, v

    def update(hh, s, v, delta, row0):
        m_prev = m_scr[hh, row0:, :]
        m_new = jnp.maximum(m_prev, jnp.max(s, axis=1, keepdims=True) + delta)
        shift = m_new - delta
        p = jnp.exp2(s - shift[:, :1])
        alpha = jnp.exp2(m_prev - m_new)
        pv = jnp.dot(p.astype(BF16), v, preferred_element_type=F32)
        acc_scr[hh, row0:, :] = alpha * acc_scr[hh, row0:, :] + pv
        m_scr[hh, row0:, :] = m_new

    for hh in heads:
        delta = zero
        for sb in range(ratio):
            kb = qi * ratio + sb
            row0 = sb * bk
            s, v = scores(hh, kb, row0)
            row = lax.broadcasted_iota(I32, s.shape, 0)
            col = lax.broadcasted_iota(I32, s.shape, 1)
            s = jnp.where(col <= row, s, -jnp.inf)
            if sb == 0:
                m0 = jnp.max(s, axis=1, keepdims=True)
                p = jnp.exp2(s - m0)
                m_scr[hh] = jnp.broadcast_to(m0, m_scr.shape[1:])
                acc_scr[hh] = jnp.dot(p.astype(BF16), v, preferred_element_type=F32)
            else:
                update(hh, s, v, delta, row0)
            delta = delta - tot_ref[hh, pl.ds(kb, 1), :]

    def body(step, deltas):
        kb = qi * ratio - 1 - step
        new_deltas = []
        for hh in heads:
            delta = deltas[hh] + tot_ref[hh, pl.ds(kb, 1), :]
            s, v = scores(hh, kb, 0)
            update(hh, s, v, delta, 0)
            new_deltas.append(delta)
        return tuple(new_deltas)

    lax.fori_loop(0, qi * ratio, body, tuple(zero for _ in heads))
    acc0, acc1 = acc_scr[0], acc_scr[1]
    out0 = acc0 / acc0[:, HEAD_DIM:HEAD_DIM + 1]
    out1 = acc1 / acc1[:, 0:1]
    o_ref[...] = splice(0, out0, out1).astype(BF16)


def _attention(q, aq, k, ak, v, tot, bsz, seq):
    t, aw = q.shape
    bq = ATTN_BQ
    nq = seq // bq
    nk = seq // ATTN_BK
    assert HEADS_PER_STEP * HEAD_DIM == LANES
    tot = tot.reshape(bsz, nk, LANES)[:, :, :N_HEADS].transpose(0, 2, 1)
    tot = jnp.broadcast_to(tot[..., None], (bsz, N_HEADS, nk, LANES))
    qblk = pl.BlockSpec((bq, LANES), lambda b, h, i: (b * nq + i, h))
    kblk = pl.BlockSpec((None, seq, LANES), lambda b, h, i: (b, 0, h), pipeline_mode=pl.Buffered(1))
    seq3 = lambda a: a.reshape(bsz, seq, aw)
    return pl.pallas_call(
        _attn_kernel,
        grid=(bsz, N_HEADS // HEADS_PER_STEP, nq),
        in_specs=[qblk, qblk, kblk, kblk, kblk,
                  pl.BlockSpec((None, HEADS_PER_STEP, nk, LANES), lambda b, h, i: (b, h, 0, 0))],
        out_specs=qblk,
        out_shape=jax.ShapeDtypeStruct((t, aw), BF16),
        scratch_shapes=[pltpu.VMEM((HEADS_PER_STEP, bq, LANES), BF16),
                        pltpu.VMEM((HEADS_PER_STEP, bq, LANES), F32),
                        pltpu.VMEM((HEADS_PER_STEP, bq, LANES), F32)],
        compiler_params=_cparams("arbitrary", "arbitrary", "arbitrary"),
        name="attn",
    )(q, aq, seq3(k), seq3(ak), seq3(v), tot)


def _pack_bf16_pair(lo, hi):
    lo_bits = pltpu.bitcast(lo.astype(BF16).astype(F32), U32)
    hi_bits = pltpu.bitcast(hi.astype(BF16).astype(F32), U32)
    return (hi_bits & jnp.uint32(0xFFFF0000)) | (lo_bits >> 16)


def _unpack_bf16_pair(w):
    lo = pltpu.bitcast(w << 16, F32)
    hi = pltpu.bitcast(w & jnp.uint32(0xFFFF0000), F32)
    return jnp.concatenate([lo, hi], axis=1)


def _merge_kernel(x_ref, act_ref, attn_ref, sgc_ref, sga_ref, mod_ref, g2_ref, wco_ref, wao_ref, wout_ref,
                  wr_ref, br_ref, x1_ref, h2_ref, ids_ref, gates_ref):
    conv_out = jnp.dot(act_ref[...], wco_ref[...], preferred_element_type=F32)
    attn_out = jnp.dot(attn_ref[...], wao_ref[...], preferred_element_type=F32)
    mixed = sgc_ref[...].astype(F32) * conv_out + sga_ref[...].astype(F32) * attn_out
    upd = jnp.dot(mixed.astype(BF16), wout_ref[...], preferred_element_type=F32)
    x1 = x_ref[...] + mod_ref[2:3, :] * upd
    x1_ref[...] = x1
    h2 = _rms(x1) * g2_ref[...] * (1.0 + mod_ref[4:5, :]) + mod_ref[3:4, :]
    h2_hi = h2.astype(BF16)
    h2_ref[...] = h2_hi
    h2_lo = (h2 - h2_hi.astype(F32)).astype(BF16)
    h2_cat = jnp.concatenate([h2_hi, h2_lo, h2_hi], axis=1)
    logits = jnp.dot(h2_cat, wr_ref[...], preferred_element_type=F32) + br_ref[...]
    lane = lax.broadcasted_iota(I32, logits.shape, 1)
    work = jnp.where(lane < N_EXPERTS, logits, -jnp.inf)
    ids = jnp.zeros(logits.shape, I32)
    vals = []
    for kk in range(TOP_K):
        mx = jnp.max(work, axis=1, keepdims=True)
        idx = jnp.min(jnp.where(work == mx, lane, LANES), axis=1, keepdims=True)
        ids = jnp.where(lane == kk, idx, ids)
        vals.append(mx)
        work = jnp.where(lane == idx, -jnp.inf, work)
    exps = [jnp.exp(vv - vals[0]) for vv in vals]
    den = exps[0] + exps[1] + exps[2] + exps[3]
    gates = jnp.zeros(logits.shape, F32)
    for kk in range(TOP_K):
        gates = jnp.where(lane == kk, exps[kk] / den, gates)
    ids_ref[...] = ids
    gates_ref[...] = gates


def _merge(x2, act, attn, sgc, sga, mod8, norm2_g, w_conv_out, w_attn_out, w_out, w_router, b_router, seq):
    t, d = x2.shape
    tm = MERGE_TILE
    cw = act.shape[1]
    sw = attn.shape[1]
    wco = w_conv_out.astype(BF16)
    wao = w_attn_out.astype(BF16)
    wout = w_out.astype(BF16)
    wr = jnp.pad(w_router, ((0, 0), (0, LANES - N_EXPERTS)))
    wr_hi = wr.astype(BF16)
    wr_lo = (wr - wr_hi.astype(F32)).astype(BF16)
    wr = jnp.concatenate([wr_hi, wr_hi, wr_lo], axis=0)
    br = jnp.pad(b_router, (0, LANES - N_EXPERTS))[None]
    tiles_per_seq = seq // tm
    row = lambda i: (i, 0)
    return pl.pallas_call(
        _merge_kernel,
        grid=(t // tm,),
        in_specs=[pl.BlockSpec((tm, d), row), pl.BlockSpec((tm, cw), row), pl.BlockSpec((tm, sw), row),
                  pl.BlockSpec((tm, d), row), pl.BlockSpec((tm, d), row),
                  pl.BlockSpec((None, SUBLANES, d), lambda i: (i // tiles_per_seq, 0, 0)),
                  _resident((1, d)), _resident(wco.shape), _resident(wao.shape), _resident(wout.shape),
                  _resident(wr.shape), _resident(br.shape)],
        out_specs=[pl.BlockSpec((tm, d), row), pl.BlockSpec((tm, d), row),
                   pl.BlockSpec((tm, LANES), row), pl.BlockSpec((tm, LANES), row)],
        out_shape=[jax.ShapeDtypeStruct((t, d), F32), jax.ShapeDtypeStruct((t, d), BF16),
                   jax.ShapeDtypeStruct((t, LANES), I32), jax.ShapeDtypeStruct((t, LANES), F32)],
        compiler_params=_cparams("arbitrary"),
        name="merge",
    )(x2, act, attn, sgc, sga, mod8, norm2_g[None], wco, wao, wout, wr, br)


def _expert_onehots(ids):
    lane = lax.broadcasted_iota(I32, ids.shape, 1)
    return lane, [lane == ids[:, kk:kk + 1] for kk in range(TOP_K)]


def _route_kernel(ids_ref, gates_ref, tri_ref, utri_ref, slot_ref, slott_ref, gatest_ref, meta_ref,
                  cnt_ref, carry_scr):
    @pl.when(pl.program_id(0) == 0)
    def _():
        carry_scr[...] = jnp.zeros(carry_scr.shape, F32)

    lane, hots = _expert_onehots(ids_ref[...])
    multi = jnp.zeros(lane.shape, F32)
    for hot in hots:
        multi = multi + jnp.where(hot, 1.0, 0.0)
    carry = carry_scr[0:1, :]
    before = jnp.dot(tri_ref[...], multi.astype(BF16), preferred_element_type=F32)
    units = jnp.ceil(jnp.sum(multi, axis=0, keepdims=True) * (1.0 / CHUNK))
    units8 = jnp.broadcast_to(units, (SUBLANES, LANES)).astype(BF16)
    lstart = jnp.dot(units8, utri_ref[...], preferred_element_type=F32)[0:1, :] * CHUNK
    csize = units * CHUNK
    slot = jnp.zeros(lane.shape, F32)
    for kk, hot in enumerate(hots):
        sl = jnp.sum(jnp.where(hot, lstart + before, 0.0), axis=1, keepdims=True)
        slot = jnp.where(lane == kk, sl, slot)
    slot_ref[...] = slot.astype(I32)
    slott_ref[...] = slot.T[0:SUBLANES, :].astype(I32)
    gatest_ref[...] = gates_ref[...].T[0:SUBLANES, :]
    row = lax.broadcasted_iota(I32, (SUBLANES, LANES), 0)
    meta = jnp.where(row == 0, csize, jnp.where(row == 1, lstart, jnp.where(row == 2, carry, 0.0)))
    meta_ref[0] = meta
    carry = carry + csize
    carry_scr[...] = jnp.broadcast_to(carry, carry_scr.shape)
    cnt_ref[...] = jnp.broadcast_to(carry, cnt_ref.shape)


def _route(ids, gates):
    t = ids.shape[0]
    tm = TOKEN_TILE
    nt = t // tm
    r = jnp.arange(tm)
    tri = (r[None, :] < r[:, None]).astype(BF16)
    e = jnp.arange(LANES)
    utri = (e[:, None] < e[None, :]).astype(BF16)
    tile = pl.BlockSpec((tm, LANES), lambda i: (i, 0))
    return pl.pallas_call(
        _route_kernel,
        grid=(nt,),
        in_specs=[tile, tile, _resident(tri.shape), _resident(utri.shape)],
        out_specs=[tile, pl.BlockSpec((SUBLANES, tm), lambda i: (i, 0)),
                   pl.BlockSpec((SUBLANES, tm), lambda i: (i, 0)),
                   pl.BlockSpec((1, SUBLANES, LANES), lambda i: (i, 0, 0)),
                   pl.BlockSpec((SUBLANES, LANES), lambda i: (0, 0))],
        out_shape=[jax.ShapeDtypeStruct((t, LANES), I32), jax.ShapeDtypeStruct((nt * SUBLANES, tm), I32),
                   jax.ShapeDtypeStruct((nt * SUBLANES, tm), F32),
                   jax.ShapeDtypeStruct((nt, SUBLANES, LANES), F32),
                   jax.ShapeDtypeStruct((SUBLANES, LANES), F32)],
        scratch_shapes=[pltpu.VMEM((SUBLANES, LANES), F32)],
        compiler_params=_cparams("arbitrary"),
        name="route",
    )(ids, gates, tri, utri)


def _dest_kernel(cnt_ref, meta_ref, tri_ref, gd_ref, blk_ref):
    cnt = cnt_ref[...]
    padded = jnp.ceil(cnt * (1.0 / EXPERT_BLOCK)) * EXPERT_BLOCK
    pend = jnp.dot(padded, tri_ref[...], preferred_element_type=F32, precision=HIGHEST)
    pstart = (pend - padded)[0:1, :]

    meta = meta_ref[0]
    csize, lstart, seg_off = meta[0:1, :], meta[1:2, :], meta[2:3, :]
    lend = lstart + csize
    base = pstart + seg_off - lstart
    grow = lax.broadcasted_iota(I32, (SUBLANES, LANES), 0)
    glane = lax.broadcasted_iota(I32, (SUBLANES, LANES), 1)
    gpos = ((grow * LANES + glane) * CHUNK).astype(F32)
    owner = jnp.zeros(gpos.shape, I32)
    for e in range(N_EXPERTS):
        owner = owner + jnp.where(gpos >= lend[0:1, e:e + 1], 1, 0)
    gdst = gpos
    for e in range(N_EXPERTS):
        gdst = gdst + jnp.where(owner == e, base[0:1, e:e + 1], 0.0)
    n_groups = lend[0:1, N_EXPERTS - 1:N_EXPERTS] * (1.0 / CHUNK)
    last = jnp.logical_and(grow == SUBLANES - 1, glane == LANES - 1)
    gd_ref[0] = jnp.where(last, n_groups, gdst).astype(I32)

    @pl.when(pl.program_id(0) == 0)
    def _():
        shape = blk_ref.shape[1:]
        pos = (lax.broadcasted_iota(I32, shape, 0) * LANES + lax.broadcasted_iota(I32, shape, 1))
        pos = pos.astype(F32) * EXPERT_BLOCK
        be = jnp.zeros(shape, I32)
        for e in range(N_EXPERTS):
            be = be + jnp.where(pos >= pend[0:1, e:e + 1], 1, 0)
        blk_ref[0] = jnp.minimum(be, N_EXPERTS - 1)
        nact = pend[0:1, N_EXPERTS - 1:N_EXPERTS] * (1.0 / EXPERT_BLOCK)
        blk_ref[1] = jnp.broadcast_to(nact, shape).astype(I32)
        blk_ref[2] = jnp.broadcast_to(pend[0:1, :], shape).astype(I32)


def _dest(cnt, meta):
    nt = meta.shape[0]
    r = jnp.arange(LANES)
    tri = (r[:, None] <= r[None, :]).astype(F32)
    return pl.pallas_call(
        _dest_kernel,
        grid=(nt,),
        in_specs=[pl.BlockSpec((SUBLANES, LANES), lambda i: (0, 0)),
                  pl.BlockSpec((1, SUBLANES, LANES), lambda i: (i, 0, 0)),
                  pl.BlockSpec((LANES, LANES), lambda i: (0, 0))],
        out_specs=[pl.BlockSpec((1, SUBLANES, LANES), lambda i: (i, 0, 0)),
                   pl.BlockSpec((3, SUBLANES, LANES), lambda i: (0, 0, 0))],
        out_shape=[jax.ShapeDtypeStruct((nt, SUBLANES, LANES), I32),
                   jax.ShapeDtypeStruct((3, SUBLANES, LANES), I32)],
        compiler_params=_cparams("arbitrary"),
        name="dest",
    )(cnt, meta, tri)


def _row_copy(src, dst, sem):
    return pltpu.make_async_copy(src, dst, sem)


def _dispatch_kernel(pend_ref, na_ref, gd_ref, slott_ref, gatest_ref, h2_ref, xs_ref, zero_ref, sorted_scr, sem):
    tm = h2_ref.shape[0]
    bm = zero_ref.shape[0]
    nb = xs_ref.shape[0] // bm

    @pl.when(pl.program_id(0) == 0)
    def _():
        zero_ref[...] = jnp.zeros(zero_ref.shape, zero_ref.dtype)

        def zero_block(start):
            return _row_copy(zero_ref, xs_ref.at[pl.ds(pl.multiple_of(start, bm), bm)], sem)

        def nonempty(e):
            return pend_ref[e] > (pend_ref[e - 1] if e > 0 else 0)

        for e in range(N_EXPERTS):
            @pl.when(nonempty(e))
            def _():
                zero_block(pend_ref[e] - bm).start()

        def tail(b, carry):
            zero_block(b * bm).start()
            return carry

        lax.fori_loop(na_ref[0], nb, tail, 0)
        for e in range(N_EXPERTS):
            @pl.when(nonempty(e))
            def _():
                zero_block(0).wait()

        def tail_wait(b, carry):
            zero_block(0).wait()
            return carry

        lax.fori_loop(na_ref[0], nb, tail_wait, 0)

    srow = lax.broadcasted_iota(I32, (SLOT_ROWS, tm), 0)
    perm = jnp.zeros((SLOT_ROWS, tm), F32)
    gate = jnp.zeros((SLOT_ROWS, 1), F32)
    for kk in range(TOP_K):
        hit = srow == slott_ref[kk:kk + 1, :]
        perm = perm + jnp.where(hit, 1.0, 0.0)
        gate = gate + jnp.sum(jnp.where(hit, gatest_ref[kk:kk + 1, :], 0.0), axis=1, keepdims=True)
    perm = perm.astype(BF16)
    half = h2_ref.shape[1] // 2
    lo = jnp.dot(perm, h2_ref[:, :half], preferred_element_type=F32)
    hi = jnp.dot(perm, h2_ref[:, half:], preferred_element_type=F32)
    sorted_scr[:, :half] = _pack_bf16_pair(lo, hi)
    sorted_scr[:, half:] = pltpu.bitcast(jnp.broadcast_to(gate, (SLOT_ROWS, LANES)), U32)

    n_groups = gd_ref[GROUP_TABLE - 1]

    def group_copy(j, d):
        src = sorted_scr.at[pl.ds(pl.multiple_of(j * CHUNK, CHUNK), CHUNK)]
        return _row_copy(src, xs_ref.at[pl.ds(pl.multiple_of(d, CHUNK), CHUNK)], sem)

    def issue(j, carry):
        group_copy(j, gd_ref[j]).start()
        return carry

    def retire(j, carry):
        group_copy(0, 0).wait()
        return carry

    lax.fori_loop(0, n_groups, issue, 0)
    lax.fori_loop(0, n_groups, retire, 0)


def _dispatch(pend, n_active, gd_flat, slott, gatest, h2b, n_rows):
    t, d = h2b.shape
    w = d // 2 + LANES
    tm = TOKEN_TILE
    grid_spec = pltpu.PrefetchScalarGridSpec(
        num_scalar_prefetch=2,
        grid=(t // tm,),
        in_specs=[pl.BlockSpec((GROUP_TABLE,), lambda i, pe, na: (i,), memory_space=pltpu.SMEM),
                  pl.BlockSpec((SUBLANES, tm), lambda i, pe, na: (i, 0)),
                  pl.BlockSpec((SUBLANES, tm), lambda i, pe, na: (i, 0)),
                  pl.BlockSpec((tm, d), lambda i, pe, na: (i, 0))],
        out_specs=pl.BlockSpec(memory_space=pl.ANY),
        scratch_shapes=[pltpu.VMEM((EXPERT_BLOCK, w), U32), pltpu.VMEM((SLOT_ROWS, w), U32),
                        pltpu.SemaphoreType.DMA(())],
    )
    return pl.pallas_call(
        _dispatch_kernel,
        grid_spec=grid_spec,
        out_shape=jax.ShapeDtypeStruct((n_rows, w), U32),
        compiler_params=_cparams("arbitrary"),
        name="dispatch",
    )(pend, n_active, gd_flat, slott, gatest, h2b)


def _expert_kernel(be_ref, na_ref, xs_ref, w1_ref, b1_ref, w2_ref, b2_ref, y_ref, w1b_ref, w2b_ref, *, d_expert):
    i = pl.program_id(0)
    active = i < na_ref[0]
    prev = be_ref[jnp.maximum(i - 1, 0)]
    new_expert = jnp.logical_or(i == 0, be_ref[i] != prev)
    half = y_ref.shape[1]

    @pl.when(jnp.logical_and(active, new_expert))
    def _():
        w1b_ref[...] = w1_ref[...].astype(BF16)
        w2b_ref[...] = w2_ref[...].astype(BF16)

    @pl.when(active)
    def _():
        x = _unpack_bf16_pair(xs_ref[:, :half]).astype(BF16)
        gate = pltpu.bitcast(xs_ref[:, half:], F32)[:, 0:1]
        hc = jnp.dot(x, w1b_ref[...], preferred_element_type=F32) + b1_ref[...]
        g = jnp.minimum(hc[:, :d_expert], SWIGLU_LIMIT)
        u = jnp.clip(hc[:, d_expert:], -SWIGLU_LIMIT, SWIGLU_LIMIT)
        hm = (u + 1.0) * (g * jax.nn.sigmoid(SWIGLU_ALPHA * g))
        y = (jnp.dot(hm.astype(BF16), w2b_ref[...], preferred_element_type=F32) + b2_ref[...]) * gate
        y_ref[...] = _pack_bf16_pair(y[:, :half], y[:, half:])

    @pl.when(jnp.logical_not(active))
    def _():
        y_ref[...] = jnp.zeros(y_ref.shape, U32)


def _experts(block_e, n_active, xs, w1, b1, w2, b2):
    n_rows, w = xs.shape
    n_exp, d, two_de = w1.shape
    de = two_de // 2
    bm = EXPERT_BLOCK
    nb = n_rows // bm

    def blk(i, be, na):
        return jnp.maximum(jnp.minimum(i, na[0] - 1), 0)

    grid_spec = pltpu.PrefetchScalarGridSpec(
        num_scalar_prefetch=2,
        grid=(nb,),
        in_specs=[pl.BlockSpec((bm, w), lambda i, be, na: (blk(i, be, na), 0)),
                  pl.BlockSpec((None, d, two_de), lambda i, be, na: (be[blk(i, be, na)], 0, 0)),
                  pl.BlockSpec((None, 1, two_de), lambda i, be, na: (be[blk(i, be, na)], 0, 0)),
                  pl.BlockSpec((None, de, d), lambda i, be, na: (be[blk(i, be, na)], 0, 0)),
                  pl.BlockSpec((None, 1, d), lambda i, be, na: (be[blk(i, be, na)], 0, 0))],
        out_specs=pl.BlockSpec((bm, d // 2), lambda i, be, na: (i, 0)),
        scratch_shapes=[pltpu.VMEM((d, two_de), BF16), pltpu.VMEM((de, d), BF16)],
    )
    return pl.pallas_call(
        functools.partial(_expert_kernel, d_expert=de),
        grid_spec=grid_spec,
        out_shape=jax.ShapeDtypeStruct((n_rows, d // 2), U32),
        compiler_params=_cparams("arbitrary"),
        name="expert",
    )(block_e, n_active, xs, w1, b1[:, None, :], w2, b2[:, None, :])


def _combine_kernel(gd_ref, gd_next_ref, slot_ref, x1_ref, mod_ref, fg_ref, yb_ref, o_ref, sorted_scr, sem):
    tm = x1_ref.shape[0]
    step = pl.program_id(0)
    cur = step % 2

    def group_copy(buf, j, d):
        dst = sorted_scr.at[buf, pl.ds(pl.multiple_of(j * CHUNK, CHUNK), CHUNK)]
        return _row_copy(yb_ref.at[pl.ds(pl.multiple_of(d, CHUNK), CHUNK)], dst, sem.at[buf])

    def fetch(table_ref, buf):
        def issue(j, carry):
            group_copy(buf, j, table_ref[j]).start()
            return carry
        lax.fori_loop(0, table_ref[GROUP_TABLE - 1], issue, 0)

    @pl.when(step == 0)
    def _():
        sorted_scr[...] = jnp.zeros(sorted_scr.shape, sorted_scr.dtype)
        fetch(gd_ref, 0)

    @pl.when(step + 1 < pl.num_programs(0))
    def _():
        fetch(gd_next_ref, 1 - cur)

    def retire(j, carry):
        group_copy(cur, 0, 0).wait()
        return carry

    lax.fori_loop(0, gd_ref[GROUP_TABLE - 1], retire, 0)

    slot = slot_ref[...]
    scol = lax.broadcasted_iota(I32, (tm, SLOT_ROWS), 1)
    pick = jnp.zeros((tm, SLOT_ROWS), F32)
    for kk in range(TOP_K):
        pick = pick + jnp.where(scol == slot[:, kk:kk + 1], 1.0, 0.0)
    pick = pick.astype(BF16)
    words = sorted_scr[cur]
    lo = pltpu.bitcast(words << 16, F32).astype(BF16)
    hi = pltpu.bitcast(words & jnp.uint32(0xFFFF0000), F32).astype(BF16)
    y = jnp.concatenate([jnp.dot(pick, lo, preferred_element_type=F32),
                         jnp.dot(pick, hi, preferred_element_type=F32)], axis=1)
    x2 = x1_ref[...] + mod_ref[5:6, :] * y
    o_ref[...] = _rms(x2) * fg_ref[...]


def _combine(gd_flat, slot, x1, mod8, final_g, yb, seq):
    t, d = x1.shape
    tm = TOKEN_TILE
    nt = t // tm
    tiles_per_seq = seq // tm
    return pl.pallas_call(
        _combine_kernel,
        grid=(nt,),
        in_specs=[pl.BlockSpec((GROUP_TABLE,), lambda i: (i,), memory_space=pltpu.SMEM),
                  pl.BlockSpec((GROUP_TABLE,), lambda i: (jnp.minimum(i + 1, nt - 1),), memory_space=pltpu.SMEM),
                  pl.BlockSpec((tm, LANES), lambda i: (i, 0)),
                  pl.BlockSpec((tm, d), lambda i: (i, 0)),
                  pl.BlockSpec((None, SUBLANES, d), lambda i: (i // tiles_per_seq, 0, 0)),
                  pl.BlockSpec((1, d), lambda i: (0, 0)),
                  pl.BlockSpec(memory_space=pl.ANY)],
        out_specs=pl.BlockSpec((tm, d), lambda i: (i, 0)),
        out_shape=jax.ShapeDtypeStruct((t, d), F32),
        scratch_shapes=[pltpu.VMEM((2, SLOT_ROWS, d // 2), U32), pltpu.SemaphoreType.DMA((2,))],
        compiler_params=_cparams("arbitrary"),
        name="combine",
    )(gd_flat, gd_flat, slot, x1, mod8, final_g[None], yb)


def kernel(x, c, w_ada, b_ada, norm1_g, w_in, b_forget, conv_w, conv_b, conv_ln_g, conv_ln_b, w_conv_out,
           w_attn_out, w_out, norm2_g, w_router, b_router, w_exp_in, b_exp_in, w_exp_out, b_exp_out, final_g):
    bsz, seq, d = x.shape
    t = bsz * seq
    depth = w_ada.shape[0]
    xf = x.reshape(t, d)
    n_rows = t * TOP_K + (t // TOKEN_TILE) * N_EXPERTS * (CHUNK - 1) + N_EXPERTS * (EXPERT_BLOCK - CHUNK)
    n_rows = -(-n_rows // EXPERT_BLOCK) * EXPERT_BLOCK
    n_blocks = n_rows // EXPERT_BLOCK
    for l in range(depth):
        mod8 = _mod(c, w_ada[l], b_ada[l])
        act, q, k, v, aq, ak, tot, sgc, sga = _inproj(xf, mod8, norm1_g[l], w_in[l], b_forget[l], conv_w[l],
                                                      conv_b[l], conv_ln_g[l], conv_ln_b[l], seq)
        attn = _attention(q, aq, k, ak, v, tot, bsz, seq)
        x1, h2b, ids, gates = _merge(xf, act, attn, sgc, sga, mod8, norm2_g[l], w_conv_out[l], w_attn_out[l],
                                     w_out[l], w_router[l], b_router[l], seq)
        slot, slott, gatest, meta, cnt = _route(ids, gates)
        gd, blk = _dest(cnt, meta)
        gd_flat = gd.reshape(-1)
        block_e = blk[0].reshape(-1)[:n_blocks]
        n_active = blk[1, 0, :1]
        pend = blk[2, 0, :N_EXPERTS]
        xs = _dispatch(pend, n_active, gd_flat, slott, gatest, h2b, n_rows)
        yb = _experts(block_e, n_active, xs, w_exp_in[l], b_exp_in[l], w_exp_out[l], b_exp_out[l])
        assert depth == 1
        xf = _combine(gd_flat, slot, x1, mod8, final_g, yb, seq)
    return xf.reshape(bsz, seq, d)
```

```python
import functools

import jax
import jax.numpy as jnp
from jax import lax
from jax.experimental import pallas as pl
from jax.experimental.pallas import tpu as pltpu

F32 = jnp.float32
BF16 = jnp.bfloat16
I32 = jnp.int32
U32 = jnp.uint32
HIGHEST = lax.Precision.HIGHEST

LANES = 128
SUBLANES = 8
VMEM_LIMIT_BYTES = 56 * 1024 * 1024

N_HEADS = 8
HEAD_DIM = 64
CONV_KERNEL = 31
CONV_HALO = 32
N_EXPERTS = 32
TOP_K = 4
SWIGLU_LIMIT = 7.0
SWIGLU_ALPHA = 1.702
RMS_EPS = 1e-5
LN_EPS = 1e-5
N_MOD = 6
N_SPLIT = 3
LOG2_E = 1.4426950408889634
HEADS_PER_STEP = 2

INPROJ_TILE = 512
ATTN_BQ = 2048
ATTN_BK = 512
MERGE_TILE = 512
TOKEN_TILE = 256
EXPERT_BLOCK = 512
CHUNK = SUBLANES
SLOT_ROWS = 1280
GROUP_TABLE = SUBLANES * LANES
DEST_TILES_PER_STEP = 8
assert SLOT_ROWS >= TOKEN_TILE * TOP_K + N_EXPERTS * (CHUNK - 1) and SLOT_ROWS // CHUNK < GROUP_TABLE


def _cparams(*sem):
    return pltpu.CompilerParams(dimension_semantics=sem, vmem_limit_bytes=VMEM_LIMIT_BYTES)


def _resident(shape):
    nd = len(shape)
    return pl.BlockSpec(shape, lambda *_: (0,) * nd, pipeline_mode=pl.Buffered(1))


def _split3(x):
    p1 = x.astype(BF16)
    r1 = x - p1.astype(F32)
    p2 = r1.astype(BF16)
    p3 = (r1 - p2.astype(F32)).astype(BF16)
    return jnp.concatenate([p1, p2, p3], axis=1)


def _rms(x):
    return x * lax.rsqrt(jnp.mean(x * x, axis=-1, keepdims=True) + RMS_EPS)


def _mod_kernel(ct_ref, w_ref, b_ref, o_ref, *, batch):
    rows = []
    for b in range(batch):
        col = ct_ref[:, b:b + 1]
        rows.append(jnp.sum((col * jax.nn.sigmoid(col)) * w_ref[...], axis=0, keepdims=True) + b_ref[...])
    rows.append(jnp.zeros((SUBLANES - batch, rows[0].shape[1]), F32))
    o_ref[...] = jnp.concatenate(rows, axis=0)


def _mod(c, w_ada, b_ada):
    bsz, d = c.shape
    n = w_ada.shape[1]
    tn = 1024
    out = pl.pallas_call(
        functools.partial(_mod_kernel, batch=bsz),
        grid=(n // tn,),
        in_specs=[pl.BlockSpec((d, bsz), lambda j: (0, 0)),
                  pl.BlockSpec((d, tn), lambda j: (0, j)),
                  pl.BlockSpec((1, tn), lambda j: (0, j))],
        out_specs=pl.BlockSpec((SUBLANES, tn), lambda j: (0, j)),
        out_shape=jax.ShapeDtypeStruct((SUBLANES, n), F32),
        compiler_params=_cparams("arbitrary"),
        name="mod",
    )(c.T, w_ada, b_ada[None])
    mod = out[:bsz].reshape(bsz, N_MOD, d)
    return jnp.pad(mod, ((0, 0), (0, SUBLANES - N_MOD), (0, 0)))


def _conv_module(a, first_of_seq, w_ref, b_ref, g_ref, beta_ref, xs_scr, ys_scr):
    ts = a.shape[0]
    xs_scr[0:CONV_HALO, :] = jnp.where(first_of_seq, 0.0, xs_scr[ts:ts + CONV_HALO, :])
    xs_scr[CONV_HALO:, :] = a
    acc = jnp.zeros(a.shape, F32) + b_ref[...]
    base = CONV_HALO - (CONV_KERNEL - 1)
    for rho in range(SUBLANES):
        offs = [o for o in range(base, base + CONV_KERNEL) if o % SUBLANES == rho]
        if not offs:
            continue
        rows = offs[-1] + ts - rho
        ys_scr[0:rows, :] = xs_scr[rho:rho + rows, :]
        for o in offs:
            lo = o - rho
            acc = acc + w_ref[o - base:o - base + 1, :] * ys_scr[lo:lo + ts, :]
    mu = jnp.mean(acc, axis=-1, keepdims=True)
    cen = acc - mu
    var = jnp.mean(cen * cen, axis=-1, keepdims=True)
    y = cen * lax.rsqrt(var + LN_EPS) * g_ref[...] + beta_ref[...]
    return (y * jax.nn.sigmoid(y)).astype(BF16)


def _inproj_kernel(x_ref, mod_ref, g_ref, wa_ref, wqkv_ref, wf_ref, wg_ref, bf_ref,
                   tri_ref, eq_ref, ek_ref, oneq_ref, onek_ref, cw_ref, cb_ref, cg_ref, cbeta_ref,
                   act_ref, q_ref, k_ref, v_ref, aq_ref, ak_ref, tot_ref, sgc_ref, sga_ref,
                   carry_scr, xs_scr, ys_scr, *, conv_width, d_model, tiles_per_seq):
    first_of_seq = pl.program_id(0) % tiles_per_seq == 0

    @pl.when(pl.program_id(0) == 0)
    def _():
        xs_scr[...] = jnp.zeros(xs_scr.shape, F32)
        carry_scr[...] = jnp.zeros(carry_scr.shape, F32)

    x = x_ref[...]
    h = _rms(x) * g_ref[...] * (1.0 + mod_ref[1:2, :]) + mod_ref[0:1, :]
    hb = h.astype(BF16)

    u = jnp.dot(hb, wa_ref[...], preferred_element_type=F32)
    a = u[:, :conv_width] * jax.nn.sigmoid(u[:, conv_width:])
    act_ref[...] = _conv_module(a, first_of_seq, cw_ref, cb_ref, cg_ref, cbeta_ref, xs_scr, ys_scr)

    fl = jnp.dot(hb, wf_ref[...], preferred_element_type=F32) + bf_ref[...]
    lane = lax.broadcasted_iota(I32, fl.shape, 1)
    log_f = -(jnp.maximum(-fl, 0.0) + jnp.log1p(jnp.exp(-jnp.abs(fl))))
    log_f = jnp.where(lane < N_HEADS, log_f, 0.0)
    cs = jnp.dot(tri_ref[...], _split3(log_f), preferred_element_type=F32)
    cum = (cs[:, :LANES] + cs[:, LANES:2 * LANES] + cs[:, 2 * LANES:]) * LOG2_E
    tm = cum.shape[0]

    pieces = []
    for sb in range(tm // ATTN_BK):
        blk = cum[sb * ATTN_BK:(sb + 1) * ATTN_BK, :]
        if sb > 0:
            blk = blk - cum[sb * ATTN_BK - 1:sb * ATTN_BK, :]
        tot_ref[sb] = blk[ATTN_BK - 1:, :]
        pieces.append(blk)
    cum_k = pieces[0] if len(pieces) == 1 else jnp.concatenate(pieces, axis=0)

    first_of_block = pl.program_id(0) % (ATTN_BQ // tm) == 0
    cum_q = cum + jnp.where(first_of_block, 0.0, carry_scr[0:1, :])
    carry_scr[...] = jnp.broadcast_to(cum_q[tm - 1:, :], carry_scr.shape)

    qkv = jnp.dot(hb, wqkv_ref[...], preferred_element_type=F32)
    aw = qkv.shape[1] // 3
    q_ref[...] = (qkv[:, :aw] * (HEAD_DIM ** -0.5 * LOG2_E)).astype(BF16)
    k_ref[...] = qkv[:, aw:2 * aw].astype(BF16)
    v_ref[...] = qkv[:, 2 * aw:].astype(BF16)
    aq_ref[...] = (jnp.dot(_split3(cum_q), eq_ref[...], preferred_element_type=F32) + oneq_ref[...]).astype(BF16)
    ak_ref[...] = (jnp.dot(_split3(cum_k), ek_ref[...], preferred_element_type=F32) + onek_ref[...]).astype(BF16)

    gts = jnp.dot(hb, wg_ref[...], preferred_element_type=F32)
    sgc_ref[...] = jax.nn.sigmoid(gts[:, :d_model]).astype(BF16)
    sga_ref[...] = jax.nn.sigmoid(gts[:, d_model:]).astype(BF16)


def _aug_constants():
    rows = jnp.arange(N_SPLIT * LANES)
    piece, head = rows // LANES, rows % LANES
    cols = jnp.arange(N_HEADS * HEAD_DIM)
    pair, off = cols // LANES, cols % LANES
    chead = 2 * pair + jnp.where(off >= HEAD_DIM, 0, 1)
    pos = off % HEAD_DIM
    valid = (head[:, None] < N_HEADS) & (head[:, None] == chead[None, :])
    eq = jnp.where(valid & (pos[None, :] == piece[:, None]), 1.0, 0.0).astype(BF16)
    ek = jnp.where(valid & (pos[None, :] == piece[:, None] + N_SPLIT), -1.0, 0.0).astype(BF16)
    oneq = jnp.where((pos >= N_SPLIT) & (pos < 2 * N_SPLIT), 1.0, 0.0).astype(F32)[None]
    onek = jnp.where(pos < N_SPLIT, 1.0, 0.0).astype(F32)[None]
    return eq, ek, oneq, onek


def _inproj(x2, mod8, norm_g, w_in, b_forget, conv_w, conv_b, conv_ln_g, conv_ln_b, seq):
    t, d = x2.shape
    tm = INPROJ_TILE
    nsub = tm // ATTN_BK
    cw = d // 2
    aw = N_HEADS * HEAD_DIM
    o0 = 2 * cw
    wa = w_in[:, :o0].astype(BF16)
    o1 = o0 + 3 * aw
    wqkv = w_in[:, o0:o1].astype(BF16)
    wf = jnp.pad(w_in[:, o1:o1 + N_HEADS], ((0, 0), (0, LANES - N_HEADS))).astype(BF16)
    bfp = jnp.pad(b_forget, (0, LANES - N_HEADS))[None]
    wg = w_in[:, o1 + N_HEADS:].astype(BF16)
    r = jnp.arange(tm)
    tri = (r[None, :] <= r[:, None]).astype(BF16)
    eq, ek, oneq, onek = _aug_constants()
    tiles_per_seq = seq // tm
    row = lambda i: (i, 0)
    outs = pl.pallas_call(
        functools.partial(_inproj_kernel, conv_width=cw, d_model=d, tiles_per_seq=tiles_per_seq),
        grid=(t // tm,),
        in_specs=[pl.BlockSpec((tm, d), row),
                  pl.BlockSpec((None, SUBLANES, d), lambda i: (i // tiles_per_seq, 0, 0)),
                  _resident((1, d)),
                  _resident(wa.shape), _resident(wqkv.shape),
                  _resident(wf.shape), _resident(wg.shape), _resident(bfp.shape),
                  _resident(tri.shape), _resident(eq.shape), _resident(ek.shape),
                  _resident(oneq.shape), _resident(onek.shape),
                  _resident(conv_w.shape), _resident((1, cw)), _resident((1, cw)), _resident((1, cw))],
        out_specs=[pl.BlockSpec((tm, cw), row)] + [pl.BlockSpec((tm, aw), row)] * 5 + [
                   pl.BlockSpec((nsub, 1, LANES), lambda i: (i, 0, 0)),
                   pl.BlockSpec((tm, d), row), pl.BlockSpec((tm, d), row)],
        out_shape=[jax.ShapeDtypeStruct((t, cw), BF16)] + [jax.ShapeDtypeStruct((t, aw), BF16)] * 5 + [
                   jax.ShapeDtypeStruct((t // ATTN_BK, 1, LANES), F32),
                   jax.ShapeDtypeStruct((t, d), BF16), jax.ShapeDtypeStruct((t, d), BF16)],
        scratch_shapes=[pltpu.VMEM((SUBLANES, LANES), F32),
                        pltpu.VMEM((CONV_HALO + tm, cw), F32), pltpu.VMEM((CONV_HALO + tm, cw), F32)],
        compiler_params=_cparams("arbitrary"),
        name="inproj",
    )(x2, mod8, norm_g[None], wa, wqkv, wf, wg, bfp, tri, eq, ek, oneq, onek,
      conv_w, conv_b[None], conv_ln_g[None], conv_ln_b[None])
    return outs


def _attn_kernel(q_ref, aq_ref, k_ref, ak_ref, v_ref, tot_ref, o_ref, qop_scr, m_scr, acc_scr):
    bq = q_ref.shape[0]
    bk = ATTN_BK
    ratio = bq // bk
    qi = pl.program_id(2)
    nt = (((1,), (1,)), ((), ()))
    heads = range(HEADS_PER_STEP)
    zero = jnp.zeros((1, LANES), F32)

    def splice(hh, main, extra):
        first = lax.broadcasted_iota(I32, main.shape, 1) < HEAD_DIM
        return jnp.where(first, main, extra) if hh == 0 else jnp.where(first, extra, main)

    for hh in heads:
        qop_scr[hh] = splice(hh, q_ref[...], aq_ref[...])

    def scores(hh, kb, row0):
        start = pl.multiple_of(kb * bk, bk)
        k = splice(hh, k_ref[pl.ds(start, bk), :], ak_ref[pl.ds(start, bk), :])
        v = v_ref[pl.ds(start, bk), :]
        v = splice(hh, v, jnp.ones(v.shape, v.dtype))
        s = lax.dot_general(qop_scr[hh, row0:, :], k, nt, preferred_element_type=F32)
        return s, v

    def update(hh, s, v, delta, row0):
        m_prev = m_scr[hh, row0:, :]
        m_new = jnp.maximum(m_prev, jnp.max(s, axis=1, keepdims=True) + delta)
        shift = m_new - delta
        p = jnp.exp2(s - shift[:, :1])
        alpha = jnp.exp2(m_prev - m_new)
        pv = jnp.dot(p.astype(BF16), v, preferred_element_type=F32)
        acc_scr[hh, row0:, :] = alpha * acc_scr[hh, row0:, :] + pv
        m_scr[hh, row0:, :] = m_new

    for hh in heads:
        delta = zero
        for sb in range(ratio):
            kb = qi * ratio + sb
            row0 = sb * bk
            s, v = scores(hh, kb, row0)
            row = lax.broadcasted_iota(I32, s.shape, 0)
            col = lax.broadcasted_iota(I32, s.shape, 1)
            s = jnp.where(col <= row, s, -jnp.inf)
            if sb == 0:
                m0 = jnp.max(s, axis=1, keepdims=True)
                p = jnp.exp2(s - m0)
                m_scr[hh] = jnp.broadcast_to(m0, m_scr.shape[1:])
                acc_scr[hh] = jnp.dot(p.astype(BF16), v, preferred_element_type=F32)
            else:
                update(hh, s, v, delta, row0)
            delta = delta - tot_ref[hh, pl.ds(kb, 1), :]

    def body(step, deltas):
        kb = qi * ratio - 1 - step
        new_deltas = []
        for hh in heads:
            delta = deltas[hh] + tot_ref[hh, pl.ds(kb, 1), :]
            s, v = scores(hh, kb, 0)
            update(hh, s, v, delta, 0)
            new_deltas.append(delta)
        return tuple(new_deltas)

    lax.fori_loop(0, qi * ratio, body, tuple(zero for _ in heads))
    acc0, acc1 = acc_scr[0], acc_scr[1]
    out0 = acc0 / acc0[:, HEAD_DIM:HEAD_DIM + 1]
    out1 = acc1 / acc1[:, 0:1]
    o_ref[...] = splice(0, out0, out1).astype(BF16)


def _attention(q, aq, k, ak, v, tot, bsz, seq):
    t, aw = q.shape
    bq = ATTN_BQ
    nq = seq // bq
    nk = seq // ATTN_BK
    assert HEADS_PER_STEP * HEAD_DIM == LANES
    tot = tot.reshape(bsz, nk, LANES)[:, :, :N_HEADS].transpose(0, 2, 1)
    tot = jnp.broadcast_to(tot[..., None], (bsz, N_HEADS, nk, LANES))
    qblk = pl.BlockSpec((bq, LANES), lambda b, h, i: (b * nq + i, h))
    kblk = pl.BlockSpec((None, seq, LANES), lambda b, h, i: (b, 0, h), pipeline_mode=pl.Buffered(1))
    seq3 = lambda a: a.reshape(bsz, seq, aw)
    return pl.pallas_call(
        _attn_kernel,
        grid=(bsz, N_HEADS // HEADS_PER_STEP, nq),
        in_specs=[qblk, qblk, kblk, kblk, kblk,
                  pl.BlockSpec((None, HEADS_PER_STEP, nk, LANES), lambda b, h, i: (b, h, 0, 0))],
        out_specs=qblk,
        out_shape=jax.ShapeDtypeStruct((t, aw), BF16),
        scratch_shapes=[pltpu.VMEM((HEADS_PER_STEP, bq, LANES), BF16),
                        pltpu.VMEM((HEADS_PER_STEP, bq, LANES), F32),
                        pltpu.VMEM((HEADS_PER_STEP, bq, LANES), F32)],
        compiler_params=_cparams("arbitrary", "arbitrary", "arbitrary"),
        name="attn",
    )(q, aq, seq3(k), seq3(ak), seq3(v), tot)


def _pack_bf16_pair(lo, hi):
    lo_bits = pltpu.bitcast(lo.astype(BF16).astype(F32), U32)
    hi_bits = pltpu.bitcast(hi.astype(BF16).astype(F32), U32)
    return (hi_bits & jnp.uint32(0xFFFF0000)) | (lo_bits >> 16)


def _unpack_bf16_pair(w):
    lo = pltpu.bitcast(w << 16, F32)
    hi = pltpu.bitcast(w & jnp.uint32(0xFFFF0000), F32)
    return jnp.concatenate([lo, hi], axis=1)


def _expert_onehots(ids):
    lane = lax.broadcasted_iota(I32, ids.shape, 1)
    return lane, [lane == ids[:, kk:kk + 1] for kk in range(TOP_K)]


def _route_tile(ids, gates, carry, tri, utri):
    lane, hots = _expert_onehots(ids)
    multi = jnp.zeros(lane.shape, F32)
    for hot in hots:
        multi = multi + jnp.where(hot, 1.0, 0.0)
    before = jnp.dot(tri, multi.astype(BF16), preferred_element_type=F32)
    units = jnp.ceil(jnp.sum(multi, axis=0, keepdims=True) * (1.0 / CHUNK))
    units8 = jnp.broadcast_to(units, (SUBLANES, LANES)).astype(BF16)
    lstart = jnp.dot(units8, utri, preferred_element_type=F32)[0:1, :] * CHUNK
    csize = units * CHUNK
    slot = jnp.zeros(lane.shape, F32)
    for kk, hot in enumerate(hots):
        sl = jnp.sum(jnp.where(hot, lstart + before, 0.0), axis=1, keepdims=True)
        slot = jnp.where(lane == kk, sl, slot)
    row = lax.broadcasted_iota(I32, (SUBLANES, LANES), 0)
    meta = jnp.where(row == 0, csize, jnp.where(row == 1, lstart, jnp.where(row == 2, carry, 0.0)))
    slott = slot.T[0:SUBLANES, :].astype(I32)
    gatest = gates.T[0:SUBLANES, :]
    return slot.astype(I32), slott, gatest, meta, carry + csize


def _merge_kernel(x_ref, act_ref, attn_ref, sgc_ref, sga_ref, mod_ref, g2_ref, wco_ref, wao_ref, wout_ref,
                  wr_ref, br_ref, tri_ref, utri_ref,
                  x1_ref, h2_ref, slot_ref, slott_ref, gatest_ref, meta_ref, cnt_ref, carry_scr):
    @pl.when(pl.program_id(0) == 0)
    def _():
        carry_scr[...] = jnp.zeros(carry_scr.shape, F32)

    conv_out = jnp.dot(act_ref[...], wco_ref[...], preferred_element_type=F32)
    attn_out = jnp.dot(attn_ref[...], wao_ref[...], preferred_element_type=F32)
    mixed = sgc_ref[...].astype(F32) * conv_out + sga_ref[...].astype(F32) * attn_out
    upd = jnp.dot(mixed.astype(BF16), wout_ref[...], preferred_element_type=F32)
    x1 = x_ref[...] + mod_ref[2:3, :] * upd
    x1_ref[...] = x1
    h2 = _rms(x1) * g2_ref[...] * (1.0 + mod_ref[4:5, :]) + mod_ref[3:4, :]
    h2_hi = h2.astype(BF16)
    h2_ref[...] = h2_hi
    h2_lo = (h2 - h2_hi.astype(F32)).astype(BF16)
    h2_cat = jnp.concatenate([h2_hi, h2_lo, h2_hi], axis=1)
    logits = jnp.dot(h2_cat, wr_ref[...], preferred_element_type=F32) + br_ref[...]
    lane = lax.broadcasted_iota(I32, logits.shape, 1)
    work = jnp.where(lane < N_EXPERTS, logits, -jnp.inf)
    ids = jnp.zeros(logits.shape, I32)
    vals = []
    for kk in range(TOP_K):
        mx = jnp.max(work, axis=1, keepdims=True)
        idx = jnp.min(jnp.where(work == mx, lane, LANES), axis=1, keepdims=True)
        ids = jnp.where(lane == kk, idx, ids)
        vals.append(mx)
        work = jnp.where(lane == idx, -jnp.inf, work)
    exps = [jnp.exp(vv - vals[0]) for vv in vals]
    den = exps[0] + exps[1] + exps[2] + exps[3]
    gates = jnp.zeros(logits.shape, F32)
    for kk in range(TOP_K):
        gates = jnp.where(lane == kk, exps[kk] / den, gates)

    carry = carry_scr[0:1, :]
    tt = TOKEN_TILE
    for sub in range(logits.shape[0] // tt):
        rows = slice(sub * tt, (sub + 1) * tt)
        slot, slott, gatest, meta, carry = _route_tile(ids[rows, :], gates[rows, :], carry,
                                                       tri_ref[...], utri_ref[...])
        slot_ref[rows, :] = slot
        slott_ref[sub * SUBLANES:(sub + 1) * SUBLANES, :] = slott
        gatest_ref[sub * SUBLANES:(sub + 1) * SUBLANES, :] = gatest
        meta_ref[sub] = meta
    carry_scr[...] = jnp.broadcast_to(carry, carry_scr.shape)
    cnt_ref[...] = jnp.broadcast_to(carry, cnt_ref.shape)


def _merge(x2, act, attn, sgc, sga, mod8, norm2_g, w_conv_out, w_attn_out, w_out, w_router, b_router, seq):
    t, d = x2.shape
    tm = MERGE_TILE
    tt = TOKEN_TILE
    sub = tm // tt
    nt = t // tt
    cw = act.shape[1]
    sw = attn.shape[1]
    wco = w_conv_out.astype(BF16)
    wao = w_attn_out.astype(BF16)
    wout = w_out.astype(BF16)
    wr = jnp.pad(w_router, ((0, 0), (0, LANES - N_EXPERTS)))
    wr_hi = wr.astype(BF16)
    wr_lo = (wr - wr_hi.astype(F32)).astype(BF16)
    wr = jnp.concatenate([wr_hi, wr_hi, wr_lo], axis=0)
    br = jnp.pad(b_router, (0, LANES - N_EXPERTS))[None]
    r = jnp.arange(tt)
    tri = (r[None, :] < r[:, None]).astype(BF16)
    e = jnp.arange(LANES)
    utri = (e[:, None] < e[None, :]).astype(BF16)
    tiles_per_seq = seq // tm
    row = lambda i: (i, 0)
    return pl.pallas_call(
        _merge_kernel,
        grid=(t // tm,),
        in_specs=[pl.BlockSpec((tm, d), row), pl.BlockSpec((tm, cw), row), pl.BlockSpec((tm, sw), row),
                  pl.BlockSpec((tm, d), row), pl.BlockSpec((tm, d), row),
                  pl.BlockSpec((None, SUBLANES, d), lambda i: (i // tiles_per_seq, 0, 0)),
                  _resident((1, d)), _resident(wco.shape), _resident(wao.shape), _resident(wout.shape),
                  _resident(wr.shape), _resident(br.shape), _resident(tri.shape), _resident(utri.shape)],
        out_specs=[pl.BlockSpec((tm, d), row), pl.BlockSpec((tm, d), row),
                   pl.BlockSpec((tm, LANES), row),
                   pl.BlockSpec((sub * SUBLANES, tt), row), pl.BlockSpec((sub * SUBLANES, tt), row),
                   pl.BlockSpec((sub, SUBLANES, LANES), lambda i: (i, 0, 0)),
                   pl.BlockSpec((SUBLANES, LANES), lambda i: (0, 0))],
        out_shape=[jax.ShapeDtypeStruct((t, d), F32), jax.ShapeDtypeStruct((t, d), BF16),
                   jax.ShapeDtypeStruct((t, LANES), I32),
                   jax.ShapeDtypeStruct((nt * SUBLANES, tt), I32), jax.ShapeDtypeStruct((nt * SUBLANES, tt), F32),
                   jax.ShapeDtypeStruct((nt, SUBLANES, LANES), F32),
                   jax.ShapeDtypeStruct((SUBLANES, LANES), F32)],
        scratch_shapes=[pltpu.VMEM((SUBLANES, LANES), F32)],
        compiler_params=_cparams("arbitrary"),
        name="merge",
    )(x2, act, attn, sgc, sga, mod8, norm2_g[None], wco, wao, wout, wr, br, tri, utri)


def _dest_kernel(cnt_ref, meta_ref, tri_ref, gd_ref, blk_ref):
    cnt = cnt_ref[...]
    padded = jnp.ceil(cnt * (1.0 / EXPERT_BLOCK)) * EXPERT_BLOCK
    pend = jnp.dot(padded, tri_ref[...], preferred_element_type=F32, precision=HIGHEST)
    pstart = (pend - padded)[0:1, :]

    grow = lax.broadcasted_iota(I32, (SUBLANES, LANES), 0)
    glane = lax.broadcasted_iota(I32, (SUBLANES, LANES), 1)
    gpos = ((grow * LANES + glane) * CHUNK).astype(F32)
    last = jnp.logical_and(grow == SUBLANES - 1, glane == LANES - 1)
    for tile in range(meta_ref.shape[0]):
        meta = meta_ref[tile]
        csize, lstart, seg_off = meta[0:1, :], meta[1:2, :], meta[2:3, :]
        lend = lstart + csize
        base = pstart + seg_off - lstart
        owner = jnp.zeros(gpos.shape, I32)
        for e in range(N_EXPERTS):
            owner = owner + jnp.where(gpos >= lend[0:1, e:e + 1], 1, 0)
        gdst = gpos
        for e in range(N_EXPERTS):
            gdst = gdst + jnp.where(owner == e, base[0:1, e:e + 1], 0.0)
        n_groups = lend[0:1, N_EXPERTS - 1:N_EXPERTS] * (1.0 / CHUNK)
        gd_ref[tile] = jnp.where(last, n_groups, gdst).astype(I32)

    @pl.when(pl.program_id(0) == 0)
    def _():
        shape = blk_ref.shape[1:]
        pos = (lax.broadcasted_iota(I32, shape, 0) * LANES + lax.broadcasted_iota(I32, shape, 1))
        pos = pos.astype(F32) * EXPERT_BLOCK
        be = jnp.zeros(shape, I32)
        for e in range(N_EXPERTS):
            be = be + jnp.where(pos >= pend[0:1, e:e + 1], 1, 0)
        blk_ref[0] = jnp.minimum(be, N_EXPERTS - 1)
        nact = pend[0:1, N_EXPERTS - 1:N_EXPERTS] * (1.0 / EXPERT_BLOCK)
        blk_ref[1] = jnp.broadcast_to(nact, shape).astype(I32)
        blk_ref[2] = jnp.broadcast_to(pend[0:1, :], shape).astype(I32)


def _dest(cnt, meta):
    nt = meta.shape[0]
    per_step = DEST_TILES_PER_STEP
    r = jnp.arange(LANES)
    tri = (r[:, None] <= r[None, :]).astype(F32)
    return pl.pallas_call(
        _dest_kernel,
        grid=(nt // per_step,),
        in_specs=[pl.BlockSpec((SUBLANES, LANES), lambda i: (0, 0)),
                  pl.BlockSpec((per_step, SUBLANES, LANES), lambda i: (i, 0, 0)),
                  pl.BlockSpec((LANES, LANES), lambda i: (0, 0))],
        out_specs=[pl.BlockSpec((per_step, SUBLANES, LANES), lambda i: (i, 0, 0)),
                   pl.BlockSpec((3, SUBLANES, LANES), lambda i: (0, 0, 0))],
        out_shape=[jax.ShapeDtypeStruct((nt, SUBLANES, LANES), I32),
                   jax.ShapeDtypeStruct((3, SUBLANES, LANES), I32)],
        compiler_params=_cparams("arbitrary"),
        name="dest",
    )(cnt, meta, tri)


def _row_copy(src, dst, sem):
    return pltpu.make_async_copy(src, dst, sem)


def _dispatch_kernel(pend_ref, na_ref, gd_ref, slott_ref, gatest_ref, h2_ref, xs_ref, zero_ref, sorted_scr, sem):
    tm = h2_ref.shape[0]
    bm = zero_ref.shape[0]
    nb = xs_ref.shape[0] // bm

    @pl.when(pl.program_id(0) == 0)
    def _():
        zero_ref[...] = jnp.zeros(zero_ref.shape, zero_ref.dtype)

        def zero_block(start):
            return _row_copy(zero_ref, xs_ref.at[pl.ds(pl.multiple_of(start, bm), bm)], sem)

        def nonempty(e):
            return pend_ref[e] > (pend_ref[e - 1] if e > 0 else 0)

        for e in range(N_EXPERTS):
            @pl.when(nonempty(e))
            def _():
                zero_block(pend_ref[e] - bm).start()

        def tail(b, carry):
            zero_block(b * bm).start()
            return carry

        lax.fori_loop(na_ref[0], nb, tail, 0)
        for e in range(N_EXPERTS):
            @pl.when(nonempty(e))
            def _():
                zero_block(0).wait()

        def tail_wait(b, carry):
            zero_block(0).wait()
            return carry

        lax.fori_loop(na_ref[0], nb, tail_wait, 0)

    srow = lax.broadcasted_iota(I32, (SLOT_ROWS, tm), 0)
    perm = jnp.zeros((SLOT_ROWS, tm), F32)
    gate = jnp.zeros((SLOT_ROWS, 1), F32)
    for kk in range(TOP_K):
        hit = srow == slott_ref[kk:kk + 1, :]
        perm = perm + jnp.where(hit, 1.0, 0.0)
        gate = gate + jnp.sum(jnp.where(hit, gatest_ref[kk:kk + 1, :], 0.0), axis=1, keepdims=True)
    perm = perm.astype(BF16)
    half = h2_ref.shape[1] // 2
    lo = jnp.dot(perm, h2_ref[:, :half], preferred_element_type=F32)
    hi = jnp.dot(perm, h2_ref[:, half:], preferred_element_type=F32)
    sorted_scr[:, :half] = _pack_bf16_pair(lo, hi)
    sorted_scr[:, half:] = pltpu.bitcast(jnp.broadcast_to(gate, (SLOT_ROWS, LANES)), U32)

    n_groups = gd_ref[GROUP_TABLE - 1]

    def group_copy(j, d):
        src = sorted_scr.at[pl.ds(pl.multiple_of(j * CHUNK, CHUNK), CHUNK)]
        return _row_copy(src, xs_ref.at[pl.ds(pl.multiple_of(d, CHUNK), CHUNK)], sem)

    def issue(j, carry):
        group_copy(j, gd_ref[j]).start()
        return carry

    def retire(j, carry):
        group_copy(0, 0).wait()
        return carry

    lax.fori_loop(0, n_groups, issue, 0)
    lax.fori_loop(0, n_groups, retire, 0)


def _dispatch(pend, n_active, gd_flat, slott, gatest, h2b, n_rows):
    t, d = h2b.shape
    w = d // 2 + LANES
    tm = TOKEN_TILE
    grid_spec = pltpu.PrefetchScalarGridSpec(
        num_scalar_prefetch=2,
        grid=(t // tm,),
        in_specs=[pl.BlockSpec((GROUP_TABLE,), lambda i, pe, na: (i,), memory_space=pltpu.SMEM),
                  pl.BlockSpec((SUBLANES, tm), lambda i, pe, na: (i, 0)),
                  pl.BlockSpec((SUBLANES, tm), lambda i, pe, na: (i, 0)),
                  pl.BlockSpec((tm, d), lambda i, pe, na: (i, 0))],
        out_specs=pl.BlockSpec(memory_space=pl.ANY),
        scratch_shapes=[pltpu.VMEM((EXPERT_BLOCK, w), U32), pltpu.VMEM((SLOT_ROWS, w), U32),
                        pltpu.SemaphoreType.DMA(())],
    )
    return pl.pallas_call(
        _dispatch_kernel,
        grid_spec=grid_spec,
        out_shape=jax.ShapeDtypeStruct((n_rows, w), U32),
        compiler_params=_cparams("arbitrary"),
        name="dispatch",
    )(pend, n_active, gd_flat, slott, gatest, h2b)


def _expert_kernel(be_ref, na_ref, xs_ref, w1_ref, b1_ref, w2_ref, b2_ref, y_ref, w1b_ref, w2b_ref, *, d_expert):
    i = pl.program_id(0)
    active = i < na_ref[0]
    prev = be_ref[jnp.maximum(i - 1, 0)]
    new_expert = jnp.logical_or(i == 0, be_ref[i] != prev)
    half = y_ref.shape[1]

    @pl.when(jnp.logical_and(active, new_expert))
    def _():
        w1b_ref[...] = w1_ref[...].astype(BF16)
        w2b_ref[...] = w2_ref[...].astype(BF16)

    @pl.when(active)
    def _():
        x = _unpack_bf16_pair(xs_ref[:, :half]).astype(BF16)
        gate = pltpu.bitcast(xs_ref[:, half:], F32)[:, 0:1]
        hc = jnp.dot(x, w1b_ref[...], preferred_element_type=F32) + b1_ref[...]
        g = jnp.minimum(hc[:, :d_expert], SWIGLU_LIMIT)
        u = jnp.clip(hc[:, d_expert:], -SWIGLU_LIMIT, SWIGLU_LIMIT)
        hm = (u + 1.0) * (g * jax.nn.sigmoid(SWIGLU_ALPHA * g))
        y = (jnp.dot(hm.astype(BF16), w2b_ref[...], preferred_element_type=F32) + b2_ref[...]) * gate
        y_ref[...] = _pack_bf16_pair(y[:, :half], y[:, half:])

    @pl.when(jnp.logical_not(active))
    def _():
        y_ref[...] = jnp.zeros(y_ref.shape, U32)


def _experts(block_e, n_active, xs, w1, b1, w2, b2):
    n_rows, w = xs.shape
    n_exp, d, two_de = w1.shape
    de = two_de // 2
    bm = EXPERT_BLOCK
    nb = n_rows // bm

    def blk(i, be, na):
        return jnp.maximum(jnp.minimum(i, na[0] - 1), 0)

    grid_spec = pltpu.PrefetchScalarGridSpec(
        num_scalar_prefetch=2,
        grid=(nb,),
        in_specs=[pl.BlockSpec((bm, w), lambda i, be, na: (blk(i, be, na), 0)),
                  pl.BlockSpec((None, d, two_de), lambda i, be, na: (be[blk(i, be, na)], 0, 0)),
                  pl.BlockSpec((None, 1, two_de), lambda i, be, na: (be[blk(i, be, na)], 0, 0)),
                  pl.BlockSpec((None, de, d), lambda i, be, na: (be[blk(i, be, na)], 0, 0)),
                  pl.BlockSpec((None, 1, d), lambda i, be, na: (be[blk(i, be, na)], 0, 0))],
        out_specs=pl.BlockSpec((bm, d // 2), lambda i, be, na: (i, 0)),
        scratch_shapes=[pltpu.VMEM((d, two_de), BF16), pltpu.VMEM((de, d), BF16)],
    )
    return pl.pallas_call(
        functools.partial(_expert_kernel, d_expert=de),
        grid_spec=grid_spec,
        out_shape=jax.ShapeDtypeStruct((n_rows, d // 2), U32),
        compiler_params=_cparams("arbitrary"),
        name="expert",
    )(block_e, n_active, xs, w1, b1[:, None, :], w2, b2[:, None, :])


def _combine_kernel(gd_ref, gd_next_ref, slot_ref, x1_ref, mod_ref, fg_ref, yb_ref, o_ref, sorted_scr, sem):
    tm = x1_ref.shape[0]
    step = pl.program_id(0)
    cur = step % 2

    def group_copy(buf, j, d):
        dst = sorted_scr.at[buf, pl.ds(pl.multiple_of(j * CHUNK, CHUNK), CHUNK)]
        return _row_copy(yb_ref.at[pl.ds(pl.multiple_of(d, CHUNK), CHUNK)], dst, sem.at[buf])

    def fetch(table_ref, buf):
        def issue(j, carry):
            group_copy(buf, j, table_ref[j]).start()
            return carry
        lax.fori_loop(0, table_ref[GROUP_TABLE - 1], issue, 0)

    @pl.when(step == 0)
    def _():
        sorted_scr[...] = jnp.zeros(sorted_scr.shape, sorted_scr.dtype)
        fetch(gd_ref, 0)

    @pl.when(step + 1 < pl.num_programs(0))
    def _():
        fetch(gd_next_ref, 1 - cur)

    def retire(j, carry):
        group_copy(cur, 0, 0).wait()
        return carry

    lax.fori_loop(0, gd_ref[GROUP_TABLE - 1], retire, 0)

    slot = slot_ref[...]
    scol = lax.broadcasted_iota(I32, (tm, SLOT_ROWS), 1)
    pick = jnp.zeros((tm, SLOT_ROWS), F32)
    for kk in range(TOP_K):
        pick = pick + jnp.where(scol == slot[:, kk:kk + 1], 1.0, 0.0)
    pick = pick.astype(BF16)
    words = sorted_scr[cur]
    lo = pltpu.bitcast(words << 16, F32).astype(BF16)
    hi = pltpu.bitcast(words & jnp.uint32(0xFFFF0000), F32).astype(BF16)
    y = jnp.concatenate([jnp.dot(pick, lo, preferred_element_type=F32),
                         jnp.dot(pick, hi, preferred_element_type=F32)], axis=1)
    x2 = x1_ref[...] + mod_ref[5:6, :] * y
    o_ref[...] = _rms(x2) * fg_ref[...]


def _combine(gd_flat, slot, x1, mod8, final_g, yb, seq):
    t, d = x1.shape
    tm = TOKEN_TILE
    nt = t // tm
    tiles_per_seq = seq // tm
    return pl.pallas_call(
        _combine_kernel,
        grid=(nt,),
        in_specs=[pl.BlockSpec((GROUP_TABLE,), lambda i: (i,), memory_space=pltpu.SMEM),
                  pl.BlockSpec((GROUP_TABLE,), lambda i: (jnp.minimum(i + 1, nt - 1),), memory_space=pltpu.SMEM),
                  pl.BlockSpec((tm, LANES), lambda i: (i, 0)),
                  pl.BlockSpec((tm, d), lambda i: (i, 0)),
                  pl.BlockSpec((None, SUBLANES, d), lambda i: (i // tiles_per_seq, 0, 0)),
                  pl.BlockSpec((1, d), lambda i: (0, 0)),
                  pl.BlockSpec(memory_space=pl.ANY)],
        out_specs=pl.BlockSpec((tm, d), lambda i: (i, 0)),
        out_shape=jax.ShapeDtypeStruct((t, d), F32),
        scratch_shapes=[pltpu.VMEM((2, SLOT_ROWS, d // 2), U32), pltpu.SemaphoreType.DMA((2,))],
        compiler_params=_cparams("arbitrary"),
        name="combine",
    )(gd_flat, gd_flat, slot, x1, mod8, final_g[None], yb)


def kernel(x, c, w_ada, b_ada, norm1_g, w_in, b_forget, conv_w, conv_b, conv_ln_g, conv_ln_b, w_conv_out,
           w_attn_out, w_out, norm2_g, w_router, b_router, w_exp_in, b_exp_in, w_exp_out, b_exp_out, final_g):
    bsz, seq, d = x.shape
    t = bsz * seq
    depth = w_ada.shape[0]
    xf = x.reshape(t, d)
    n_rows = t * TOP_K + (t // TOKEN_TILE) * N_EXPERTS * (CHUNK - 1) + N_EXPERTS * (EXPERT_BLOCK - CHUNK)
    n_rows = -(-n_rows // EXPERT_BLOCK) * EXPERT_BLOCK
    n_blocks = n_rows // EXPERT_BLOCK
    for l in range(depth):
        mod8 = _mod(c, w_ada[l], b_ada[l])
        act, q, k, v, aq, ak, tot, sgc, sga = _inproj(xf, mod8, norm1_g[l], w_in[l], b_forget[l], conv_w[l],
                                                      conv_b[l], conv_ln_g[l], conv_ln_b[l], seq)
        attn = _attention(q, aq, k, ak, v, tot, bsz, seq)
        x1, h2b, slot, slott, gatest, meta, cnt = _merge(xf, act, attn, sgc, sga, mod8, norm2_g[l], w_conv_out[l],
                                                         w_attn_out[l], w_out[l], w_router[l], b_router[l], seq)
        gd, blk = _dest(cnt, meta)
        gd_flat = gd.reshape(-1)
        block_e = blk[0].reshape(-1)[:n_blocks]
        n_active = blk[1, 0, :1]
        pend = blk[2, 0, :N_EXPERTS]
        xs = _dispatch(pend, n_active, gd_flat, slott, gatest, h2b, n_rows)
        yb = _experts(block_e, n_active, xs, w_exp_in[l], b_exp_in[l], w_exp_out[l], b_exp_out[l])
        assert depth == 1
        xf = _combine(gd_flat, slot, x1, mod8, final_g, yb, seq)
    return xf.reshape(bsz, seq, d)
```

```python
import functools

import jax
import jax.numpy as jnp
from jax import lax
from jax.experimental import pallas as pl
from jax.experimental.pallas import tpu as pltpu

F32 = jnp.float32
BF16 = jnp.bfloat16
I32 = jnp.int32
U32 = jnp.uint32
HIGHEST = lax.Precision.HIGHEST

LANES = 128
SUBLANES = 8
VMEM_LIMIT_BYTES = 56 * 1024 * 1024

N_HEADS = 8
HEAD_DIM = 64
CONV_KERNEL = 31
CONV_HALO = 32
N_EXPERTS = 32
TOP_K = 4
SWIGLU_LIMIT = 7.0
SWIGLU_ALPHA = 1.702
RMS_EPS = 1e-5
LN_EPS = 1e-5
N_MOD = 6
N_SPLIT = 3
LOG2_E = 1.4426950408889634
HEADS_PER_STEP = 2

INPROJ_TILE = 512
ATTN_BQ = 2048
ATTN_BK = 512
MERGE_TILE = 512
TOKEN_TILE = 256
EXPERT_BLOCK = 768
CHUNK = SUBLANES
SLOT_ROWS = 1280
GROUP_TABLE = SUBLANES * LANES
DEST_TILES_PER_STEP = 8
assert SLOT_ROWS >= TOKEN_TILE * TOP_K + N_EXPERTS * (CHUNK - 1) and SLOT_ROWS // CHUNK < GROUP_TABLE


def _cparams(*sem):
    return pltpu.CompilerParams(dimension_semantics=sem, vmem_limit_bytes=VMEM_LIMIT_BYTES)


def _resident(shape):
    nd = len(shape)
    return pl.BlockSpec(shape, lambda *_: (0,) * nd, pipeline_mode=pl.Buffered(1))


def _split3(x):
    p1 = x.astype(BF16)
    r1 = x - p1.astype(F32)
    p2 = r1.astype(BF16)
    p3 = (r1 - p2.astype(F32)).astype(BF16)
    return jnp.concatenate([p1, p2, p3], axis=1)


def _rms(x):
    return x * lax.rsqrt(jnp.mean(x * x, axis=-1, keepdims=True) + RMS_EPS)


def _mod_kernel(ct_ref, w_ref, b_ref, o_ref, *, batch):
    rows = []
    for b in range(batch):
        col = ct_ref[:, b:b + 1]
        rows.append(jnp.sum((col * jax.nn.sigmoid(col)) * w_ref[...], axis=0, keepdims=True) + b_ref[...])
    rows.append(jnp.zeros((SUBLANES - batch, rows[0].shape[1]), F32))
    o_ref[...] = jnp.concatenate(rows, axis=0)


def _mod(c, w_ada, b_ada, layer):
    bsz, d = c.shape
    n = w_ada.shape[2]
    tn = 1024
    out = pl.pallas_call(
        functools.partial(_mod_kernel, batch=bsz),
        grid=(n // tn,),
        in_specs=[pl.BlockSpec((d, bsz), lambda j: (0, 0)),
                  pl.BlockSpec((None, d, tn), lambda j: (layer, 0, j)),
                  pl.BlockSpec((1, tn), lambda j: (0, j))],
        out_specs=pl.BlockSpec((SUBLANES, tn), lambda j: (0, j)),
        out_shape=jax.ShapeDtypeStruct((SUBLANES, n), F32),
        compiler_params=_cparams("arbitrary"),
        name="mod",
    )(c.T, w_ada, b_ada[None])
    mod = out[:bsz].reshape(bsz, N_MOD, d)
    return jnp.pad(mod, ((0, 0), (0, SUBLANES - N_MOD), (0, 0)))


def _conv_module(a, first_of_seq, w_ref, b_ref, g_ref, beta_ref, xs_scr, ys_scr):
    ts = a.shape[0]
    xs_scr[0:CONV_HALO, :] = jnp.where(first_of_seq, 0.0, xs_scr[ts:ts + CONV_HALO, :])
    xs_scr[CONV_HALO:, :] = a
    acc = jnp.zeros(a.shape, F32) + b_ref[...]
    base = CONV_HALO - (CONV_KERNEL - 1)
    for rho in range(SUBLANES):
        offs = [o for o in range(base, base + CONV_KERNEL) if o % SUBLANES == rho]
        if not offs:
            continue
        rows = offs[-1] + ts - rho
        ys_scr[0:rows, :] = xs_scr[rho:rho + rows, :]
        for o in offs:
            lo = o - rho
            acc = acc + w_ref[o - base:o - base + 1, :] * ys_scr[lo:lo + ts, :]
    mu = jnp.mean(acc, axis=-1, keepdims=True)
    cen = acc - mu
    var = jnp.mean(cen * cen, axis=-1, keepdims=True)
    y = cen * lax.rsqrt(var + LN_EPS) * g_ref[...] + beta_ref[...]
    return (y * jax.nn.sigmoid(y)).astype(BF16)


def _inproj_kernel(x_ref, mod_ref, g_ref, wa_ref, wqkv_ref, wf_ref, wg_ref, bf_ref,
                   tri_ref, eq_ref, ek_ref, oneq_ref, onek_ref, cw_ref, cb_ref, cg_ref, cbeta_ref,
                   act_ref, q_ref, k_ref, v_ref, aq_ref, ak_ref, tot_ref, sgc_ref, sga_ref,
                   carry_scr, xs_scr, ys_scr, *, conv_width, d_model, tiles_per_seq):
    first_of_seq = pl.program_id(0) % tiles_per_seq == 0

    @pl.when(pl.program_id(0) == 0)
    def _():
        xs_scr[...] = jnp.zeros(xs_scr.shape, F32)
        carry_scr[...] = jnp.zeros(carry_scr.shape, F32)

    x = x_ref[...]
    h = _rms(x) * g_ref[...] * (1.0 + mod_ref[1:2, :]) + mod_ref[0:1, :]
    hb = h.astype(BF16)

    u = jnp.dot(hb, wa_ref[...], preferred_element_type=F32)
    a = u[:, :conv_width] * jax.nn.sigmoid(u[:, conv_width:])
    act_ref[...] = _conv_module(a, first_of_seq, cw_ref, cb_ref, cg_ref, cbeta_ref, xs_scr, ys_scr)

    fl = jnp.dot(hb, wf_ref[...], preferred_element_type=F32) + bf_ref[...]
    lane = lax.broadcasted_iota(I32, fl.shape, 1)
    log_f = -(jnp.maximum(-fl, 0.0) + jnp.log1p(jnp.exp(-jnp.abs(fl))))
    log_f = jnp.where(lane < N_HEADS, log_f, 0.0)
    cs = jnp.dot(tri_ref[...], _split3(log_f), preferred_element_type=F32)
    cum = (cs[:, :LANES] + cs[:, LANES:2 * LANES] + cs[:, 2 * LANES:]) * LOG2_E
    tm = cum.shape[0]

    pieces = []
    for sb in range(tm // ATTN_BK):
        blk = cum[sb * ATTN_BK:(sb + 1) * ATTN_BK, :]
        if sb > 0:
            blk = blk - cum[sb * ATTN_BK - 1:sb * ATTN_BK, :]
        tot_ref[sb] = blk[ATTN_BK - 1:, :]
        pieces.append(blk)
    cum_k = pieces[0] if len(pieces) == 1 else jnp.concatenate(pieces, axis=0)

    first_of_block = pl.program_id(0) % (ATTN_BQ // tm) == 0
    cum_q = cum + jnp.where(first_of_block, 0.0, carry_scr[0:1, :])
    carry_scr[...] = jnp.broadcast_to(cum_q[tm - 1:, :], carry_scr.shape)

    qkv = jnp.dot(hb, wqkv_ref[...], preferred_element_type=F32)
    aw = qkv.shape[1] // 3
    q_ref[...] = (qkv[:, :aw] * (HEAD_DIM ** -0.5 * LOG2_E)).astype(BF16)
    k_ref[...] = qkv[:, aw:2 * aw].astype(BF16)
    v_ref[...] = qkv[:, 2 * aw:].astype(BF16)
    aq_ref[...] = (jnp.dot(_split3(cum_q), eq_ref[...], preferred_element_type=F32) + oneq_ref[...]).astype(BF16)
    ak_ref[...] = (jnp.dot(_split3(cum_k), ek_ref[...], preferred_element_type=F32) + onek_ref[...]).astype(BF16)

    gts = jnp.dot(hb, wg_ref[...], preferred_element_type=F32)
    sgc_ref[...] = jax.nn.sigmoid(gts[:, :d_model]).astype(BF16)
    sga_ref[...] = jax.nn.sigmoid(gts[:, d_model:]).astype(BF16)


def _aug_constants():
    rows = jnp.arange(N_SPLIT * LANES)
    piece, head = rows // LANES, rows % LANES
    cols = jnp.arange(N_HEADS * HEAD_DIM)
    pair, off = cols // LANES, cols % LANES
    chead = 2 * pair + jnp.where(off >= HEAD_DIM, 0, 1)
    pos = off % HEAD_DIM
    valid = (head[:, None] < N_HEADS) & (head[:, None] == chead[None, :])
    eq = jnp.where(valid & (pos[None, :] == piece[:, None]), 1.0, 0.0).astype(BF16)
    ek = jnp.where(valid & (pos[None, :] == piece[:, None] + N_SPLIT), -1.0, 0.0).astype(BF16)
    oneq = jnp.where((pos >= N_SPLIT) & (pos < 2 * N_SPLIT), 1.0, 0.0).astype(F32)[None]
    onek = jnp.where(pos < N_SPLIT, 1.0, 0.0).astype(F32)[None]
    return eq, ek, oneq, onek


def _inproj(x2, mod8, norm_g, w_in, b_forget, conv_w, conv_b, conv_ln_g, conv_ln_b, seq):
    t, d = x2.shape
    tm = INPROJ_TILE
    nsub = tm // ATTN_BK
    cw = d // 2
    aw = N_HEADS * HEAD_DIM
    o0 = 2 * cw
    wa = w_in[:, :o0].astype(BF16)
    o1 = o0 + 3 * aw
    wqkv = w_in[:, o0:o1].astype(BF16)
    wf = jnp.pad(w_in[:, o1:o1 + N_HEADS], ((0, 0), (0, LANES - N_HEADS))).astype(BF16)
    bfp = jnp.pad(b_forget, (0, LANES - N_HEADS))[None]
    wg = w_in[:, o1 + N_HEADS:].astype(BF16)
    r = jnp.arange(tm)
    tri = (r[None, :] <= r[:, None]).astype(BF16)
    eq, ek, oneq, onek = _aug_constants()
    tiles_per_seq = seq // tm
    row = lambda i: (i, 0)
    outs = pl.pallas_call(
        functools.partial(_inproj_kernel, conv_width=cw, d_model=d, tiles_per_seq=tiles_per_seq),
        grid=(t // tm,),
        in_specs=[pl.BlockSpec((tm, d), row),
                  pl.BlockSpec((None, SUBLANES, d), lambda i: (i // tiles_per_seq, 0, 0)),
                  _resident((1, d)),
                  _resident(wa.shape), _resident(wqkv.shape),
                  _resident(wf.shape), _resident(wg.shape), _resident(bfp.shape),
                  _resident(tri.shape), _resident(eq.shape), _resident(ek.shape),
                  _resident(oneq.shape), _resident(onek.shape),
                  _resident(conv_w.shape), _resident((1, cw)), _resident((1, cw)), _resident((1, cw))],
        out_specs=[pl.BlockSpec((tm, cw), row)] + [pl.BlockSpec((tm, aw), row)] * 5 + [
                   pl.BlockSpec((nsub, 1, LANES), lambda i: (i, 0, 0)),
                   pl.BlockSpec((tm, d), row), pl.BlockSpec((tm, d), row)],
        out_shape=[jax.ShapeDtypeStruct((t, cw), BF16)] + [jax.ShapeDtypeStruct((t, aw), BF16)] * 5 + [
                   jax.ShapeDtypeStruct((t // ATTN_BK, 1, LANES), F32),
                   jax.ShapeDtypeStruct((t, d), BF16), jax.ShapeDtypeStruct((t, d), BF16)],
        scratch_shapes=[pltpu.VMEM((SUBLANES, LANES), F32),
                        pltpu.VMEM((CONV_HALO + tm, cw), F32), pltpu.VMEM((CONV_HALO + tm, cw), F32)],
        compiler_params=_cparams("arbitrary"),
        name="inproj",
    )(x2, mod8, norm_g[None], wa, wqkv, wf, wg, bfp, tri, eq, ek, oneq, onek,
      conv_w, conv_b[None], conv_ln_g[None], conv_ln_b[None])
    return outs


def _attn_kernel(q_ref, aq_ref, k_ref, ak_ref, v_ref, tot_ref, o_ref, qop_scr, m_scr, acc_scr):
    bq = q_ref.shape[0]
    bk = ATTN_BK
    ratio = bq // bk
    qi = pl.program_id(2)
    nt = (((1,), (1,)), ((), ()))
    heads = range(HEADS_PER_STEP)
    zero = jnp.zeros((1, LANES), F32)

    def splice(hh, main, extra):
        first = lax.broadcasted_iota(I32, main.shape, 1) < HEAD_DIM
        return jnp.where(first, main, extra) if hh == 0 else jnp.where(first, extra, main)

    for hh in heads:
        qop_scr[hh] = splice(hh, q_ref[...], aq_ref[...])

    def scores(hh, kb, row0):
        start = pl.multiple_of(kb * bk, bk)
        k = splice(hh, k_ref[pl.ds(start, bk), :], ak_ref[pl.ds(start, bk), :])
        v = v_ref[pl.ds(start, bk), :]
        v = splice(hh, v, jnp.ones(v.shape, v.dtype))
        s = lax.dot_general(qop_scr[hh, row0:, :], k, nt, preferred_element_type=F32)
        return s, v

    def update(hh, s, v, delta, row0):
        m_prev = m_scr[hh, row0:, :]
        m_new = jnp.maximum(m_prev, jnp.max(s, axis=1, keepdims=True) + delta)
        shift = m_new - delta
        p = jnp.exp2(s - shift[:, :1])
        alpha = jnp.exp2(m_prev - m_new)
        pv = jnp.dot(p.astype(BF16), v, preferred_element_type=F32)
        acc_scr[hh, row0:, :] = alpha * acc_scr[hh, row0:, :] + pv
        m_scr[hh, row0:, :] = m_new

    for hh in heads:
        delta = zero
        for sb in range(ratio):
            kb = qi * ratio + sb
            row0 = sb * bk
            s, v = scores(hh, kb, row0)
            row = lax.broadcasted_iota(I32, s.shape, 0)
            col = lax.broadcasted_iota(I32, s.shape, 1)
            s = jnp.where(col <= row, s, -jnp.inf)
            if sb == 0:
                m0 = jnp.max(s, axis=1, keepdims=True)
                p = jnp.exp2(s - m0)
                m_scr[hh] = jnp.broadcast_to(m0, m_scr.shape[1:])
                acc_scr[hh] = jnp.dot(p.astype(BF16), v, preferred_element_type=F32)
            else:
                update(hh, s, v, delta, row0)
            delta = delta - tot_ref[hh, pl.ds(kb, 1), :]

    def body(step, deltas):
        kb = qi * ratio - 1 - step
        new_deltas = []
        for hh in heads:
            delta = deltas[hh] + tot_ref[hh, pl.ds(kb, 1), :]
            s, v = scores(hh, kb, 0)
            update(hh, s, v, delta, 0)
            new_deltas.append(delta)
        return tuple(new_deltas)

    lax.fori_loop(0, qi * ratio, body, tuple(zero for _ in heads))
    acc0, acc1 = acc_scr[0], acc_scr[1]
    out0 = acc0 / acc0[:, HEAD_DIM:HEAD_DIM + 1]
    out1 = acc1 / acc1[:, 0:1]
    o_ref[...] = splice(0, out0, out1).astype(BF16)


def _attention(q, aq, k, ak, v, tot, bsz, seq):
    t, aw = q.shape
    bq = ATTN_BQ
    nq = seq // bq
    nk = seq // ATTN_BK
    assert HEADS_PER_STEP * HEAD_DIM == LANES
    tot = tot.reshape(bsz, nk, LANES)[:, :, :N_HEADS].transpose(0, 2, 1)
    tot = jnp.broadcast_to(tot[..., None], (bsz, N_HEADS, nk, LANES))
    qblk = pl.BlockSpec((bq, LANES), lambda b, h, i: (b * nq + i, h))
    kblk = pl.BlockSpec((None, seq, LANES), lambda b, h, i: (b, 0, h), pipeline_mode=pl.Buffered(1))
    seq3 = lambda a: a.reshape(bsz, seq, aw)
    return pl.pallas_call(
        _attn_kernel,
        grid=(bsz, N_HEADS // HEADS_PER_STEP, nq),
        in_specs=[qblk, qblk, kblk, kblk, kblk,
                  pl.BlockSpec((None, HEADS_PER_STEP, nk, LANES), lambda b, h, i: (b, h, 0, 0))],
        out_specs=qblk,
        out_shape=jax.ShapeDtypeStruct((t, aw), BF16),
        scratch_shapes=[pltpu.VMEM((HEADS_PER_STEP, bq, LANES), BF16),
                        pltpu.VMEM((HEADS_PER_STEP, bq, LANES), F32),
                        pltpu.VMEM((HEADS_PER_STEP, bq, LANES), F32)],
        compiler_params=_cparams("arbitrary", "arbitrary", "arbitrary"),
        name="attn",
    )(q, aq, seq3(k), seq3(ak), seq3(v), tot)


def _pack_bf16_pair(lo, hi):
    lo_bits = pltpu.bitcast(lo.astype(BF16).astype(F32), U32)
    hi_bits = pltpu.bitcast(hi.astype(BF16).astype(F32), U32)
    return (hi_bits & jnp.uint32(0xFFFF0000)) | (lo_bits >> 16)


def _unpack_bf16_pair(w):
    lo = pltpu.bitcast(w << 16, F32)
    hi = pltpu.bitcast(w & jnp.uint32(0xFFFF0000), F32)
    return jnp.concatenate([lo, hi], axis=1)


def _expert_onehots(ids):
    lane = lax.broadcasted_iota(I32, ids.shape, 1)
    return lane, [lane == ids[:, kk:kk + 1] for kk in range(TOP_K)]


def _route_tile(ids, gates, carry, tri, utri):
    lane, hots = _expert_onehots(ids)
    multi = jnp.zeros(lane.shape, F32)
    for hot in hots:
        multi = multi + jnp.where(hot, 1.0, 0.0)
    before = jnp.dot(tri, multi.astype(BF16), preferred_element_type=F32)
    units = jnp.ceil(jnp.sum(multi, axis=0, keepdims=True) * (1.0 / CHUNK))
    units8 = jnp.broadcast_to(units, (SUBLANES, LANES)).astype(BF16)
    lstart = jnp.dot(units8, utri, preferred_element_type=F32)[0:1, :] * CHUNK
    csize = units * CHUNK
    slot = jnp.zeros(lane.shape, F32)
    for kk, hot in enumerate(hots):
        sl = jnp.sum(jnp.where(hot, lstart + before, 0.0), axis=1, keepdims=True)
        slot = jnp.where(lane == kk, sl, slot)
    row = lax.broadcasted_iota(I32, (SUBLANES, LANES), 0)
    meta = jnp.where(row == 0, csize, jnp.where(row == 1, lstart, jnp.where(row == 2, carry, 0.0)))
    slott = slot.T[0:SUBLANES, :].astype(I32)
    gatest = gates.T[0:SUBLANES, :]
    return slot.astype(I32), slott, gatest, meta, carry + csize


def _merge_kernel(x_ref, act_ref, attn_ref, sgc_ref, sga_ref, mod_ref, g2_ref, wco_ref, wao_ref, wout_ref,
                  wr_ref, br_ref, tri_ref, utri_ref,
                  x1_ref, h2_ref, slot_ref, slott_ref, gatest_ref, meta_ref, cnt_ref, carry_scr):
    @pl.when(pl.program_id(0) == 0)
    def _():
        carry_scr[...] = jnp.zeros(carry_scr.shape, F32)

    conv_out = jnp.dot(act_ref[...], wco_ref[...], preferred_element_type=F32)
    attn_out = jnp.dot(attn_ref[...], wao_ref[...], preferred_element_type=F32)
    mixed = sgc_ref[...].astype(F32) * conv_out + sga_ref[...].astype(F32) * attn_out
    upd = jnp.dot(mixed.astype(BF16), wout_ref[...], preferred_element_type=F32)
    x1 = x_ref[...] + mod_ref[2:3, :] * upd
    x1_ref[...] = x1
    h2 = _rms(x1) * g2_ref[...] * (1.0 + mod_ref[4:5, :]) + mod_ref[3:4, :]
    h2_hi = h2.astype(BF16)
    h2_ref[...] = h2_hi
    h2_lo = (h2 - h2_hi.astype(F32)).astype(BF16)
    h2_cat = jnp.concatenate([h2_hi, h2_lo, h2_hi], axis=1)
    logits = jnp.dot(h2_cat, wr_ref[...], preferred_element_type=F32) + br_ref[...]
    lane = lax.broadcasted_iota(I32, logits.shape, 1)
    work = jnp.where(lane < N_EXPERTS, logits, -jnp.inf)
    ids = jnp.zeros(logits.shape, I32)
    vals = []
    for kk in range(TOP_K):
        mx = jnp.max(work, axis=1, keepdims=True)
        idx = jnp.min(jnp.where(work == mx, lane, LANES), axis=1, keepdims=True)
        ids = jnp.where(lane == kk, idx, ids)
        vals.append(mx)
        work = jnp.where(lane == idx, -jnp.inf, work)
    exps = [jnp.exp(vv - vals[0]) for vv in vals]
    den = exps[0] + exps[1] + exps[2] + exps[3]
    gates = jnp.zeros(logits.shape, F32)
    for kk in range(TOP_K):
        gates = jnp.where(lane == kk, exps[kk] / den, gates)

    carry = carry_scr[0:1, :]
    tt = TOKEN_TILE
    for sub in range(logits.shape[0] // tt):
        rows = slice(sub * tt, (sub + 1) * tt)
        slot, slott, gatest, meta, carry = _route_tile(ids[rows, :], gates[rows, :], carry,
                                                       tri_ref[...], utri_ref[...])
        slot_ref[rows, :] = slot
        slott_ref[sub * SUBLANES:(sub + 1) * SUBLANES, :] = slott
        gatest_ref[sub * SUBLANES:(sub + 1) * SUBLANES, :] = gatest
        meta_ref[sub] = meta
    carry_scr[...] = jnp.broadcast_to(carry, carry_scr.shape)
    cnt_ref[...] = jnp.broadcast_to(carry, cnt_ref.shape)


def _merge(x2, act, attn, sgc, sga, mod8, norm2_g, w_conv_out, w_attn_out, w_out, w_router, b_router, seq):
    t, d = x2.shape
    tm = MERGE_TILE
    tt = TOKEN_TILE
    sub = tm // tt
    nt = t // tt
    cw = act.shape[1]
    sw = attn.shape[1]
    wco = w_conv_out.astype(BF16)
    wao = w_attn_out.astype(BF16)
    wout = w_out.astype(BF16)
    wr = jnp.pad(w_router, ((0, 0), (0, LANES - N_EXPERTS)))
    wr_hi = wr.astype(BF16)
    wr_lo = (wr - wr_hi.astype(F32)).astype(BF16)
    wr = jnp.concatenate([wr_hi, wr_hi, wr_lo], axis=0)
    br = jnp.pad(b_router, (0, LANES - N_EXPERTS))[None]
    r = jnp.arange(tt)
    tri = (r[None, :] < r[:, None]).astype(BF16)
    e = jnp.arange(LANES)
    utri = (e[:, None] < e[None, :]).astype(BF16)
    tiles_per_seq = seq // tm
    row = lambda i: (i, 0)
    return pl.pallas_call(
        _merge_kernel,
        grid=(t // tm,),
        in_specs=[pl.BlockSpec((tm, d), row), pl.BlockSpec((tm, cw), row), pl.BlockSpec((tm, sw), row),
                  pl.BlockSpec((tm, d), row), pl.BlockSpec((tm, d), row),
                  pl.BlockSpec((None, SUBLANES, d), lambda i: (i // tiles_per_seq, 0, 0)),
                  _resident((1, d)), _resident(wco.shape), _resident(wao.shape), _resident(wout.shape),
                  _resident(wr.shape), _resident(br.shape), _resident(tri.shape), _resident(utri.shape)],
        out_specs=[pl.BlockSpec((tm, d), row), pl.BlockSpec((tm, d), row),
                   pl.BlockSpec((tm, LANES), row),
                   pl.BlockSpec((sub * SUBLANES, tt), row), pl.BlockSpec((sub * SUBLANES, tt), row),
                   pl.BlockSpec((sub, SUBLANES, LANES), lambda i: (i, 0, 0)),
                   pl.BlockSpec((SUBLANES, LANES), lambda i: (0, 0))],
        out_shape=[jax.ShapeDtypeStruct((t, d), F32), jax.ShapeDtypeStruct((t, d), BF16),
                   jax.ShapeDtypeStruct((t, LANES), I32),
                   jax.ShapeDtypeStruct((nt * SUBLANES, tt), I32), jax.ShapeDtypeStruct((nt * SUBLANES, tt), F32),
                   jax.ShapeDtypeStruct((nt, SUBLANES, LANES), F32),
                   jax.ShapeDtypeStruct((SUBLANES, LANES), F32)],
        scratch_shapes=[pltpu.VMEM((SUBLANES, LANES), F32)],
        compiler_params=_cparams("arbitrary"),
        name="merge",
    )(x2, act, attn, sgc, sga, mod8, norm2_g[None], wco, wao, wout, wr, br, tri, utri)


def _dest_kernel(cnt_ref, meta_ref, tri_ref, gd_ref, blk_ref):
    cnt = cnt_ref[...]
    padded = jnp.ceil(cnt * (1.0 / EXPERT_BLOCK)) * EXPERT_BLOCK
    pend = jnp.dot(padded, tri_ref[...], preferred_element_type=F32, precision=HIGHEST)
    pstart = (pend - padded)[0:1, :]

    grow = lax.broadcasted_iota(I32, (SUBLANES, LANES), 0)
    glane = lax.broadcasted_iota(I32, (SUBLANES, LANES), 1)
    gpos = ((grow * LANES + glane) * CHUNK).astype(F32)
    last = jnp.logical_and(grow == SUBLANES - 1, glane == LANES - 1)
    for tile in range(meta_ref.shape[0]):
        meta = meta_ref[tile]
        csize, lstart, seg_off = meta[0:1, :], meta[1:2, :], meta[2:3, :]
        lend = lstart + csize
        base = pstart + seg_off - lstart
        owner = jnp.zeros(gpos.shape, I32)
        for e in range(N_EXPERTS):
            owner = owner + jnp.where(gpos >= lend[0:1, e:e + 1], 1, 0)
        gdst = gpos
        for e in range(N_EXPERTS):
            gdst = gdst + jnp.where(owner == e, base[0:1, e:e + 1], 0.0)
        n_groups = lend[0:1, N_EXPERTS - 1:N_EXPERTS] * (1.0 / CHUNK)
        gd_ref[tile] = jnp.where(last, n_groups, gdst).astype(I32)

    @pl.when(pl.program_id(0) == 0)
    def _():
        shape = blk_ref.shape[1:]
        pos = (lax.broadcasted_iota(I32, shape, 0) * LANES + lax.broadcasted_iota(I32, shape, 1))
        pos = pos.astype(F32) * EXPERT_BLOCK
        be = jnp.zeros(shape, I32)
        for e in range(N_EXPERTS):
            be = be + jnp.where(pos >= pend[0:1, e:e + 1], 1, 0)
        blk_ref[0] = jnp.minimum(be, N_EXPERTS - 1)
        nact = pend[0:1, N_EXPERTS - 1:N_EXPERTS] * (1.0 / EXPERT_BLOCK)
        blk_ref[1] = jnp.broadcast_to(nact, shape).astype(I32)
        blk_ref[2] = jnp.broadcast_to(pend[0:1, :], shape).astype(I32)


def _dest(cnt, meta):
    nt = meta.shape[0]
    per_step = DEST_TILES_PER_STEP
    r = jnp.arange(LANES)
    tri = (r[:, None] <= r[None, :]).astype(F32)
    return pl.pallas_call(
        _dest_kernel,
        grid=(nt // per_step,),
        in_specs=[pl.BlockSpec((SUBLANES, LANES), lambda i: (0, 0)),
                  pl.BlockSpec((per_step, SUBLANES, LANES), lambda i: (i, 0, 0)),
                  pl.BlockSpec((LANES, LANES), lambda i: (0, 0))],
        out_specs=[pl.BlockSpec((per_step, SUBLANES, LANES), lambda i: (i, 0, 0)),
                   pl.BlockSpec((3, SUBLANES, LANES), lambda i: (0, 0, 0))],
        out_shape=[jax.ShapeDtypeStruct((nt, SUBLANES, LANES), I32),
                   jax.ShapeDtypeStruct((3, SUBLANES, LANES), I32)],
        compiler_params=_cparams("arbitrary"),
        name="dest",
    )(cnt, meta, tri)


def _row_copy(src, dst, sem):
    return pltpu.make_async_copy(src, dst, sem)


def _dispatch_kernel(pend_ref, na_ref, gd_ref, slott_ref, gatest_ref, h2_ref, xs_ref, zero_ref, sorted_scr, sem):
    tm = h2_ref.shape[0]
    bm = zero_ref.shape[0]
    nb = xs_ref.shape[0] // bm

    @pl.when(pl.program_id(0) == 0)
    def _():
        zero_ref[...] = jnp.zeros(zero_ref.shape, zero_ref.dtype)

        def zero_block(start):
            return _row_copy(zero_ref, xs_ref.at[pl.ds(pl.multiple_of(start, bm), bm)], sem)

        def nonempty(e):
            return pend_ref[e] > (pend_ref[e - 1] if e > 0 else 0)

        for e in range(N_EXPERTS):
            @pl.when(nonempty(e))
            def _():
                zero_block(pend_ref[e] - bm).start()

        def tail(b, carry):
            zero_block(b * bm).start()
            return carry

        lax.fori_loop(na_ref[0], nb, tail, 0)
        for e in range(N_EXPERTS):
            @pl.when(nonempty(e))
            def _():
                zero_block(0).wait()

        def tail_wait(b, carry):
            zero_block(0).wait()
            return carry

        lax.fori_loop(na_ref[0], nb, tail_wait, 0)

    srow = lax.broadcasted_iota(I32, (SLOT_ROWS, tm), 0)
    perm = jnp.zeros((SLOT_ROWS, tm), F32)
    gate = jnp.zeros((SLOT_ROWS, 1), F32)
    for kk in range(TOP_K):
        hit = srow == slott_ref[kk:kk + 1, :]
        perm = perm + jnp.where(hit, 1.0, 0.0)
        gate = gate + jnp.sum(jnp.where(hit, gatest_ref[kk:kk + 1, :], 0.0), axis=1, keepdims=True)
    perm = perm.astype(BF16)
    half = h2_ref.shape[1] // 2
    lo = jnp.dot(perm, h2_ref[:, :half], preferred_element_type=F32)
    hi = jnp.dot(perm, h2_ref[:, half:], preferred_element_type=F32)
    sorted_scr[:, :half] = _pack_bf16_pair(lo, hi)
    sorted_scr[:, half:] = pltpu.bitcast(jnp.broadcast_to(gate, (SLOT_ROWS, LANES)), U32)

    n_groups = gd_ref[GROUP_TABLE - 1]

    def group_copy(j, d):
        src = sorted_scr.at[pl.ds(pl.multiple_of(j * CHUNK, CHUNK), CHUNK)]
        return _row_copy(src, xs_ref.at[pl.ds(pl.multiple_of(d, CHUNK), CHUNK)], sem)

    def issue(j, carry):
        group_copy(j, gd_ref[j]).start()
        return carry

    def retire(j, carry):
        group_copy(0, 0).wait()
        return carry

    lax.fori_loop(0, n_groups, issue, 0)
    lax.fori_loop(0, n_groups, retire, 0)


def _dispatch(pend, n_active, gd_flat, slott, gatest, h2b, n_rows):
    t, d = h2b.shape
    w = d // 2 + LANES
    tm = TOKEN_TILE
    grid_spec = pltpu.PrefetchScalarGridSpec(
        num_scalar_prefetch=2,
        grid=(t // tm,),
        in_specs=[pl.BlockSpec((GROUP_TABLE,), lambda i, pe, na: (i,), memory_space=pltpu.SMEM),
                  pl.BlockSpec((SUBLANES, tm), lambda i, pe, na: (i, 0)),
                  pl.BlockSpec((SUBLANES, tm), lambda i, pe, na: (i, 0)),
                  pl.BlockSpec((tm, d), lambda i, pe, na: (i, 0))],
        out_specs=pl.BlockSpec(memory_space=pl.ANY),
        scratch_shapes=[pltpu.VMEM((EXPERT_BLOCK, w), U32), pltpu.VMEM((SLOT_ROWS, w), U32),
                        pltpu.SemaphoreType.DMA(())],
    )
    return pl.pallas_call(
        _dispatch_kernel,
        grid_spec=grid_spec,
        out_shape=jax.ShapeDtypeStruct((n_rows, w), U32),
        compiler_params=_cparams("arbitrary"),
        name="dispatch",
    )(pend, n_active, gd_flat, slott, gatest, h2b)


def _expert_kernel(be_ref, na_ref, xs_ref, w1_ref, b1_ref, w2_ref, b2_ref, y_ref, w1b_ref, w2b_ref, *, d_expert):
    i = pl.program_id(0)
    active = i < na_ref[0]
    prev = be_ref[jnp.maximum(i - 1, 0)]
    new_expert = jnp.logical_or(i == 0, be_ref[i] != prev)
    half = y_ref.shape[1]

    @pl.when(jnp.logical_and(active, new_expert))
    def _():
        w1b_ref[...] = w1_ref[...].astype(BF16)
        w2b_ref[...] = w2_ref[...].astype(BF16)

    @pl.when(active)
    def _():
        x = _unpack_bf16_pair(xs_ref[:, :half]).astype(BF16)
        gate = pltpu.bitcast(xs_ref[:, half:], F32)[:, 0:1]
        hc = jnp.dot(x, w1b_ref[...], preferred_element_type=F32) + b1_ref[...]
        g = jnp.minimum(hc[:, :d_expert], SWIGLU_LIMIT)
        u = jnp.clip(hc[:, d_expert:], -SWIGLU_LIMIT, SWIGLU_LIMIT)
        hm = (u + 1.0) * (g * jax.nn.sigmoid(SWIGLU_ALPHA * g))
        y = (jnp.dot(hm.astype(BF16), w2b_ref[...], preferred_element_type=F32) + b2_ref[...]) * gate
        y_ref[...] = _pack_bf16_pair(y[:, :half], y[:, half:])

    @pl.when(jnp.logical_not(active))
    def _():
        y_ref[...] = jnp.zeros(y_ref.shape, U32)


def _experts(block_e, n_active, xs, w1, b1, w2, b2):
    n_rows, w = xs.shape
    n_exp, d, two_de = w1.shape
    de = two_de // 2
    bm = EXPERT_BLOCK
    nb = n_rows // bm

    def blk(i, be, na):
        return jnp.maximum(jnp.minimum(i, na[0] - 1), 0)

    grid_spec = pltpu.PrefetchScalarGridSpec(
        num_scalar_prefetch=2,
        grid=(nb,),
        in_specs=[pl.BlockSpec((bm, w), lambda i, be, na: (blk(i, be, na), 0)),
                  pl.BlockSpec((None, d, two_de), lambda i, be, na: (be[blk(i, be, na)], 0, 0)),
                  pl.BlockSpec((None, 1, two_de), lambda i, be, na: (be[blk(i, be, na)], 0, 0)),
                  pl.BlockSpec((None, de, d), lambda i, be, na: (be[blk(i, be, na)], 0, 0)),
                  pl.BlockSpec((None, 1, d), lambda i, be, na: (be[blk(i, be, na)], 0, 0))],
        out_specs=pl.BlockSpec((bm, d // 2), lambda i, be, na: (i, 0)),
        scratch_shapes=[pltpu.VMEM((d, two_de), BF16), pltpu.VMEM((de, d), BF16)],
    )
    return pl.pallas_call(
        functools.partial(_expert_kernel, d_expert=de),
        grid_spec=grid_spec,
        out_shape=jax.ShapeDtypeStruct((n_rows, d // 2), U32),
        compiler_params=_cparams("arbitrary"),
        name="expert",
    )(block_e, n_active, xs, w1, b1[:, None, :], w2, b2[:, None, :])


def _combine_kernel(gd_ref, gd_next_ref, slot_ref, x1_ref, mod_ref, fg_ref, yb_ref, o_ref, sorted_scr, sem):
    tm = x1_ref.shape[0]
    step = pl.program_id(0)
    cur = step % 2

    def group_copy(buf, j, d):
        dst = sorted_scr.at[buf, pl.ds(pl.multiple_of(j * CHUNK, CHUNK), CHUNK)]
        return _row_copy(yb_ref.at[pl.ds(pl.multiple_of(d, CHUNK), CHUNK)], dst, sem.at[buf])

    def fetch(table_ref, buf):
        def issue(j, carry):
            group_copy(buf, j, table_ref[j]).start()
            return carry
        lax.fori_loop(0, table_ref[GROUP_TABLE - 1], issue, 0)

    @pl.when(step == 0)
    def _():
        sorted_scr[...] = jnp.zeros(sorted_scr.shape, sorted_scr.dtype)
        fetch(gd_ref, 0)

    @pl.when(step + 1 < pl.num_programs(0))
    def _():
        fetch(gd_next_ref, 1 - cur)

    def retire(j, carry):
        group_copy(cur, 0, 0).wait()
        return carry

    lax.fori_loop(0, gd_ref[GROUP_TABLE - 1], retire, 0)

    slot = slot_ref[...]
    scol = lax.broadcasted_iota(I32, (tm, SLOT_ROWS), 1)
    pick = jnp.zeros((tm, SLOT_ROWS), F32)
    for kk in range(TOP_K):
        pick = pick + jnp.where(scol == slot[:, kk:kk + 1], 1.0, 0.0)
    pick = pick.astype(BF16)
    words = sorted_scr[cur]
    lo = pltpu.bitcast(words << 16, F32).astype(BF16)
    hi = pltpu.bitcast(words & jnp.uint32(0xFFFF0000), F32).astype(BF16)
    y = jnp.concatenate([jnp.dot(pick, lo, preferred_element_type=F32),
                         jnp.dot(pick, hi, preferred_element_type=F32)], axis=1)
    x2 = x1_ref[...] + mod_ref[5:6, :] * y
    o_ref[...] = _rms(x2) * fg_ref[...]


def _combine(gd_flat, slot, x1, mod8, final_g, yb, seq):
    t, d = x1.shape
    tm = TOKEN_TILE
    nt = t // tm
    tiles_per_seq = seq // tm
    return pl.pallas_call(
        _combine_kernel,
        grid=(nt,),
        in_specs=[pl.BlockSpec((GROUP_TABLE,), lambda i: (i,), memory_space=pltpu.SMEM),
                  pl.BlockSpec((GROUP_TABLE,), lambda i: (jnp.minimum(i + 1, nt - 1),), memory_space=pltpu.SMEM),
                  pl.BlockSpec((tm, LANES), lambda i: (i, 0)),
                  pl.BlockSpec((tm, d), lambda i: (i, 0)),
                  pl.BlockSpec((None, SUBLANES, d), lambda i: (i // tiles_per_seq, 0, 0)),
                  pl.BlockSpec((1, d), lambda i: (0, 0)),
                  pl.BlockSpec(memory_space=pl.ANY)],
        out_specs=pl.BlockSpec((tm, d), lambda i: (i, 0)),
        out_shape=jax.ShapeDtypeStruct((t, d), F32),
        scratch_shapes=[pltpu.VMEM((2, SLOT_ROWS, d // 2), U32), pltpu.SemaphoreType.DMA((2,))],
        compiler_params=_cparams("arbitrary"),
        name="combine",
    )(gd_flat, gd_flat, slot, x1, mod8, final_g[None], yb)


def kernel(x, c, w_ada, b_ada, norm1_g, w_in, b_forget, conv_w, conv_b, conv_ln_g, conv_ln_b, w_conv_out,
           w_attn_out, w_out, norm2_g, w_router, b_router, w_exp_in, b_exp_in, w_exp_out, b_exp_out, final_g):
    bsz, seq, d = x.shape
    t = bsz * seq
    depth = w_ada.shape[0]
    xf = x.reshape(t, d)
    n_rows = t * TOP_K + (t // TOKEN_TILE) * N_EXPERTS * (CHUNK - 1) + N_EXPERTS * (EXPERT_BLOCK - CHUNK)
    n_rows = -(-n_rows // EXPERT_BLOCK) * EXPERT_BLOCK
    n_blocks = n_rows // EXPERT_BLOCK
    for l in range(depth):
        mod8 = _mod(c, w_ada, b_ada[l], l)
        act, q, k, v, aq, ak, tot, sgc, sga = _inproj(xf, mod8, norm1_g[l], w_in[l], b_forget[l], conv_w[l],
                                                      conv_b[l], conv_ln_g[l], conv_ln_b[l], seq)
        attn = _attention(q, aq, k, ak, v, tot, bsz, seq)
        x1, h2b, slot, slott, gatest, meta, cnt = _merge(xf, act, attn, sgc, sga, mod8, norm2_g[l], w_conv_out[l],
                                                         w_attn_out[l], w_out[l], w_router[l], b_router[l], seq)
        gd, blk = _dest(cnt, meta)
        gd_flat = gd.reshape(-1)
        block_e = blk[0].reshape(-1)[:n_blocks]
        n_active = blk[1, 0, :1]
        pend = blk[2, 0, :N_EXPERTS]
        xs = _dispatch(pend, n_active, gd_flat, slott, gatest, h2b, n_rows)
        yb = _experts(block_e, n_active, xs, w_exp_in[l], b_exp_in[l], w_exp_out[l], b_exp_out[l])
        assert depth == 1
        xf = _combine(gd_flat, slot, x1, mod8, final_g, yb, seq)
    return xf.reshape(bsz, seq, d)
```

```python
import functools

import jax
import jax.numpy as jnp
from jax import lax
from jax.experimental import pallas as pl
from jax.experimental.pallas import tpu as pltpu

F32 = jnp.float32
BF16 = jnp.bfloat16
I32 = jnp.int32
U32 = jnp.uint32
HIGHEST = lax.Precision.HIGHEST

LANES = 128
SUBLANES = 8
VMEM_LIMIT_BYTES = 56 * 1024 * 1024

N_HEADS = 8
HEAD_DIM = 64
CONV_KERNEL = 31
CONV_HALO = 32
N_EXPERTS = 32
TOP_K = 4
SWIGLU_LIMIT = 7.0
SWIGLU_ALPHA = 1.702
RMS_EPS = 1e-5
LN_EPS = 1e-5
N_MOD = 6
N_SPLIT = 3
LOG2_E = 1.4426950408889634
HEADS_PER_STEP = 2

INPROJ_TILE = 512
ATTN_BQ = 4096
ATTN_BK = 512
MERGE_TILE = 512
TOKEN_TILE = 256
EXPERT_BLOCK = 768
CHUNK = SUBLANES
SLOT_ROWS = 1280
GROUP_TABLE = SUBLANES * LANES
DEST_TILES_PER_STEP = 8
assert SLOT_ROWS >= TOKEN_TILE * TOP_K + N_EXPERTS * (CHUNK - 1) and SLOT_ROWS // CHUNK < GROUP_TABLE


def _cparams(*sem):
    return pltpu.CompilerParams(dimension_semantics=sem, vmem_limit_bytes=VMEM_LIMIT_BYTES)


def _resident(shape):
    nd = len(shape)
    return pl.BlockSpec(shape, lambda *_: (0,) * nd, pipeline_mode=pl.Buffered(1))


def _split3(x):
    p1 = x.astype(BF16)
    r1 = x - p1.astype(F32)
    p2 = r1.astype(BF16)
    p3 = (r1 - p2.astype(F32)).astype(BF16)
    return jnp.concatenate([p1, p2, p3], axis=1)


def _rms(x):
    return x * lax.rsqrt(jnp.mean(x * x, axis=-1, keepdims=True) + RMS_EPS)


def _mod_kernel(ct_ref, w_ref, b_ref, o_ref, *, batch):
    rows = []
    for b in range(batch):
        col = ct_ref[:, b:b + 1]
        rows.append(jnp.sum((col * jax.nn.sigmoid(col)) * w_ref[...], axis=0, keepdims=True) + b_ref[...])
    rows.append(jnp.zeros((SUBLANES - batch, rows[0].shape[1]), F32))
    o_ref[...] = jnp.concatenate(rows, axis=0)


def _mod(c, w_ada, b_ada, layer):
    bsz, d = c.shape
    n = w_ada.shape[2]
    tn = 1024
    out = pl.pallas_call(
        functools.partial(_mod_kernel, batch=bsz),
        grid=(n // tn,),
        in_specs=[pl.BlockSpec((d, bsz), lambda j: (0, 0)),
                  pl.BlockSpec((None, d, tn), lambda j: (layer, 0, j)),
                  pl.BlockSpec((1, tn), lambda j: (0, j))],
        out_specs=pl.BlockSpec((SUBLANES, tn), lambda j: (0, j)),
        out_shape=jax.ShapeDtypeStruct((SUBLANES, n), F32),
        compiler_params=_cparams("arbitrary"),
        name="mod",
    )(c.T, w_ada, b_ada[None])
    mod = out[:bsz].reshape(bsz, N_MOD, d)
    return jnp.pad(mod, ((0, 0), (0, SUBLANES - N_MOD), (0, 0)))


def _conv_module(a, first_of_seq, w_ref, b_ref, g_ref, beta_ref, xs_scr, ys_scr):
    ts = a.shape[0]
    xs_scr[0:CONV_HALO, :] = jnp.where(first_of_seq, 0.0, xs_scr[ts:ts + CONV_HALO, :])
    xs_scr[CONV_HALO:, :] = a
    acc = jnp.zeros(a.shape, F32) + b_ref[...]
    base = CONV_HALO - (CONV_KERNEL - 1)
    for rho in range(SUBLANES):
        offs = [o for o in range(base, base + CONV_KERNEL) if o % SUBLANES == rho]
        if not offs:
            continue
        rows = offs[-1] + ts - rho
        ys_scr[0:rows, :] = xs_scr[rho:rho + rows, :]
        for o in offs:
            lo = o - rho
            acc = acc + w_ref[o - base:o - base + 1, :] * ys_scr[lo:lo + ts, :]
    mu = jnp.mean(acc, axis=-1, keepdims=True)
    cen = acc - mu
    var = jnp.mean(cen * cen, axis=-1, keepdims=True)
    y = cen * lax.rsqrt(var + LN_EPS) * g_ref[...] + beta_ref[...]
    return (y * jax.nn.sigmoid(y)).astype(BF16)


def _inproj_kernel(x_ref, mod_ref, g_ref, wa_ref, wqkv_ref, wf_ref, wg_ref, bf_ref,
                   tri_ref, eq_ref, ek_ref, oneq_ref, onek_ref, cw_ref, cb_ref, cg_ref, cbeta_ref,
                   act_ref, q_ref, k_ref, v_ref, aq_ref, ak_ref, tot_ref, sgc_ref, sga_ref,
                   carry_scr, xs_scr, ys_scr, *, conv_width, d_model, tiles_per_seq):
    first_of_seq = pl.program_id(0) % tiles_per_seq == 0

    @pl.when(pl.program_id(0) == 0)
    def _():
        xs_scr[...] = jnp.zeros(xs_scr.shape, F32)
        carry_scr[...] = jnp.zeros(carry_scr.shape, F32)

    x = x_ref[...]
    h = _rms(x) * g_ref[...] * (1.0 + mod_ref[1:2, :]) + mod_ref[0:1, :]
    hb = h.astype(BF16)

    u = jnp.dot(hb, wa_ref[...], preferred_element_type=F32)
    a = u[:, :conv_width] * jax.nn.sigmoid(u[:, conv_width:])
    act_ref[...] = _conv_module(a, first_of_seq, cw_ref, cb_ref, cg_ref, cbeta_ref, xs_scr, ys_scr)

    fl = jnp.dot(hb, wf_ref[...], preferred_element_type=F32) + bf_ref[...]
    lane = lax.broadcasted_iota(I32, fl.shape, 1)
    log_f = -(jnp.maximum(-fl, 0.0) + jnp.log1p(jnp.exp(-jnp.abs(fl))))
    log_f = jnp.where(lane < N_HEADS, log_f, 0.0)
    cs = jnp.dot(tri_ref[...], _split3(log_f), preferred_element_type=F32)
    cum = (cs[:, :LANES] + cs[:, LANES:2 * LANES] + cs[:, 2 * LANES:]) * LOG2_E
    tm = cum.shape[0]

    pieces = []
    for sb in range(tm // ATTN_BK):
        blk = cum[sb * ATTN_BK:(sb + 1) * ATTN_BK, :]
        if sb > 0:
            blk = blk - cum[sb * ATTN_BK - 1:sb * ATTN_BK, :]
        tot_ref[sb] = blk[ATTN_BK - 1:, :]
        pieces.append(blk)
    cum_k = pieces[0] if len(pieces) == 1 else jnp.concatenate(pieces, axis=0)

    first_of_block = pl.program_id(0) % (ATTN_BQ // tm) == 0
    cum_q = cum + jnp.where(first_of_block, 0.0, carry_scr[0:1, :])
    carry_scr[...] = jnp.broadcast_to(cum_q[tm - 1:, :], carry_scr.shape)

    qkv = jnp.dot(hb, wqkv_ref[...], preferred_element_type=F32)
    aw = qkv.shape[1] // 3
    q_ref[...] = (qkv[:, :aw] * (HEAD_DIM ** -0.5 * LOG2_E)).astype(BF16)
    k_ref[...] = qkv[:, aw:2 * aw].astype(BF16)
    v_ref[...] = qkv[:, 2 * aw:].astype(BF16)
    aq_ref[...] = (jnp.dot(_split3(cum_q), eq_ref[...], preferred_element_type=F32) + oneq_ref[...]).astype(BF16)
    ak_ref[...] = (jnp.dot(_split3(cum_k), ek_ref[...], preferred_element_type=F32) + onek_ref[...]).astype(BF16)

    gts = jnp.dot(hb, wg_ref[...], preferred_element_type=F32)
    sgc_ref[...] = jax.nn.sigmoid(gts[:, :d_model]).astype(BF16)
    sga_ref[...] = jax.nn.sigmoid(gts[:, d_model:]).astype(BF16)


def _aug_constants():
    rows = jnp.arange(N_SPLIT * LANES)
    piece, head = rows // LANES, rows % LANES
    cols = jnp.arange(N_HEADS * HEAD_DIM)
    pair, off = cols // LANES, cols % LANES
    chead = 2 * pair + jnp.where(off >= HEAD_DIM, 0, 1)
    pos = off % HEAD_DIM
    valid = (head[:, None] < N_HEADS) & (head[:, None] == chead[None, :])
    eq = jnp.where(valid & (pos[None, :] == piece[:, None]), 1.0, 0.0).astype(BF16)
    ek = jnp.where(valid & (pos[None, :] == piece[:, None] + N_SPLIT), -1.0, 0.0).astype(BF16)
    oneq = jnp.where((pos >= N_SPLIT) & (pos < 2 * N_SPLIT), 1.0, 0.0).astype(F32)[None]
    onek = jnp.where(pos < N_SPLIT, 1.0, 0.0).astype(F32)[None]
    return eq, ek, oneq, onek


def _inproj(x2, mod8, norm_g, w_in, b_forget, conv_w, conv_b, conv_ln_g, conv_ln_b, seq):
    t, d = x2.shape
    tm = INPROJ_TILE
    nsub = tm // ATTN_BK
    cw = d // 2
    aw = N_HEADS * HEAD_DIM
    o0 = 2 * cw
    wa = w_in[:, :o0].astype(BF16)
    o1 = o0 + 3 * aw
    wqkv = w_in[:, o0:o1].astype(BF16)
    wf = jnp.pad(w_in[:, o1:o1 + N_HEADS], ((0, 0), (0, LANES - N_HEADS))).astype(BF16)
    bfp = jnp.pad(b_forget, (0, LANES - N_HEADS))[None]
    wg = w_in[:, o1 + N_HEADS:].astype(BF16)
    r = jnp.arange(tm)
    tri = (r[None, :] <= r[:, None]).astype(BF16)
    eq, ek, oneq, onek = _aug_constants()
    tiles_per_seq = seq // tm
    row = lambda i: (i, 0)
    outs = pl.pallas_call(
        functools.partial(_inproj_kernel, conv_width=cw, d_model=d, tiles_per_seq=tiles_per_seq),
        grid=(t // tm,),
        in_specs=[pl.BlockSpec((tm, d), row),
                  pl.BlockSpec((None, SUBLANES, d), lambda i: (i // tiles_per_seq, 0, 0)),
                  _resident((1, d)),
                  _resident(wa.shape), _resident(wqkv.shape),
                  _resident(wf.shape), _resident(wg.shape), _resident(bfp.shape),
                  _resident(tri.shape), _resident(eq.shape), _resident(ek.shape),
                  _resident(oneq.shape), _resident(onek.shape),
                  _resident(conv_w.shape), _resident((1, cw)), _resident((1, cw)), _resident((1, cw))],
        out_specs=[pl.BlockSpec((tm, cw), row)] + [pl.BlockSpec((tm, aw), row)] * 5 + [
                   pl.BlockSpec((nsub, 1, LANES), lambda i: (i, 0, 0)),
                   pl.BlockSpec((tm, d), row), pl.BlockSpec((tm, d), row)],
        out_shape=[jax.ShapeDtypeStruct((t, cw), BF16)] + [jax.ShapeDtypeStruct((t, aw), BF16)] * 5 + [
                   jax.ShapeDtypeStruct((t // ATTN_BK, 1, LANES), F32),
                   jax.ShapeDtypeStruct((t, d), BF16), jax.ShapeDtypeStruct((t, d), BF16)],
        scratch_shapes=[pltpu.VMEM((SUBLANES, LANES), F32),
                        pltpu.VMEM((CONV_HALO + tm, cw), F32), pltpu.VMEM((CONV_HALO + tm, cw), F32)],
        compiler_params=_cparams("arbitrary"),
        name="inproj",
    )(x2, mod8, norm_g[None], wa, wqkv, wf, wg, bfp, tri, eq, ek, oneq, onek,
      conv_w, conv_b[None], conv_ln_g[None], conv_ln_b[None])
    return outs


def _attn_kernel(q_ref, aq_ref, k_ref, ak_ref, v_ref, tot_ref, o_ref, qop_scr, m_scr, acc_scr):
    bq = q_ref.shape[0]
    bk = ATTN_BK
    ratio = bq // bk
    qi = pl.program_id(2)
    nt = (((1,), (1,)), ((), ()))
    heads = range(HEADS_PER_STEP)
    zero = jnp.zeros((1, LANES), F32)

    def splice(hh, main, extra):
        first = lax.broadcasted_iota(I32, main.shape, 1) < HEAD_DIM
        return jnp.where(first, main, extra) if hh == 0 else jnp.where(first, extra, main)

    for hh in heads:
        qop_scr[hh] = splice(hh, q_ref[...], aq_ref[...])

    def scores(hh, kb, row0):
        start = pl.multiple_of(kb * bk, bk)
        k = splice(hh, k_ref[pl.ds(start, bk), :], ak_ref[pl.ds(start, bk), :])
        v = v_ref[pl.ds(start, bk), :]
        v = splice(hh, v, jnp.ones(v.shape, v.dtype))
        s = lax.dot_general(qop_scr[hh, row0:, :], k, nt, preferred_element_type=F32)
        return s, v

    def update(hh, s, v, delta, row0):
        m_prev = m_scr[hh, row0:, :]
        m_new = jnp.maximum(m_prev, jnp.max(s, axis=1, keepdims=True) + delta)
        shift = m_new - delta
        p = jnp.exp2(s - shift[:, :1])
        alpha = jnp.exp2(m_prev - m_new)
        pv = jnp.dot(p.astype(BF16), v, preferred_element_type=F32)
        acc_scr[hh, row0:, :] = alpha * acc_scr[hh, row0:, :] + pv
        m_scr[hh, row0:, :] = m_new

    for hh in heads:
        delta = zero
        for sb in range(ratio):
            kb = qi * ratio + sb
            row0 = sb * bk
            s, v = scores(hh, kb, row0)
            row = lax.broadcasted_iota(I32, s.shape, 0)
            col = lax.broadcasted_iota(I32, s.shape, 1)
            s = jnp.where(col <= row, s, -jnp.inf)
            if sb == 0:
                m0 = jnp.max(s, axis=1, keepdims=True)
                p = jnp.exp2(s - m0)
                m_scr[hh] = jnp.broadcast_to(m0, m_scr.shape[1:])
                acc_scr[hh] = jnp.dot(p.astype(BF16), v, preferred_element_type=F32)
            else:
                update(hh, s, v, delta, row0)
            delta = delta - tot_ref[hh, pl.ds(kb, 1), :]

    def body(step, deltas):
        kb = qi * ratio - 1 - step
        new_deltas = []
        for hh in heads:
            delta = deltas[hh] + tot_ref[hh, pl.ds(kb, 1), :]
            s, v = scores(hh, kb, 0)
            update(hh, s, v, delta, 0)
            new_deltas.append(delta)
        return tuple(new_deltas)

    lax.fori_loop(0, qi * ratio, body, tuple(zero for _ in heads))
    acc0, acc1 = acc_scr[0], acc_scr[1]
    out0 = acc0 / acc0[:, HEAD_DIM:HEAD_DIM + 1]
    out1 = acc1 / acc1[:, 0:1]
    o_ref[...] = splice(0, out0, out1).astype(BF16)


def _attention(q, aq, k, ak, v, tot, bsz, seq):
    t, aw = q.shape
    bq = ATTN_BQ
    nq = seq // bq
    nk = seq // ATTN_BK
    assert HEADS_PER_STEP * HEAD_DIM == LANES
    tot = tot.reshape(bsz, nk, LANES)[:, :, :N_HEADS].transpose(0, 2, 1)
    tot = jnp.broadcast_to(tot[..., None], (bsz, N_HEADS, nk, LANES))
    qblk = pl.BlockSpec((bq, LANES), lambda b, h, i: (b * nq + i, h))
    kblk = pl.BlockSpec((None, seq, LANES), lambda b, h, i: (b, 0, h), pipeline_mode=pl.Buffered(1))
    seq3 = lambda a: a.reshape(bsz, seq, aw)
    return pl.pallas_call(
        _attn_kernel,
        grid=(bsz, N_HEADS // HEADS_PER_STEP, nq),
        in_specs=[qblk, qblk, kblk, kblk, kblk,
                  pl.BlockSpec((None, HEADS_PER_STEP, nk, LANES), lambda b, h, i: (b, h, 0, 0))],
        out_specs=qblk,
        out_shape=jax.ShapeDtypeStruct((t, aw), BF16),
        scratch_shapes=[pltpu.VMEM((HEADS_PER_STEP, bq, LANES), BF16),
                        pltpu.VMEM((HEADS_PER_STEP, bq, LANES), F32),
                        pltpu.VMEM((HEADS_PER_STEP, bq, LANES), F32)],
        compiler_params=_cparams("arbitrary", "arbitrary", "arbitrary"),
        name="attn",
    )(q, aq, seq3(k), seq3(ak), seq3(v), tot)


def _pack_bf16_pair(lo, hi):
    lo_bits = pltpu.bitcast(lo.astype(BF16).astype(F32), U32)
    hi_bits = pltpu.bitcast(hi.astype(BF16).astype(F32), U32)
    return (hi_bits & jnp.uint32(0xFFFF0000)) | (lo_bits >> 16)


def _unpack_bf16_pair(w):
    lo = pltpu.bitcast(w << 16, F32)
    hi = pltpu.bitcast(w & jnp.uint32(0xFFFF0000), F32)
    return jnp.concatenate([lo, hi], axis=1)


def _expert_onehots(ids):
    lane = lax.broadcasted_iota(I32, ids.shape, 1)
    return lane, [lane == ids[:, kk:kk + 1] for kk in range(TOP_K)]


def _route_tile(ids, gates, carry, tri, utri):
    lane, hots = _expert_onehots(ids)
    multi = jnp.zeros(lane.shape, F32)
    for hot in hots:
        multi = multi + jnp.where(hot, 1.0, 0.0)
    before = jnp.dot(tri, multi.astype(BF16), preferred_element_type=F32)
    units = jnp.ceil(jnp.sum(multi, axis=0, keepdims=True) * (1.0 / CHUNK))
    units8 = jnp.broadcast_to(units, (SUBLANES, LANES)).astype(BF16)
    lstart = jnp.dot(units8, utri, preferred_element_type=F32)[0:1, :] * CHUNK
    csize = units * CHUNK
    slot = jnp.zeros(lane.shape, F32)
    for kk, hot in enumerate(hots):
        sl = jnp.sum(jnp.where(hot, lstart + before, 0.0), axis=1, keepdims=True)
        slot = jnp.where(lane == kk, sl, slot)
    row = lax.broadcasted_iota(I32, (SUBLANES, LANES), 0)
    meta = jnp.where(row == 0, csize, jnp.where(row == 1, lstart, jnp.where(row == 2, carry, 0.0)))
    slott = slot.T[0:SUBLANES, :].astype(I32)
    gatest = gates.T[0:SUBLANES, :]
    return slot.astype(I32), slott, gatest, meta, carry + csize


def _merge_kernel(x_ref, act_ref, attn_ref, sgc_ref, sga_ref, mod_ref, g2_ref, wco_ref, wao_ref, wout_ref,
                  wr_ref, br_ref, tri_ref, utri_ref,
                  x1_ref, h2_ref, slot_ref, slott_ref, gatest_ref, meta_ref, cnt_ref, carry_scr):
    @pl.when(pl.program_id(0) == 0)
    def _():
        carry_scr[...] = jnp.zeros(carry_scr.shape, F32)

    conv_out = jnp.dot(act_ref[...], wco_ref[...], preferred_element_type=F32)
    attn_out = jnp.dot(attn_ref[...], wao_ref[...], preferred_element_type=F32)
    mixed = sgc_ref[...].astype(F32) * conv_out + sga_ref[...].astype(F32) * attn_out
    upd = jnp.dot(mixed.astype(BF16), wout_ref[...], preferred_element_type=F32)
    x1 = x_ref[...] + mod_ref[2:3, :] * upd
    x1_ref[...] = x1
    h2 = _rms(x1) * g2_ref[...] * (1.0 + mod_ref[4:5, :]) + mod_ref[3:4, :]
    h2_hi = h2.astype(BF16)
    h2_ref[...] = h2_hi
    h2_lo = (h2 - h2_hi.astype(F32)).astype(BF16)
    h2_cat = jnp.concatenate([h2_hi, h2_lo, h2_hi], axis=1)
    logits = jnp.dot(h2_cat, wr_ref[...], preferred_element_type=F32) + br_ref[...]
    lane = lax.broadcasted_iota(I32, logits.shape, 1)
    work = jnp.where(lane < N_EXPERTS, logits, -jnp.inf)
    ids = jnp.zeros(logits.shape, I32)
    vals = []
    for kk in range(TOP_K):
        mx = jnp.max(work, axis=1, keepdims=True)
        idx = jnp.min(jnp.where(work == mx, lane, LANES), axis=1, keepdims=True)
        ids = jnp.where(lane == kk, idx, ids)
        vals.append(mx)
        work = jnp.where(lane == idx, -jnp.inf, work)
    exps = [jnp.exp(vv - vals[0]) for vv in vals]
    den = exps[0] + exps[1] + exps[2] + exps[3]
    gates = jnp.zeros(logits.shape, F32)
    for kk in range(TOP_K):
        gates = jnp.where(lane == kk, exps[kk] / den, gates)

    carry = carry_scr[0:1, :]
    tt = TOKEN_TILE
    for sub in range(logits.shape[0] // tt):
        rows = slice(sub * tt, (sub + 1) * tt)
        slot, slott, gatest, meta, carry = _route_tile(ids[rows, :], gates[rows, :], carry,
                                                       tri_ref[...], utri_ref[...])
        slot_ref[rows, :] = slot
        slott_ref[sub * SUBLANES:(sub + 1) * SUBLANES, :] = slott
        gatest_ref[sub * SUBLANES:(sub + 1) * SUBLANES, :] = gatest
        meta_ref[sub] = meta
    carry_scr[...] = jnp.broadcast_to(carry, carry_scr.shape)
    cnt_ref[...] = jnp.broadcast_to(carry, cnt_ref.shape)


def _merge(x2, act, attn, sgc, sga, mod8, norm2_g, w_conv_out, w_attn_out, w_out, w_router, b_router, seq):
    t, d = x2.shape
    tm = MERGE_TILE
    tt = TOKEN_TILE
    sub = tm // tt
    nt = t // tt
    cw = act.shape[1]
    sw = attn.shape[1]
    wco = w_conv_out.astype(BF16)
    wao = w_attn_out.astype(BF16)
    wout = w_out.astype(BF16)
    wr = jnp.pad(w_router, ((0, 0), (0, LANES - N_EXPERTS)))
    wr_hi = wr.astype(BF16)
    wr_lo = (wr - wr_hi.astype(F32)).astype(BF16)
    wr = jnp.concatenate([wr_hi, wr_hi, wr_lo], axis=0)
    br = jnp.pad(b_router, (0, LANES - N_EXPERTS))[None]
    r = jnp.arange(tt)
    tri = (r[None, :] < r[:, None]).astype(BF16)
    e = jnp.arange(LANES)
    utri = (e[:, None] < e[None, :]).astype(BF16)
    tiles_per_seq = seq // tm
    row = lambda i: (i, 0)
    return pl.pallas_call(
        _merge_kernel,
        grid=(t // tm,),
        in_specs=[pl.BlockSpec((tm, d), row), pl.BlockSpec((tm, cw), row), pl.BlockSpec((tm, sw), row),
                  pl.BlockSpec((tm, d), row), pl.BlockSpec((tm, d), row),
                  pl.BlockSpec((None, SUBLANES, d), lambda i: (i // tiles_per_seq, 0, 0)),
                  _resident((1, d)), _resident(wco.shape), _resident(wao.shape), _resident(wout.shape),
                  _resident(wr.shape), _resident(br.shape), _resident(tri.shape), _resident(utri.shape)],
        out_specs=[pl.BlockSpec((tm, d), row), pl.BlockSpec((tm, d), row),
                   pl.BlockSpec((tm, LANES), row),
                   pl.BlockSpec((sub * SUBLANES, tt), row), pl.BlockSpec((sub * SUBLANES, tt), row),
                   pl.BlockSpec((sub, SUBLANES, LANES), lambda i: (i, 0, 0)),
                   pl.BlockSpec((SUBLANES, LANES), lambda i: (0, 0))],
        out_shape=[jax.ShapeDtypeStruct((t, d), F32), jax.ShapeDtypeStruct((t, d), BF16),
                   jax.ShapeDtypeStruct((t, LANES), I32),
                   jax.ShapeDtypeStruct((nt * SUBLANES, tt), I32), jax.ShapeDtypeStruct((nt * SUBLANES, tt), F32),
                   jax.ShapeDtypeStruct((nt, SUBLANES, LANES), F32),
                   jax.ShapeDtypeStruct((SUBLANES, LANES), F32)],
        scratch_shapes=[pltpu.VMEM((SUBLANES, LANES), F32)],
        compiler_params=_cparams("arbitrary"),
        name="merge",
    )(x2, act, attn, sgc, sga, mod8, norm2_g[None], wco, wao, wout, wr, br, tri, utri)


def _dest_kernel(cnt_ref, meta_ref, tri_ref, gd_ref, blk_ref):
    cnt = cnt_ref[...]
    padded = jnp.ceil(cnt * (1.0 / EXPERT_BLOCK)) * EXPERT_BLOCK
    pend = jnp.dot(padded, tri_ref[...], preferred_element_type=F32, precision=HIGHEST)
    pstart = (pend - padded)[0:1, :]

    grow = lax.broadcasted_iota(I32, (SUBLANES, LANES), 0)
    glane = lax.broadcasted_iota(I32, (SUBLANES, LANES), 1)
    gpos = ((grow * LANES + glane) * CHUNK).astype(F32)
    last = jnp.logical_and(grow == SUBLANES - 1, glane == LANES - 1)
    for tile in range(meta_ref.shape[0]):
        meta = meta_ref[tile]
        csize, lstart, seg_off = meta[0:1, :], meta[1:2, :], meta[2:3, :]
        lend = lstart + csize
        base = pstart + seg_off - lstart
        owner = jnp.zeros(gpos.shape, I32)
        for e in range(N_EXPERTS):
            owner = owner + jnp.where(gpos >= lend[0:1, e:e + 1], 1, 0)
        gdst = gpos
        for e in range(N_EXPERTS):
            gdst = gdst + jnp.where(owner == e, base[0:1, e:e + 1], 0.0)
        n_groups = lend[0:1, N_EXPERTS - 1:N_EXPERTS] * (1.0 / CHUNK)
        gd_ref[tile] = jnp.where(last, n_groups, gdst).astype(I32)

    @pl.when(pl.program_id(0) == 0)
    def _():
        shape = blk_ref.shape[1:]
        pos = (lax.broadcasted_iota(I32, shape, 0) * LANES + lax.broadcasted_iota(I32, shape, 1))
        pos = pos.astype(F32) * EXPERT_BLOCK
        be = jnp.zeros(shape, I32)
        for e in range(N_EXPERTS):
            be = be + jnp.where(pos >= pend[0:1, e:e + 1], 1, 0)
        blk_ref[0] = jnp.minimum(be, N_EXPERTS - 1)
        nact = pend[0:1, N_EXPERTS - 1:N_EXPERTS] * (1.0 / EXPERT_BLOCK)
        blk_ref[1] = jnp.broadcast_to(nact, shape).astype(I32)
        blk_ref[2] = jnp.broadcast_to(pend[0:1, :], shape).astype(I32)


def _dest(cnt, meta):
    nt = meta.shape[0]
    per_step = DEST_TILES_PER_STEP
    r = jnp.arange(LANES)
    tri = (r[:, None] <= r[None, :]).astype(F32)
    return pl.pallas_call(
        _dest_kernel,
        grid=(nt // per_step,),
        in_specs=[pl.BlockSpec((SUBLANES, LANES), lambda i: (0, 0)),
                  pl.BlockSpec((per_step, SUBLANES, LANES), lambda i: (i, 0, 0)),
                  pl.BlockSpec((LANES, LANES), lambda i: (0, 0))],
        out_specs=[pl.BlockSpec((per_step, SUBLANES, LANES), lambda i: (i, 0, 0)),
                   pl.BlockSpec((3, SUBLANES, LANES), lambda i: (0, 0, 0))],
        out_shape=[jax.ShapeDtypeStruct((nt, SUBLANES, LANES), I32),
                   jax.ShapeDtypeStruct((3, SUBLANES, LANES), I32)],
        compiler_params=_cparams("arbitrary"),
        name="dest",
    )(cnt, meta, tri)


def _row_copy(src, dst, sem):
    return pltpu.make_async_copy(src, dst, sem)


def _dispatch_kernel(pend_ref, na_ref, gd_ref, slott_ref, gatest_ref, h2_ref, xs_ref, zero_ref, sorted_scr, sem):
    tm = h2_ref.shape[0]
    bm = zero_ref.shape[0]
    nb = xs_ref.shape[0] // bm

    @pl.when(pl.program_id(0) == 0)
    def _():
        zero_ref[...] = jnp.zeros(zero_ref.shape, zero_ref.dtype)

        def zero_block(start):
            return _row_copy(zero_ref, xs_ref.at[pl.ds(pl.multiple_of(start, bm), bm)], sem)

        def nonempty(e):
            return pend_ref[e] > (pend_ref[e - 1] if e > 0 else 0)

        for e in range(N_EXPERTS):
            @pl.when(nonempty(e))
            def _():
                zero_block(pend_ref[e] - bm).start()

        def tail(b, carry):
            zero_block(b * bm).start()
            return carry

        lax.fori_loop(na_ref[0], nb, tail, 0)
        for e in range(N_EXPERTS):
            @pl.when(nonempty(e))
            def _():
                zero_block(0).wait()

        def tail_wait(b, carry):
            zero_block(0).wait()
            return carry

        lax.fori_loop(na_ref[0], nb, tail_wait, 0)

    srow = lax.broadcasted_iota(I32, (SLOT_ROWS, tm), 0)
    perm = jnp.zeros((SLOT_ROWS, tm), F32)
    gate = jnp.zeros((SLOT_ROWS, 1), F32)
    for kk in range(TOP_K):
        hit = srow == slott_ref[kk:kk + 1, :]
        perm = perm + jnp.where(hit, 1.0, 0.0)
        gate = gate + jnp.sum(jnp.where(hit, gatest_ref[kk:kk + 1, :], 0.0), axis=1, keepdims=True)
    perm = perm.astype(BF16)
    half = h2_ref.shape[1] // 2
    lo = jnp.dot(perm, h2_ref[:, :half], preferred_element_type=F32)
    hi = jnp.dot(perm, h2_ref[:, half:], preferred_element_type=F32)
    sorted_scr[:, :half] = _pack_bf16_pair(lo, hi)
    sorted_scr[:, half:] = pltpu.bitcast(jnp.broadcast_to(gate, (SLOT_ROWS, LANES)), U32)

    n_groups = gd_ref[GROUP_TABLE - 1]

    def group_copy(j, d):
        src = sorted_scr.at[pl.ds(pl.multiple_of(j * CHUNK, CHUNK), CHUNK)]
        return _row_copy(src, xs_ref.at[pl.ds(pl.multiple_of(d, CHUNK), CHUNK)], sem)

    def issue(j, carry):
        group_copy(j, gd_ref[j]).start()
        return carry

    def retire(j, carry):
        group_copy(0, 0).wait()
        return carry

    lax.fori_loop(0, n_groups, issue, 0)
    lax.fori_loop(0, n_groups, retire, 0)


def _dispatch(pend, n_active, gd_flat, slott, gatest, h2b, n_rows):
    t, d = h2b.shape
    w = d // 2 + LANES
    tm = TOKEN_TILE
    grid_spec = pltpu.PrefetchScalarGridSpec(
        num_scalar_prefetch=2,
        grid=(t // tm,),
        in_specs=[pl.BlockSpec((GROUP_TABLE,), lambda i, pe, na: (i,), memory_space=pltpu.SMEM),
                  pl.BlockSpec((SUBLANES, tm), lambda i, pe, na: (i, 0)),
                  pl.BlockSpec((SUBLANES, tm), lambda i, pe, na: (i, 0)),
                  pl.BlockSpec((tm, d), lambda i, pe, na: (i, 0))],
        out_specs=pl.BlockSpec(memory_space=pl.ANY),
        scratch_shapes=[pltpu.VMEM((EXPERT_BLOCK, w), U32), pltpu.VMEM((SLOT_ROWS, w), U32),
                        pltpu.SemaphoreType.DMA(())],
    )
    return pl.pallas_call(
        _dispatch_kernel,
        grid_spec=grid_spec,
        out_shape=jax.ShapeDtypeStruct((n_rows, w), U32),
        compiler_params=_cparams("arbitrary"),
        name="dispatch",
    )(pend, n_active, gd_flat, slott, gatest, h2b)


def _expert_kernel(be_ref, na_ref, xs_ref, w1_ref, b1_ref, w2_ref, b2_ref, y_ref, w1b_ref, w2b_ref, *, d_expert):
    i = pl.program_id(0)
    active = i < na_ref[0]
    prev = be_ref[jnp.maximum(i - 1, 0)]
    new_expert = jnp.logical_or(i == 0, be_ref[i] != prev)
    half = y_ref.shape[1]

    @pl.when(jnp.logical_and(active, new_expert))
    def _():
        w1b_ref[...] = w1_ref[...].astype(BF16)
        w2b_ref[...] = w2_ref[...].astype(BF16)

    @pl.when(active)
    def _():
        x = _unpack_bf16_pair(xs_ref[:, :half]).astype(BF16)
        gate = pltpu.bitcast(xs_ref[:, half:], F32)[:, 0:1]
        hc = jnp.dot(x, w1b_ref[...], preferred_element_type=F32) + b1_ref[...]
        g = jnp.minimum(hc[:, :d_expert], SWIGLU_LIMIT)
        u = jnp.clip(hc[:, d_expert:], -SWIGLU_LIMIT, SWIGLU_LIMIT)
        hm = (u + 1.0) * (g * jax.nn.sigmoid(SWIGLU_ALPHA * g))
        y = (jnp.dot(hm.astype(BF16), w2b_ref[...], preferred_element_type=F32) + b2_ref[...]) * gate
        y_ref[...] = _pack_bf16_pair(y[:, :half], y[:, half:])

    @pl.when(jnp.logical_not(active))
    def _():
        y_ref[...] = jnp.zeros(y_ref.shape, U32)


def _experts(block_e, n_active, xs, w1, b1, w2, b2):
    n_rows, w = xs.shape
    n_exp, d, two_de = w1.shape
    de = two_de // 2
    bm = EXPERT_BLOCK
    nb = n_rows // bm

    def blk(i, be, na):
        return jnp.maximum(jnp.minimum(i, na[0] - 1), 0)

    grid_spec = pltpu.PrefetchScalarGridSpec(
        num_scalar_prefetch=2,
        grid=(nb,),
        in_specs=[pl.BlockSpec((bm, w), lambda i, be, na: (blk(i, be, na), 0)),
                  pl.BlockSpec((None, d, two_de), lambda i, be, na: (be[blk(i, be, na)], 0, 0)),
                  pl.BlockSpec((None, 1, two_de), lambda i, be, na: (be[blk(i, be, na)], 0, 0)),
                  pl.BlockSpec((None, de, d), lambda i, be, na: (be[blk(i, be, na)], 0, 0)),
                  pl.BlockSpec((None, 1, d), lambda i, be, na: (be[blk(i, be, na)], 0, 0))],
        out_specs=pl.BlockSpec((bm, d // 2), lambda i, be, na: (i, 0)),
        scratch_shapes=[pltpu.VMEM((d, two_de), BF16), pltpu.VMEM((de, d), BF16)],
    )
    return pl.pallas_call(
        functools.partial(_expert_kernel, d_expert=de),
        grid_spec=grid_spec,
        out_shape=jax.ShapeDtypeStruct((n_rows, d // 2), U32),
        compiler_params=_cparams("arbitrary"),
        name="expert",
    )(block_e, n_active, xs, w1, b1[:, None, :], w2, b2[:, None, :])


def _combine_kernel(gd_ref, gd_next_ref, slot_ref, x1_ref, mod_ref, fg_ref, yb_ref, o_ref, sorted_scr, sem):
    tm = x1_ref.shape[0]
    step = pl.program_id(0)
    cur = step % 2

    def group_copy(buf, j, d):
        dst = sorted_scr.at[buf, pl.ds(pl.multiple_of(j * CHUNK, CHUNK), CHUNK)]
        return _row_copy(yb_ref.at[pl.ds(pl.multiple_of(d, CHUNK), CHUNK)], dst, sem.at[buf])

    def fetch(table_ref, buf):
        def issue(j, carry):
            group_copy(buf, j, table_ref[j]).start()
            return carry
        lax.fori_loop(0, table_ref[GROUP_TABLE - 1], issue, 0)

    @pl.when(step == 0)
    def _():
        sorted_scr[...] = jnp.zeros(sorted_scr.shape, sorted_scr.dtype)
        fetch(gd_ref, 0)

    @pl.when(step + 1 < pl.num_programs(0))
    def _():
        fetch(gd_next_ref, 1 - cur)

    def retire(j, carry):
        group_copy(cur, 0, 0).wait()
        return carry

    lax.fori_loop(0, gd_ref[GROUP_TABLE - 1], retire, 0)

    slot = slot_ref[...]
    scol = lax.broadcasted_iota(I32, (tm, SLOT_ROWS), 1)
    pick = jnp.zeros((tm, SLOT_ROWS), F32)
    for kk in range(TOP_K):
        pick = pick + jnp.where(scol == slot[:, kk:kk + 1], 1.0, 0.0)
    pick = pick.astype(BF16)
    words = sorted_scr[cur]
    lo = pltpu.bitcast(words << 16, F32).astype(BF16)
    hi = pltpu.bitcast(words & jnp.uint32(0xFFFF0000), F32).astype(BF16)
    y = jnp.concatenate([jnp.dot(pick, lo, preferred_element_type=F32),
                         jnp.dot(pick, hi, preferred_element_type=F32)], axis=1)
    x2 = x1_ref[...] + mod_ref[5:6, :] * y
    o_ref[...] = _rms(x2) * fg_ref[...]


def _combine(gd_flat, slot, x1, mod8, final_g, yb, seq):
    t, d = x1.shape
    tm = TOKEN_TILE
    nt = t // tm
    tiles_per_seq = seq // tm
    return pl.pallas_call(
        _combine_kernel,
        grid=(nt,),
        in_specs=[pl.BlockSpec((GROUP_TABLE,), lambda i: (i,), memory_space=pltpu.SMEM),
                  pl.BlockSpec((GROUP_TABLE,), lambda i: (jnp.minimum(i + 1, nt - 1),), memory_space=pltpu.SMEM),
                  pl.BlockSpec((tm, LANES), lambda i: (i, 0)),
                  pl.BlockSpec((tm, d), lambda i: (i, 0)),
                  pl.BlockSpec((None, SUBLANES, d), lambda i: (i // tiles_per_seq, 0, 0)),
                  pl.BlockSpec((1, d), lambda i: (0, 0)),
                  pl.BlockSpec(memory_space=pl.ANY)],
        out_specs=pl.BlockSpec((tm, d), lambda i: (i, 0)),
        out_shape=jax.ShapeDtypeStruct((t, d), F32),
        scratch_shapes=[pltpu.VMEM((2, SLOT_ROWS, d // 2), U32), pltpu.SemaphoreType.DMA((2,))],
        compiler_params=_cparams("arbitrary"),
        name="combine",
    )(gd_flat, gd_flat, slot, x1, mod8, final_g[None], yb)


def kernel(x, c, w_ada, b_ada, norm1_g, w_in, b_forget, conv_w, conv_b, conv_ln_g, conv_ln_b, w_conv_out,
           w_attn_out, w_out, norm2_g, w_router, b_router, w_exp_in, b_exp_in, w_exp_out, b_exp_out, final_g):
    bsz, seq, d = x.shape
    t = bsz * seq
    depth = w_ada.shape[0]
    xf = x.reshape(t, d)
    n_rows = t * TOP_K + (t // TOKEN_TILE) * N_EXPERTS * (CHUNK - 1) + N_EXPERTS * (EXPERT_BLOCK - CHUNK)
    n_rows = -(-n_rows // EXPERT_BLOCK) * EXPERT_BLOCK
    n_blocks = n_rows // EXPERT_BLOCK
    for l in range(depth):
        mod8 = _mod(c, w_ada, b_ada[l], l)
        act, q, k, v, aq, ak, tot, sgc, sga = _inproj(xf, mod8, norm1_g[l], w_in[l], b_forget[l], conv_w[l],
                                                      conv_b[l], conv_ln_g[l], conv_ln_b[l], seq)
        attn = _attention(q, aq, k, ak, v, tot, bsz, seq)
        x1, h2b, slot, slott, gatest, meta, cnt = _merge(xf, act, attn, sgc, sga, mod8, norm2_g[l], w_conv_out[l],
                                                         w_attn_out[l], w_out[l], w_router[l], b_router[l], seq)
        gd, blk = _dest(cnt, meta)
        gd_flat = gd.reshape(-1)
        block_e = blk[0].reshape(-1)[:n_blocks]
        n_active = blk[1, 0, :1]
        pend = blk[2, 0, :N_EXPERTS]
        xs = _dispatch(pend, n_active, gd_flat, slott, gatest, h2b, n_rows)
        yb = _experts(block_e, n_active, xs, w_exp_in[l], b_exp_in[l], w_exp_out[l], b_exp_out[l])
        assert depth == 1
        xf = _combine(gd_flat, slot, x1, mod8, final_g, yb, seq)
    return xf.reshape(bsz, seq, d)
```

```python
import functools

import jax
import jax.numpy as jnp
from jax import lax
from jax.experimental import pallas as pl
from jax.experimental.pallas import tpu as pltpu

F32 = jnp.float32
BF16 = jnp.bfloat16
I32 = jnp.int32
U32 = jnp.uint32
HIGHEST = lax.Precision.HIGHEST

LANES = 128
SUBLANES = 8
VMEM_LIMIT_BYTES = 56 * 1024 * 1024

N_HEADS = 8
HEAD_DIM = 64
CONV_KERNEL = 31
CONV_HALO = 32
N_EXPERTS = 32
TOP_K = 4
SWIGLU_LIMIT = 7.0
SWIGLU_ALPHA = 1.702
RMS_EPS = 1e-5
LN_EPS = 1e-5
N_MOD = 6
N_SPLIT = 3
LOG2_E = 1.4426950408889634
HEADS_PER_STEP = 2

INPROJ_TILE = 512
ATTN_BQ = 4096
ATTN_BK = 512
MERGE_TILE = 1024
TOKEN_TILE = 256
EXPERT_BLOCK = 768
CHUNK = SUBLANES
SLOT_ROWS = 1280
GROUP_TABLE = SUBLANES * LANES
DEST_TILES_PER_STEP = 8
assert SLOT_ROWS >= TOKEN_TILE * TOP_K + N_EXPERTS * (CHUNK - 1) and SLOT_ROWS // CHUNK < GROUP_TABLE


def _cparams(*sem):
    return pltpu.CompilerParams(dimension_semantics=sem, vmem_limit_bytes=VMEM_LIMIT_BYTES)


def _resident(shape):
    nd = len(shape)
    return pl.BlockSpec(shape, lambda *_: (0,) * nd, pipeline_mode=pl.Buffered(1))


def _split3(x):
    p1 = x.astype(BF16)
    r1 = x - p1.astype(F32)
    p2 = r1.astype(BF16)
    p3 = (r1 - p2.astype(F32)).astype(BF16)
    return jnp.concatenate([p1, p2, p3], axis=1)


def _rms(x):
    return x * lax.rsqrt(jnp.mean(x * x, axis=-1, keepdims=True) + RMS_EPS)


def _mod_kernel(ct_ref, w_ref, b_ref, o_ref, *, batch):
    rows = []
    for b in range(batch):
        col = ct_ref[:, b:b + 1]
        rows.append(jnp.sum((col * jax.nn.sigmoid(col)) * w_ref[...], axis=0, keepdims=True) + b_ref[...])
    rows.append(jnp.zeros((SUBLANES - batch, rows[0].shape[1]), F32))
    o_ref[...] = jnp.concatenate(rows, axis=0)


def _mod(c, w_ada, b_ada, layer):
    bsz, d = c.shape
    n = w_ada.shape[2]
    tn = 1024
    out = pl.pallas_call(
        functools.partial(_mod_kernel, batch=bsz),
        grid=(n // tn,),
        in_specs=[pl.BlockSpec((d, bsz), lambda j: (0, 0)),
                  pl.BlockSpec((None, d, tn), lambda j: (layer, 0, j)),
                  pl.BlockSpec((1, tn), lambda j: (0, j))],
        out_specs=pl.BlockSpec((SUBLANES, tn), lambda j: (0, j)),
        out_shape=jax.ShapeDtypeStruct((SUBLANES, n), F32),
        compiler_params=_cparams("arbitrary"),
        name="mod",
    )(c.T, w_ada, b_ada[None])
    mod = out[:bsz].reshape(bsz, N_MOD, d)
    return jnp.pad(mod, ((0, 0), (0, SUBLANES - N_MOD), (0, 0)))


def _conv_module(a, first_of_seq, w_ref, b_ref, g_ref, beta_ref, xs_scr, ys_scr):
    ts = a.shape[0]
    xs_scr[0:CONV_HALO, :] = jnp.where(first_of_seq, 0.0, xs_scr[ts:ts + CONV_HALO, :])
    xs_scr[CONV_HALO:, :] = a
    acc = jnp.zeros(a.shape, F32) + b_ref[...]
    base = CONV_HALO - (CONV_KERNEL - 1)
    for rho in range(SUBLANES):
        offs = [o for o in range(base, base + CONV_KERNEL) if o % SUBLANES == rho]
        if not offs:
            continue
        rows = offs[-1] + ts - rho
        ys_scr[0:rows, :] = xs_scr[rho:rho + rows, :]
        for o in offs:
            lo = o - rho
            acc = acc + w_ref[o - base:o - base + 1, :] * ys_scr[lo:lo + ts, :]
    mu = jnp.mean(acc, axis=-1, keepdims=True)
    cen = acc - mu
    var = jnp.mean(cen * cen, axis=-1, keepdims=True)
    y = cen * lax.rsqrt(var + LN_EPS) * g_ref[...] + beta_ref[...]
    return (y * jax.nn.sigmoid(y)).astype(BF16)


def _inproj_kernel(x_ref, mod_ref, g_ref, wa_ref, wqkv_ref, wf_ref, wg_ref, bf_ref,
                   tri_ref, eq_ref, ek_ref, oneq_ref, onek_ref, cw_ref, cb_ref, cg_ref, cbeta_ref,
                   act_ref, q_ref, k_ref, v_ref, aq_ref, ak_ref, tot_ref, sgc_ref, sga_ref,
                   carry_scr, xs_scr, ys_scr, *, conv_width, d_model, tiles_per_seq):
    first_of_seq = pl.program_id(0) % tiles_per_seq == 0

    @pl.when(pl.program_id(0) == 0)
    def _():
        xs_scr[...] = jnp.zeros(xs_scr.shape, F32)
        carry_scr[...] = jnp.zeros(carry_scr.shape, F32)

    x = x_ref[...]
    h = _rms(x) * g_ref[...] * (1.0 + mod_ref[1:2, :]) + mod_ref[0:1, :]
    hb = h.astype(BF16)

    u = jnp.dot(hb, wa_ref[...], preferred_element_type=F32)
    a = u[:, :conv_width] * jax.nn.sigmoid(u[:, conv_width:])
    act_ref[...] = _conv_module(a, first_of_seq, cw_ref, cb_ref, cg_ref, cbeta_ref, xs_scr, ys_scr)

    fl = jnp.dot(hb, wf_ref[...], preferred_element_type=F32) + bf_ref[...]
    lane = lax.broadcasted_iota(I32, fl.shape, 1)
    log_f = -(jnp.maximum(-fl, 0.0) + jnp.log1p(jnp.exp(-jnp.abs(fl))))
    log_f = jnp.where(lane < N_HEADS, log_f, 0.0)
    cs = jnp.dot(tri_ref[...], _split3(log_f), preferred_element_type=F32)
    cum = (cs[:, :LANES] + cs[:, LANES:2 * LANES] + cs[:, 2 * LANES:]) * LOG2_E
    tm = cum.shape[0]

    pieces = []
    for sb in range(tm // ATTN_BK):
        blk = cum[sb * ATTN_BK:(sb + 1) * ATTN_BK, :]
        if sb > 0:
            blk = blk - cum[sb * ATTN_BK - 1:sb * ATTN_BK, :]
        tot_ref[sb] = blk[ATTN_BK - 1:, :]
        pieces.append(blk)
    cum_k = pieces[0] if len(pieces) == 1 else jnp.concatenate(pieces, axis=0)

    first_of_block = pl.program_id(0) % (ATTN_BQ // tm) == 0
    cum_q = cum + jnp.where(first_of_block, 0.0, carry_scr[0:1, :])
    carry_scr[...] = jnp.broadcast_to(cum_q[tm - 1:, :], carry_scr.shape)

    qkv = jnp.dot(hb, wqkv_ref[...], preferred_element_type=F32)
    aw = qkv.shape[1] // 3
    q_ref[...] = (qkv[:, :aw] * (HEAD_DIM ** -0.5 * LOG2_E)).astype(BF16)
    k_ref[...] = qkv[:, aw:2 * aw].astype(BF16)
    v_ref[...] = qkv[:, 2 * aw:].astype(BF16)
    aq_ref[...] = (jnp.dot(_split3(cum_q), eq_ref[...], preferred_element_type=F32) + oneq_ref[...]).astype(BF16)
    ak_ref[...] = (jnp.dot(_split3(cum_k), ek_ref[...], preferred_element_type=F32) + onek_ref[...]).astype(BF16)

    gts = jnp.dot(hb, wg_ref[...], preferred_element_type=F32)
    sgc_ref[...] = jax.nn.sigmoid(gts[:, :d_model]).astype(BF16)
    sga_ref[...] = jax.nn.sigmoid(gts[:, d_model:]).astype(BF16)


def _aug_constants():
    rows = jnp.arange(N_SPLIT * LANES)
    piece, head = rows // LANES, rows % LANES
    cols = jnp.arange(N_HEADS * HEAD_DIM)
    pair, off = cols // LANES, cols % LANES
    chead = 2 * pair + jnp.where(off >= HEAD_DIM, 0, 1)
    pos = off % HEAD_DIM
    valid = (head[:, None] < N_HEADS) & (head[:, None] == chead[None, :])
    eq = jnp.where(valid & (pos[None, :] == piece[:, None]), 1.0, 0.0).astype(BF16)
    ek = jnp.where(valid & (pos[None, :] == piece[:, None] + N_SPLIT), -1.0, 0.0).astype(BF16)
    oneq = jnp.where((pos >= N_SPLIT) & (pos < 2 * N_SPLIT), 1.0, 0.0).astype(F32)[None]
    onek = jnp.where(pos < N_SPLIT, 1.0, 0.0).astype(F32)[None]
    return eq, ek, oneq, onek


def _inproj(x2, mod8, norm_g, w_in, b_forget, conv_w, conv_b, conv_ln_g, conv_ln_b, seq):
    t, d = x2.shape
    tm = INPROJ_TILE
    nsub = tm // ATTN_BK
    cw = d // 2
    aw = N_HEADS * HEAD_DIM
    o0 = 2 * cw
    wa = w_in[:, :o0].astype(BF16)
    o1 = o0 + 3 * aw
    wqkv = w_in[:, o0:o1].astype(BF16)
    wf = jnp.pad(w_in[:, o1:o1 + N_HEADS], ((0, 0), (0, LANES - N_HEADS))).astype(BF16)
    bfp = jnp.pad(b_forget, (0, LANES - N_HEADS))[None]
    wg = w_in[:, o1 + N_HEADS:].astype(BF16)
    r = jnp.arange(tm)
    tri = (r[None, :] <= r[:, None]).astype(BF16)
    eq, ek, oneq, onek = _aug_constants()
    tiles_per_seq = seq // tm
    row = lambda i: (i, 0)
    outs = pl.pallas_call(
        functools.partial(_inproj_kernel, conv_width=cw, d_model=d, tiles_per_seq=tiles_per_seq),
        grid=(t // tm,),
        in_specs=[pl.BlockSpec((tm, d), row),
                  pl.BlockSpec((None, SUBLANES, d), lambda i: (i // tiles_per_seq, 0, 0)),
                  _resident((1, d)),
                  _resident(wa.shape), _resident(wqkv.shape),
                  _resident(wf.shape), _resident(wg.shape), _resident(bfp.shape),
                  _resident(tri.shape), _resident(eq.shape), _resident(ek.shape),
                  _resident(oneq.shape), _resident(onek.shape),
                  _resident(conv_w.shape), _resident((1, cw)), _resident((1, cw)), _resident((1, cw))],
        out_specs=[pl.BlockSpec((tm, cw), row)] + [pl.BlockSpec((tm, aw), row)] * 5 + [
                   pl.BlockSpec((nsub, 1, LANES), lambda i: (i, 0, 0)),
                   pl.BlockSpec((tm, d), row), pl.BlockSpec((tm, d), row)],
        out_shape=[jax.ShapeDtypeStruct((t, cw), BF16)] + [jax.ShapeDtypeStruct((t, aw), BF16)] * 5 + [
                   jax.ShapeDtypeStruct((t // ATTN_BK, 1, LANES), F32),
                   jax.ShapeDtypeStruct((t, d), BF16), jax.ShapeDtypeStruct((t, d), BF16)],
        scratch_shapes=[pltpu.VMEM((SUBLANES, LANES), F32),
                        pltpu.VMEM((CONV_HALO + tm, cw), F32), pltpu.VMEM((CONV_HALO + tm, cw), F32)],
        compiler_params=_cparams("arbitrary"),
        name="inproj",
    )(x2, mod8, norm_g[None], wa, wqkv, wf, wg, bfp, tri, eq, ek, oneq, onek,
      conv_w, conv_b[None], conv_ln_g[None], conv_ln_b[None])
    return outs


def _attn_kernel(q_ref, aq_ref, k_ref, ak_ref, v_ref, tot_ref, o_ref, qop_scr, m_scr, acc_scr):
    bq = q_ref.shape[0]
    bk = ATTN_BK
    ratio = bq // bk
    qi = pl.program_id(2)
    nt = (((1,), (1,)), ((), ()))
    heads = range(HEADS_PER_STEP)
    zero = jnp.zeros((1, LANES), F32)

    def splice(hh, main, extra):
        first = lax.broadcasted_iota(I32, main.shape, 1) < HEAD_DIM
        return jnp.where(first, main, extra) if hh == 0 else jnp.where(first, extra, main)

    for hh in heads:
        qop_scr[hh] = splice(hh, q_ref[...], aq_ref[...])

    def scores(hh, kb, row0):
        start = pl.multiple_of(kb * bk, bk)
        k = splice(hh, k_ref[pl.ds(start, bk), :], ak_ref[pl.ds(start, bk), :])
        v = v_ref[pl.ds(start, bk), :]
        v = splice(hh, v, jnp.ones(v.shape, v.dtype))
        s = lax.dot_general(qop_scr[hh, row0:, :], k, nt, preferred_element_type=F32)
        return s, v

    def update(hh, s, v, delta, row0):
        m_prev = m_scr[hh, row0:, :]
        m_new = jnp.maximum(m_prev, jnp.max(s, axis=1, keepdims=True) + delta)
        shift = m_new - delta
        p = jnp.exp2(s - shift[:, :1])
        alpha = jnp.exp2(m_prev - m_new)
        pv = jnp.dot(p.astype(BF16), v, preferred_element_type=F32)
        acc_scr[hh, row0:, :] = alpha * acc_scr[hh, row0:, :] + pv
        m_scr[hh, row0:, :] = m_new

    for hh in heads:
        delta = zero
        for sb in range(ratio):
            kb = qi * ratio + sb
            row0 = sb * bk
            s, v = scores(hh, kb, row0)
            row = lax.broadcasted_iota(I32, s.shape, 0)
            col = lax.broadcasted_iota(I32, s.shape, 1)
            s = jnp.where(col <= row, s, -jnp.inf)
            if sb == 0:
                m0 = jnp.max(s, axis=1, keepdims=True)
                p = jnp.exp2(s - m0)
                m_scr[hh] = jnp.broadcast_to(m0, m_scr.shape[1:])
                acc_scr[hh] = jnp.dot(p.astype(BF16), v, preferred_element_type=F32)
            else:
                update(hh, s, v, delta, row0)
            delta = delta - tot_ref[hh, pl.ds(kb, 1), :]

    def body(step, deltas):
        kb = qi * ratio - 1 - step
        new_deltas = []
        for hh in heads:
            delta = deltas[hh] + tot_ref[hh, pl.ds(kb, 1), :]
            s, v = scores(hh, kb, 0)
            update(hh, s, v, delta, 0)
            new_deltas.append(delta)
        return tuple(new_deltas)

    lax.fori_loop(0, qi * ratio, body, tuple(zero for _ in heads))
    acc0, acc1 = acc_scr[0], acc_scr[1]
    out0 = acc0 / acc0[:, HEAD_DIM:HEAD_DIM + 1]
    out1 = acc1 / acc1[:, 0:1]
    o_ref[...] = splice(0, out0, out1).astype(BF16)


def _attention(q, aq, k, ak, v, tot, bsz, seq):
    t, aw = q.shape
    bq = ATTN_BQ
    nq = seq // bq
    nk = seq // ATTN_BK
    assert HEADS_PER_STEP * HEAD_DIM == LANES
    tot = tot.reshape(bsz, nk, LANES)[:, :, :N_HEADS].transpose(0, 2, 1)
    tot = jnp.broadcast_to(tot[..., None], (bsz, N_HEADS, nk, LANES))
    qblk = pl.BlockSpec((bq, LANES), lambda b, h, i: (b * nq + i, h))
    kblk = pl.BlockSpec((None, seq, LANES), lambda b, h, i: (b, 0, h), pipeline_mode=pl.Buffered(1))
    seq3 = lambda a: a.reshape(bsz, seq, aw)
    return pl.pallas_call(
        _attn_kernel,
        grid=(bsz, N_HEADS // HEADS_PER_STEP, nq),
        in_specs=[qblk, qblk, kblk, kblk, kblk,
                  pl.BlockSpec((None, HEADS_PER_STEP, nk, LANES), lambda b, h, i: (b, h, 0, 0))],
        out_specs=qblk,
        out_shape=jax.ShapeDtypeStruct((t, aw), BF16),
        scratch_shapes=[pltpu.VMEM((HEADS_PER_STEP, bq, LANES), BF16),
                        pltpu.VMEM((HEADS_PER_STEP, bq, LANES), F32),
                        pltpu.VMEM((HEADS_PER_STEP, bq, LANES), F32)],
        compiler_params=_cparams("arbitrary", "arbitrary", "arbitrary"),
        name="attn",
    )(q, aq, seq3(k), seq3(ak), seq3(v), tot)


def _pack_bf16_pair(lo, hi):
    lo_bits = pltpu.bitcast(lo.astype(BF16).astype(F32), U32)
    hi_bits = pltpu.bitcast(hi.astype(BF16).astype(F32), U32)
    return (hi_bits & jnp.uint32(0xFFFF0000)) | (lo_bits >> 16)


def _unpack_bf16_pair(w):
    lo = pltpu.bitcast(w << 16, F32)
    hi = pltpu.bitcast(w & jnp.uint32(0xFFFF0000), F32)
    return jnp.concatenate([lo, hi], axis=1)


def _expert_onehots(ids):
    lane = lax.broadcasted_iota(I32, ids.shape, 1)
    return lane, [lane == ids[:, kk:kk + 1] for kk in range(TOP_K)]


def _route_tile(ids, gates, carry, tri, utri):
    lane, hots = _expert_onehots(ids)
    multi = jnp.zeros(lane.shape, F32)
    for hot in hots:
        multi = multi + jnp.where(hot, 1.0, 0.0)
    before = jnp.dot(tri, multi.astype(BF16), preferred_element_type=F32)
    units = jnp.ceil(jnp.sum(multi, axis=0, keepdims=True) * (1.0 / CHUNK))
    units8 = jnp.broadcast_to(units, (SUBLANES, LANES)).astype(BF16)
    lstart = jnp.dot(units8, utri, preferred_element_type=F32)[0:1, :] * CHUNK
    csize = units * CHUNK
    slot = jnp.zeros(lane.shape, F32)
    for kk, hot in enumerate(hots):
        sl = jnp.sum(jnp.where(hot, lstart + before, 0.0), axis=1, keepdims=True)
        slot = jnp.where(lane == kk, sl, slot)
    row = lax.broadcasted_iota(I32, (SUBLANES, LANES), 0)
    meta = jnp.where(row == 0, csize, jnp.where(row == 1, lstart, jnp.where(row == 2, carry, 0.0)))
    slott = slot.T[0:SUBLANES, :].astype(I32)
    gatest = gates.T[0:SUBLANES, :]
    return slot.astype(I32), slott, gatest, meta, carry + csize


def _merge_kernel(x_ref, act_ref, attn_ref, sgc_ref, sga_ref, mod_ref, g2_ref, wco_ref, wao_ref, wout_ref,
                  wr_ref, br_ref, tri_ref, utri_ref,
                  x1_ref, h2_ref, slot_ref, slott_ref, gatest_ref, meta_ref, cnt_ref, carry_scr):
    @pl.when(pl.program_id(0) == 0)
    def _():
        carry_scr[...] = jnp.zeros(carry_scr.shape, F32)

    conv_out = jnp.dot(act_ref[...], wco_ref[...], preferred_element_type=F32)
    attn_out = jnp.dot(attn_ref[...], wao_ref[...], preferred_element_type=F32)
    mixed = sgc_ref[...].astype(F32) * conv_out + sga_ref[...].astype(F32) * attn_out
    upd = jnp.dot(mixed.astype(BF16), wout_ref[...], preferred_element_type=F32)
    x1 = x_ref[...] + mod_ref[2:3, :] * upd
    x1_ref[...] = x1
    h2 = _rms(x1) * g2_ref[...] * (1.0 + mod_ref[4:5, :]) + mod_ref[3:4, :]
    h2_hi = h2.astype(BF16)
    h2_ref[...] = h2_hi
    h2_lo = (h2 - h2_hi.astype(F32)).astype(BF16)
    h2_cat = jnp.concatenate([h2_hi, h2_lo, h2_hi], axis=1)
    logits = jnp.dot(h2_cat, wr_ref[...], preferred_element_type=F32) + br_ref[...]
    lane = lax.broadcasted_iota(I32, logits.shape, 1)
    work = jnp.where(lane < N_EXPERTS, logits, -jnp.inf)
    ids = jnp.zeros(logits.shape, I32)
    vals = []
    for kk in range(TOP_K):
        mx = jnp.max(work, axis=1, keepdims=True)
        idx = jnp.min(jnp.where(work == mx, lane, LANES), axis=1, keepdims=True)
        ids = jnp.where(lane == kk, idx, ids)
        vals.append(mx)
        work = jnp.where(lane == idx, -jnp.inf, work)
    exps = [jnp.exp(vv - vals[0]) for vv in vals]
    den = exps[0] + exps[1] + exps[2] + exps[3]
    gates = jnp.zeros(logits.shape, F32)
    for kk in range(TOP_K):
        gates = jnp.where(lane == kk, exps[kk] / den, gates)

    carry = carry_scr[0:1, :]
    tt = TOKEN_TILE
    for sub in range(logits.shape[0] // tt):
        rows = slice(sub * tt, (sub + 1) * tt)
        slot, slott, gatest, meta, carry = _route_tile(ids[rows, :], gates[rows, :], carry,
                                                       tri_ref[...], utri_ref[...])
        slot_ref[rows, :] = slot
        slott_ref[sub * SUBLANES:(sub + 1) * SUBLANES, :] = slott
        gatest_ref[sub * SUBLANES:(sub + 1) * SUBLANES, :] = gatest
        meta_ref[sub] = meta
    carry_scr[...] = jnp.broadcast_to(carry, carry_scr.shape)
    cnt_ref[...] = jnp.broadcast_to(carry, cnt_ref.shape)


def _merge(x2, act, attn, sgc, sga, mod8, norm2_g, w_conv_out, w_attn_out, w_out, w_router, b_router, seq):
    t, d = x2.shape
    tm = MERGE_TILE
    tt = TOKEN_TILE
    sub = tm // tt
    nt = t // tt
    cw = act.shape[1]
    sw = attn.shape[1]
    wco = w_conv_out.astype(BF16)
    wao = w_attn_out.astype(BF16)
    wout = w_out.astype(BF16)
    wr = jnp.pad(w_router, ((0, 0), (0, LANES - N_EXPERTS)))
    wr_hi = wr.astype(BF16)
    wr_lo = (wr - wr_hi.astype(F32)).astype(BF16)
    wr = jnp.concatenate([wr_hi, wr_hi, wr_lo], axis=0)
    br = jnp.pad(b_router, (0, LANES - N_EXPERTS))[None]
    r = jnp.arange(tt)
    tri = (r[None, :] < r[:, None]).astype(BF16)
    e = jnp.arange(LANES)
    utri = (e[:, None] < e[None, :]).astype(BF16)
    tiles_per_seq = seq // tm
    row = lambda i: (i, 0)
    return pl.pallas_call(
        _merge_kernel,
        grid=(t // tm,),
        in_specs=[pl.BlockSpec((tm, d), row), pl.BlockSpec((tm, cw), row), pl.BlockSpec((tm, sw), row),
                  pl.BlockSpec((tm, d), row), pl.BlockSpec((tm, d), row),
                  pl.BlockSpec((None, SUBLANES, d), lambda i: (i // tiles_per_seq, 0, 0)),
                  _resident((1, d)), _resident(wco.shape), _resident(wao.shape), _resident(wout.shape),
                  _resident(wr.shape), _resident(br.shape), _resident(tri.shape), _resident(utri.shape)],
        out_specs=[pl.BlockSpec((tm, d), row), pl.BlockSpec((tm, d), row),
                   pl.BlockSpec((tm, LANES), row),
                   pl.BlockSpec((sub * SUBLANES, tt), row), pl.BlockSpec((sub * SUBLANES, tt), row),
                   pl.BlockSpec((sub, SUBLANES, LANES), lambda i: (i, 0, 0)),
                   pl.BlockSpec((SUBLANES, LANES), lambda i: (0, 0))],
        out_shape=[jax.ShapeDtypeStruct((t, d), F32), jax.ShapeDtypeStruct((t, d), BF16),
                   jax.ShapeDtypeStruct((t, LANES), I32),
                   jax.ShapeDtypeStruct((nt * SUBLANES, tt), I32), jax.ShapeDtypeStruct((nt * SUBLANES, tt), F32),
                   jax.ShapeDtypeStruct((nt, SUBLANES, LANES), F32),
                   jax.ShapeDtypeStruct((SUBLANES, LANES), F32)],
        scratch_shapes=[pltpu.VMEM((SUBLANES, LANES), F32)],
        compiler_params=_cparams("arbitrary"),
        name="merge",
    )(x2, act, attn, sgc, sga, mod8, norm2_g[None], wco, wao, wout, wr, br, tri, utri)


def _dest_kernel(cnt_ref, meta_ref, tri_ref, gd_ref, blk_ref):
    cnt = cnt_ref[...]
    padded = jnp.ceil(cnt * (1.0 / EXPERT_BLOCK)) * EXPERT_BLOCK
    pend = jnp.dot(padded, tri_ref[...], preferred_element_type=F32, precision=HIGHEST)
    pstart = (pend - padded)[0:1, :]

    grow = lax.broadcasted_iota(I32, (SUBLANES, LANES), 0)
    glane = lax.broadcasted_iota(I32, (SUBLANES, LANES), 1)
    gpos = ((grow * LANES + glane) * CHUNK).astype(F32)
    last = jnp.logical_and(grow == SUBLANES - 1, glane == LANES - 1)
    for tile in range(meta_ref.shape[0]):
        meta = meta_ref[tile]
        csize, lstart, seg_off = meta[0:1, :], meta[1:2, :], meta[2:3, :]
        lend = lstart + csize
        base = pstart + seg_off - lstart
        owner = jnp.zeros(gpos.shape, I32)
        for e in range(N_EXPERTS):
            owner = owner + jnp.where(gpos >= lend[0:1, e:e + 1], 1, 0)
        gdst = gpos
        for e in range(N_EXPERTS):
            gdst = gdst + jnp.where(owner == e, base[0:1, e:e + 1], 0.0)
        n_groups = lend[0:1, N_EXPERTS - 1:N_EXPERTS] * (1.0 / CHUNK)
        gd_ref[tile] = jnp.where(last, n_groups, gdst).astype(I32)

    @pl.when(pl.program_id(0) == 0)
    def _():
        shape = blk_ref.shape[1:]
        pos = (lax.broadcasted_iota(I32, shape, 0) * LANES + lax.broadcasted_iota(I32, shape, 1))
        pos = pos.astype(F32) * EXPERT_BLOCK
        be = jnp.zeros(shape, I32)
        for e in range(N_EXPERTS):
            be = be + jnp.where(pos >= pend[0:1, e:e + 1], 1, 0)
        blk_ref[0] = jnp.minimum(be, N_EXPERTS - 1)
        nact = pend[0:1, N_EXPERTS - 1:N_EXPERTS] * (1.0 / EXPERT_BLOCK)
        blk_ref[1] = jnp.broadcast_to(nact, shape).astype(I32)
        blk_ref[2] = jnp.broadcast_to(pend[0:1, :], shape).astype(I32)


def _dest(cnt, meta):
    nt = meta.shape[0]
    per_step = DEST_TILES_PER_STEP
    r = jnp.arange(LANES)
    tri = (r[:, None] <= r[None, :]).astype(F32)
    return pl.pallas_call(
        _dest_kernel,
        grid=(nt // per_step,),
        in_specs=[pl.BlockSpec((SUBLANES, LANES), lambda i: (0, 0)),
                  pl.BlockSpec((per_step, SUBLANES, LANES), lambda i: (i, 0, 0)),
                  pl.BlockSpec((LANES, LANES), lambda i: (0, 0))],
        out_specs=[pl.BlockSpec((per_step, SUBLANES, LANES), lambda i: (i, 0, 0)),
                   pl.BlockSpec((3, SUBLANES, LANES), lambda i: (0, 0, 0))],
        out_shape=[jax.ShapeDtypeStruct((nt, SUBLANES, LANES), I32),
                   jax.ShapeDtypeStruct((3, SUBLANES, LANES), I32)],
        compiler_params=_cparams("arbitrary"),
        name="dest",
    )(cnt, meta, tri)


def _row_copy(src, dst, sem):
    return pltpu.make_async_copy(src, dst, sem)


def _dispatch_kernel(pend_ref, na_ref, gd_ref, slott_ref, gatest_ref, h2_ref, xs_ref, zero_ref, sorted_scr, sem):
    tm = h2_ref.shape[0]
    bm = zero_ref.shape[0]
    nb = xs_ref.shape[0] // bm

    @pl.when(pl.program_id(0) == 0)
    def _():
        zero_ref[...] = jnp.zeros(zero_ref.shape, zero_ref.dtype)

        def zero_block(start):
            return _row_copy(zero_ref, xs_ref.at[pl.ds(pl.multiple_of(start, bm), bm)], sem)

        def nonempty(e):
            return pend_ref[e] > (pend_ref[e - 1] if e > 0 else 0)

        for e in range(N_EXPERTS):
            @pl.when(nonempty(e))
            def _():
                zero_block(pend_ref[e] - bm).start()

        def tail(b, carry):
            zero_block(b * bm).start()
            return carry

        lax.fori_loop(na_ref[0], nb, tail, 0)
        for e in range(N_EXPERTS):
            @pl.when(nonempty(e))
            def _():
                zero_block(0).wait()

        def tail_wait(b, carry):
            zero_block(0).wait()
            return carry

        lax.fori_loop(na_ref[0], nb, tail_wait, 0)

    srow = lax.broadcasted_iota(I32, (SLOT_ROWS, tm), 0)
    perm = jnp.zeros((SLOT_ROWS, tm), F32)
    gate = jnp.zeros((SLOT_ROWS, 1), F32)
    for kk in range(TOP_K):
        hit = srow == slott_ref[kk:kk + 1, :]
        perm = perm + jnp.where(hit, 1.0, 0.0)
        gate = gate + jnp.sum(jnp.where(hit, gatest_ref[kk:kk + 1, :], 0.0), axis=1, keepdims=True)
    perm = perm.astype(BF16)
    half = h2_ref.shape[1] // 2
    lo = jnp.dot(perm, h2_ref[:, :half], preferred_element_type=F32)
    hi = jnp.dot(perm, h2_ref[:, half:], preferred_element_type=F32)
    sorted_scr[:, :half] = _pack_bf16_pair(lo, hi)
    sorted_scr[:, half:] = pltpu.bitcast(jnp.broadcast_to(gate, (SLOT_ROWS, LANES)), U32)

    n_groups = gd_ref[GROUP_TABLE - 1]

    def group_copy(j, d):
        src = sorted_scr.at[pl.ds(pl.multiple_of(j * CHUNK, CHUNK), CHUNK)]
        return _row_copy(src, xs_ref.at[pl.ds(pl.multiple_of(d, CHUNK), CHUNK)], sem)

    def issue(j, carry):
        group_copy(j, gd_ref[j]).start()
        return carry

    def retire(j, carry):
        group_copy(0, 0).wait()
        return carry

    lax.fori_loop(0, n_groups, issue, 0)
    lax.fori_loop(0, n_groups, retire, 0)


def _dispatch(pend, n_active, gd_flat, slott, gatest, h2b, n_rows):
    t, d = h2b.shape
    w = d // 2 + LANES
    tm = TOKEN_TILE
    grid_spec = pltpu.PrefetchScalarGridSpec(
        num_scalar_prefetch=2,
        grid=(t // tm,),
        in_specs=[pl.BlockSpec((GROUP_TABLE,), lambda i, pe, na: (i,), memory_space=pltpu.SMEM),
                  pl.BlockSpec((SUBLANES, tm), lambda i, pe, na: (i, 0)),
                  pl.BlockSpec((SUBLANES, tm), lambda i, pe, na: (i, 0)),
                  pl.BlockSpec((tm, d), lambda i, pe, na: (i, 0))],
        out_specs=pl.BlockSpec(memory_space=pl.ANY),
        scratch_shapes=[pltpu.VMEM((EXPERT_BLOCK, w), U32), pltpu.VMEM((SLOT_ROWS, w), U32),
                        pltpu.SemaphoreType.DMA(())],
    )
    return pl.pallas_call(
        _dispatch_kernel,
        grid_spec=grid_spec,
        out_shape=jax.ShapeDtypeStruct((n_rows, w), U32),
        compiler_params=_cparams("arbitrary"),
        name="dispatch",
    )(pend, n_active, gd_flat, slott, gatest, h2b)


def _expert_kernel(be_ref, na_ref, xs_ref, w1_ref, b1_ref, w2_ref, b2_ref, y_ref, w1b_ref, w2b_ref, *, d_expert):
    i = pl.program_id(0)
    active = i < na_ref[0]
    prev = be_ref[jnp.maximum(i - 1, 0)]
    new_expert = jnp.logical_or(i == 0, be_ref[i] != prev)
    half = y_ref.shape[1]

    @pl.when(jnp.logical_and(active, new_expert))
    def _():
        w1b_ref[...] = w1_ref[...].astype(BF16)
        w2b_ref[...] = w2_ref[...].astype(BF16)

    @pl.when(active)
    def _():
        x = _unpack_bf16_pair(xs_ref[:, :half]).astype(BF16)
        gate = pltpu.bitcast(xs_ref[:, half:], F32)[:, 0:1]
        hc = jnp.dot(x, w1b_ref[...], preferred_element_type=F32) + b1_ref[...]
        g = jnp.minimum(hc[:, :d_expert], SWIGLU_LIMIT)
        u = jnp.clip(hc[:, d_expert:], -SWIGLU_LIMIT, SWIGLU_LIMIT)
        hm = (u + 1.0) * (g * jax.nn.sigmoid(SWIGLU_ALPHA * g))
        y = (jnp.dot(hm.astype(BF16), w2b_ref[...], preferred_element_type=F32) + b2_ref[...]) * gate
        y_ref[...] = _pack_bf16_pair(y[:, :half], y[:, half:])

    @pl.when(jnp.logical_not(active))
    def _():
        y_ref[...] = jnp.zeros(y_ref.shape, U32)


def _experts(block_e, n_active, xs, w1, b1, w2, b2):
    n_rows, w = xs.shape
    n_exp, d, two_de = w1.shape
    de = two_de // 2
    bm = EXPERT_BLOCK
    nb = n_rows // bm

    def blk(i, be, na):
        return jnp.maximum(jnp.minimum(i, na[0] - 1), 0)

    grid_spec = pltpu.PrefetchScalarGridSpec(
        num_scalar_prefetch=2,
        grid=(nb,),
        in_specs=[pl.BlockSpec((bm, w), lambda i, be, na: (blk(i, be, na), 0)),
                  pl.BlockSpec((None, d, two_de), lambda i, be, na: (be[blk(i, be, na)], 0, 0)),
                  pl.BlockSpec((None, 1, two_de), lambda i, be, na: (be[blk(i, be, na)], 0, 0)),
                  pl.BlockSpec((None, de, d), lambda i, be, na: (be[blk(i, be, na)], 0, 0)),
                  pl.BlockSpec((None, 1, d), lambda i, be, na: (be[blk(i, be, na)], 0, 0))],
        out_specs=pl.BlockSpec((bm, d // 2), lambda i, be, na: (i, 0)),
        scratch_shapes=[pltpu.VMEM((d, two_de), BF16), pltpu.VMEM((de, d), BF16)],
    )
    return pl.pallas_call(
        functools.partial(_expert_kernel, d_expert=de),
        grid_spec=grid_spec,
        out_shape=jax.ShapeDtypeStruct((n_rows, d // 2), U32),
        compiler_params=_cparams("arbitrary"),
        name="expert",
    )(block_e, n_active, xs, w1, b1[:, None, :], w2, b2[:, None, :])


def _combine_kernel(gd_ref, gd_next_ref, slot_ref, x1_ref, mod_ref, fg_ref, yb_ref, o_ref, sorted_scr, sem):
    tm = x1_ref.shape[0]
    step = pl.program_id(0)
    cur = step % 2

    def group_copy(buf, j, d):
        dst = sorted_scr.at[buf, pl.ds(pl.multiple_of(j * CHUNK, CHUNK), CHUNK)]
        return _row_copy(yb_ref.at[pl.ds(pl.multiple_of(d, CHUNK), CHUNK)], dst, sem.at[buf])

    def fetch(table_ref, buf):
        def issue(j, carry):
            group_copy(buf, j, table_ref[j]).start()
            return carry
        lax.fori_loop(0, table_ref[GROUP_TABLE - 1], issue, 0)

    @pl.when(step == 0)
    def _():
        sorted_scr[...] = jnp.zeros(sorted_scr.shape, sorted_scr.dtype)
        fetch(gd_ref, 0)

    @pl.when(step + 1 < pl.num_programs(0))
    def _():
        fetch(gd_next_ref, 1 - cur)

    def retire(j, carry):
        group_copy(cur, 0, 0).wait()
        return carry

    lax.fori_loop(0, gd_ref[GROUP_TABLE - 1], retire, 0)

    slot = slot_ref[...]
    scol = lax.broadcasted_iota(I32, (tm, SLOT_ROWS), 1)
    pick = jnp.zeros((tm, SLOT_ROWS), F32)
    for kk in range(TOP_K):
        pick = pick + jnp.where(scol == slot[:, kk:kk + 1], 1.0, 0.0)
    pick = pick.astype(BF16)
    words = sorted_scr[cur]
    lo = pltpu.bitcast(words << 16, F32).astype(BF16)
    hi = pltpu.bitcast(words & jnp.uint32(0xFFFF0000), F32).astype(BF16)
    y = jnp.concatenate([jnp.dot(pick, lo, preferred_element_type=F32),
                         jnp.dot(pick, hi, preferred_element_type=F32)], axis=1)
    x2 = x1_ref[...] + mod_ref[5:6, :] * y
    o_ref[...] = _rms(x2) * fg_ref[...]


def _combine(gd_flat, slot, x1, mod8, final_g, yb, seq):
    t, d = x1.shape
    tm = TOKEN_TILE
    nt = t // tm
    tiles_per_seq = seq // tm
    return pl.pallas_call(
        _combine_kernel,
        grid=(nt,),
        in_specs=[pl.BlockSpec((GROUP_TABLE,), lambda i: (i,), memory_space=pltpu.SMEM),
                  pl.BlockSpec((GROUP_TABLE,), lambda i: (jnp.minimum(i + 1, nt - 1),), memory_space=pltpu.SMEM),
                  pl.BlockSpec((tm, LANES), lambda i: (i, 0)),
                  pl.BlockSpec((tm, d), lambda i: (i, 0)),
                  pl.BlockSpec((None, SUBLANES, d), lambda i: (i // tiles_per_seq, 0, 0)),
                  pl.BlockSpec((1, d), lambda i: (0, 0)),
                  pl.BlockSpec(memory_space=pl.ANY)],
        out_specs=pl.BlockSpec((tm, d), lambda i: (i, 0)),
        out_shape=jax.ShapeDtypeStruct((t, d), F32),
        scratch_shapes=[pltpu.VMEM((2, SLOT_ROWS, d // 2), U32), pltpu.SemaphoreType.DMA((2,))],
        compiler_params=_cparams("arbitrary"),
        name="combine",
    )(gd_flat, gd_flat, slot, x1, mod8, final_g[None], yb)


def kernel(x, c, w_ada, b_ada, norm1_g, w_in, b_forget, conv_w, conv_b, conv_ln_g, conv_ln_b, w_conv_out,
           w_attn_out, w_out, norm2_g, w_router, b_router, w_exp_in, b_exp_in, w_exp_out, b_exp_out, final_g):
    bsz, seq, d = x.shape
    t = bsz * seq
    depth = w_ada.shape[0]
    xf = x.reshape(t, d)
    n_rows = t * TOP_K + (t // TOKEN_TILE) * N_EXPERTS * (CHUNK - 1) + N_EXPERTS * (EXPERT_BLOCK - CHUNK)
    n_rows = -(-n_rows // EXPERT_BLOCK) * EXPERT_BLOCK
    n_blocks = n_rows // EXPERT_BLOCK
    for l in range(depth):
        mod8 = _mod(c, w_ada, b_ada[l], l)
        act, q, k, v, aq, ak, tot, sgc, sga = _inproj(xf, mod8, norm1_g[l], w_in[l], b_forget[l], conv_w[l],
                                                      conv_b[l], conv_ln_g[l], conv_ln_b[l], seq)
        attn = _attention(q, aq, k, ak, v, tot, bsz, seq)
        x1, h2b, slot, slott, gatest, meta, cnt = _merge(xf, act, attn, sgc, sga, mod8, norm2_g[l], w_conv_out[l],
                                                         w_attn_out[l], w_out[l], w_router[l], b_router[l], seq)
        gd, blk = _dest(cnt, meta)
        gd_flat = gd.reshape(-1)
        block_e = blk[0].reshape(-1)[:n_blocks]
        n_active = blk[1, 0, :1]
        pend = blk[2, 0, :N_EXPERTS]
        xs = _dispatch(pend, n_active, gd_flat, slott, gatest, h2b, n_rows)
        yb = _experts(block_e, n_active, xs, w_exp_in[l], b_exp_in[l], w_exp_out[l], b_exp_out[l])
        assert depth == 1
        xf = _combine(gd_flat, slot, x1, mod8, final_g, yb, seq)
    return xf.reshape(bsz, seq, d)
```

```python
import functools

import jax
import jax.numpy as jnp
from jax import lax
from jax.experimental import pallas as pl
from jax.experimental.pallas import tpu as pltpu

F32 = jnp.float32
BF16 = jnp.bfloat16
I32 = jnp.int32
U32 = jnp.uint32
HIGHEST = lax.Precision.HIGHEST

LANES = 128
SUBLANES = 8
VMEM_LIMIT_BYTES = 56 * 1024 * 1024

N_HEADS = 8
HEAD_DIM = 64
CONV_KERNEL = 31
CONV_HALO = 32
N_EXPERTS = 32
TOP_K = 4
SWIGLU_LIMIT = 7.0
SWIGLU_ALPHA = 1.702
RMS_EPS = 1e-5
LN_EPS = 1e-5
N_MOD = 6
N_SPLIT = 3
LOG2_E = 1.4426950408889634
HEADS_PER_STEP = 2

INPROJ_TILE = 512
ATTN_BQ = 4096
ATTN_BK = 512
MERGE_TILE = 1024
TOKEN_TILE = 256
EXPERT_BLOCK = 768
CHUNK = SUBLANES
SLOT_ROWS = 1280
GROUP_TABLE = SUBLANES * LANES
DEST_TILES_PER_STEP = 8
assert SLOT_ROWS >= TOKEN_TILE * TOP_K + N_EXPERTS * (CHUNK - 1) and SLOT_ROWS // CHUNK < GROUP_TABLE


def _cparams(*sem):
    return pltpu.CompilerParams(dimension_semantics=sem, vmem_limit_bytes=VMEM_LIMIT_BYTES)


def _resident(shape):
    nd = len(shape)
    return pl.BlockSpec(shape, lambda *_: (0,) * nd, pipeline_mode=pl.Buffered(1))


def _split3(x):
    p1 = x.astype(BF16)
    r1 = x - p1.astype(F32)
    p2 = r1.astype(BF16)
    p3 = (r1 - p2.astype(F32)).astype(BF16)
    return jnp.concatenate([p1, p2, p3], axis=1)


def _rms(x):
    return x * lax.rsqrt(jnp.mean(x * x, axis=-1, keepdims=True) + RMS_EPS)


def _mod_kernel(ct_ref, w_ref, b_ref, o_ref, *, batch):
    rows = []
    for b in range(batch):
        col = ct_ref[:, b:b + 1]
        rows.append(jnp.sum((col * jax.nn.sigmoid(col)) * w_ref[...], axis=0, keepdims=True) + b_ref[...])
    rows.append(jnp.zeros((SUBLANES - batch, rows[0].shape[1]), F32))
    o_ref[...] = jnp.concatenate(rows, axis=0)


def _mod(c, w_ada, b_ada, layer):
    bsz, d = c.shape
    n = w_ada.shape[2]
    tn = 1024
    out = pl.pallas_call(
        functools.partial(_mod_kernel, batch=bsz),
        grid=(n // tn,),
        in_specs=[pl.BlockSpec((d, bsz), lambda j: (0, 0)),
                  pl.BlockSpec((None, d, tn), lambda j: (layer, 0, j)),
                  pl.BlockSpec((1, tn), lambda j: (0, j))],
        out_specs=pl.BlockSpec((SUBLANES, tn), lambda j: (0, j)),
        out_shape=jax.ShapeDtypeStruct((SUBLANES, n), F32),
        compiler_params=_cparams("arbitrary"),
        name="mod",
    )(c.T, w_ada, b_ada[None])
    mod = out[:bsz].reshape(bsz, N_MOD, d)
    return jnp.pad(mod, ((0, 0), (0, SUBLANES - N_MOD), (0, 0)))


def _conv_module(a, first_of_seq, w_ref, b_ref, g_ref, beta_ref, xs_scr, ys_scr):
    ts = a.shape[0]
    xs_scr[0:CONV_HALO, :] = jnp.where(first_of_seq, 0.0, xs_scr[ts:ts + CONV_HALO, :])
    xs_scr[CONV_HALO:, :] = a
    acc = jnp.zeros(a.shape, F32) + b_ref[...]
    base = CONV_HALO - (CONV_KERNEL - 1)
    for rho in range(SUBLANES):
        offs = [o for o in range(base, base + CONV_KERNEL) if o % SUBLANES == rho]
        if not offs:
            continue
        rows = offs[-1] + ts - rho
        ys_scr[0:rows, :] = xs_scr[rho:rho + rows, :]
        for o in offs:
            lo = o - rho
            acc = acc + w_ref[o - base:o - base + 1, :] * ys_scr[lo:lo + ts, :]
    mu = jnp.mean(acc, axis=-1, keepdims=True)
    cen = acc - mu
    var = jnp.mean(cen * cen, axis=-1, keepdims=True)
    y = cen * lax.rsqrt(var + LN_EPS) * g_ref[...] + beta_ref[...]
    return (y * jax.nn.sigmoid(y)).astype(BF16)


def _inproj_kernel(x_ref, mod_ref, g_ref, wa_ref, wqkv_ref, wf_ref, wg_ref, bf_ref,
                   tri_ref, eq_ref, ek_ref, oneq_ref, onek_ref, cw_ref, cb_ref, cg_ref, cbeta_ref,
                   act_ref, q_ref, k_ref, v_ref, aq_ref, ak_ref, tot_ref, sgc_ref, sga_ref,
                   carry_scr, xs_scr, ys_scr, *, conv_width, d_model, tiles_per_seq):
    first_of_seq = pl.program_id(0) % tiles_per_seq == 0

    @pl.when(pl.program_id(0) == 0)
    def _():
        xs_scr[...] = jnp.zeros(xs_scr.shape, F32)
        carry_scr[...] = jnp.zeros(carry_scr.shape, F32)

    x = x_ref[...]
    h = _rms(x) * g_ref[...] * (1.0 + mod_ref[1:2, :]) + mod_ref[0:1, :]
    hb = h.astype(BF16)

    u = jnp.dot(hb, wa_ref[...], preferred_element_type=F32)
    a = u[:, :conv_width] * jax.nn.sigmoid(u[:, conv_width:])
    act_ref[...] = _conv_module(a, first_of_seq, cw_ref, cb_ref, cg_ref, cbeta_ref, xs_scr, ys_scr)

    fl = jnp.dot(hb, wf_ref[...], preferred_element_type=F32) + bf_ref[...]
    lane = lax.broadcasted_iota(I32, fl.shape, 1)
    log_f = -(jnp.maximum(-fl, 0.0) + jnp.log1p(jnp.exp(-jnp.abs(fl))))
    log_f = jnp.where(lane < N_HEADS, log_f, 0.0)
    cs = jnp.dot(tri_ref[...], _split3(log_f), preferred_element_type=F32)
    cum = (cs[:, :LANES] + cs[:, LANES:2 * LANES] + cs[:, 2 * LANES:]) * LOG2_E
    tm = cum.shape[0]

    pieces = []
    for sb in range(tm // ATTN_BK):
        blk = cum[sb * ATTN_BK:(sb + 1) * ATTN_BK, :]
        if sb > 0:
            blk = blk - cum[sb * ATTN_BK - 1:sb * ATTN_BK, :]
        tot_ref[sb] = blk[ATTN_BK - 1:, :]
        pieces.append(blk)
    cum_k = pieces[0] if len(pieces) == 1 else jnp.concatenate(pieces, axis=0)

    first_of_block = pl.program_id(0) % (ATTN_BQ // tm) == 0
    cum_q = cum + jnp.where(first_of_block, 0.0, carry_scr[0:1, :])
    carry_scr[...] = jnp.broadcast_to(cum_q[tm - 1:, :], carry_scr.shape)

    qkv = jnp.dot(hb, wqkv_ref[...], preferred_element_type=F32)
    aw = qkv.shape[1] // 3
    q_ref[...] = (qkv[:, :aw] * (HEAD_DIM ** -0.5 * LOG2_E)).astype(BF16)
    k_ref[...] = qkv[:, aw:2 * aw].astype(BF16)
    v_ref[...] = qkv[:, 2 * aw:].astype(BF16)
    aq_ref[...] = (jnp.dot(_split3(cum_q), eq_ref[...], preferred_element_type=F32) + oneq_ref[...]).astype(BF16)
    ak_ref[...] = (jnp.dot(_split3(cum_k), ek_ref[...], preferred_element_type=F32) + onek_ref[...]).astype(BF16)

    gts = jnp.dot(hb, wg_ref[...], preferred_element_type=F32)
    sgc_ref[...] = jax.nn.sigmoid(gts[:, :d_model]).astype(BF16)
    sga_ref[...] = jax.nn.sigmoid(gts[:, d_model:]).astype(BF16)


def _aug_constants():
    rows = jnp.arange(N_SPLIT * LANES)
    piece, head = rows // LANES, rows % LANES
    cols = jnp.arange(N_HEADS * HEAD_DIM)
    pair, off = cols // LANES, cols % LANES
    chead = 2 * pair + jnp.where(off >= HEAD_DIM, 0, 1)
    pos = off % HEAD_DIM
    valid = (head[:, None] < N_HEADS) & (head[:, None] == chead[None, :])
    eq = jnp.where(valid & (pos[None, :] == piece[:, None]), 1.0, 0.0).astype(BF16)
    ek = jnp.where(valid & (pos[None, :] == piece[:, None] + N_SPLIT), -1.0, 0.0).astype(BF16)
    oneq = jnp.where((pos >= N_SPLIT) & (pos < 2 * N_SPLIT), 1.0, 0.0).astype(F32)[None]
    onek = jnp.where(pos < N_SPLIT, 1.0, 0.0).astype(F32)[None]
    return eq, ek, oneq, onek


def _inproj(x2, mod8, norm_g, w_in, b_forget, conv_w, conv_b, conv_ln_g, conv_ln_b, seq):
    t, d = x2.shape
    tm = INPROJ_TILE
    nsub = tm // ATTN_BK
    cw = d // 2
    aw = N_HEADS * HEAD_DIM
    o0 = 2 * cw
    wa = w_in[:, :o0].astype(BF16)
    o1 = o0 + 3 * aw
    wqkv = w_in[:, o0:o1].astype(BF16)
    wf = jnp.pad(w_in[:, o1:o1 + N_HEADS], ((0, 0), (0, LANES - N_HEADS))).astype(BF16)
    bfp = jnp.pad(b_forget, (0, LANES - N_HEADS))[None]
    wg = w_in[:, o1 + N_HEADS:].astype(BF16)
    r = jnp.arange(tm)
    tri = (r[None, :] <= r[:, None]).astype(BF16)
    eq, ek, oneq, onek = _aug_constants()
    tiles_per_seq = seq // tm
    row = lambda i: (i, 0)
    outs = pl.pallas_call(
        functools.partial(_inproj_kernel, conv_width=cw, d_model=d, tiles_per_seq=tiles_per_seq),
        grid=(t // tm,),
        in_specs=[pl.BlockSpec((tm, d), row),
                  pl.BlockSpec((None, SUBLANES, d), lambda i: (i // tiles_per_seq, 0, 0)),
                  _resident((1, d)),
                  _resident(wa.shape), _resident(wqkv.shape),
                  _resident(wf.shape), _resident(wg.shape), _resident(bfp.shape),
                  _resident(tri.shape), _resident(eq.shape), _resident(ek.shape),
                  _resident(oneq.shape), _resident(onek.shape),
                  _resident(conv_w.shape), _resident((1, cw)), _resident((1, cw)), _resident((1, cw))],
        out_specs=[pl.BlockSpec((tm, cw), row)] + [pl.BlockSpec((tm, aw), row)] * 5 + [
                   pl.BlockSpec((nsub, 1, LANES), lambda i: (i, 0, 0)),
                   pl.BlockSpec((tm, d), row), pl.BlockSpec((tm, d), row)],
        out_shape=[jax.ShapeDtypeStruct((t, cw), BF16)] + [jax.ShapeDtypeStruct((t, aw), BF16)] * 5 + [
                   jax.ShapeDtypeStruct((t // ATTN_BK, 1, LANES), F32),
                   jax.ShapeDtypeStruct((t, d), BF16), jax.ShapeDtypeStruct((t, d), BF16)],
        scratch_shapes=[pltpu.VMEM((SUBLANES, LANES), F32),
                        pltpu.VMEM((CONV_HALO + tm, cw), F32), pltpu.VMEM((CONV_HALO + tm, cw), F32)],
        compiler_params=_cparams("arbitrary"),
        name="inproj",
    )(x2, mod8, norm_g[None], wa, wqkv, wf, wg, bfp, tri, eq, ek, oneq, onek,
      conv_w, conv_b[None], conv_ln_g[None], conv_ln_b[None])
    return outs


def _attn_kernel(q_ref, aq_ref, k_ref, ak_ref, v_ref, tot_ref, o_ref, qop_scr, m_scr, acc_scr):
    bq = q_ref.shape[0]
    bk = ATTN_BK
    ratio = bq // bk
    qi = pl.program_id(2)
    nt = (((1,), (1,)), ((), ()))
    heads = range(HEADS_PER_STEP)
    zero = jnp.zeros((1, LANES), F32)

    def splice(hh, main, extra):
        first = lax.broadcasted_iota(I32, main.shape, 1) < HEAD_DIM
        return jnp.where(first, main, extra) if hh == 0 else jnp.where(first, extra, main)

    for hh in heads:
        qop_scr[hh] = splice(hh, q_ref[...], aq_ref[...])

    def scores(hh, kb, row0):
        start = pl.multiple_of(kb * bk, bk)
        k = splice(hh, k_ref[pl.ds(start, bk), :], ak_ref[pl.ds(start, bk), :])
        v = v_ref[pl.ds(start, bk), :]
        v = splice(hh, v, jnp.ones(v.shape, v.dtype))
        s = lax.dot_general(qop_scr[hh, row0:, :], k, nt, preferred_element_type=F32)
        return s, v

    def update(hh, s, v, delta, row0):
        m_prev = m_scr[hh, row0:, :]
        m_new = jnp.maximum(m_prev, jnp.max(s, axis=1, keepdims=True) + delta)
        shift = m_new - delta
        p = jnp.exp2(s - shift[:, :1])
        alpha = jnp.exp2(m_prev - m_new)
        pv = jnp.dot(p.astype(BF16), v, preferred_element_type=F32)
        acc_scr[hh, row0:, :] = alpha * acc_scr[hh, row0:, :] + pv
        m_scr[hh, row0:, :] = m_new

    for hh in heads:
        delta = zero
        for sb in range(ratio):
            kb = qi * ratio + sb
            row0 = sb * bk
            s, v = scores(hh, kb, row0)
            row = lax.broadcasted_iota(I32, s.shape, 0)
            col = lax.broadcasted_iota(I32, s.shape, 1)
            s = jnp.where(col <= row, s, -jnp.inf)
            if sb == 0:
                m0 = jnp.max(s, axis=1, keepdims=True)
                p = jnp.exp2(s - m0)
                m_scr[hh] = jnp.broadcast_to(m0, m_scr.shape[1:])
                acc_scr[hh] = jnp.dot(p.astype(BF16), v, preferred_element_type=F32)
            else:
                update(hh, s, v, delta, row0)
            delta = delta - tot_ref[hh, pl.ds(kb, 1), :]

    def body(step, deltas):
        kb = qi * ratio - 1 - step
        new_deltas = []
        for hh in heads:
            delta = deltas[hh] + tot_ref[hh, pl.ds(kb, 1), :]
            s, v = scores(hh, kb, 0)
            update(hh, s, v, delta, 0)
            new_deltas.append(delta)
        return tuple(new_deltas)

    lax.fori_loop(0, qi * ratio, body, tuple(zero for _ in heads))
    acc0, acc1 = acc_scr[0], acc_scr[1]
    out0 = acc0 / acc0[:, HEAD_DIM:HEAD_DIM + 1]
    out1 = acc1 / acc1[:, 0:1]
    o_ref[...] = splice(0, out0, out1).astype(BF16)


def _attention(q, aq, k, ak, v, tot, bsz, seq):
    t, aw = q.shape
    bq = ATTN_BQ
    nq = seq // bq
    nk = seq // ATTN_BK
    assert HEADS_PER_STEP * HEAD_DIM == LANES
    tot = tot.reshape(bsz, nk, LANES)[:, :, :N_HEADS].transpose(0, 2, 1)
    tot = jnp.broadcast_to(tot[..., None], (bsz, N_HEADS, nk, LANES))
    qblk = pl.BlockSpec((bq, LANES), lambda b, h, i: (b * nq + i, h))
    kblk = pl.BlockSpec((None, seq, LANES), lambda b, h, i: (b, 0, h))
    seq3 = lambda a: a.reshape(bsz, seq, aw)
    return pl.pallas_call(
        _attn_kernel,
        grid=(bsz, N_HEADS // HEADS_PER_STEP, nq),
        in_specs=[qblk, qblk, kblk, kblk, kblk,
                  pl.BlockSpec((None, HEADS_PER_STEP, nk, LANES), lambda b, h, i: (b, h, 0, 0))],
        out_specs=qblk,
        out_shape=jax.ShapeDtypeStruct((t, aw), BF16),
        scratch_shapes=[pltpu.VMEM((HEADS_PER_STEP, bq, LANES), BF16),
                        pltpu.VMEM((HEADS_PER_STEP, bq, LANES), F32),
                        pltpu.VMEM((HEADS_PER_STEP, bq, LANES), F32)],
        compiler_params=_cparams("arbitrary", "arbitrary", "arbitrary"),
        name="attn",
    )(q, aq, seq3(k), seq3(ak), seq3(v), tot)


def _pack_bf16_pair(lo, hi):
    lo_bits = pltpu.bitcast(lo.astype(BF16).astype(F32), U32)
    hi_bits = pltpu.bitcast(hi.astype(BF16).astype(F32), U32)
    return (hi_bits & jnp.uint32(0xFFFF0000)) | (lo_bits >> 16)


def _unpack_bf16_pair(w):
    lo = pltpu.bitcast(w << 16, F32)
    hi = pltpu.bitcast(w & jnp.uint32(0xFFFF0000), F32)
    return jnp.concatenate([lo, hi], axis=1)


def _expert_onehots(ids):
    lane = lax.broadcasted_iota(I32, ids.shape, 1)
    return lane, [lane == ids[:, kk:kk + 1] for kk in range(TOP_K)]


def _route_tile(ids, gates, carry, tri, utri):
    lane, hots = _expert_onehots(ids)
    multi = jnp.zeros(lane.shape, F32)
    for hot in hots:
        multi = multi + jnp.where(hot, 1.0, 0.0)
    before = jnp.dot(tri, multi.astype(BF16), preferred_element_type=F32)
    units = jnp.ceil(jnp.sum(multi, axis=0, keepdims=True) * (1.0 / CHUNK))
    units8 = jnp.broadcast_to(units, (SUBLANES, LANES)).astype(BF16)
    lstart = jnp.dot(units8, utri, preferred_element_type=F32)[0:1, :] * CHUNK
    csize = units * CHUNK
    slot = jnp.zeros(lane.shape, F32)
    for kk, hot in enumerate(hots):
        sl = jnp.sum(jnp.where(hot, lstart + before, 0.0), axis=1, keepdims=True)
        slot = jnp.where(lane == kk, sl, slot)
    row = lax.broadcasted_iota(I32, (SUBLANES, LANES), 0)
    meta = jnp.where(row == 0, csize, jnp.where(row == 1, lstart, jnp.where(row == 2, carry, 0.0)))
    slott = slot.T[0:SUBLANES, :].astype(I32)
    gatest = gates.T[0:SUBLANES, :]
    return slot.astype(I32), slott, gatest, meta, carry + csize


def _merge_kernel(x_ref, act_ref, attn_ref, sgc_ref, sga_ref, mod_ref, g2_ref, wco_ref, wao_ref, wout_ref,
                  wr_ref, br_ref, tri_ref, utri_ref,
                  x1_ref, h2_ref, slot_ref, slott_ref, gatest_ref, meta_ref, cnt_ref, carry_scr):
    @pl.when(pl.program_id(0) == 0)
    def _():
        carry_scr[...] = jnp.zeros(carry_scr.shape, F32)

    conv_out = jnp.dot(act_ref[...], wco_ref[...], preferred_element_type=F32)
    attn_out = jnp.dot(attn_ref[...], wao_ref[...], preferred_element_type=F32)
    mixed = sgc_ref[...].astype(F32) * conv_out + sga_ref[...].astype(F32) * attn_out
    upd = jnp.dot(mixed.astype(BF16), wout_ref[...], preferred_element_type=F32)
    x1 = x_ref[...] + mod_ref[2:3, :] * upd
    x1_ref[...] = x1
    h2 = _rms(x1) * g2_ref[...] * (1.0 + mod_ref[4:5, :]) + mod_ref[3:4, :]
    h2_hi = h2.astype(BF16)
    h2_ref[...] = h2_hi
    h2_lo = (h2 - h2_hi.astype(F32)).astype(BF16)
    h2_cat = jnp.concatenate([h2_hi, h2_lo, h2_hi], axis=1)
    logits = jnp.dot(h2_cat, wr_ref[...], preferred_element_type=F32) + br_ref[...]
    lane = lax.broadcasted_iota(I32, logits.shape, 1)
    work = jnp.where(lane < N_EXPERTS, logits, -jnp.inf)
    ids = jnp.zeros(logits.shape, I32)
    vals = []
    for kk in range(TOP_K):
        mx = jnp.max(work, axis=1, keepdims=True)
        idx = jnp.min(jnp.where(work == mx, lane, LANES), axis=1, keepdims=True)
        ids = jnp.where(lane == kk, idx, ids)
        vals.append(mx)
        work = jnp.where(lane == idx, -jnp.inf, work)
    exps = [jnp.exp(vv - vals[0]) for vv in vals]
    den = exps[0] + exps[1] + exps[2] + exps[3]
    gates = jnp.zeros(logits.shape, F32)
    for kk in range(TOP_K):
        gates = jnp.where(lane == kk, exps[kk] / den, gates)

    carry = carry_scr[0:1, :]
    tt = TOKEN_TILE
    for sub in range(logits.shape[0] // tt):
        rows = slice(sub * tt, (sub + 1) * tt)
        slot, slott, gatest, meta, carry = _route_tile(ids[rows, :], gates[rows, :], carry,
                                                       tri_ref[...], utri_ref[...])
        slot_ref[rows, :] = slot
        slott_ref[sub * SUBLANES:(sub + 1) * SUBLANES, :] = slott
        gatest_ref[sub * SUBLANES:(sub + 1) * SUBLANES, :] = gatest
        meta_ref[sub] = meta
    carry_scr[...] = jnp.broadcast_to(carry, carry_scr.shape)
    cnt_ref[...] = jnp.broadcast_to(carry, cnt_ref.shape)


def _merge(x2, act, attn, sgc, sga, mod8, norm2_g, w_conv_out, w_attn_out, w_out, w_router, b_router, seq):
    t, d = x2.shape
    tm = MERGE_TILE
    tt = TOKEN_TILE
    sub = tm // tt
    nt = t // tt
    cw = act.shape[1]
    sw = attn.shape[1]
    wco = w_conv_out.astype(BF16)
    wao = w_attn_out.astype(BF16)
    wout = w_out.astype(BF16)
    wr = jnp.pad(w_router, ((0, 0), (0, LANES - N_EXPERTS)))
    wr_hi = wr.astype(BF16)
    wr_lo = (wr - wr_hi.astype(F32)).astype(BF16)
    wr = jnp.concatenate([wr_hi, wr_hi, wr_lo], axis=0)
    br = jnp.pad(b_router, (0, LANES - N_EXPERTS))[None]
    r = jnp.arange(tt)
    tri = (r[None, :] < r[:, None]).astype(BF16)
    e = jnp.arange(LANES)
    utri = (e[:, None] < e[None, :]).astype(BF16)
    tiles_per_seq = seq // tm
    row = lambda i: (i, 0)
    return pl.pallas_call(
        _merge_kernel,
        grid=(t // tm,),
        in_specs=[pl.BlockSpec((tm, d), row), pl.BlockSpec((tm, cw), row), pl.BlockSpec((tm, sw), row),
                  pl.BlockSpec((tm, d), row), pl.BlockSpec((tm, d), row),
                  pl.BlockSpec((None, SUBLANES, d), lambda i: (i // tiles_per_seq, 0, 0)),
                  _resident((1, d)), _resident(wco.shape), _resident(wao.shape), _resident(wout.shape),
                  _resident(wr.shape), _resident(br.shape), _resident(tri.shape), _resident(utri.shape)],
        out_specs=[pl.BlockSpec((tm, d), row), pl.BlockSpec((tm, d), row),
                   pl.BlockSpec((tm, LANES), row),
                   pl.BlockSpec((sub * SUBLANES, tt), row), pl.BlockSpec((sub * SUBLANES, tt), row),
                   pl.BlockSpec((sub, SUBLANES, LANES), lambda i: (i, 0, 0)),
                   pl.BlockSpec((SUBLANES, LANES), lambda i: (0, 0))],
        out_shape=[jax.ShapeDtypeStruct((t, d), F32), jax.ShapeDtypeStruct((t, d), BF16),
                   jax.ShapeDtypeStruct((t, LANES), I32),
                   jax.ShapeDtypeStruct((nt * SUBLANES, tt), I32), jax.ShapeDtypeStruct((nt * SUBLANES, tt), F32),
                   jax.ShapeDtypeStruct((nt, SUBLANES, LANES), F32),
                   jax.ShapeDtypeStruct((SUBLANES, LANES), F32)],
        scratch_shapes=[pltpu.VMEM((SUBLANES, LANES), F32)],
        compiler_params=_cparams("arbitrary"),
        name="merge",
    )(x2, act, attn, sgc, sga, mod8, norm2_g[None], wco, wao, wout, wr, br, tri, utri)


def _dest_kernel(cnt_ref, meta_ref, tri_ref, gd_ref, blk_ref):
    cnt = cnt_ref[...]
    padded = jnp.ceil(cnt * (1.0 / EXPERT_BLOCK)) * EXPERT_BLOCK
    pend = jnp.dot(padded, tri_ref[...], preferred_element_type=F32, precision=HIGHEST)
    pstart = (pend - padded)[0:1, :]

    grow = lax.broadcasted_iota(I32, (SUBLANES, LANES), 0)
    glane = lax.broadcasted_iota(I32, (SUBLANES, LANES), 1)
    gpos = ((grow * LANES + glane) * CHUNK).astype(F32)
    last = jnp.logical_and(grow == SUBLANES - 1, glane == LANES - 1)
    for tile in range(meta_ref.shape[0]):
        meta = meta_ref[tile]
        csize, lstart, seg_off = meta[0:1, :], meta[1:2, :], meta[2:3, :]
        lend = lstart + csize
        base = pstart + seg_off - lstart
        owner = jnp.zeros(gpos.shape, I32)
        for e in range(N_EXPERTS):
            owner = owner + jnp.where(gpos >= lend[0:1, e:e + 1], 1, 0)
        gdst = gpos
        for e in range(N_EXPERTS):
            gdst = gdst + jnp.where(owner == e, base[0:1, e:e + 1], 0.0)
        n_groups = lend[0:1, N_EXPERTS - 1:N_EXPERTS] * (1.0 / CHUNK)
        gd_ref[tile] = jnp.where(last, n_groups, gdst).astype(I32)

    @pl.when(pl.program_id(0) == 0)
    def _():
        shape = blk_ref.shape[1:]
        pos = (lax.broadcasted_iota(I32, shape, 0) * LANES + lax.broadcasted_iota(I32, shape, 1))
        pos = pos.astype(F32) * EXPERT_BLOCK
        be = jnp.zeros(shape, I32)
        for e in range(N_EXPERTS):
            be = be + jnp.where(pos >= pend[0:1, e:e + 1], 1, 0)
        blk_ref[0] = jnp.minimum(be, N_EXPERTS - 1)
        nact = pend[0:1, N_EXPERTS - 1:N_EXPERTS] * (1.0 / EXPERT_BLOCK)
        blk_ref[1] = jnp.broadcast_to(nact, shape).astype(I32)
        blk_ref[2] = jnp.broadcast_to(pend[0:1, :], shape).astype(I32)


def _dest(cnt, meta):
    nt = meta.shape[0]
    per_step = DEST_TILES_PER_STEP
    r = jnp.arange(LANES)
    tri = (r[:, None] <= r[None, :]).astype(F32)
    return pl.pallas_call(
        _dest_kernel,
        grid=(nt // per_step,),
        in_specs=[pl.BlockSpec((SUBLANES, LANES), lambda i: (0, 0)),
                  pl.BlockSpec((per_step, SUBLANES, LANES), lambda i: (i, 0, 0)),
                  pl.BlockSpec((LANES, LANES), lambda i: (0, 0))],
        out_specs=[pl.BlockSpec((per_step, SUBLANES, LANES), lambda i: (i, 0, 0)),
                   pl.BlockSpec((3, SUBLANES, LANES), lambda i: (0, 0, 0))],
        out_shape=[jax.ShapeDtypeStruct((nt, SUBLANES, LANES), I32),
                   jax.ShapeDtypeStruct((3, SUBLANES, LANES), I32)],
        compiler_params=_cparams("arbitrary"),
        name="dest",
    )(cnt, meta, tri)


def _row_copy(src, dst, sem):
    return pltpu.make_async_copy(src, dst, sem)


def _dispatch_kernel(pend_ref, na_ref, gd_ref, slott_ref, gatest_ref, h2_ref, xs_ref, zero_ref, sorted_scr, sem):
    tm = h2_ref.shape[0]
    bm = zero_ref.shape[0]
    nb = xs_ref.shape[0] // bm

    @pl.when(pl.program_id(0) == 0)
    def _():
        zero_ref[...] = jnp.zeros(zero_ref.shape, zero_ref.dtype)

        def zero_block(start):
            return _row_copy(zero_ref, xs_ref.at[pl.ds(pl.multiple_of(start, bm), bm)], sem)

        def nonempty(e):
            return pend_ref[e] > (pend_ref[e - 1] if e > 0 else 0)

        for e in range(N_EXPERTS):
            @pl.when(nonempty(e))
            def _():
                zero_block(pend_ref[e] - bm).start()

        def tail(b, carry):
            zero_block(b * bm).start()
            return carry

        lax.fori_loop(na_ref[0], nb, tail, 0)
        for e in range(N_EXPERTS):
            @pl.when(nonempty(e))
            def _():
                zero_block(0).wait()

        def tail_wait(b, carry):
            zero_block(0).wait()
            return carry

        lax.fori_loop(na_ref[0], nb, tail_wait, 0)

    srow = lax.broadcasted_iota(I32, (SLOT_ROWS, tm), 0)
    perm = jnp.zeros((SLOT_ROWS, tm), F32)
    gate = jnp.zeros((SLOT_ROWS, 1), F32)
    for kk in range(TOP_K):
        hit = srow == slott_ref[kk:kk + 1, :]
        perm = perm + jnp.where(hit, 1.0, 0.0)
        gate = gate + jnp.sum(jnp.where(hit, gatest_ref[kk:kk + 1, :], 0.0), axis=1, keepdims=True)
    perm = perm.astype(BF16)
    half = h2_ref.shape[1] // 2
    lo = jnp.dot(perm, h2_ref[:, :half], preferred_element_type=F32)
    hi = jnp.dot(perm, h2_ref[:, half:], preferred_element_type=F32)
    sorted_scr[:, :half] = pltpu.bitcast(hi, U32) | (pltpu.bitcast(lo, U32) >> 16)
    sorted_scr[:, half:] = pltpu.bitcast(jnp.broadcast_to(gate, (SLOT_ROWS, LANES)), U32)

    n_groups = gd_ref[GROUP_TABLE - 1]

    def group_copy(j, d):
        src = sorted_scr.at[pl.ds(pl.multiple_of(j * CHUNK, CHUNK), CHUNK)]
        return _row_copy(src, xs_ref.at[pl.ds(pl.multiple_of(d, CHUNK), CHUNK)], sem)

    def issue(j, carry):
        group_copy(j, gd_ref[j]).start()
        return carry

    def retire(j, carry):
        group_copy(0, 0).wait()
        return carry

    lax.fori_loop(0, n_groups, issue, 0)
    lax.fori_loop(0, n_groups, retire, 0)


def _dispatch(pend, n_active, gd_flat, slott, gatest, h2b, n_rows):
    t, d = h2b.shape
    w = d // 2 + LANES
    tm = TOKEN_TILE
    grid_spec = pltpu.PrefetchScalarGridSpec(
        num_scalar_prefetch=2,
        grid=(t // tm,),
        in_specs=[pl.BlockSpec((GROUP_TABLE,), lambda i, pe, na: (i,), memory_space=pltpu.SMEM),
                  pl.BlockSpec((SUBLANES, tm), lambda i, pe, na: (i, 0)),
                  pl.BlockSpec((SUBLANES, tm), lambda i, pe, na: (i, 0)),
                  pl.BlockSpec((tm, d), lambda i, pe, na: (i, 0))],
        out_specs=pl.BlockSpec(memory_space=pl.ANY),
        scratch_shapes=[pltpu.VMEM((EXPERT_BLOCK, w), U32), pltpu.VMEM((SLOT_ROWS, w), U32),
                        pltpu.SemaphoreType.DMA(())],
    )
    return pl.pallas_call(
        _dispatch_kernel,
        grid_spec=grid_spec,
        out_shape=jax.ShapeDtypeStruct((n_rows, w), U32),
        compiler_params=_cparams("arbitrary"),
        name="dispatch",
    )(pend, n_active, gd_flat, slott, gatest, h2b)


def _expert_kernel(be_ref, na_ref, xs_ref, w1_ref, b1_ref, w2_ref, b2_ref, y_ref, w1b_ref, w2b_ref, *, d_expert):
    i = pl.program_id(0)
    active = i < na_ref[0]
    prev = be_ref[jnp.maximum(i - 1, 0)]
    new_expert = jnp.logical_or(i == 0, be_ref[i] != prev)
    half = y_ref.shape[1]

    @pl.when(jnp.logical_and(active, new_expert))
    def _():
        w1b_ref[...] = w1_ref[...].astype(BF16)
        w2b_ref[...] = w2_ref[...].astype(BF16)

    @pl.when(active)
    def _():
        x = _unpack_bf16_pair(xs_ref[:, :half]).astype(BF16)
        gate = pltpu.bitcast(xs_ref[:, half:], F32)[:, 0:1]
        hc = jnp.dot(x, w1b_ref[...], preferred_element_type=F32) + b1_ref[...]
        g = jnp.minimum(hc[:, :d_expert], SWIGLU_LIMIT)
        u = jnp.clip(hc[:, d_expert:], -SWIGLU_LIMIT, SWIGLU_LIMIT)
        hm = (u + 1.0) * (g * jax.nn.sigmoid(SWIGLU_ALPHA * g))
        y = (jnp.dot(hm.astype(BF16), w2b_ref[...], preferred_element_type=F32) + b2_ref[...]) * gate
        y_ref[...] = _pack_bf16_pair(y[:, :half], y[:, half:])

    @pl.when(jnp.logical_not(active))
    def _():
        y_ref[...] = jnp.zeros(y_ref.shape, U32)


def _experts(block_e, n_active, xs, w1, b1, w2, b2):
    n_rows, w = xs.shape
    n_exp, d, two_de = w1.shape
    de = two_de // 2
    bm = EXPERT_BLOCK
    nb = n_rows // bm

    def blk(i, be, na):
        return jnp.maximum(jnp.minimum(i, na[0] - 1), 0)

    grid_spec = pltpu.PrefetchScalarGridSpec(
        num_scalar_prefetch=2,
        grid=(nb,),
        in_specs=[pl.BlockSpec((bm, w), lambda i, be, na: (blk(i, be, na), 0)),
                  pl.BlockSpec((None, d, two_de), lambda i, be, na: (be[blk(i, be, na)], 0, 0)),
                  pl.BlockSpec((None, 1, two_de), lambda i, be, na: (be[blk(i, be, na)], 0, 0)),
                  pl.BlockSpec((None, de, d), lambda i, be, na: (be[blk(i, be, na)], 0, 0)),
                  pl.BlockSpec((None, 1, d), lambda i, be, na: (be[blk(i, be, na)], 0, 0))],
        out_specs=pl.BlockSpec((bm, d // 2), lambda i, be, na: (i, 0)),
        scratch_shapes=[pltpu.VMEM((d, two_de), BF16), pltpu.VMEM((de, d), BF16)],
    )
    return pl.pallas_call(
        functools.partial(_expert_kernel, d_expert=de),
        grid_spec=grid_spec,
        out_shape=jax.ShapeDtypeStruct((n_rows, d // 2), U32),
        compiler_params=_cparams("arbitrary"),
        name="expert",
    )(block_e, n_active, xs, w1, b1[:, None, :], w2, b2[:, None, :])


def _combine_kernel(gd_ref, gd_next_ref, slot_ref, x1_ref, mod_ref, fg_ref, yb_ref, o_ref, sorted_scr, sem):
    tm = x1_ref.shape[0]
    step = pl.program_id(0)
    cur = step % 2

    def group_copy(buf, j, d):
        dst = sorted_scr.at[buf, pl.ds(pl.multiple_of(j * CHUNK, CHUNK), CHUNK)]
        return _row_copy(yb_ref.at[pl.ds(pl.multiple_of(d, CHUNK), CHUNK)], dst, sem.at[buf])

    def fetch(table_ref, buf):
        def issue(j, carry):
            group_copy(buf, j, table_ref[j]).start()
            return carry
        lax.fori_loop(0, table_ref[GROUP_TABLE - 1], issue, 0)

    @pl.when(step == 0)
    def _():
        sorted_scr[...] = jnp.zeros(sorted_scr.shape, sorted_scr.dtype)
        fetch(gd_ref, 0)

    @pl.when(step + 1 < pl.num_programs(0))
    def _():
        fetch(gd_next_ref, 1 - cur)

    def retire(j, carry):
        group_copy(cur, 0, 0).wait()
        return carry

    lax.fori_loop(0, gd_ref[GROUP_TABLE - 1], retire, 0)

    slot = slot_ref[...]
    scol = lax.broadcasted_iota(I32, (tm, SLOT_ROWS), 1)
    pick = jnp.zeros((tm, SLOT_ROWS), F32)
    for kk in range(TOP_K):
        pick = pick + jnp.where(scol == slot[:, kk:kk + 1], 1.0, 0.0)
    pick = pick.astype(BF16)
    words = sorted_scr[cur]
    lo = pltpu.bitcast(words << 16, F32).astype(BF16)
    hi = pltpu.bitcast(words & jnp.uint32(0xFFFF0000), F32).astype(BF16)
    y = jnp.concatenate([jnp.dot(pick, lo, preferred_element_type=F32),
                         jnp.dot(pick, hi, preferred_element_type=F32)], axis=1)
    x2 = x1_ref[...] + mod_ref[5:6, :] * y
    o_ref[...] = _rms(x2) * fg_ref[...]


def _combine(gd_flat, slot, x1, mod8, final_g, yb, seq):
    t, d = x1.shape
    tm = TOKEN_TILE
    nt = t // tm
    tiles_per_seq = seq // tm
    return pl.pallas_call(
        _combine_kernel,
        grid=(nt,),
        in_specs=[pl.BlockSpec((GROUP_TABLE,), lambda i: (i,), memory_space=pltpu.SMEM),
                  pl.BlockSpec((GROUP_TABLE,), lambda i: (jnp.minimum(i + 1, nt - 1),), memory_space=pltpu.SMEM),
                  pl.BlockSpec((tm, LANES), lambda i: (i, 0)),
                  pl.BlockSpec((tm, d), lambda i: (i, 0)),
                  pl.BlockSpec((None, SUBLANES, d), lambda i: (i // tiles_per_seq, 0, 0)),
                  pl.BlockSpec((1, d), lambda i: (0, 0)),
                  pl.BlockSpec(memory_space=pl.ANY)],
        out_specs=pl.BlockSpec((tm, d), lambda i: (i, 0)),
        out_shape=jax.ShapeDtypeStruct((t, d), F32),
        scratch_shapes=[pltpu.VMEM((2, SLOT_ROWS, d // 2), U32), pltpu.SemaphoreType.DMA((2,))],
        compiler_params=_cparams("arbitrary"),
        name="combine",
    )(gd_flat, gd_flat, slot, x1, mod8, final_g[None], yb)


def kernel(x, c, w_ada, b_ada, norm1_g, w_in, b_forget, conv_w, conv_b, conv_ln_g, conv_ln_b, w_conv_out,
           w_attn_out, w_out, norm2_g, w_router, b_router, w_exp_in, b_exp_in, w_exp_out, b_exp_out, final_g):
    bsz, seq, d = x.shape
    t = bsz * seq
    depth = w_ada.shape[0]
    xf = x.reshape(t, d)
    n_rows = t * TOP_K + (t // TOKEN_TILE) * N_EXPERTS * (CHUNK - 1) + N_EXPERTS * (EXPERT_BLOCK - CHUNK)
    n_rows = -(-n_rows // EXPERT_BLOCK) * EXPERT_BLOCK
    n_blocks = n_rows // EXPERT_BLOCK
    for l in range(depth):
        mod8 = _mod(c, w_ada, b_ada[l], l)
        act, q, k, v, aq, ak, tot, sgc, sga = _inproj(xf, mod8, norm1_g[l], w_in[l], b_forget[l], conv_w[l],
                                                      conv_b[l], conv_ln_g[l], conv_ln_b[l], seq)
        attn = _attention(q, aq, k, ak, v, tot, bsz, seq)
        x1, h2b, slot, slott, gatest, meta, cnt = _merge(xf, act, attn, sgc, sga, mod8, norm2_g[l], w_conv_out[l],
                                                         w_attn_out[l], w_out[l], w_router[l], b_router[l], seq)
        gd, blk = _dest(cnt, meta)
        gd_flat = gd.reshape(-1)
        block_e = blk[0].reshape(-1)[:n_blocks]
        n_active = blk[1, 0, :1]
        pend = blk[2, 0, :N_EXPERTS]
        xs = _dispatch(pend, n_active, gd_flat, slott, gatest, h2b, n_rows)
        yb = _experts(block_e, n_active, xs, w_exp_in[l], b_exp_in[l], w_exp_out[l], b_exp_out[l])
        assert depth == 1
        xf = _combine(gd_flat, slot, x1, mod8, final_g, yb, seq)
    return xf.reshape(bsz, seq, d)
```

```python
import functools

import jax
import jax.numpy as jnp
from jax import lax
from jax.experimental import pallas as pl
from jax.experimental.pallas import tpu as pltpu

F32 = jnp.float32
BF16 = jnp.bfloat16
I32 = jnp.int32
U32 = jnp.uint32
HIGHEST = lax.Precision.HIGHEST

LANES = 128
SUBLANES = 8
VMEM_LIMIT_BYTES = 56 * 1024 * 1024

N_HEADS = 8
HEAD_DIM = 64
CONV_KERNEL = 31
CONV_HALO = 32
N_EXPERTS = 32
TOP_K = 4
SWIGLU_LIMIT = 7.0
SWIGLU_ALPHA = 1.702
RMS_EPS = 1e-5
LN_EPS = 1e-5
N_MOD = 6
N_SPLIT = 3
LOG2_E = 1.4426950408889634
HEADS_PER_STEP = 2

INPROJ_TILE = 512
ATTN_BQ = 4096
ATTN_BK = 512
MERGE_TILE = 1024
TOKEN_TILE = 256
EXPERT_BLOCK = 768
EXPERT_ROW_STEP = 256
CHUNK = SUBLANES
SLOT_ROWS = 1280
GROUP_TABLE = SUBLANES * LANES
DEST_TILES_PER_STEP = 8
assert SLOT_ROWS >= TOKEN_TILE * TOP_K + N_EXPERTS * (CHUNK - 1) and SLOT_ROWS // CHUNK < GROUP_TABLE


def _cparams(*sem):
    return pltpu.CompilerParams(dimension_semantics=sem, vmem_limit_bytes=VMEM_LIMIT_BYTES)


def _resident(shape):
    nd = len(shape)
    return pl.BlockSpec(shape, lambda *_: (0,) * nd, pipeline_mode=pl.Buffered(1))


def _split3(x):
    p1 = x.astype(BF16)
    r1 = x - p1.astype(F32)
    p2 = r1.astype(BF16)
    p3 = (r1 - p2.astype(F32)).astype(BF16)
    return jnp.concatenate([p1, p2, p3], axis=1)


def _rms(x):
    return x * lax.rsqrt(jnp.mean(x * x, axis=-1, keepdims=True) + RMS_EPS)


def _mod_kernel(ct_ref, w_ref, b_ref, o_ref, *, batch):
    rows = []
    for b in range(batch):
        col = ct_ref[:, b:b + 1]
        rows.append(jnp.sum((col * jax.nn.sigmoid(col)) * w_ref[...], axis=0, keepdims=True) + b_ref[...])
    rows.append(jnp.zeros((SUBLANES - batch, rows[0].shape[1]), F32))
    o_ref[...] = jnp.concatenate(rows, axis=0)


def _mod(c, w_ada, b_ada, layer):
    bsz, d = c.shape
    n = w_ada.shape[2]
    tn = 1024
    out = pl.pallas_call(
        functools.partial(_mod_kernel, batch=bsz),
        grid=(n // tn,),
        in_specs=[pl.BlockSpec((d, bsz), lambda j: (0, 0)),
                  pl.BlockSpec((None, d, tn), lambda j: (layer, 0, j)),
                  pl.BlockSpec((1, tn), lambda j: (0, j))],
        out_specs=pl.BlockSpec((SUBLANES, tn), lambda j: (0, j)),
        out_shape=jax.ShapeDtypeStruct((SUBLANES, n), F32),
        compiler_params=_cparams("arbitrary"),
        name="mod",
    )(c.T, w_ada, b_ada[None])
    mod = out[:bsz].reshape(bsz, N_MOD, d)
    return jnp.pad(mod, ((0, 0), (0, SUBLANES - N_MOD), (0, 0)))


def _conv_module(a, first_of_seq, w_ref, b_ref, g_ref, beta_ref, xs_scr, ys_scr):
    ts = a.shape[0]
    xs_scr[0:CONV_HALO, :] = jnp.where(first_of_seq, 0.0, xs_scr[ts:ts + CONV_HALO, :])
    xs_scr[CONV_HALO:, :] = a
    acc = jnp.zeros(a.shape, F32) + b_ref[...]
    base = CONV_HALO - (CONV_KERNEL - 1)
    for rho in range(SUBLANES):
        offs = [o for o in range(base, base + CONV_KERNEL) if o % SUBLANES == rho]
        if not offs:
            continue
        rows = offs[-1] + ts - rho
        ys_scr[0:rows, :] = xs_scr[rho:rho + rows, :]
        for o in offs:
            lo = o - rho
            acc = acc + w_ref[o - base:o - base + 1, :] * ys_scr[lo:lo + ts, :]
    mu = jnp.mean(acc, axis=-1, keepdims=True)
    cen = acc - mu
    var = jnp.mean(cen * cen, axis=-1, keepdims=True)
    y = cen * lax.rsqrt(var + LN_EPS) * g_ref[...] + beta_ref[...]
    return (y * jax.nn.sigmoid(y)).astype(BF16)


def _inproj_kernel(x_ref, mod_ref, g_ref, wa_ref, wqkv_ref, wf_ref, wg_ref, bf_ref,
                   tri_ref, eq_ref, ek_ref, oneq_ref, onek_ref, cw_ref, cb_ref, cg_ref, cbeta_ref,
                   act_ref, q_ref, k_ref, v_ref, aq_ref, ak_ref, tot_ref, sgc_ref, sga_ref,
                   carry_scr, xs_scr, ys_scr, *, conv_width, d_model, tiles_per_seq):
    first_of_seq = pl.program_id(0) % tiles_per_seq == 0

    @pl.when(pl.program_id(0) == 0)
    def _():
        xs_scr[...] = jnp.zeros(xs_scr.shape, F32)
        carry_scr[...] = jnp.zeros(carry_scr.shape, F32)

    x = x_ref[...]
    h = _rms(x) * g_ref[...] * (1.0 + mod_ref[1:2, :]) + mod_ref[0:1, :]
    hb = h.astype(BF16)

    u = jnp.dot(hb, wa_ref[...], preferred_element_type=F32)
    a = u[:, :conv_width] * jax.nn.sigmoid(u[:, conv_width:])
    act_ref[...] = _conv_module(a, first_of_seq, cw_ref, cb_ref, cg_ref, cbeta_ref, xs_scr, ys_scr)

    fl = jnp.dot(hb, wf_ref[...], preferred_element_type=F32) + bf_ref[...]
    lane = lax.broadcasted_iota(I32, fl.shape, 1)
    log_f = -(jnp.maximum(-fl, 0.0) + jnp.log1p(jnp.exp(-jnp.abs(fl))))
    log_f = jnp.where(lane < N_HEADS, log_f, 0.0)
    cs = jnp.dot(tri_ref[...], _split3(log_f), preferred_element_type=F32)
    cum = (cs[:, :LANES] + cs[:, LANES:2 * LANES] + cs[:, 2 * LANES:]) * LOG2_E
    tm = cum.shape[0]

    pieces = []
    for sb in range(tm // ATTN_BK):
        blk = cum[sb * ATTN_BK:(sb + 1) * ATTN_BK, :]
        if sb > 0:
            blk = blk - cum[sb * ATTN_BK - 1:sb * ATTN_BK, :]
        tot_ref[sb] = blk[ATTN_BK - 1:, :]
        pieces.append(blk)
    cum_k = pieces[0] if len(pieces) == 1 else jnp.concatenate(pieces, axis=0)

    first_of_block = pl.program_id(0) % (ATTN_BQ // tm) == 0
    cum_q = cum + jnp.where(first_of_block, 0.0, carry_scr[0:1, :])
    carry_scr[...] = jnp.broadcast_to(cum_q[tm - 1:, :], carry_scr.shape)

    qkv = jnp.dot(hb, wqkv_ref[...], preferred_element_type=F32)
    aw = qkv.shape[1] // 3
    q_ref[...] = (qkv[:, :aw] * (HEAD_DIM ** -0.5 * LOG2_E)).astype(BF16)
    k_ref[...] = qkv[:, aw:2 * aw].astype(BF16)
    v_ref[...] = qkv[:, 2 * aw:].astype(BF16)
    aq_ref[...] = (jnp.dot(_split3(cum_q), eq_ref[...], preferred_element_type=F32) + oneq_ref[...]).astype(BF16)
    ak_ref[...] = (jnp.dot(_split3(cum_k), ek_ref[...], preferred_element_type=F32) + onek_ref[...]).astype(BF16)

    gts = jnp.dot(hb, wg_ref[...], preferred_element_type=F32)
    sgc_ref[...] = jax.nn.sigmoid(gts[:, :d_model]).astype(BF16)
    sga_ref[...] = jax.nn.sigmoid(gts[:, d_model:]).astype(BF16)


def _aug_constants():
    rows = jnp.arange(N_SPLIT * LANES)
    piece, head = rows // LANES, rows % LANES
    cols = jnp.arange(N_HEADS * HEAD_DIM)
    pair, off = cols // LANES, cols % LANES
    chead = 2 * pair + jnp.where(off >= HEAD_DIM, 0, 1)
    pos = off % HEAD_DIM
    valid = (head[:, None] < N_HEADS) & (head[:, None] == chead[None, :])
    eq = jnp.where(valid & (pos[None, :] == piece[:, None]), 1.0, 0.0).astype(BF16)
    ek = jnp.where(valid & (pos[None, :] == piece[:, None] + N_SPLIT), -1.0, 0.0).astype(BF16)
    oneq = jnp.where((pos >= N_SPLIT) & (pos < 2 * N_SPLIT), 1.0, 0.0).astype(F32)[None]
    onek = jnp.where(pos < N_SPLIT, 1.0, 0.0).astype(F32)[None]
    return eq, ek, oneq, onek


def _inproj(x2, mod8, norm_g, w_in, b_forget, conv_w, conv_b, conv_ln_g, conv_ln_b, seq):
    t, d = x2.shape
    tm = INPROJ_TILE
    nsub = tm // ATTN_BK
    cw = d // 2
    aw = N_HEADS * HEAD_DIM
    o0 = 2 * cw
    wa = w_in[:, :o0].astype(BF16)
    o1 = o0 + 3 * aw
    wqkv = w_in[:, o0:o1].astype(BF16)
    wf = jnp.pad(w_in[:, o1:o1 + N_HEADS], ((0, 0), (0, LANES - N_HEADS))).astype(BF16)
    bfp = jnp.pad(b_forget, (0, LANES - N_HEADS))[None]
    wg = w_in[:, o1 + N_HEADS:].astype(BF16)
    r = jnp.arange(tm)
    tri = (r[None, :] <= r[:, None]).astype(BF16)
    eq, ek, oneq, onek = _aug_constants()
    tiles_per_seq = seq // tm
    row = lambda i: (i, 0)
    outs = pl.pallas_call(
        functools.partial(_inproj_kernel, conv_width=cw, d_model=d, tiles_per_seq=tiles_per_seq),
        grid=(t // tm,),
        in_specs=[pl.BlockSpec((tm, d), row),
                  pl.BlockSpec((None, SUBLANES, d), lambda i: (i // tiles_per_seq, 0, 0)),
                  _resident((1, d)),
                  _resident(wa.shape), _resident(wqkv.shape),
                  _resident(wf.shape), _resident(wg.shape), _resident(bfp.shape),
                  _resident(tri.shape), _resident(eq.shape), _resident(ek.shape),
                  _resident(oneq.shape), _resident(onek.shape),
                  _resident(conv_w.shape), _resident((1, cw)), _resident((1, cw)), _resident((1, cw))],
        out_specs=[pl.BlockSpec((tm, cw), row)] + [pl.BlockSpec((tm, aw), row)] * 5 + [
                   pl.BlockSpec((nsub, 1, LANES), lambda i: (i, 0, 0)),
                   pl.BlockSpec((tm, d), row), pl.BlockSpec((tm, d), row)],
        out_shape=[jax.ShapeDtypeStruct((t, cw), BF16)] + [jax.ShapeDtypeStruct((t, aw), BF16)] * 5 + [
                   jax.ShapeDtypeStruct((t // ATTN_BK, 1, LANES), F32),
                   jax.ShapeDtypeStruct((t, d), BF16), jax.ShapeDtypeStruct((t, d), BF16)],
        scratch_shapes=[pltpu.VMEM((SUBLANES, LANES), F32),
                        pltpu.VMEM((CONV_HALO + tm, cw), F32), pltpu.VMEM((CONV_HALO + tm, cw), F32)],
        compiler_params=_cparams("arbitrary"),
        name="inproj",
    )(x2, mod8, norm_g[None], wa, wqkv, wf, wg, bfp, tri, eq, ek, oneq, onek,
      conv_w, conv_b[None], conv_ln_g[None], conv_ln_b[None])
    return outs


def _attn_kernel(q_ref, aq_ref, k_ref, ak_ref, v_ref, tot_ref, o_ref, qop_scr, m_scr, acc_scr):
    bq = q_ref.shape[0]
    bk = ATTN_BK
    ratio = bq // bk
    qi = pl.program_id(2)
    nt = (((1,), (1,)), ((), ()))
    heads = range(HEADS_PER_STEP)
    zero = jnp.zeros((1, LANES), F32)

    def splice(hh, main, extra):
        first = lax.broadcasted_iota(I32, main.shape, 1) < HEAD_DIM
        return jnp.where(first, main, extra) if hh == 0 else jnp.where(first, extra, main)

    for hh in heads:
        qop_scr[hh] = splice(hh, q_ref[...], aq_ref[...])

    def scores(hh, kb, row0):
        start = pl.multiple_of(kb * bk, bk)
        k = splice(hh, k_ref[pl.ds(start, bk), :], ak_ref[pl.ds(start, bk), :])
        v = v_ref[pl.ds(start, bk), :]
        v = splice(hh, v, jnp.ones(v.shape, v.dtype))
        s = lax.dot_general(qop_scr[hh, row0:, :], k, nt, preferred_element_type=F32)
        return s, v

    def update(hh, s, v, delta, row0):
        m_prev = m_scr[hh, row0:, :]
        m_new = jnp.maximum(m_prev, jnp.max(s, axis=1, keepdims=True) + delta)
        shift = m_new - delta
        p = jnp.exp2(s - shift[:, :1])
        alpha = jnp.exp2(m_prev - m_new)
        pv = jnp.dot(p.astype(BF16), v, preferred_element_type=F32)
        acc_scr[hh, row0:, :] = alpha * acc_scr[hh, row0:, :] + pv
        m_scr[hh, row0:, :] = m_new

    for hh in heads:
        delta = zero
        for sb in range(ratio):
            kb = qi * ratio + sb
            row0 = sb * bk
            s, v = scores(hh, kb, row0)
            row = lax.broadcasted_iota(I32, s.shape, 0)
            col = lax.broadcasted_iota(I32, s.shape, 1)
            s = jnp.where(col <= row, s, -jnp.inf)
            if sb == 0:
                m0 = jnp.max(s, axis=1, keepdims=True)
                p = jnp.exp2(s - m0)
                m_scr[hh] = jnp.broadcast_to(m0, m_scr.shape[1:])
                acc_scr[hh] = jnp.dot(p.astype(BF16), v, preferred_element_type=F32)
            else:
                update(hh, s, v, delta, row0)
            delta = delta - tot_ref[hh, pl.ds(kb, 1), :]

    def body(step, deltas):
        kb = qi * ratio - 1 - step
        new_deltas = []
        for hh in heads:
            delta = deltas[hh] + tot_ref[hh, pl.ds(kb, 1), :]
            s, v = scores(hh, kb, 0)
            update(hh, s, v, delta, 0)
            new_deltas.append(delta)
        return tuple(new_deltas)

    lax.fori_loop(0, qi * ratio, body, tuple(zero for _ in heads))
    acc0, acc1 = acc_scr[0], acc_scr[1]
    out0 = acc0 / acc0[:, HEAD_DIM:HEAD_DIM + 1]
    out1 = acc1 / acc1[:, 0:1]
    o_ref[...] = splice(0, out0, out1).astype(BF16)


def _attention(q, aq, k, ak, v, tot, bsz, seq):
    t, aw = q.shape
    bq = ATTN_BQ
    nq = seq // bq
    nk = seq // ATTN_BK
    assert HEADS_PER_STEP * HEAD_DIM == LANES
    tot = tot.reshape(bsz, nk, LANES)[:, :, :N_HEADS].transpose(0, 2, 1)
    tot = jnp.broadcast_to(tot[..., None], (bsz, N_HEADS, nk, LANES))
    qblk = pl.BlockSpec((bq, LANES), lambda b, h, i: (b * nq + i, h))
    kblk = pl.BlockSpec((None, seq, LANES), lambda b, h, i: (b, 0, h))
    seq3 = lambda a: a.reshape(bsz, seq, aw)
    return pl.pallas_call(
        _attn_kernel,
        grid=(bsz, N_HEADS // HEADS_PER_STEP, nq),
        in_specs=[qblk, qblk, kblk, kblk, kblk,
                  pl.BlockSpec((None, HEADS_PER_STEP, nk, LANES), lambda b, h, i: (b, h, 0, 0))],
        out_specs=qblk,
        out_shape=jax.ShapeDtypeStruct((t, aw), BF16),
        scratch_shapes=[pltpu.VMEM((HEADS_PER_STEP, bq, LANES), BF16),
                        pltpu.VMEM((HEADS_PER_STEP, bq, LANES), F32),
                        pltpu.VMEM((HEADS_PER_STEP, bq, LANES), F32)],
        compiler_params=_cparams("arbitrary", "arbitrary", "arbitrary"),
        name="attn",
    )(q, aq, seq3(k), seq3(ak), seq3(v), tot)


def _pack_bf16_pair(lo, hi):
    lo_bits = pltpu.bitcast(lo.astype(BF16).astype(F32), U32)
    hi_bits = pltpu.bitcast(hi.astype(BF16).astype(F32), U32)
    return (hi_bits & jnp.uint32(0xFFFF0000)) | (lo_bits >> 16)


def _unpack_bf16_pair(w):
    lo = pltpu.bitcast(w << 16, F32)
    hi = pltpu.bitcast(w & jnp.uint32(0xFFFF0000), F32)
    return jnp.concatenate([lo, hi], axis=1)


def _expert_onehots(ids):
    lane = lax.broadcasted_iota(I32, ids.shape, 1)
    return lane, [lane == ids[:, kk:kk + 1] for kk in range(TOP_K)]


def _route_tile(ids, gates, carry, tri, utri):
    lane, hots = _expert_onehots(ids)
    multi = jnp.zeros(lane.shape, F32)
    for hot in hots:
        multi = multi + jnp.where(hot, 1.0, 0.0)
    before = jnp.dot(tri, multi.astype(BF16), preferred_element_type=F32)
    units = jnp.ceil(jnp.sum(multi, axis=0, keepdims=True) * (1.0 / CHUNK))
    units8 = jnp.broadcast_to(units, (SUBLANES, LANES)).astype(BF16)
    lstart = jnp.dot(units8, utri, preferred_element_type=F32)[0:1, :] * CHUNK
    csize = units * CHUNK
    slot = jnp.zeros(lane.shape, F32)
    for kk, hot in enumerate(hots):
        sl = jnp.sum(jnp.where(hot, lstart + before, 0.0), axis=1, keepdims=True)
        slot = jnp.where(lane == kk, sl, slot)
    row = lax.broadcasted_iota(I32, (SUBLANES, LANES), 0)
    meta = jnp.where(row == 0, csize, jnp.where(row == 1, lstart, jnp.where(row == 2, carry, 0.0)))
    slott = slot.T[0:SUBLANES, :].astype(I32)
    gatest = gates.T[0:SUBLANES, :]
    return slot.astype(I32), slott, gatest, meta, carry + csize


def _merge_kernel(x_ref, act_ref, attn_ref, sgc_ref, sga_ref, mod_ref, g2_ref, wco_ref, wao_ref, wout_ref,
                  wr_ref, br_ref, tri_ref, utri_ref,
                  x1_ref, h2_ref, slot_ref, slott_ref, gatest_ref, meta_ref, cnt_ref, carry_scr):
    @pl.when(pl.program_id(0) == 0)
    def _():
        carry_scr[...] = jnp.zeros(carry_scr.shape, F32)

    conv_out = jnp.dot(act_ref[...], wco_ref[...], preferred_element_type=F32)
    attn_out = jnp.dot(attn_ref[...], wao_ref[...], preferred_element_type=F32)
    mixed = sgc_ref[...].astype(F32) * conv_out + sga_ref[...].astype(F32) * attn_out
    upd = jnp.dot(mixed.astype(BF16), wout_ref[...], preferred_element_type=F32)
    x1 = x_ref[...] + mod_ref[2:3, :] * upd
    x1_ref[...] = x1
    h2 = _rms(x1) * g2_ref[...] * (1.0 + mod_ref[4:5, :]) + mod_ref[3:4, :]
    h2_hi = h2.astype(BF16)
    h2_ref[...] = h2_hi
    h2_lo = (h2 - h2_hi.astype(F32)).astype(BF16)
    h2_cat = jnp.concatenate([h2_hi, h2_lo, h2_hi], axis=1)
    logits = jnp.dot(h2_cat, wr_ref[...], preferred_element_type=F32) + br_ref[...]
    lane = lax.broadcasted_iota(I32, logits.shape, 1)
    work = jnp.where(lane < N_EXPERTS, logits, -jnp.inf)
    ids = jnp.zeros(logits.shape, I32)
    vals = []
    for kk in range(TOP_K):
        mx = jnp.max(work, axis=1, keepdims=True)
        idx = jnp.min(jnp.where(work == mx, lane, LANES), axis=1, keepdims=True)
        ids = jnp.where(lane == kk, idx, ids)
        vals.append(mx)
        work = jnp.where(lane == idx, -jnp.inf, work)
    exps = [jnp.exp(vv - vals[0]) for vv in vals]
    den = exps[0] + exps[1] + exps[2] + exps[3]
    gates = jnp.zeros(logits.shape, F32)
    for kk in range(TOP_K):
        gates = jnp.where(lane == kk, exps[kk] / den, gates)

    carry = carry_scr[0:1, :]
    tt = TOKEN_TILE
    for sub in range(logits.shape[0] // tt):
        rows = slice(sub * tt, (sub + 1) * tt)
        slot, slott, gatest, meta, carry = _route_tile(ids[rows, :], gates[rows, :], carry,
                                                       tri_ref[...], utri_ref[...])
        slot_ref[rows, :] = slot
        slott_ref[sub * SUBLANES:(sub + 1) * SUBLANES, :] = slott
        gatest_ref[sub * SUBLANES:(sub + 1) * SUBLANES, :] = gatest
        meta_ref[sub] = meta
    carry_scr[...] = jnp.broadcast_to(carry, carry_scr.shape)
    cnt_ref[...] = jnp.broadcast_to(carry, cnt_ref.shape)


def _merge(x2, act, attn, sgc, sga, mod8, norm2_g, w_conv_out, w_attn_out, w_out, w_router, b_router, seq):
    t, d = x2.shape
    tm = MERGE_TILE
    tt = TOKEN_TILE
    sub = tm // tt
    nt = t // tt
    cw = act.shape[1]
    sw = attn.shape[1]
    wco = w_conv_out.astype(BF16)
    wao = w_attn_out.astype(BF16)
    wout = w_out.astype(BF16)
    wr = jnp.pad(w_router, ((0, 0), (0, LANES - N_EXPERTS)))
    wr_hi = wr.astype(BF16)
    wr_lo = (wr - wr_hi.astype(F32)).astype(BF16)
    wr = jnp.concatenate([wr_hi, wr_hi, wr_lo], axis=0)
    br = jnp.pad(b_router, (0, LANES - N_EXPERTS))[None]
    r = jnp.arange(tt)
    tri = (r[None, :] < r[:, None]).astype(BF16)
    e = jnp.arange(LANES)
    utri = (e[:, None] < e[None, :]).astype(BF16)
    tiles_per_seq = seq // tm
    row = lambda i: (i, 0)
    return pl.pallas_call(
        _merge_kernel,
        grid=(t // tm,),
        in_specs=[pl.BlockSpec((tm, d), row), pl.BlockSpec((tm, cw), row), pl.BlockSpec((tm, sw), row),
                  pl.BlockSpec((tm, d), row), pl.BlockSpec((tm, d), row),
                  pl.BlockSpec((None, SUBLANES, d), lambda i: (i // tiles_per_seq, 0, 0)),
                  _resident((1, d)), _resident(wco.shape), _resident(wao.shape), _resident(wout.shape),
                  _resident(wr.shape), _resident(br.shape), _resident(tri.shape), _resident(utri.shape)],
        out_specs=[pl.BlockSpec((tm, d), row), pl.BlockSpec((tm, d), row),
                   pl.BlockSpec((tm, LANES), row),
                   pl.BlockSpec((sub * SUBLANES, tt), row), pl.BlockSpec((sub * SUBLANES, tt), row),
                   pl.BlockSpec((sub, SUBLANES, LANES), lambda i: (i, 0, 0)),
                   pl.BlockSpec((SUBLANES, LANES), lambda i: (0, 0))],
        out_shape=[jax.ShapeDtypeStruct((t, d), F32), jax.ShapeDtypeStruct((t, d), BF16),
                   jax.ShapeDtypeStruct((t, LANES), I32),
                   jax.ShapeDtypeStruct((nt * SUBLANES, tt), I32), jax.ShapeDtypeStruct((nt * SUBLANES, tt), F32),
                   jax.ShapeDtypeStruct((nt, SUBLANES, LANES), F32),
                   jax.ShapeDtypeStruct((SUBLANES, LANES), F32)],
        scratch_shapes=[pltpu.VMEM((SUBLANES, LANES), F32)],
        compiler_params=_cparams("arbitrary"),
        name="merge",
    )(x2, act, attn, sgc, sga, mod8, norm2_g[None], wco, wao, wout, wr, br, tri, utri)


def _dest_kernel(cnt_ref, meta_ref, tri_ref, gd_ref, blk_ref):
    cnt = cnt_ref[...]
    padded = jnp.ceil(cnt * (1.0 / EXPERT_BLOCK)) * EXPERT_BLOCK
    pend = jnp.dot(padded, tri_ref[...], preferred_element_type=F32, precision=HIGHEST)
    pstart = (pend - padded)[0:1, :]

    grow = lax.broadcasted_iota(I32, (SUBLANES, LANES), 0)
    glane = lax.broadcasted_iota(I32, (SUBLANES, LANES), 1)
    gpos = ((grow * LANES + glane) * CHUNK).astype(F32)
    last = jnp.logical_and(grow == SUBLANES - 1, glane == LANES - 1)
    for tile in range(meta_ref.shape[0]):
        meta = meta_ref[tile]
        csize, lstart, seg_off = meta[0:1, :], meta[1:2, :], meta[2:3, :]
        lend = lstart + csize
        base = pstart + seg_off - lstart
        owner = jnp.zeros(gpos.shape, I32)
        for e in range(N_EXPERTS):
            owner = owner + jnp.where(gpos >= lend[0:1, e:e + 1], 1, 0)
        gdst = gpos
        for e in range(N_EXPERTS):
            gdst = gdst + jnp.where(owner == e, base[0:1, e:e + 1], 0.0)
        n_groups = lend[0:1, N_EXPERTS - 1:N_EXPERTS] * (1.0 / CHUNK)
        gd_ref[tile] = jnp.where(last, n_groups, gdst).astype(I32)

    @pl.when(pl.program_id(0) == 0)
    def _():
        shape = blk_ref.shape[1:]
        pos = (lax.broadcasted_iota(I32, shape, 0) * LANES + lax.broadcasted_iota(I32, shape, 1))
        pos = pos.astype(F32) * EXPERT_BLOCK
        be = jnp.zeros(shape, I32)
        for e in range(N_EXPERTS):
            be = be + jnp.where(pos >= pend[0:1, e:e + 1], 1, 0)
        be = jnp.minimum(be, N_EXPERTS - 1)
        blk_ref[0] = be
        nact = pend[0:1, N_EXPERTS - 1:N_EXPERTS] * (1.0 / EXPERT_BLOCK)
        blk_ref[1] = jnp.broadcast_to(nact, shape).astype(I32)
        blk_ref[2] = jnp.broadcast_to(pend[0:1, :], shape).astype(I32)
        used_end = pstart + cnt[0:1, :]
        block_end = jnp.zeros(shape, F32)
        for e in range(N_EXPERTS):
            block_end = block_end + jnp.where(be == e, used_end[0:1, e:e + 1], 0.0)
        valid = jnp.clip(block_end - pos, 0.0, float(EXPERT_BLOCK))
        blk_ref[3] = (jnp.ceil(valid * (1.0 / EXPERT_ROW_STEP)) * EXPERT_ROW_STEP).astype(I32)


def _dest(cnt, meta):
    nt = meta.shape[0]
    per_step = DEST_TILES_PER_STEP
    r = jnp.arange(LANES)
    tri = (r[:, None] <= r[None, :]).astype(F32)
    return pl.pallas_call(
        _dest_kernel,
        grid=(nt // per_step,),
        in_specs=[pl.BlockSpec((SUBLANES, LANES), lambda i: (0, 0)),
                  pl.BlockSpec((per_step, SUBLANES, LANES), lambda i: (i, 0, 0)),
                  pl.BlockSpec((LANES, LANES), lambda i: (0, 0))],
        out_specs=[pl.BlockSpec((per_step, SUBLANES, LANES), lambda i: (i, 0, 0)),
                   pl.BlockSpec((4, SUBLANES, LANES), lambda i: (0, 0, 0))],
        out_shape=[jax.ShapeDtypeStruct((nt, SUBLANES, LANES), I32),
                   jax.ShapeDtypeStruct((4, SUBLANES, LANES), I32)],
        compiler_params=_cparams("arbitrary"),
        name="dest",
    )(cnt, meta, tri)


def _row_copy(src, dst, sem):
    return pltpu.make_async_copy(src, dst, sem)


def _dispatch_kernel(pend_ref, na_ref, gd_ref, slott_ref, gatest_ref, h2_ref, xs_ref, zero_ref, sorted_scr, sem):
    tm = h2_ref.shape[0]
    bm = zero_ref.shape[0]
    nb = xs_ref.shape[0] // bm

    @pl.when(pl.program_id(0) == 0)
    def _():
        zero_ref[...] = jnp.zeros(zero_ref.shape, zero_ref.dtype)

        def zero_block(start):
            return _row_copy(zero_ref, xs_ref.at[pl.ds(pl.multiple_of(start, bm), bm)], sem)

        def nonempty(e):
            return pend_ref[e] > (pend_ref[e - 1] if e > 0 else 0)

        for e in range(N_EXPERTS):
            @pl.when(nonempty(e))
            def _():
                zero_block(pend_ref[e] - bm).start()

        def tail(b, carry):
            zero_block(b * bm).start()
            return carry

        lax.fori_loop(na_ref[0], nb, tail, 0)
        for e in range(N_EXPERTS):
            @pl.when(nonempty(e))
            def _():
                zero_block(0).wait()

        def tail_wait(b, carry):
            zero_block(0).wait()
            return carry

        lax.fori_loop(na_ref[0], nb, tail_wait, 0)

    srow = lax.broadcasted_iota(I32, (SLOT_ROWS, tm), 0)
    perm = jnp.zeros((SLOT_ROWS, tm), F32)
    gate = jnp.zeros((SLOT_ROWS, 1), F32)
    for kk in range(TOP_K):
        hit = srow == slott_ref[kk:kk + 1, :]
        perm = perm + jnp.where(hit, 1.0, 0.0)
        gate = gate + jnp.sum(jnp.where(hit, gatest_ref[kk:kk + 1, :], 0.0), axis=1, keepdims=True)
    perm = perm.astype(BF16)
    half = h2_ref.shape[1] // 2
    lo = jnp.dot(perm, h2_ref[:, :half], preferred_element_type=F32)
    hi = jnp.dot(perm, h2_ref[:, half:], preferred_element_type=F32)
    sorted_scr[:, :half] = pltpu.bitcast(hi, U32) | (pltpu.bitcast(lo, U32) >> 16)
    sorted_scr[:, half:] = pltpu.bitcast(jnp.broadcast_to(gate, (SLOT_ROWS, LANES)), U32)

    n_groups = gd_ref[GROUP_TABLE - 1]

    def group_copy(j, d):
        src = sorted_scr.at[pl.ds(pl.multiple_of(j * CHUNK, CHUNK), CHUNK)]
        return _row_copy(src, xs_ref.at[pl.ds(pl.multiple_of(d, CHUNK), CHUNK)], sem)

    def issue(j, carry):
        group_copy(j, gd_ref[j]).start()
        return carry

    def retire(j, carry):
        group_copy(0, 0).wait()
        return carry

    lax.fori_loop(0, n_groups, issue, 0)
    lax.fori_loop(0, n_groups, retire, 0)


def _dispatch(pend, n_active, gd_flat, slott, gatest, h2b, n_rows):
    t, d = h2b.shape
    w = d // 2 + LANES
    tm = TOKEN_TILE
    grid_spec = pltpu.PrefetchScalarGridSpec(
        num_scalar_prefetch=2,
        grid=(t // tm,),
        in_specs=[pl.BlockSpec((GROUP_TABLE,), lambda i, pe, na: (i,), memory_space=pltpu.SMEM),
                  pl.BlockSpec((SUBLANES, tm), lambda i, pe, na: (i, 0)),
                  pl.BlockSpec((SUBLANES, tm), lambda i, pe, na: (i, 0)),
                  pl.BlockSpec((tm, d), lambda i, pe, na: (i, 0))],
        out_specs=pl.BlockSpec(memory_space=pl.ANY),
        scratch_shapes=[pltpu.VMEM((EXPERT_BLOCK, w), U32), pltpu.VMEM((SLOT_ROWS, w), U32),
                        pltpu.SemaphoreType.DMA(())],
    )
    return pl.pallas_call(
        _dispatch_kernel,
        grid_spec=grid_spec,
        out_shape=jax.ShapeDtypeStruct((n_rows, w), U32),
        compiler_params=_cparams("arbitrary"),
        name="dispatch",
    )(pend, n_active, gd_flat, slott, gatest, h2b)


def _expert_kernel(be_ref, na_ref, rows_ref, xs_ref, w1_ref, b1_ref, w2_ref, b2_ref, y_ref, w1b_ref, w2b_ref,
                   *, d_expert):
    i = pl.program_id(0)
    active = i < na_ref[0]
    prev = be_ref[jnp.maximum(i - 1, 0)]
    new_expert = jnp.logical_or(i == 0, be_ref[i] != prev)
    bm, half = y_ref.shape

    @pl.when(jnp.logical_and(active, new_expert))
    def _():
        w1b_ref[...] = w1_ref[...].astype(BF16)
        w2b_ref[...] = w2_ref[...].astype(BF16)

    def compute(rows):
        x = _unpack_bf16_pair(xs_ref[0:rows, :half]).astype(BF16)
        gate = pltpu.bitcast(xs_ref[0:rows, half:], F32)[:, 0:1]
        hc = jnp.dot(x, w1b_ref[...], preferred_element_type=F32) + b1_ref[...]
        g = jnp.minimum(hc[:, :d_expert], SWIGLU_LIMIT)
        u = jnp.clip(hc[:, d_expert:], -SWIGLU_LIMIT, SWIGLU_LIMIT)
        hm = (u + 1.0) * (g * jax.nn.sigmoid(SWIGLU_ALPHA * g))
        y = (jnp.dot(hm.astype(BF16), w2b_ref[...], preferred_element_type=F32) + b2_ref[...]) * gate
        y_ref[0:rows, :] = _pack_bf16_pair(y[:, :half], y[:, half:])
        if rows < bm:
            y_ref[rows:, :] = jnp.zeros((bm - rows, half), U32)

    for rows in range(EXPERT_ROW_STEP, bm + 1, EXPERT_ROW_STEP):
        @pl.when(jnp.logical_and(active, rows_ref[i] == rows))
        def _():
            compute(rows)

    @pl.when(jnp.logical_or(jnp.logical_not(active), rows_ref[i] == 0))
    def _():
        y_ref[...] = jnp.zeros(y_ref.shape, U32)


def _experts(block_e, n_active, block_rows, xs, w1, b1, w2, b2):
    n_rows, w = xs.shape
    n_exp, d, two_de = w1.shape
    de = two_de // 2
    bm = EXPERT_BLOCK
    nb = n_rows // bm

    def blk(i, na):
        return jnp.maximum(jnp.minimum(i, na[0] - 1), 0)

    def weights(i, be, na, br):
        return (be[blk(i, na)], 0, 0)

    grid_spec = pltpu.PrefetchScalarGridSpec(
        num_scalar_prefetch=3,
        grid=(nb,),
        in_specs=[pl.BlockSpec((bm, w), lambda i, be, na, br: (blk(i, na), 0)),
                  pl.BlockSpec((None, d, two_de), weights), pl.BlockSpec((None, 1, two_de), weights),
                  pl.BlockSpec((None, de, d), weights), pl.BlockSpec((None, 1, d), weights)],
        out_specs=pl.BlockSpec((bm, d // 2), lambda i, be, na, br: (i, 0)),
        scratch_shapes=[pltpu.VMEM((d, two_de), BF16), pltpu.VMEM((de, d), BF16)],
    )
    return pl.pallas_call(
        functools.partial(_expert_kernel, d_expert=de),
        grid_spec=grid_spec,
        out_shape=jax.ShapeDtypeStruct((n_rows, d // 2), U32),
        compiler_params=_cparams("arbitrary"),
        name="expert",
    )(block_e, n_active, block_rows, xs, w1, b1[:, None, :], w2, b2[:, None, :])


def _combine_kernel(gd_ref, gd_next_ref, slot_ref, x1_ref, mod_ref, fg_ref, yb_ref, o_ref, sorted_scr, sem):
    tm = x1_ref.shape[0]
    step = pl.program_id(0)
    cur = step % 2

    def group_copy(buf, j, d):
        dst = sorted_scr.at[buf, pl.ds(pl.multiple_of(j * CHUNK, CHUNK), CHUNK)]
        return _row_copy(yb_ref.at[pl.ds(pl.multiple_of(d, CHUNK), CHUNK)], dst, sem.at[buf])

    def fetch(table_ref, buf):
        def issue(j, carry):
            group_copy(buf, j, table_ref[j]).start()
            return carry
        lax.fori_loop(0, table_ref[GROUP_TABLE - 1], issue, 0)

    @pl.when(step == 0)
    def _():
        sorted_scr[...] = jnp.zeros(sorted_scr.shape, sorted_scr.dtype)
        fetch(gd_ref, 0)

    @pl.when(step + 1 < pl.num_programs(0))
    def _():
        fetch(gd_next_ref, 1 - cur)

    def retire(j, carry):
        group_copy(cur, 0, 0).wait()
        return carry

    lax.fori_loop(0, gd_ref[GROUP_TABLE - 1], retire, 0)

    slot = slot_ref[...]
    scol = lax.broadcasted_iota(I32, (tm, SLOT_ROWS), 1)
    pick = jnp.zeros((tm, SLOT_ROWS), F32)
    for kk in range(TOP_K):
        pick = pick + jnp.where(scol == slot[:, kk:kk + 1], 1.0, 0.0)
    pick = pick.astype(BF16)
    words = sorted_scr[cur]
    lo = pltpu.bitcast(words << 16, F32).astype(BF16)
    hi = pltpu.bitcast(words & jnp.uint32(0xFFFF0000), F32).astype(BF16)
    y = jnp.concatenate([jnp.dot(pick, lo, preferred_element_type=F32),
                         jnp.dot(pick, hi, preferred_element_type=F32)], axis=1)
    x2 = x1_ref[...] + mod_ref[5:6, :] * y
    o_ref[...] = _rms(x2) * fg_ref[...]


def _combine(gd_flat, slot, x1, mod8, final_g, yb, seq):
    t, d = x1.shape
    tm = TOKEN_TILE
    nt = t // tm
    tiles_per_seq = seq // tm
    return pl.pallas_call(
        _combine_kernel,
        grid=(nt,),
        in_specs=[pl.BlockSpec((GROUP_TABLE,), lambda i: (i,), memory_space=pltpu.SMEM),
                  pl.BlockSpec((GROUP_TABLE,), lambda i: (jnp.minimum(i + 1, nt - 1),), memory_space=pltpu.SMEM),
                  pl.BlockSpec((tm, LANES), lambda i: (i, 0)),
                  pl.BlockSpec((tm, d), lambda i: (i, 0)),
                  pl.BlockSpec((None, SUBLANES, d), lambda i: (i // tiles_per_seq, 0, 0)),
                  pl.BlockSpec((1, d), lambda i: (0, 0)),
                  pl.BlockSpec(memory_space=pl.ANY)],
        out_specs=pl.BlockSpec((tm, d), lambda i: (i, 0)),
        out_shape=jax.ShapeDtypeStruct((t, d), F32),
        scratch_shapes=[pltpu.VMEM((2, SLOT_ROWS, d // 2), U32), pltpu.SemaphoreType.DMA((2,))],
        compiler_params=_cparams("arbitrary"),
        name="combine",
    )(gd_flat, gd_flat, slot, x1, mod8, final_g[None], yb)


def kernel(x, c, w_ada, b_ada, norm1_g, w_in, b_forget, conv_w, conv_b, conv_ln_g, conv_ln_b, w_conv_out,
           w_attn_out, w_out, norm2_g, w_router, b_router, w_exp_in, b_exp_in, w_exp_out, b_exp_out, final_g):
    bsz, seq, d = x.shape
    t = bsz * seq
    depth = w_ada.shape[0]
    xf = x.reshape(t, d)
    n_rows = t * TOP_K + (t // TOKEN_TILE) * N_EXPERTS * (CHUNK - 1) + N_EXPERTS * (EXPERT_BLOCK - CHUNK)
    n_rows = -(-n_rows // EXPERT_BLOCK) * EXPERT_BLOCK
    n_blocks = n_rows // EXPERT_BLOCK
    for l in range(depth):
        mod8 = _mod(c, w_ada, b_ada[l], l)
        act, q, k, v, aq, ak, tot, sgc, sga = _inproj(xf, mod8, norm1_g[l], w_in[l], b_forget[l], conv_w[l],
                                                      conv_b[l], conv_ln_g[l], conv_ln_b[l], seq)
        attn = _attention(q, aq, k, ak, v, tot, bsz, seq)
        x1, h2b, slot, slott, gatest, meta, cnt = _merge(xf, act, attn, sgc, sga, mod8, norm2_g[l], w_conv_out[l],
                                                         w_attn_out[l], w_out[l], w_router[l], b_router[l], seq)
        gd, blk = _dest(cnt, meta)
        gd_flat = gd.reshape(-1)
        block_e = blk[0].reshape(-1)[:n_blocks]
        n_active = blk[1, 0, :1]
        pend = blk[2, 0, :N_EXPERTS]
        xs = _dispatch(pend, n_active, gd_flat, slott, gatest, h2b, n_rows)
        block_rows = blk[3].reshape(-1)[:n_blocks]
        yb = _experts(block_e, n_active, block_rows, xs, w_exp_in[l], b_exp_in[l], w_exp_out[l], b_exp_out[l])
        assert depth == 1
        xf = _combine(gd_flat, slot, x1, mod8, final_g, yb, seq)
    return xf.reshape(bsz, seq, d)
```

```python
import functools

import jax
import jax.numpy as jnp
from jax import lax
from jax.experimental import pallas as pl
from jax.experimental.pallas import tpu as pltpu

F32 = jnp.float32
BF16 = jnp.bfloat16
I32 = jnp.int32
U32 = jnp.uint32
HIGHEST = lax.Precision.HIGHEST

LANES = 128
SUBLANES = 8
VMEM_LIMIT_BYTES = 56 * 1024 * 1024

N_HEADS = 8
HEAD_DIM = 64
CONV_KERNEL = 31
CONV_HALO = 32
N_EXPERTS = 32
TOP_K = 4
SWIGLU_LIMIT = 7.0
SWIGLU_ALPHA = 1.702
RMS_EPS = 1e-5
LN_EPS = 1e-5
N_MOD = 6
N_SPLIT = 3
LOG2_E = 1.4426950408889634
HEADS_PER_STEP = 2

INPROJ_TILE = 512
ATTN_BQ = 4096
ATTN_BK = 512
MERGE_TILE = 1024
TOKEN_TILE = 256
EXPERT_BLOCK = 1024
CHUNK = SUBLANES
SLOT_ROWS = 1280
GROUP_TABLE = SUBLANES * LANES
DEST_TILES_PER_STEP = 8
assert SLOT_ROWS >= TOKEN_TILE * TOP_K + N_EXPERTS * (CHUNK - 1) and SLOT_ROWS // CHUNK < GROUP_TABLE


def _cparams(*sem):
    return pltpu.CompilerParams(dimension_semantics=sem, vmem_limit_bytes=VMEM_LIMIT_BYTES)


def _resident(shape):
    nd = len(shape)
    return pl.BlockSpec(shape, lambda *_: (0,) * nd, pipeline_mode=pl.Buffered(1))


def _split3(x):
    p1 = x.astype(BF16)
    r1 = x - p1.astype(F32)
    p2 = r1.astype(BF16)
    p3 = (r1 - p2.astype(F32)).astype(BF16)
    return jnp.concatenate([p1, p2, p3], axis=1)


def _rms(x):
    return x * lax.rsqrt(jnp.mean(x * x, axis=-1, keepdims=True) + RMS_EPS)


def _mod_kernel(ct_ref, w_ref, b_ref, o_ref, *, batch):
    rows = []
    for b in range(batch):
        col = ct_ref[:, b:b + 1]
        rows.append(jnp.sum((col * jax.nn.sigmoid(col)) * w_ref[...], axis=0, keepdims=True) + b_ref[...])
    rows.append(jnp.zeros((SUBLANES - batch, rows[0].shape[1]), F32))
    o_ref[...] = jnp.concatenate(rows, axis=0)


def _mod(c, w_ada, b_ada, layer):
    bsz, d = c.shape
    n = w_ada.shape[2]
    tn = 1024
    out = pl.pallas_call(
        functools.partial(_mod_kernel, batch=bsz),
        grid=(n // tn,),
        in_specs=[pl.BlockSpec((d, bsz), lambda j: (0, 0)),
                  pl.BlockSpec((None, d, tn), lambda j: (layer, 0, j)),
                  pl.BlockSpec((1, tn), lambda j: (0, j))],
        out_specs=pl.BlockSpec((SUBLANES, tn), lambda j: (0, j)),
        out_shape=jax.ShapeDtypeStruct((SUBLANES, n), F32),
        compiler_params=_cparams("arbitrary"),
        name="mod",
    )(c.T, w_ada, b_ada[None])
    mod = out[:bsz].reshape(bsz, N_MOD, d)
    return jnp.pad(mod, ((0, 0), (0, SUBLANES - N_MOD), (0, 0)))


def _conv_module(a, first_of_seq, w_ref, b_ref, g_ref, beta_ref, xs_scr, ys_scr):
    ts = a.shape[0]
    xs_scr[0:CONV_HALO, :] = jnp.where(first_of_seq, 0.0, xs_scr[ts:ts + CONV_HALO, :])
    xs_scr[CONV_HALO:, :] = a
    acc = jnp.zeros(a.shape, F32) + b_ref[...]
    base = CONV_HALO - (CONV_KERNEL - 1)
    for rho in range(SUBLANES):
        offs = [o for o in range(base, base + CONV_KERNEL) if o % SUBLANES == rho]
        if not offs:
            continue
        rows = offs[-1] + ts - rho
        ys_scr[0:rows, :] = xs_scr[rho:rho + rows, :]
        for o in offs:
            lo = o - rho
            acc = acc + w_ref[o - base:o - base + 1, :] * ys_scr[lo:lo + ts, :]
    mu = jnp.mean(acc, axis=-1, keepdims=True)
    cen = acc - mu
    var = jnp.mean(cen * cen, axis=-1, keepdims=True)
    y = cen * lax.rsqrt(var + LN_EPS) * g_ref[...] + beta_ref[...]
    return (y * jax.nn.sigmoid(y)).astype(BF16)


def _inproj_kernel(x_ref, mod_ref, g_ref, wa_ref, wqkv_ref, wf_ref, wg_ref, bf_ref,
                   tri_ref, eq_ref, ek_ref, oneq_ref, onek_ref, cw_ref, cb_ref, cg_ref, cbeta_ref,
                   act_ref, q_ref, k_ref, v_ref, aq_ref, ak_ref, tot_ref, sgc_ref, sga_ref,
                   carry_scr, xs_scr, ys_scr, *, conv_width, d_model, tiles_per_seq):
    first_of_seq = pl.program_id(0) % tiles_per_seq == 0

    @pl.when(pl.program_id(0) == 0)
    def _():
        xs_scr[...] = jnp.zeros(xs_scr.shape, F32)
        carry_scr[...] = jnp.zeros(carry_scr.shape, F32)

    x = x_ref[...]
    h = _rms(x) * g_ref[...] * (1.0 + mod_ref[1:2, :]) + mod_ref[0:1, :]
    hb = h.astype(BF16)

    u = jnp.dot(hb, wa_ref[...], preferred_element_type=F32)
    a = u[:, :conv_width] * jax.nn.sigmoid(u[:, conv_width:])
    act_ref[...] = _conv_module(a, first_of_seq, cw_ref, cb_ref, cg_ref, cbeta_ref, xs_scr, ys_scr)

    fl = jnp.dot(hb, wf_ref[...], preferred_element_type=F32) + bf_ref[...]
    lane = lax.broadcasted_iota(I32, fl.shape, 1)
    log_f = -(jnp.maximum(-fl, 0.0) + jnp.log1p(jnp.exp(-jnp.abs(fl))))
    log_f = jnp.where(lane < N_HEADS, log_f, 0.0)
    cs = jnp.dot(tri_ref[...], _split3(log_f), preferred_element_type=F32)
    cum = (cs[:, :LANES] + cs[:, LANES:2 * LANES] + cs[:, 2 * LANES:]) * LOG2_E
    tm = cum.shape[0]

    pieces = []
    for sb in range(tm // ATTN_BK):
        blk = cum[sb * ATTN_BK:(sb + 1) * ATTN_BK, :]
        if sb > 0:
            blk = blk - cum[sb * ATTN_BK - 1:sb * ATTN_BK, :]
        tot_ref[sb] = blk[ATTN_BK - 1:, :]
        pieces.append(blk)
    cum_k = pieces[0] if len(pieces) == 1 else jnp.concatenate(pieces, axis=0)

    first_of_block = pl.program_id(0) % (ATTN_BQ // tm) == 0
    cum_q = cum + jnp.where(first_of_block, 0.0, carry_scr[0:1, :])
    carry_scr[...] = jnp.broadcast_to(cum_q[tm - 1:, :], carry_scr.shape)

    qkv = jnp.dot(hb, wqkv_ref[...], preferred_element_type=F32)
    aw = qkv.shape[1] // 3
    q_ref[...] = (qkv[:, :aw] * (HEAD_DIM ** -0.5 * LOG2_E)).astype(BF16)
    k_ref[...] = qkv[:, aw:2 * aw].astype(BF16)
    v_ref[...] = qkv[:, 2 * aw:].astype(BF16)
    aq_ref[...] = (jnp.dot(_split3(cum_q), eq_ref[...], preferred_element_type=F32) + oneq_ref[...]).astype(BF16)
    ak_ref[...] = (jnp.dot(_split3(cum_k), ek_ref[...], preferred_element_type=F32) + onek_ref[...]).astype(BF16)

    gts = jnp.dot(hb, wg_ref[...], preferred_element_type=F32)
    sgc_ref[...] = jax.nn.sigmoid(gts[:, :d_model]).astype(BF16)
    sga_ref[...] = jax.nn.sigmoid(gts[:, d_model:]).astype(BF16)


def _aug_constants():
    rows = jnp.arange(N_SPLIT * LANES)
    piece, head = rows // LANES, rows % LANES
    cols = jnp.arange(N_HEADS * HEAD_DIM)
    pair, off = cols // LANES, cols % LANES
    chead = 2 * pair + jnp.where(off >= HEAD_DIM, 0, 1)
    pos = off % HEAD_DIM
    valid = (head[:, None] < N_HEADS) & (head[:, None] == chead[None, :])
    eq = jnp.where(valid & (pos[None, :] == piece[:, None]), 1.0, 0.0).astype(BF16)
    ek = jnp.where(valid & (pos[None, :] == piece[:, None] + N_SPLIT), -1.0, 0.0).astype(BF16)
    oneq = jnp.where((pos >= N_SPLIT) & (pos < 2 * N_SPLIT), 1.0, 0.0).astype(F32)[None]
    onek = jnp.where(pos < N_SPLIT, 1.0, 0.0).astype(F32)[None]
    return eq, ek, oneq, onek


def _inproj(x2, mod8, norm_g, w_in, b_forget, conv_w, conv_b, conv_ln_g, conv_ln_b, seq):
    t, d = x2.shape
    tm = INPROJ_TILE
    nsub = tm // ATTN_BK
    cw = d // 2
    aw = N_HEADS * HEAD_DIM
    o0 = 2 * cw
    wa = w_in[:, :o0].astype(BF16)
    o1 = o0 + 3 * aw
    wqkv = w_in[:, o0:o1].astype(BF16)
    wf = jnp.pad(w_in[:, o1:o1 + N_HEADS], ((0, 0), (0, LANES - N_HEADS))).astype(BF16)
    bfp = jnp.pad(b_forget, (0, LANES - N_HEADS))[None]
    wg = w_in[:, o1 + N_HEADS:].astype(BF16)
    r = jnp.arange(tm)
    tri = (r[None, :] <= r[:, None]).astype(BF16)
    eq, ek, oneq, onek = _aug_constants()
    tiles_per_seq = seq // tm
    row = lambda i: (i, 0)
    outs = pl.pallas_call(
        functools.partial(_inproj_kernel, conv_width=cw, d_model=d, tiles_per_seq=tiles_per_seq),
        grid=(t // tm,),
        in_specs=[pl.BlockSpec((tm, d), row),
                  pl.BlockSpec((None, SUBLANES, d), lambda i: (i // tiles_per_seq, 0, 0)),
                  _resident((1, d)),
                  _resident(wa.shape), _resident(wqkv.shape),
                  _resident(wf.shape), _resident(wg.shape), _resident(bfp.shape),
                  _resident(tri.shape), _resident(eq.shape), _resident(ek.shape),
                  _resident(oneq.shape), _resident(onek.shape),
                  _resident(conv_w.shape), _resident((1, cw)), _resident((1, cw)), _resident((1, cw))],
        out_specs=[pl.BlockSpec((tm, cw), row)] + [pl.BlockSpec((tm, aw), row)] * 5 + [
                   pl.BlockSpec((nsub, 1, LANES), lambda i: (i, 0, 0)),
                   pl.BlockSpec((tm, d), row), pl.BlockSpec((tm, d), row)],
        out_shape=[jax.ShapeDtypeStruct((t, cw), BF16)] + [jax.ShapeDtypeStruct((t, aw), BF16)] * 5 + [
                   jax.ShapeDtypeStruct((t // ATTN_BK, 1, LANES), F32),
                   jax.ShapeDtypeStruct((t, d), BF16), jax.ShapeDtypeStruct((t, d), BF16)],
        scratch_shapes=[pltpu.VMEM((SUBLANES, LANES), F32),
                        pltpu.VMEM((CONV_HALO + tm, cw), F32), pltpu.VMEM((CONV_HALO + tm, cw), F32)],
        compiler_params=_cparams("arbitrary"),
        name="inproj",
    )(x2, mod8, norm_g[None], wa, wqkv, wf, wg, bfp, tri, eq, ek, oneq, onek,
      conv_w, conv_b[None], conv_ln_g[None], conv_ln_b[None])
    return outs


def _attn_kernel(q_ref, aq_ref, k_ref, ak_ref, v_ref, tot_ref, o_ref, qop_scr, m_scr, acc_scr):
    bq = q_ref.shape[0]
    bk = ATTN_BK
    ratio = bq // bk
    qi = pl.program_id(2)
    nt = (((1,), (1,)), ((), ()))
    heads = range(HEADS_PER_STEP)
    zero = jnp.zeros((1, LANES), F32)

    def splice(hh, main, extra):
        first = lax.broadcasted_iota(I32, main.shape, 1) < HEAD_DIM
        return jnp.where(first, main, extra) if hh == 0 else jnp.where(first, extra, main)

    for hh in heads:
        qop_scr[hh] = splice(hh, q_ref[...], aq_ref[...])

    def scores(hh, kb, row0):
        start = pl.multiple_of(kb * bk, bk)
        k = splice(hh, k_ref[pl.ds(start, bk), :], ak_ref[pl.ds(start, bk), :])
        v = v_ref[pl.ds(start, bk), :]
        v = splice(hh, v, jnp.ones(v.shape, v.dtype))
        s = lax.dot_general(qop_scr[hh, row0:, :], k, nt, preferred_element_type=F32)
        return s, v

    def update(hh, s, v, delta, row0):
        m_prev = m_scr[hh, row0:, :]
        m_new = jnp.maximum(m_prev, jnp.max(s, axis=1, keepdims=True) + delta)
        shift = m_new - delta
        p = jnp.exp2(s - shift[:, :1])
        alpha = jnp.exp2(m_prev - m_new)
        pv = jnp.dot(p.astype(BF16), v, preferred_element_type=F32)
        acc_scr[hh, row0:, :] = alpha * acc_scr[hh, row0:, :] + pv
        m_scr[hh, row0:, :] = m_new

    for hh in heads:
        delta = zero
        for sb in range(ratio):
            kb = qi * ratio + sb
            row0 = sb * bk
            s, v = scores(hh, kb, row0)
            row = lax.broadcasted_iota(I32, s.shape, 0)
            col = lax.broadcasted_iota(I32, s.shape, 1)
            s = jnp.where(col <= row, s, -jnp.inf)
            if sb == 0:
                m0 = jnp.max(s, axis=1, keepdims=True)
                p = jnp.exp2(s - m0)
                m_scr[hh] = jnp.broadcast_to(m0, m_scr.shape[1:])
                acc_scr[hh] = jnp.dot(p.astype(BF16), v, preferred_element_type=F32)
            else:
                update(hh, s, v, delta, row0)
            delta = delta - tot_ref[hh, pl.ds(kb, 1), :]

    def body(step, deltas):
        kb = qi * ratio - 1 - step
        new_deltas = []
        for hh in heads:
            delta = deltas[hh] + tot_ref[hh, pl.ds(kb, 1), :]
            s, v = scores(hh, kb, 0)
            update(hh, s, v, delta, 0)
            new_deltas.append(delta)
        return tuple(new_deltas)

    lax.fori_loop(0, qi * ratio, body, tuple(zero for _ in heads))
    acc0, acc1 = acc_scr[0], acc_scr[1]
    out0 = acc0 / acc0[:, HEAD_DIM:HEAD_DIM + 1]
    out1 = acc1 / acc1[:, 0:1]
    o_ref[...] = splice(0, out0, out1).astype(BF16)


def _attention(q, aq, k, ak, v, tot, bsz, seq):
    t, aw = q.shape
    bq = ATTN_BQ
    nq = seq // bq
    nk = seq // ATTN_BK
    assert HEADS_PER_STEP * HEAD_DIM == LANES
    tot = tot.reshape(bsz, nk, LANES)[:, :, :N_HEADS].transpose(0, 2, 1)
    tot = jnp.broadcast_to(tot[..., None], (bsz, N_HEADS, nk, LANES))
    qblk = pl.BlockSpec((bq, LANES), lambda b, h, i: (b * nq + i, h))
    kblk = pl.BlockSpec((None, seq, LANES), lambda b, h, i: (b, 0, h))
    seq3 = lambda a: a.reshape(bsz, seq, aw)
    return pl.pallas_call(
        _attn_kernel,
        grid=(bsz, N_HEADS // HEADS_PER_STEP, nq),
        in_specs=[qblk, qblk, kblk, kblk, kblk,
                  pl.BlockSpec((None, HEADS_PER_STEP, nk, LANES), lambda b, h, i: (b, h, 0, 0))],
        out_specs=qblk,
        out_shape=jax.ShapeDtypeStruct((t, aw), BF16),
        scratch_shapes=[pltpu.VMEM((HEADS_PER_STEP, bq, LANES), BF16),
                        pltpu.VMEM((HEADS_PER_STEP, bq, LANES), F32),
                        pltpu.VMEM((HEADS_PER_STEP, bq, LANES), F32)],
        compiler_params=_cparams("arbitrary", "arbitrary", "arbitrary"),
        name="attn",
    )(q, aq, seq3(k), seq3(ak), seq3(v), tot)


def _pack_bf16_pair(lo, hi):
    lo_bits = pltpu.bitcast(lo.astype(BF16).astype(F32), U32)
    hi_bits = pltpu.bitcast(hi.astype(BF16).astype(F32), U32)
    return (hi_bits & jnp.uint32(0xFFFF0000)) | (lo_bits >> 16)


def _unpack_bf16_pair(w):
    lo = pltpu.bitcast(w << 16, F32)
    hi = pltpu.bitcast(w & jnp.uint32(0xFFFF0000), F32)
    return jnp.concatenate([lo, hi], axis=1)


def _expert_onehots(ids):
    lane = lax.broadcasted_iota(I32, ids.shape, 1)
    return lane, [lane == ids[:, kk:kk + 1] for kk in range(TOP_K)]


def _route_tile(ids, gates, carry, tri, utri):
    lane, hots = _expert_onehots(ids)
    multi = jnp.zeros(lane.shape, F32)
    for hot in hots:
        multi = multi + jnp.where(hot, 1.0, 0.0)
    before = jnp.dot(tri, multi.astype(BF16), preferred_element_type=F32)
    units = jnp.ceil(jnp.sum(multi, axis=0, keepdims=True) * (1.0 / CHUNK))
    units8 = jnp.broadcast_to(units, (SUBLANES, LANES)).astype(BF16)
    lstart = jnp.dot(units8, utri, preferred_element_type=F32)[0:1, :] * CHUNK
    csize = units * CHUNK
    slot = jnp.zeros(lane.shape, F32)
    for kk, hot in enumerate(hots):
        sl = jnp.sum(jnp.where(hot, lstart + before, 0.0), axis=1, keepdims=True)
        slot = jnp.where(lane == kk, sl, slot)
    row = lax.broadcasted_iota(I32, (SUBLANES, LANES), 0)
    meta = jnp.where(row == 0, csize, jnp.where(row == 1, lstart, jnp.where(row == 2, carry, 0.0)))
    slott = slot.T[0:SUBLANES, :].astype(I32)
    gatest = gates.T[0:SUBLANES, :]
    return slot.astype(I32), slott, gatest, meta, carry + csize


def _merge_kernel(x_ref, act_ref, attn_ref, sgc_ref, sga_ref, mod_ref, g2_ref, wco_ref, wao_ref, wout_ref,
                  wr_ref, br_ref, tri_ref, utri_ref,
                  x1_ref, h2_ref, slot_ref, slott_ref, gatest_ref, meta_ref, cnt_ref, carry_scr):
    @pl.when(pl.program_id(0) == 0)
    def _():
        carry_scr[...] = jnp.zeros(carry_scr.shape, F32)

    conv_out = jnp.dot(act_ref[...], wco_ref[...], preferred_element_type=F32)
    attn_out = jnp.dot(attn_ref[...], wao_ref[...], preferred_element_type=F32)
    mixed = sgc_ref[...].astype(F32) * conv_out + sga_ref[...].astype(F32) * attn_out
    upd = jnp.dot(mixed.astype(BF16), wout_ref[...], preferred_element_type=F32)
    x1 = x_ref[...] + mod_ref[2:3, :] * upd
    x1_ref[...] = x1
    h2 = _rms(x1) * g2_ref[...] * (1.0 + mod_ref[4:5, :]) + mod_ref[3:4, :]
    h2_hi = h2.astype(BF16)
    h2_ref[...] = h2_hi
    h2_lo = (h2 - h2_hi.astype(F32)).astype(BF16)
    h2_cat = jnp.concatenate([h2_hi, h2_lo, h2_hi], axis=1)
    logits = jnp.dot(h2_cat, wr_ref[...], preferred_element_type=F32) + br_ref[...]
    lane = lax.broadcasted_iota(I32, logits.shape, 1)
    work = jnp.where(lane < N_EXPERTS, logits, -jnp.inf)
    ids = jnp.zeros(logits.shape, I32)
    vals = []
    for kk in range(TOP_K):
        mx = jnp.max(work, axis=1, keepdims=True)
        idx = jnp.min(jnp.where(work == mx, lane, LANES), axis=1, keepdims=True)
        ids = jnp.where(lane == kk, idx, ids)
        vals.append(mx)
        work = jnp.where(lane == idx, -jnp.inf, work)
    exps = [jnp.exp(vv - vals[0]) for vv in vals]
    den = exps[0] + exps[1] + exps[2] + exps[3]
    gates = jnp.zeros(logits.shape, F32)
    for kk in range(TOP_K):
        gates = jnp.where(lane == kk, exps[kk] / den, gates)

    carry = carry_scr[0:1, :]
    tt = TOKEN_TILE
    for sub in range(logits.shape[0] // tt):
        rows = slice(sub * tt, (sub + 1) * tt)
        slot, slott, gatest, meta, carry = _route_tile(ids[rows, :], gates[rows, :], carry,
                                                       tri_ref[...], utri_ref[...])
        slot_ref[rows, :] = slot
        slott_ref[sub * SUBLANES:(sub + 1) * SUBLANES, :] = slott
        gatest_ref[sub * SUBLANES:(sub + 1) * SUBLANES, :] = gatest
        meta_ref[sub] = meta
    carry_scr[...] = jnp.broadcast_to(carry, carry_scr.shape)
    cnt_ref[...] = jnp.broadcast_to(carry, cnt_ref.shape)


def _merge(x2, act, attn, sgc, sga, mod8, norm2_g, w_conv_out, w_attn_out, w_out, w_router, b_router, seq):
    t, d = x2.shape
    tm = MERGE_TILE
    tt = TOKEN_TILE
    sub = tm // tt
    nt = t // tt
    cw = act.shape[1]
    sw = attn.shape[1]
    wco = w_conv_out.astype(BF16)
    wao = w_attn_out.astype(BF16)
    wout = w_out.astype(BF16)
    wr = jnp.pad(w_router, ((0, 0), (0, LANES - N_EXPERTS)))
    wr_hi = wr.astype(BF16)
    wr_lo = (wr - wr_hi.astype(F32)).astype(BF16)
    wr = jnp.concatenate([wr_hi, wr_hi, wr_lo], axis=0)
    br = jnp.pad(b_router, (0, LANES - N_EXPERTS))[None]
    r = jnp.arange(tt)
    tri = (r[None, :] < r[:, None]).astype(BF16)
    e = jnp.arange(LANES)
    utri = (e[:, None] < e[None, :]).astype(BF16)
    tiles_per_seq = seq // tm
    row = lambda i: (i, 0)
    return pl.pallas_call(
        _merge_kernel,
        grid=(t // tm,),
        in_specs=[pl.BlockSpec((tm, d), row), pl.BlockSpec((tm, cw), row), pl.BlockSpec((tm, sw), row),
                  pl.BlockSpec((tm, d), row), pl.BlockSpec((tm, d), row),
                  pl.BlockSpec((None, SUBLANES, d), lambda i: (i // tiles_per_seq, 0, 0)),
                  _resident((1, d)), _resident(wco.shape), _resident(wao.shape), _resident(wout.shape),
                  _resident(wr.shape), _resident(br.shape), _resident(tri.shape), _resident(utri.shape)],
        out_specs=[pl.BlockSpec((tm, d), row), pl.BlockSpec((tm, d), row),
                   pl.BlockSpec((tm, LANES), row),
                   pl.BlockSpec((sub * SUBLANES, tt), row), pl.BlockSpec((sub * SUBLANES, tt), row),
                   pl.BlockSpec((sub, SUBLANES, LANES), lambda i: (i, 0, 0)),
                   pl.BlockSpec((SUBLANES, LANES), lambda i: (0, 0))],
        out_shape=[jax.ShapeDtypeStruct((t, d), F32), jax.ShapeDtypeStruct((t, d), BF16),
                   jax.ShapeDtypeStruct((t, LANES), I32),
                   jax.ShapeDtypeStruct((nt * SUBLANES, tt), I32), jax.ShapeDtypeStruct((nt * SUBLANES, tt), F32),
                   jax.ShapeDtypeStruct((nt, SUBLANES, LANES), F32),
                   jax.ShapeDtypeStruct((SUBLANES, LANES), F32)],
        scratch_shapes=[pltpu.VMEM((SUBLANES, LANES), F32)],
        compiler_params=_cparams("arbitrary"),
        name="merge",
    )(x2, act, attn, sgc, sga, mod8, norm2_g[None], wco, wao, wout, wr, br, tri, utri)


def _dest_kernel(cnt_ref, meta_ref, tri_ref, gd_ref, blk_ref):
    cnt = cnt_ref[...]
    padded = jnp.ceil(cnt * (1.0 / EXPERT_BLOCK)) * EXPERT_BLOCK
    pend = jnp.dot(padded, tri_ref[...], preferred_element_type=F32, precision=HIGHEST)
    pstart = (pend - padded)[0:1, :]

    grow = lax.broadcasted_iota(I32, (SUBLANES, LANES), 0)
    glane = lax.broadcasted_iota(I32, (SUBLANES, LANES), 1)
    gpos = ((grow * LANES + glane) * CHUNK).astype(F32)
    last = jnp.logical_and(grow == SUBLANES - 1, glane == LANES - 1)
    for tile in range(meta_ref.shape[0]):
        meta = meta_ref[tile]
        csize, lstart, seg_off = meta[0:1, :], meta[1:2, :], meta[2:3, :]
        lend = lstart + csize
        base = pstart + seg_off - lstart
        owner = jnp.zeros(gpos.shape, I32)
        for e in range(N_EXPERTS):
            owner = owner + jnp.where(gpos >= lend[0:1, e:e + 1], 1, 0)
        gdst = gpos
        for e in range(N_EXPERTS):
            gdst = gdst + jnp.where(owner == e, base[0:1, e:e + 1], 0.0)
        n_groups = lend[0:1, N_EXPERTS - 1:N_EXPERTS] * (1.0 / CHUNK)
        gd_ref[tile] = jnp.where(last, n_groups, gdst).astype(I32)

    @pl.when(pl.program_id(0) == 0)
    def _():
        shape = blk_ref.shape[1:]
        pos = (lax.broadcasted_iota(I32, shape, 0) * LANES + lax.broadcasted_iota(I32, shape, 1))
        pos = pos.astype(F32) * EXPERT_BLOCK
        be = jnp.zeros(shape, I32)
        for e in range(N_EXPERTS):
            be = be + jnp.where(pos >= pend[0:1, e:e + 1], 1, 0)
        blk_ref[0] = jnp.minimum(be, N_EXPERTS - 1)
        nact = pend[0:1, N_EXPERTS - 1:N_EXPERTS] * (1.0 / EXPERT_BLOCK)
        blk_ref[1] = jnp.broadcast_to(nact, shape).astype(I32)
        blk_ref[2] = jnp.broadcast_to(pend[0:1, :], shape).astype(I32)


def _dest(cnt, meta):
    nt = meta.shape[0]
    per_step = DEST_TILES_PER_STEP
    r = jnp.arange(LANES)
    tri = (r[:, None] <= r[None, :]).astype(F32)
    return pl.pallas_call(
        _dest_kernel,
        grid=(nt // per_step,),
        in_specs=[pl.BlockSpec((SUBLANES, LANES), lambda i: (0, 0)),
                  pl.BlockSpec((per_step, SUBLANES, LANES), lambda i: (i, 0, 0)),
                  pl.BlockSpec((LANES, LANES), lambda i: (0, 0))],
        out_specs=[pl.BlockSpec((per_step, SUBLANES, LANES), lambda i: (i, 0, 0)),
                   pl.BlockSpec((3, SUBLANES, LANES), lambda i: (0, 0, 0))],
        out_shape=[jax.ShapeDtypeStruct((nt, SUBLANES, LANES), I32),
                   jax.ShapeDtypeStruct((3, SUBLANES, LANES), I32)],
        compiler_params=_cparams("arbitrary"),
        name="dest",
    )(cnt, meta, tri)


def _row_copy(src, dst, sem):
    return pltpu.make_async_copy(src, dst, sem)


def _dispatch_kernel(pend_ref, na_ref, gd_ref, slott_ref, gatest_ref, h2_ref, xs_ref, zero_ref, sorted_scr, sem):
    tm = h2_ref.shape[0]
    bm = zero_ref.shape[0]
    nb = xs_ref.shape[0] // bm

    @pl.when(pl.program_id(0) == 0)
    def _():
        zero_ref[...] = jnp.zeros(zero_ref.shape, zero_ref.dtype)

        def zero_block(start):
            return _row_copy(zero_ref, xs_ref.at[pl.ds(pl.multiple_of(start, bm), bm)], sem)

        def nonempty(e):
            return pend_ref[e] > (pend_ref[e - 1] if e > 0 else 0)

        for e in range(N_EXPERTS):
            @pl.when(nonempty(e))
            def _():
                zero_block(pend_ref[e] - bm).start()

        def tail(b, carry):
            zero_block(b * bm).start()
            return carry

        lax.fori_loop(na_ref[0], nb, tail, 0)
        for e in range(N_EXPERTS):
            @pl.when(nonempty(e))
            def _():
                zero_block(0).wait()

        def tail_wait(b, carry):
            zero_block(0).wait()
            return carry

        lax.fori_loop(na_ref[0], nb, tail_wait, 0)

    srow = lax.broadcasted_iota(I32, (SLOT_ROWS, tm), 0)
    perm = jnp.zeros((SLOT_ROWS, tm), F32)
    gate = jnp.zeros((SLOT_ROWS, 1), F32)
    for kk in range(TOP_K):
        hit = srow == slott_ref[kk:kk + 1, :]
        perm = perm + jnp.where(hit, 1.0, 0.0)
        gate = gate + jnp.sum(jnp.where(hit, gatest_ref[kk:kk + 1, :], 0.0), axis=1, keepdims=True)
    perm = perm.astype(BF16)
    half = h2_ref.shape[1] // 2
    lo = jnp.dot(perm, h2_ref[:, :half], preferred_element_type=F32)
    hi = jnp.dot(perm, h2_ref[:, half:], preferred_element_type=F32)
    sorted_scr[:, :half] = pltpu.bitcast(hi, U32) | (pltpu.bitcast(lo, U32) >> 16)
    sorted_scr[:, half:] = pltpu.bitcast(jnp.broadcast_to(gate, (SLOT_ROWS, LANES)), U32)

    n_groups = gd_ref[GROUP_TABLE - 1]

    def group_copy(j, d):
        src = sorted_scr.at[pl.ds(pl.multiple_of(j * CHUNK, CHUNK), CHUNK)]
        return _row_copy(src, xs_ref.at[pl.ds(pl.multiple_of(d, CHUNK), CHUNK)], sem)

    def issue(j, carry):
        group_copy(j, gd_ref[j]).start()
        return carry

    def retire(j, carry):
        group_copy(0, 0).wait()
        return carry

    lax.fori_loop(0, n_groups, issue, 0)
    lax.fori_loop(0, n_groups, retire, 0)


def _dispatch(pend, n_active, gd_flat, slott, gatest, h2b, n_rows):
    t, d = h2b.shape
    w = d // 2 + LANES
    tm = TOKEN_TILE
    grid_spec = pltpu.PrefetchScalarGridSpec(
        num_scalar_prefetch=2,
        grid=(t // tm,),
        in_specs=[pl.BlockSpec((GROUP_TABLE,), lambda i, pe, na: (i,), memory_space=pltpu.SMEM),
                  pl.BlockSpec((SUBLANES, tm), lambda i, pe, na: (i, 0)),
                  pl.BlockSpec((SUBLANES, tm), lambda i, pe, na: (i, 0)),
                  pl.BlockSpec((tm, d), lambda i, pe, na: (i, 0))],
        out_specs=pl.BlockSpec(memory_space=pl.ANY),
        scratch_shapes=[pltpu.VMEM((EXPERT_BLOCK, w), U32), pltpu.VMEM((SLOT_ROWS, w), U32),
                        pltpu.SemaphoreType.DMA(())],
    )
    return pl.pallas_call(
        _dispatch_kernel,
        grid_spec=grid_spec,
        out_shape=jax.ShapeDtypeStruct((n_rows, w), U32),
        compiler_params=_cparams("arbitrary"),
        name="dispatch",
    )(pend, n_active, gd_flat, slott, gatest, h2b)


def _expert_kernel(be_ref, na_ref, xs_ref, w1_ref, b1_ref, w2_ref, b2_ref, y_ref, w1b_ref, w2b_ref, *, d_expert):
    i = pl.program_id(0)
    active = i < na_ref[0]
    prev = be_ref[jnp.maximum(i - 1, 0)]
    new_expert = jnp.logical_or(i == 0, be_ref[i] != prev)
    half = y_ref.shape[1]

    @pl.when(jnp.logical_and(active, new_expert))
    def _():
        w1b_ref[...] = w1_ref[...].astype(BF16)
        w2b_ref[...] = w2_ref[...].astype(BF16)

    @pl.when(active)
    def _():
        x = _unpack_bf16_pair(xs_ref[:, :half]).astype(BF16)
        gate = pltpu.bitcast(xs_ref[:, half:], F32)[:, 0:1]
        hc = jnp.dot(x, w1b_ref[...], preferred_element_type=F32) + b1_ref[...]
        g = jnp.minimum(hc[:, :d_expert], SWIGLU_LIMIT)
        u = jnp.clip(hc[:, d_expert:], -SWIGLU_LIMIT, SWIGLU_LIMIT)
        hm = (u + 1.0) * (g * jax.nn.sigmoid(SWIGLU_ALPHA * g))
        y = (jnp.dot(hm.astype(BF16), w2b_ref[...], preferred_element_type=F32) + b2_ref[...]) * gate
        y_ref[...] = _pack_bf16_pair(y[:, :half], y[:, half:])

    @pl.when(jnp.logical_not(active))
    def _():
        y_ref[...] = jnp.zeros(y_ref.shape, U32)


def _experts(block_e, n_active, xs, w1, b1, w2, b2):
    n_rows, w = xs.shape
    n_exp, d, two_de = w1.shape
    de = two_de // 2
    bm = EXPERT_BLOCK
    nb = n_rows // bm

    def blk(i, be, na):
        return jnp.maximum(jnp.minimum(i, na[0] - 1), 0)

    grid_spec = pltpu.PrefetchScalarGridSpec(
        num_scalar_prefetch=2,
        grid=(nb,),
        in_specs=[pl.BlockSpec((bm, w), lambda i, be, na: (blk(i, be, na), 0)),
                  pl.BlockSpec((None, d, two_de), lambda i, be, na: (be[blk(i, be, na)], 0, 0)),
                  pl.BlockSpec((None, 1, two_de), lambda i, be, na: (be[blk(i, be, na)], 0, 0)),
                  pl.BlockSpec((None, de, d), lambda i, be, na: (be[blk(i, be, na)], 0, 0)),
                  pl.BlockSpec((None, 1, d), lambda i, be, na: (be[blk(i, be, na)], 0, 0))],
        out_specs=pl.BlockSpec((bm, d // 2), lambda i, be, na: (i, 0)),
        scratch_shapes=[pltpu.VMEM((d, two_de), BF16), pltpu.VMEM((de, d), BF16)],
    )
    return pl.pallas_call(
        functools.partial(_expert_kernel, d_expert=de),
        grid_spec=grid_spec,
        out_shape=jax.ShapeDtypeStruct((n_rows, d // 2), U32),
        compiler_params=_cparams("arbitrary"),
        name="expert",
    )(block_e, n_active, xs, w1, b1[:, None, :], w2, b2[:, None, :])


def _combine_kernel(gd_ref, gd_next_ref, slot_ref, x1_ref, mod_ref, fg_ref, yb_ref, o_ref, sorted_scr, sem):
    tm = x1_ref.shape[0]
    step = pl.program_id(0)
    cur = step % 2

    def group_copy(buf, j, d):
        dst = sorted_scr.at[buf, pl.ds(pl.multiple_of(j * CHUNK, CHUNK), CHUNK)]
        return _row_copy(yb_ref.at[pl.ds(pl.multiple_of(d, CHUNK), CHUNK)], dst, sem.at[buf])

    def fetch(table_ref, buf):
        def issue(j, carry):
            group_copy(buf, j, table_ref[j]).start()
            return carry
        lax.fori_loop(0, table_ref[GROUP_TABLE - 1], issue, 0)

    @pl.when(step == 0)
    def _():
        sorted_scr[...] = jnp.zeros(sorted_scr.shape, sorted_scr.dtype)
        fetch(gd_ref, 0)

    @pl.when(step + 1 < pl.num_programs(0))
    def _():
        fetch(gd_next_ref, 1 - cur)

    def retire(j, carry):
        group_copy(cur, 0, 0).wait()
        return carry

    lax.fori_loop(0, gd_ref[GROUP_TABLE - 1], retire, 0)

    slot = slot_ref[...]
    scol = lax.broadcasted_iota(I32, (tm, SLOT_ROWS), 1)
    pick = jnp.zeros((tm, SLOT_ROWS), F32)
    for kk in range(TOP_K):
        pick = pick + jnp.where(scol == slot[:, kk:kk + 1], 1.0, 0.0)
    pick = pick.astype(BF16)
    words = sorted_scr[cur]
    lo = pltpu.bitcast(words << 16, F32).astype(BF16)
    hi = pltpu.bitcast(words & jnp.uint32(0xFFFF0000), F32).astype(BF16)
    y = jnp.concatenate([jnp.dot(pick, lo, preferred_element_type=F32),
                         jnp.dot(pick, hi, preferred_element_type=F32)], axis=1)
    x2 = x1_ref[...] + mod_ref[5:6, :] * y
    o_ref[...] = _rms(x2) * fg_ref[...]


def _combine(gd_flat, slot, x1, mod8, final_g, yb, seq):
    t, d = x1.shape
    tm = TOKEN_TILE
    nt = t // tm
    tiles_per_seq = seq // tm
    return pl.pallas_call(
        _combine_kernel,
        grid=(nt,),
        in_specs=[pl.BlockSpec((GROUP_TABLE,), lambda i: (i,), memory_space=pltpu.SMEM),
                  pl.BlockSpec((GROUP_TABLE,), lambda i: (jnp.minimum(i + 1, nt - 1),), memory_space=pltpu.SMEM),
                  pl.BlockSpec((tm, LANES), lambda i: (i, 0)),
                  pl.BlockSpec((tm, d), lambda i: (i, 0)),
                  pl.BlockSpec((None, SUBLANES, d), lambda i: (i // tiles_per_seq, 0, 0)),
                  pl.BlockSpec((1, d), lambda i: (0, 0)),
                  pl.BlockSpec(memory_space=pl.ANY)],
        out_specs=pl.BlockSpec((tm, d), lambda i: (i, 0)),
        out_shape=jax.ShapeDtypeStruct((t, d), F32),
        scratch_shapes=[pltpu.VMEM((2, SLOT_ROWS, d // 2), U32), pltpu.SemaphoreType.DMA((2,))],
        compiler_params=_cparams("arbitrary"),
        name="combine",
    )(gd_flat, gd_flat, slot, x1, mod8, final_g[None], yb)


def kernel(x, c, w_ada, b_ada, norm1_g, w_in, b_forget, conv_w, conv_b, conv_ln_g, conv_ln_b, w_conv_out,
           w_attn_out, w_out, norm2_g, w_router, b_router, w_exp_in, b_exp_in, w_exp_out, b_exp_out, final_g):
    bsz, seq, d = x.shape
    t = bsz * seq
    depth = w_ada.shape[0]
    xf = x.reshape(t, d)
    n_rows = t * TOP_K + (t // TOKEN_TILE) * N_EXPERTS * (CHUNK - 1) + N_EXPERTS * (EXPERT_BLOCK - CHUNK)
    n_rows = -(-n_rows // EXPERT_BLOCK) * EXPERT_BLOCK
    n_blocks = n_rows // EXPERT_BLOCK
    for l in range(depth):
        mod8 = _mod(c, w_ada, b_ada[l], l)
        act, q, k, v, aq, ak, tot, sgc, sga = _inproj(xf, mod8, norm1_g[l], w_in[l], b_forget[l], conv_w[l],
                                                      conv_b[l], conv_ln_g[l], conv_ln_b[l], seq)
        attn = _attention(q, aq, k, ak, v, tot, bsz, seq)
        x1, h2b, slot, slott, gatest, meta, cnt = _merge(xf, act, attn, sgc, sga, mod8, norm2_g[l], w_conv_out[l],
                                                         w_attn_out[l], w_out[l], w_router[l], b_router[l], seq)
        gd, blk = _dest(cnt, meta)
        gd_flat = gd.reshape(-1)
        block_e = blk[0].reshape(-1)[:n_blocks]
        n_active = blk[1, 0, :1]
        pend = blk[2, 0, :N_EXPERTS]
        xs = _dispatch(pend, n_active, gd_flat, slott, gatest, h2b, n_rows)
        yb = _experts(block_e, n_active, xs, w_exp_in[l], b_exp_in[l], w_exp_out[l], b_exp_out[l])
        assert depth == 1
        xf = _combine(gd_flat, slot, x1, mod8, final_g, yb, seq)
    return xf.reshape(bsz, seq, d)
```

```python
import functools

import jax
import jax.numpy as jnp
from jax import lax
from jax.experimental import pallas as pl
from jax.experimental.pallas import tpu as pltpu

F32 = jnp.float32
BF16 = jnp.bfloat16
I32 = jnp.int32
U32 = jnp.uint32
HIGHEST = lax.Precision.HIGHEST

LANES = 128
SUBLANES = 8
VMEM_LIMIT_BYTES = 56 * 1024 * 1024

N_HEADS = 8
HEAD_DIM = 64
CONV_KERNEL = 31
CONV_HALO = 32
N_EXPERTS = 32
TOP_K = 4
SWIGLU_LIMIT = 7.0
SWIGLU_ALPHA = 1.702
RMS_EPS = 1e-5
LN_EPS = 1e-5
N_MOD = 6
N_SPLIT = 3
LOG2_E = 1.4426950408889634
HEADS_PER_STEP = 2

INPROJ_TILE = 512
ATTN_BQ = 4096
ATTN_BK = 512
MERGE_TILE = 1024
TOKEN_TILE = 256
EXPERT_BLOCK = 768
CHUNK = SUBLANES
SLOT_ROWS = 1280
GROUP_TABLE = SUBLANES * LANES
DEST_TILES_PER_STEP = 8
assert SLOT_ROWS >= TOKEN_TILE * TOP_K + N_EXPERTS * (CHUNK - 1) and SLOT_ROWS // CHUNK < GROUP_TABLE


def _cparams(*sem):
    return pltpu.CompilerParams(dimension_semantics=sem, vmem_limit_bytes=VMEM_LIMIT_BYTES)


def _resident(shape):
    nd = len(shape)
    return pl.BlockSpec(shape, lambda *_: (0,) * nd, pipeline_mode=pl.Buffered(1))


def _split3(x):
    p1 = x.astype(BF16)
    r1 = x - p1.astype(F32)
    p2 = r1.astype(BF16)
    p3 = (r1 - p2.astype(F32)).astype(BF16)
    return jnp.concatenate([p1, p2, p3], axis=1)


def _rms(x):
    return x * lax.rsqrt(jnp.mean(x * x, axis=-1, keepdims=True) + RMS_EPS)


def _mod_kernel(ct_ref, w_ref, b_ref, o_ref, *, batch):
    rows = []
    for b in range(batch):
        col = ct_ref[:, b:b + 1]
        rows.append(jnp.sum((col * jax.nn.sigmoid(col)) * w_ref[...], axis=0, keepdims=True) + b_ref[...])
    rows.append(jnp.zeros((SUBLANES - batch, rows[0].shape[1]), F32))
    o_ref[...] = jnp.concatenate(rows, axis=0)


def _mod(c, w_ada, b_ada, layer):
    bsz, d = c.shape
    n = w_ada.shape[2]
    tn = 1024
    out = pl.pallas_call(
        functools.partial(_mod_kernel, batch=bsz),
        grid=(n // tn,),
        in_specs=[pl.BlockSpec((d, bsz), lambda j: (0, 0)),
                  pl.BlockSpec((None, d, tn), lambda j: (layer, 0, j)),
                  pl.BlockSpec((1, tn), lambda j: (0, j))],
        out_specs=pl.BlockSpec((SUBLANES, tn), lambda j: (0, j)),
        out_shape=jax.ShapeDtypeStruct((SUBLANES, n), F32),
        compiler_params=_cparams("arbitrary"),
        name="mod",
    )(c.T, w_ada, b_ada[None])
    mod = out[:bsz].reshape(bsz, N_MOD, d)
    return jnp.pad(mod, ((0, 0), (0, SUBLANES - N_MOD), (0, 0)))


def _conv_module(a, first_of_seq, w_ref, b_ref, g_ref, beta_ref, xs_scr, ys_scr):
    ts = a.shape[0]
    xs_scr[0:CONV_HALO, :] = jnp.where(first_of_seq, 0.0, xs_scr[ts:ts + CONV_HALO, :])
    xs_scr[CONV_HALO:, :] = a
    acc = jnp.zeros(a.shape, F32) + b_ref[...]
    base = CONV_HALO - (CONV_KERNEL - 1)
    for rho in range(SUBLANES):
        offs = [o for o in range(base, base + CONV_KERNEL) if o % SUBLANES == rho]
        if not offs:
            continue
        rows = offs[-1] + ts - rho
        ys_scr[0:rows, :] = xs_scr[rho:rho + rows, :]
        for o in offs:
            lo = o - rho
            acc = acc + w_ref[o - base:o - base + 1, :] * ys_scr[lo:lo + ts, :]
    mu = jnp.mean(acc, axis=-1, keepdims=True)
    cen = acc - mu
    var = jnp.mean(cen * cen, axis=-1, keepdims=True)
    y = cen * lax.rsqrt(var + LN_EPS) * g_ref[...] + beta_ref[...]
    return (y * jax.nn.sigmoid(y)).astype(BF16)


def _inproj_kernel(x_ref, mod_ref, g_ref, wa_ref, wqkv_ref, wf_ref, wg_ref, bf_ref,
                   tri_ref, eq_ref, ek_ref, oneq_ref, onek_ref, cw_ref, cb_ref, cg_ref, cbeta_ref,
                   act_ref, q_ref, k_ref, v_ref, aq_ref, ak_ref, tot_ref, sgc_ref, sga_ref,
                   carry_scr, xs_scr, ys_scr, *, conv_width, d_model, tiles_per_seq):
    first_of_seq = pl.program_id(0) % tiles_per_seq == 0

    @pl.when(pl.program_id(0) == 0)
    def _():
        xs_scr[...] = jnp.zeros(xs_scr.shape, F32)
        carry_scr[...] = jnp.zeros(carry_scr.shape, F32)

    x = x_ref[...]
    h = _rms(x) * g_ref[...] * (1.0 + mod_ref[1:2, :]) + mod_ref[0:1, :]
    hb = h.astype(BF16)

    u = jnp.dot(hb, wa_ref[...], preferred_element_type=F32)
    a = u[:, :conv_width] * jax.nn.sigmoid(u[:, conv_width:])
    act_ref[...] = _conv_module(a, first_of_seq, cw_ref, cb_ref, cg_ref, cbeta_ref, xs_scr, ys_scr)

    fl = jnp.dot(hb, wf_ref[...], preferred_element_type=F32) + bf_ref[...]
    lane = lax.broadcasted_iota(I32, fl.shape, 1)
    log_f = -(jnp.maximum(-fl, 0.0) + jnp.log1p(jnp.exp(-jnp.abs(fl))))
    log_f = jnp.where(lane < N_HEADS, log_f, 0.0)
    cs = jnp.dot(tri_ref[...], _split3(log_f), preferred_element_type=F32)
    cum = (cs[:, :LANES] + cs[:, LANES:2 * LANES] + cs[:, 2 * LANES:]) * LOG2_E
    tm = cum.shape[0]

    pieces = []
    for sb in range(tm // ATTN_BK):
        blk = cum[sb * ATTN_BK:(sb + 1) * ATTN_BK, :]
        if sb > 0:
            blk = blk - cum[sb * ATTN_BK - 1:sb * ATTN_BK, :]
        tot_ref[sb] = blk[ATTN_BK - 1:, :]
        pieces.append(blk)
    cum_k = pieces[0] if len(pieces) == 1 else jnp.concatenate(pieces, axis=0)

    first_of_block = pl.program_id(0) % (ATTN_BQ // tm) == 0
    cum_q = cum + jnp.where(first_of_block, 0.0, carry_scr[0:1, :])
    carry_scr[...] = jnp.broadcast_to(cum_q[tm - 1:, :], carry_scr.shape)

    qkv = jnp.dot(hb, wqkv_ref[...], preferred_element_type=F32)
    aw = qkv.shape[1] // 3
    q_ref[...] = (qkv[:, :aw] * (HEAD_DIM ** -0.5 * LOG2_E)).astype(BF16)
    k_ref[...] = qkv[:, aw:2 * aw].astype(BF16)
    v_ref[...] = qkv[:, 2 * aw:].astype(BF16)
    aq_ref[...] = (jnp.dot(_split3(cum_q), eq_ref[...], preferred_element_type=F32) + oneq_ref[...]).astype(BF16)
    ak_ref[...] = (jnp.dot(_split3(cum_k), ek_ref[...], preferred_element_type=F32) + onek_ref[...]).astype(BF16)

    gts = jnp.dot(hb, wg_ref[...], preferred_element_type=F32)
    sgc_ref[...] = jax.nn.sigmoid(gts[:, :d_model]).astype(BF16)
    sga_ref[...] = jax.nn.sigmoid(gts[:, d_model:]).astype(BF16)


def _aug_constants():
    rows = jnp.arange(N_SPLIT * LANES)
    piece, head = rows // LANES, rows % LANES
    cols = jnp.arange(N_HEADS * HEAD_DIM)
    pair, off = cols // LANES, cols % LANES
    chead = 2 * pair + jnp.where(off >= HEAD_DIM, 0, 1)
    pos = off % HEAD_DIM
    valid = (head[:, None] < N_HEADS) & (head[:, None] == chead[None, :])
    eq = jnp.where(valid & (pos[None, :] == piece[:, None]), 1.0, 0.0).astype(BF16)
    ek = jnp.where(valid & (pos[None, :] == piece[:, None] + N_SPLIT), -1.0, 0.0).astype(BF16)
    oneq = jnp.where((pos >= N_SPLIT) & (pos < 2 * N_SPLIT), 1.0, 0.0).astype(F32)[None]
    onek = jnp.where(pos < N_SPLIT, 1.0, 0.0).astype(F32)[None]
    return eq, ek, oneq, onek


def _inproj(x2, mod8, norm_g, w_in, b_forget, conv_w, conv_b, conv_ln_g, conv_ln_b, seq):
    t, d = x2.shape
    tm = INPROJ_TILE
    nsub = tm // ATTN_BK
    cw = d // 2
    aw = N_HEADS * HEAD_DIM
    o0 = 2 * cw
    wa = w_in[:, :o0].astype(BF16)
    o1 = o0 + 3 * aw
    wqkv = w_in[:, o0:o1].astype(BF16)
    wf = jnp.pad(w_in[:, o1:o1 + N_HEADS], ((0, 0), (0, LANES - N_HEADS))).astype(BF16)
    bfp = jnp.pad(b_forget, (0, LANES - N_HEADS))[None]
    wg = w_in[:, o1 + N_HEADS:].astype(BF16)
    r = jnp.arange(tm)
    tri = (r[None, :] <= r[:, None]).astype(BF16)
    eq, ek, oneq, onek = _aug_constants()
    tiles_per_seq = seq // tm
    row = lambda i: (i, 0)
    outs = pl.pallas_call(
        functools.partial(_inproj_kernel, conv_width=cw, d_model=d, tiles_per_seq=tiles_per_seq),
        grid=(t // tm,),
        in_specs=[pl.BlockSpec((tm, d), row),
                  pl.BlockSpec((None, SUBLANES, d), lambda i: (i // tiles_per_seq, 0, 0)),
                  _resident((1, d)),
                  _resident(wa.shape), _resident(wqkv.shape),
                  _resident(wf.shape), _resident(wg.shape), _resident(bfp.shape),
                  _resident(tri.shape), _resident(eq.shape), _resident(ek.shape),
                  _resident(oneq.shape), _resident(onek.shape),
                  _resident(conv_w.shape), _resident((1, cw)), _resident((1, cw)), _resident((1, cw))],
        out_specs=[pl.BlockSpec((tm, cw), row)] + [pl.BlockSpec((tm, aw), row)] * 5 + [
                   pl.BlockSpec((nsub, 1, LANES), lambda i: (i, 0, 0)),
                   pl.BlockSpec((tm, d), row), pl.BlockSpec((tm, d), row)],
        out_shape=[jax.ShapeDtypeStruct((t, cw), BF16)] + [jax.ShapeDtypeStruct((t, aw), BF16)] * 5 + [
                   jax.ShapeDtypeStruct((t // ATTN_BK, 1, LANES), F32),
                   jax.ShapeDtypeStruct((t, d), BF16), jax.ShapeDtypeStruct((t, d), BF16)],
        scratch_shapes=[pltpu.VMEM((SUBLANES, LANES), F32),
                        pltpu.VMEM((CONV_HALO + tm, cw), F32), pltpu.VMEM((CONV_HALO + tm, cw), F32)],
        compiler_params=_cparams("arbitrary"),
        name="inproj",
    )(x2, mod8, norm_g[None], wa, wqkv, wf, wg, bfp, tri, eq, ek, oneq, onek,
      conv_w, conv_b[None], conv_ln_g[None], conv_ln_b[None])
    return outs


def _attn_kernel(q_ref, aq_ref, k_ref, ak_ref, v_ref, tot_ref, o_ref, qop_scr, m_scr, acc_scr):
    bq = q_ref.shape[0]
    bk = ATTN_BK
    ratio = bq // bk
    qi = pl.program_id(2)
    nt = (((1,), (1,)), ((), ()))
    heads = range(HEADS_PER_STEP)
    zero = jnp.zeros((1, LANES), F32)

    def splice(hh, main, extra):
        first = lax.broadcasted_iota(I32, main.shape, 1) < HEAD_DIM
        return jnp.where(first, main, extra) if hh == 0 else jnp.where(first, extra, main)

    for hh in heads:
        qop_scr[hh] = splice(hh, q_ref[...], aq_ref[...])

    def scores(hh, kb, row0):
        start = pl.multiple_of(kb * bk, bk)
        k = splice(hh, k_ref[pl.ds(start, bk), :], ak_ref[pl.ds(start, bk), :])
        v = v_ref[pl.ds(start, bk), :]
        v = splice(hh, v, jnp.ones(v.shape, v.dtype))
        s = lax.dot_general(qop_scr[hh, row0:, :], k, nt, preferred_element_type=F32)
        return s, v

    def update(hh, s, v, delta, row0):
        m_prev = m_scr[hh, row0:, :]
        m_new = jnp.maximum(m_prev, jnp.max(s, axis=1, keepdims=True) + delta)
        shift = m_new - delta
        p = jnp.exp2(s - shift[:, :1])
        alpha = jnp.exp2(m_prev - m_new)
        pv = jnp.dot(p.astype(BF16), v, preferred_element_type=F32)
        acc_scr[hh, row0:, :] = alpha * acc_scr[hh, row0:, :] + pv
        m_scr[hh, row0:, :] = m_new

    for hh in heads:
        delta = zero
        for sb in range(ratio):
            kb = qi * ratio + sb
            row0 = sb * bk
            s, v = scores(hh, kb, row0)
            row = lax.broadcasted_iota(I32, s.shape, 0)
            col = lax.broadcasted_iota(I32, s.shape, 1)
            s = jnp.where(col <= row, s, -jnp.inf)
            if sb == 0:
                m0 = jnp.max(s, axis=1, keepdims=True)
                p = jnp.exp2(s - m0)
                m_scr[hh] = jnp.broadcast_to(m0, m_scr.shape[1:])
                acc_scr[hh] = jnp.dot(p.astype(BF16), v, preferred_element_type=F32)
            else:
                update(hh, s, v, delta, row0)
            delta = delta - tot_ref[hh, pl.ds(kb, 1), :]

    def body(step, deltas):
        kb = qi * ratio - 1 - step
        new_deltas = []
        for hh in heads:
            delta = deltas[hh] + tot_ref[hh, pl.ds(kb, 1), :]
            s, v = scores(hh, kb, 0)
            update(hh, s, v, delta, 0)
            new_deltas.append(delta)
        return tuple(new_deltas)

    lax.fori_loop(0, qi * ratio, body, tuple(zero for _ in heads))
    acc0, acc1 = acc_scr[0], acc_scr[1]
    out0 = acc0 / acc0[:, HEAD_DIM:HEAD_DIM + 1]
    out1 = acc1 / acc1[:, 0:1]
    o_ref[...] = splice(0, out0, out1).astype(BF16)


def _attention(q, aq, k, ak, v, tot, bsz, seq):
    t, aw = q.shape
    bq = ATTN_BQ
    nq = seq // bq
    nk = seq // ATTN_BK
    assert HEADS_PER_STEP * HEAD_DIM == LANES
    tot = tot.reshape(bsz, nk, LANES)[:, :, :N_HEADS].transpose(0, 2, 1)
    tot = jnp.broadcast_to(tot[..., None], (bsz, N_HEADS, nk, LANES))
    qblk = pl.BlockSpec((bq, LANES), lambda b, h, i: (b * nq + i, h))
    kblk = pl.BlockSpec((None, seq, LANES), lambda b, h, i: (b, 0, h))
    seq3 = lambda a: a.reshape(bsz, seq, aw)
    return pl.pallas_call(
        _attn_kernel,
        grid=(bsz, N_HEADS // HEADS_PER_STEP, nq),
        in_specs=[qblk, qblk, kblk, kblk, kblk,
                  pl.BlockSpec((None, HEADS_PER_STEP, nk, LANES), lambda b, h, i: (b, h, 0, 0))],
        out_specs=qblk,
        out_shape=jax.ShapeDtypeStruct((t, aw), BF16),
        scratch_shapes=[pltpu.VMEM((HEADS_PER_STEP, bq, LANES), BF16),
                        pltpu.VMEM((HEADS_PER_STEP, bq, LANES), F32),
                        pltpu.VMEM((HEADS_PER_STEP, bq, LANES), F32)],
        compiler_params=_cparams("arbitrary", "arbitrary", "arbitrary"),
        name="attn",
    )(q, aq, seq3(k), seq3(ak), seq3(v), tot)


def _pack_bf16_pair(lo, hi):
    lo_bits = pltpu.bitcast(lo.astype(BF16).astype(F32), U32)
    hi_bits = pltpu.bitcast(hi.astype(BF16).astype(F32), U32)
    return (hi_bits & jnp.uint32(0xFFFF0000)) | (lo_bits >> 16)


def _unpack_bf16_pair(w):
    lo = pltpu.bitcast(w << 16, F32)
    hi = pltpu.bitcast(w & jnp.uint32(0xFFFF0000), F32)
    return jnp.concatenate([lo, hi], axis=1)


def _expert_onehots(ids):
    lane = lax.broadcasted_iota(I32, ids.shape, 1)
    return lane, [lane == ids[:, kk:kk + 1] for kk in range(TOP_K)]


def _route_tile(ids, gates, carry, tri, utri):
    lane, hots = _expert_onehots(ids)
    multi = jnp.zeros(lane.shape, F32)
    for hot in hots:
        multi = multi + jnp.where(hot, 1.0, 0.0)
    before = jnp.dot(tri, multi.astype(BF16), preferred_element_type=F32)
    units = jnp.ceil(jnp.sum(multi, axis=0, keepdims=True) * (1.0 / CHUNK))
    units8 = jnp.broadcast_to(units, (SUBLANES, LANES)).astype(BF16)
    lstart = jnp.dot(units8, utri, preferred_element_type=F32)[0:1, :] * CHUNK
    csize = units * CHUNK
    slot = jnp.zeros(lane.shape, F32)
    for kk, hot in enumerate(hots):
        sl = jnp.sum(jnp.where(hot, lstart + before, 0.0), axis=1, keepdims=True)
        slot = jnp.where(lane == kk, sl, slot)
    row = lax.broadcasted_iota(I32, (SUBLANES, LANES), 0)
    meta = jnp.where(row == 0, csize, jnp.where(row == 1, lstart, jnp.where(row == 2, carry, 0.0)))
    slott = slot.T[0:SUBLANES, :].astype(I32)
    gatest = gates.T[0:SUBLANES, :]
    return slot.astype(I32), slott, gatest, meta, carry + csize


def _merge_kernel(x_ref, act_ref, attn_ref, sgc_ref, sga_ref, mod_ref, g2_ref, wco_ref, wao_ref, wout_ref,
                  wr_ref, br_ref, tri_ref, utri_ref,
                  x1_ref, h2_ref, slot_ref, slott_ref, gatest_ref, meta_ref, cnt_ref, carry_scr):
    @pl.when(pl.program_id(0) == 0)
    def _():
        carry_scr[...] = jnp.zeros(carry_scr.shape, F32)

    conv_out = jnp.dot(act_ref[...], wco_ref[...], preferred_element_type=F32)
    attn_out = jnp.dot(attn_ref[...], wao_ref[...], preferred_element_type=F32)
    mixed = sgc_ref[...].astype(F32) * conv_out + sga_ref[...].astype(F32) * attn_out
    upd = jnp.dot(mixed.astype(BF16), wout_ref[...], preferred_element_type=F32)
    x1 = x_ref[...] + mod_ref[2:3, :] * upd
    x1_ref[...] = x1
    h2 = _rms(x1) * g2_ref[...] * (1.0 + mod_ref[4:5, :]) + mod_ref[3:4, :]
    h2_hi = h2.astype(BF16)
    h2_ref[...] = h2_hi
    h2_lo = (h2 - h2_hi.astype(F32)).astype(BF16)
    h2_cat = jnp.concatenate([h2_hi, h2_lo, h2_hi], axis=1)
    logits = jnp.dot(h2_cat, wr_ref[...], preferred_element_type=F32) + br_ref[...]
    lane = lax.broadcasted_iota(I32, logits.shape, 1)
    work = jnp.where(lane < N_EXPERTS, logits, -jnp.inf)
    ids = jnp.zeros(logits.shape, I32)
    vals = []
    for kk in range(TOP_K):
        mx = jnp.max(work, axis=1, keepdims=True)
        idx = jnp.min(jnp.where(work == mx, lane, LANES), axis=1, keepdims=True)
        ids = jnp.where(lane == kk, idx, ids)
        vals.append(mx)
        work = jnp.where(lane == idx, -jnp.inf, work)
    exps = [jnp.exp(vv - vals[0]) for vv in vals]
    den = exps[0] + exps[1] + exps[2] + exps[3]
    gates = jnp.zeros(logits.shape, F32)
    for kk in range(TOP_K):
        gates = jnp.where(lane == kk, exps[kk] / den, gates)

    carry = carry_scr[0:1, :]
    tt = TOKEN_TILE
    for sub in range(logits.shape[0] // tt):
        rows = slice(sub * tt, (sub + 1) * tt)
        slot, slott, gatest, meta, carry = _route_tile(ids[rows, :], gates[rows, :], carry,
                                                       tri_ref[...], utri_ref[...])
        slot_ref[rows, :] = slot
        slott_ref[sub * SUBLANES:(sub + 1) * SUBLANES, :] = slott
        gatest_ref[sub * SUBLANES:(sub + 1) * SUBLANES, :] = gatest
        meta_ref[sub] = meta
    carry_scr[...] = jnp.broadcast_to(carry, carry_scr.shape)
    cnt_ref[...] = jnp.broadcast_to(carry, cnt_ref.shape)


def _merge(x2, act, attn, sgc, sga, mod8, norm2_g, w_conv_out, w_attn_out, w_out, w_router, b_router, seq):
    t, d = x2.shape
    tm = MERGE_TILE
    tt = TOKEN_TILE
    sub = tm // tt
    nt = t // tt
    cw = act.shape[1]
    sw = attn.shape[1]
    wco = w_conv_out.astype(BF16)
    wao = w_attn_out.astype(BF16)
    wout = w_out.astype(BF16)
    wr = jnp.pad(w_router, ((0, 0), (0, LANES - N_EXPERTS)))
    wr_hi = wr.astype(BF16)
    wr_lo = (wr - wr_hi.astype(F32)).astype(BF16)
    wr = jnp.concatenate([wr_hi, wr_hi, wr_lo], axis=0)
    br = jnp.pad(b_router, (0, LANES - N_EXPERTS))[None]
    r = jnp.arange(tt)
    tri = (r[None, :] < r[:, None]).astype(BF16)
    e = jnp.arange(LANES)
    utri = (e[:, None] < e[None, :]).astype(BF16)
    tiles_per_seq = seq // tm
    row = lambda i: (i, 0)
    return pl.pallas_call(
        _merge_kernel,
        grid=(t // tm,),
        in_specs=[pl.BlockSpec((tm, d), row), pl.BlockSpec((tm, cw), row), pl.BlockSpec((tm, sw), row),
                  pl.BlockSpec((tm, d), row), pl.BlockSpec((tm, d), row),
                  pl.BlockSpec((None, SUBLANES, d), lambda i: (i // tiles_per_seq, 0, 0)),
                  _resident((1, d)), _resident(wco.shape), _resident(wao.shape), _resident(wout.shape),
                  _resident(wr.shape), _resident(br.shape), _resident(tri.shape), _resident(utri.shape)],
        out_specs=[pl.BlockSpec((tm, d), row), pl.BlockSpec((tm, d), row),
                   pl.BlockSpec((tm, LANES), row),
                   pl.BlockSpec((sub * SUBLANES, tt), row), pl.BlockSpec((sub * SUBLANES, tt), row),
                   pl.BlockSpec((sub, SUBLANES, LANES), lambda i: (i, 0, 0)),
                   pl.BlockSpec((SUBLANES, LANES), lambda i: (0, 0))],
        out_shape=[jax.ShapeDtypeStruct((t, d), F32), jax.ShapeDtypeStruct((t, d), BF16),
                   jax.ShapeDtypeStruct((t, LANES), I32),
                   jax.ShapeDtypeStruct((nt * SUBLANES, tt), I32), jax.ShapeDtypeStruct((nt * SUBLANES, tt), F32),
                   jax.ShapeDtypeStruct((nt, SUBLANES, LANES), F32),
                   jax.ShapeDtypeStruct((SUBLANES, LANES), F32)],
        scratch_shapes=[pltpu.VMEM((SUBLANES, LANES), F32)],
        compiler_params=_cparams("arbitrary"),
        name="merge",
    )(x2, act, attn, sgc, sga, mod8, norm2_g[None], wco, wao, wout, wr, br, tri, utri)


def _dest_kernel(cnt_ref, meta_ref, tri_ref, gd_ref, blk_ref):
    cnt = cnt_ref[...]
    padded = jnp.ceil(cnt * (1.0 / EXPERT_BLOCK)) * EXPERT_BLOCK
    pend = jnp.dot(padded, tri_ref[...], preferred_element_type=F32, precision=HIGHEST)
    pstart = (pend - padded)[0:1, :]

    grow = lax.broadcasted_iota(I32, (SUBLANES, LANES), 0)
    glane = lax.broadcasted_iota(I32, (SUBLANES, LANES), 1)
    gpos = ((grow * LANES + glane) * CHUNK).astype(F32)
    last = jnp.logical_and(grow == SUBLANES - 1, glane == LANES - 1)
    for tile in range(meta_ref.shape[0]):
        meta = meta_ref[tile]
        csize, lstart, seg_off = meta[0:1, :], meta[1:2, :], meta[2:3, :]
        lend = lstart + csize
        base = pstart + seg_off - lstart
        owner = jnp.zeros(gpos.shape, I32)
        for e in range(N_EXPERTS):
            owner = owner + jnp.where(gpos >= lend[0:1, e:e + 1], 1, 0)
        gdst = gpos
        for e in range(N_EXPERTS):
            gdst = gdst + jnp.where(owner == e, base[0:1, e:e + 1], 0.0)
        n_groups = lend[0:1, N_EXPERTS - 1:N_EXPERTS] * (1.0 / CHUNK)
        gd_ref[tile] = jnp.where(last, n_groups, gdst).astype(I32)

    @pl.when(pl.program_id(0) == 0)
    def _():
        shape = blk_ref.shape[1:]
        pos = (lax.broadcasted_iota(I32, shape, 0) * LANES + lax.broadcasted_iota(I32, shape, 1))
        pos = pos.astype(F32) * EXPERT_BLOCK
        be = jnp.zeros(shape, I32)
        for e in range(N_EXPERTS):
            be = be + jnp.where(pos >= pend[0:1, e:e + 1], 1, 0)
        blk_ref[0] = jnp.minimum(be, N_EXPERTS - 1)
        nact = pend[0:1, N_EXPERTS - 1:N_EXPERTS] * (1.0 / EXPERT_BLOCK)
        blk_ref[1] = jnp.broadcast_to(nact, shape).astype(I32)
        blk_ref[2] = jnp.broadcast_to(pend[0:1, :], shape).astype(I32)


def _dest(cnt, meta):
    nt = meta.shape[0]
    per_step = DEST_TILES_PER_STEP
    r = jnp.arange(LANES)
    tri = (r[:, None] <= r[None, :]).astype(F32)
    return pl.pallas_call(
        _dest_kernel,
        grid=(nt // per_step,),
        in_specs=[pl.BlockSpec((SUBLANES, LANES), lambda i: (0, 0)),
                  pl.BlockSpec((per_step, SUBLANES, LANES), lambda i: (i, 0, 0)),
                  pl.BlockSpec((LANES, LANES), lambda i: (0, 0))],
        out_specs=[pl.BlockSpec((per_step, SUBLANES, LANES), lambda i: (i, 0, 0)),
                   pl.BlockSpec((3, SUBLANES, LANES), lambda i: (0, 0, 0))],
        out_shape=[jax.ShapeDtypeStruct((nt, SUBLANES, LANES), I32),
                   jax.ShapeDtypeStruct((3, SUBLANES, LANES), I32)],
        compiler_params=_cparams("arbitrary"),
        name="dest",
    )(cnt, meta, tri)


def _row_copy(src, dst, sem):
    return pltpu.make_async_copy(src, dst, sem)


def _dispatch_kernel(pend_ref, na_ref, gd_ref, slott_ref, gatest_ref, h2_ref, xs_ref, zero_ref, sorted_scr, sem):
    tm = h2_ref.shape[0]
    bm = zero_ref.shape[0]
    nb = xs_ref.shape[0] // bm

    @pl.when(pl.program_id(0) == 0)
    def _():
        zero_ref[...] = jnp.zeros(zero_ref.shape, zero_ref.dtype)

        def zero_block(start):
            return _row_copy(zero_ref, xs_ref.at[pl.ds(pl.multiple_of(start, bm), bm)], sem)

        def nonempty(e):
            return pend_ref[e] > (pend_ref[e - 1] if e > 0 else 0)

        for e in range(N_EXPERTS):
            @pl.when(nonempty(e))
            def _():
                zero_block(pend_ref[e] - bm).start()

        def tail(b, carry):
            zero_block(b * bm).start()
            return carry

        lax.fori_loop(na_ref[0], nb, tail, 0)
        for e in range(N_EXPERTS):
            @pl.when(nonempty(e))
            def _():
                zero_block(0).wait()

        def tail_wait(b, carry):
            zero_block(0).wait()
            return carry

        lax.fori_loop(na_ref[0], nb, tail_wait, 0)

    srow = lax.broadcasted_iota(I32, (SLOT_ROWS, tm), 0)
    perm = jnp.zeros((SLOT_ROWS, tm), F32)
    gate = jnp.zeros((SLOT_ROWS, 1), F32)
    for kk in range(TOP_K):
        hit = srow == slott_ref[kk:kk + 1, :]
        perm = perm + jnp.where(hit, 1.0, 0.0)
        gate = gate + jnp.sum(jnp.where(hit, gatest_ref[kk:kk + 1, :], 0.0), axis=1, keepdims=True)
    perm = perm.astype(BF16)
    half = h2_ref.shape[1] // 2
    lo = jnp.dot(perm, h2_ref[:, :half], preferred_element_type=F32)
    hi = jnp.dot(perm, h2_ref[:, half:], preferred_element_type=F32)
    sorted_scr[:, :half] = pltpu.bitcast(hi, U32) | (pltpu.bitcast(lo, U32) >> 16)
    sorted_scr[:, half:] = pltpu.bitcast(jnp.broadcast_to(gate, (SLOT_ROWS, LANES)), U32)

    n_groups = gd_ref[GROUP_TABLE - 1]

    def group_copy(j, d):
        src = sorted_scr.at[pl.ds(pl.multiple_of(j * CHUNK, CHUNK), CHUNK)]
        return _row_copy(src, xs_ref.at[pl.ds(pl.multiple_of(d, CHUNK), CHUNK)], sem)

    def issue(j, carry):
        group_copy(j, gd_ref[j]).start()
        return carry

    def retire(j, carry):
        group_copy(0, 0).wait()
        return carry

    lax.fori_loop(0, n_groups, issue, 0)
    lax.fori_loop(0, n_groups, retire, 0)


def _dispatch(pend, n_active, gd_flat, slott, gatest, h2b, n_rows):
    t, d = h2b.shape
    w = d // 2 + LANES
    tm = TOKEN_TILE
    grid_spec = pltpu.PrefetchScalarGridSpec(
        num_scalar_prefetch=2,
        grid=(t // tm,),
        in_specs=[pl.BlockSpec((GROUP_TABLE,), lambda i, pe, na: (i,), memory_space=pltpu.SMEM),
                  pl.BlockSpec((SUBLANES, tm), lambda i, pe, na: (i, 0)),
                  pl.BlockSpec((SUBLANES, tm), lambda i, pe, na: (i, 0)),
                  pl.BlockSpec((tm, d), lambda i, pe, na: (i, 0))],
        out_specs=pl.BlockSpec(memory_space=pl.ANY),
        scratch_shapes=[pltpu.VMEM((EXPERT_BLOCK, w), U32), pltpu.VMEM((SLOT_ROWS, w), U32),
                        pltpu.SemaphoreType.DMA(())],
    )
    return pl.pallas_call(
        _dispatch_kernel,
        grid_spec=grid_spec,
        out_shape=jax.ShapeDtypeStruct((n_rows, w), U32),
        compiler_params=_cparams("arbitrary"),
        name="dispatch",
    )(pend, n_active, gd_flat, slott, gatest, h2b)


def _expert_kernel(be_ref, na_ref, xs_ref, w1_hbm, b1_ref, w2_hbm, b2_ref, y_ref,
                   w1f_scr, w2f_scr, w1b_ref, w2b_ref, slot_scr, sem, *, d_expert):
    i = pl.program_id(0)
    n_active = na_ref[0]
    active = i < n_active
    expert = be_ref[i]
    prev = be_ref[jnp.maximum(i - 1, 0)]
    new_expert = jnp.logical_or(i == 0, expert != prev)
    half = y_ref.shape[1]

    def weight_copies(e, slot):
        return (pltpu.make_async_copy(w1_hbm.at[e], w1f_scr.at[slot], sem.at[slot, 0]),
                pltpu.make_async_copy(w2_hbm.at[e], w2f_scr.at[slot], sem.at[slot, 1]))

    @pl.when(jnp.logical_and(i == 0, active))
    def _():
        slot_scr[0] = 0
        for cp in weight_copies(expert, 0):
            cp.start()

    @pl.when(jnp.logical_and(active, new_expert))
    def _():
        slot = slot_scr[0]
        for cp in weight_copies(expert, slot):
            cp.wait()
        nxt = lax.while_loop(lambda j: jnp.logical_and(j < n_active, be_ref[jnp.minimum(j, n_active - 1)] == expert),
                             lambda j: j + 1, i + 1)

        @pl.when(nxt < n_active)
        def _():
            for cp in weight_copies(be_ref[nxt], 1 - slot):
                cp.start()

        w1b_ref[...] = w1f_scr[slot].astype(BF16)
        w2b_ref[...] = w2f_scr[slot].astype(BF16)
        slot_scr[0] = 1 - slot

    @pl.when(active)
    def _():
        x = _unpack_bf16_pair(xs_ref[:, :half]).astype(BF16)
        gate = pltpu.bitcast(xs_ref[:, half:], F32)[:, 0:1]
        hc = jnp.dot(x, w1b_ref[...], preferred_element_type=F32) + b1_ref[...]
        g = jnp.minimum(hc[:, :d_expert], SWIGLU_LIMIT)
        u = jnp.clip(hc[:, d_expert:], -SWIGLU_LIMIT, SWIGLU_LIMIT)
        hm = (u + 1.0) * (g * jax.nn.sigmoid(SWIGLU_ALPHA * g))
        y = (jnp.dot(hm.astype(BF16), w2b_ref[...], preferred_element_type=F32) + b2_ref[...]) * gate
        y_ref[...] = _pack_bf16_pair(y[:, :half], y[:, half:])

    @pl.when(jnp.logical_not(active))
    def _():
        y_ref[...] = jnp.zeros(y_ref.shape, U32)


def _experts(block_e, n_active, xs, w1, b1, w2, b2):
    n_rows, w = xs.shape
    n_exp, d, two_de = w1.shape
    de = two_de // 2
    bm = EXPERT_BLOCK
    nb = n_rows // bm

    def blk(i, be, na):
        return jnp.maximum(jnp.minimum(i, na[0] - 1), 0)

    grid_spec = pltpu.PrefetchScalarGridSpec(
        num_scalar_prefetch=2,
        grid=(nb,),
        in_specs=[pl.BlockSpec((bm, w), lambda i, be, na: (blk(i, be, na), 0)),
                  pl.BlockSpec(memory_space=pl.ANY),
                  pl.BlockSpec((None, 1, two_de), lambda i, be, na: (be[blk(i, be, na)], 0, 0)),
                  pl.BlockSpec(memory_space=pl.ANY),
                  pl.BlockSpec((None, 1, d), lambda i, be, na: (be[blk(i, be, na)], 0, 0))],
        out_specs=pl.BlockSpec((bm, d // 2), lambda i, be, na: (i, 0)),
        scratch_shapes=[pltpu.VMEM((2, d, two_de), F32), pltpu.VMEM((2, de, d), F32),
                        pltpu.VMEM((d, two_de), BF16), pltpu.VMEM((de, d), BF16),
                        pltpu.SMEM((1,), I32), pltpu.SemaphoreType.DMA((2, 2))],
    )
    return pl.pallas_call(
        functools.partial(_expert_kernel, d_expert=de),
        grid_spec=grid_spec,
        out_shape=jax.ShapeDtypeStruct((n_rows, d // 2), U32),
        compiler_params=_cparams("arbitrary"),
        name="expert",
    )(block_e, n_active, xs, w1, b1[:, None, :], w2, b2[:, None, :])


def _combine_kernel(gd_ref, gd_next_ref, slot_ref, x1_ref, mod_ref, fg_ref, yb_ref, o_ref, sorted_scr, sem):
    tm = x1_ref.shape[0]
    step = pl.program_id(0)
    cur = step % 2

    def group_copy(buf, j, d):
        dst = sorted_scr.at[buf, pl.ds(pl.multiple_of(j * CHUNK, CHUNK), CHUNK)]
        return _row_copy(yb_ref.at[pl.ds(pl.multiple_of(d, CHUNK), CHUNK)], dst, sem.at[buf])

    def fetch(table_ref, buf):
        def issue(j, carry):
            group_copy(buf, j, table_ref[j]).start()
            return carry
        lax.fori_loop(0, table_ref[GROUP_TABLE - 1], issue, 0)

    @pl.when(step == 0)
    def _():
        sorted_scr[...] = jnp.zeros(sorted_scr.shape, sorted_scr.dtype)
        fetch(gd_ref, 0)

    @pl.when(step + 1 < pl.num_programs(0))
    def _():
        fetch(gd_next_ref, 1 - cur)

    def retire(j, carry):
        group_copy(cur, 0, 0).wait()
        return carry

    lax.fori_loop(0, gd_ref[GROUP_TABLE - 1], retire, 0)

    slot = slot_ref[...]
    scol = lax.broadcasted_iota(I32, (tm, SLOT_ROWS), 1)
    pick = jnp.zeros((tm, SLOT_ROWS), F32)
    for kk in range(TOP_K):
        pick = pick + jnp.where(scol == slot[:, kk:kk + 1], 1.0, 0.0)
    pick = pick.astype(BF16)
    words = sorted_scr[cur]
    lo = pltpu.bitcast(words << 16, F32).astype(BF16)
    hi = pltpu.bitcast(words & jnp.uint32(0xFFFF0000), F32).astype(BF16)
    y = jnp.concatenate([jnp.dot(pick, lo, preferred_element_type=F32),
                         jnp.dot(pick, hi, preferred_element_type=F32)], axis=1)
    x2 = x1_ref[...] + mod_ref[5:6, :] * y
    o_ref[...] = _rms(x2) * fg_ref[...]


def _combine(gd_flat, slot, x1, mod8, final_g, yb, seq):
    t, d = x1.shape
    tm = TOKEN_TILE
    nt = t // tm
    tiles_per_seq = seq // tm
    return pl.pallas_call(
        _combine_kernel,
        grid=(nt,),
        in_specs=[pl.BlockSpec((GROUP_TABLE,), lambda i: (i,), memory_space=pltpu.SMEM),
                  pl.BlockSpec((GROUP_TABLE,), lambda i: (jnp.minimum(i + 1, nt - 1),), memory_space=pltpu.SMEM),
                  pl.BlockSpec((tm, LANES), lambda i: (i, 0)),
                  pl.BlockSpec((tm, d), lambda i: (i, 0)),
                  pl.BlockSpec((None, SUBLANES, d), lambda i: (i // tiles_per_seq, 0, 0)),
                  pl.BlockSpec((1, d), lambda i: (0, 0)),
                  pl.BlockSpec(memory_space=pl.ANY)],
        out_specs=pl.BlockSpec((tm, d), lambda i: (i, 0)),
        out_shape=jax.ShapeDtypeStruct((t, d), F32),
        scratch_shapes=[pltpu.VMEM((2, SLOT_ROWS, d // 2), U32), pltpu.SemaphoreType.DMA((2,))],
        compiler_params=_cparams("arbitrary"),
        name="combine",
    )(gd_flat, gd_flat, slot, x1, mod8, final_g[None], yb)


def kernel(x, c, w_ada, b_ada, norm1_g, w_in, b_forget, conv_w, conv_b, conv_ln_g, conv_ln_b, w_conv_out,
           w_attn_out, w_out, norm2_g, w_router, b_router, w_exp_in, b_exp_in, w_exp_out, b_exp_out, final_g):
    bsz, seq, d = x.shape
    t = bsz * seq
    depth = w_ada.shape[0]
    xf = x.reshape(t, d)
    n_rows = t * TOP_K + (t // TOKEN_TILE) * N_EXPERTS * (CHUNK - 1) + N_EXPERTS * (EXPERT_BLOCK - CHUNK)
    n_rows = -(-n_rows // EXPERT_BLOCK) * EXPERT_BLOCK
    n_blocks = n_rows // EXPERT_BLOCK
    for l in range(depth):
        mod8 = _mod(c, w_ada, b_ada[l], l)
        act, q, k, v, aq, ak, tot, sgc, sga = _inproj(xf, mod8, norm1_g[l], w_in[l], b_forget[l], conv_w[l],
                                                      conv_b[l], conv_ln_g[l], conv_ln_b[l], seq)
        attn = _attention(q, aq, k, ak, v, tot, bsz, seq)
        x1, h2b, slot, slott, gatest, meta, cnt = _merge(xf, act, attn, sgc, sga, mod8, norm2_g[l], w_conv_out[l],
                                                         w_attn_out[l], w_out[l], w_router[l], b_router[l], seq)
        gd, blk = _dest(cnt, meta)
        gd_flat = gd.reshape(-1)
        block_e = blk[0].reshape(-1)[:n_blocks]
        n_active = blk[1, 0, :1]
        pend = blk[2, 0, :N_EXPERTS]
        xs = _dispatch(pend, n_active, gd_flat, slott, gatest, h2b, n_rows)
        yb = _experts(block_e, n_active, xs, w_exp_in[l], b_exp_in[l], w_exp_out[l], b_exp_out[l])
        assert depth == 1
        xf = _combine(gd_flat, slot, x1, mod8, final_g, yb, seq)
    return xf.reshape(bsz, seq, d)
```

```python
import functools

import jax
import jax.numpy as jnp
from jax import lax
from jax.experimental import pallas as pl
from jax.experimental.pallas import tpu as pltpu

F32 = jnp.float32
BF16 = jnp.bfloat16
I32 = jnp.int32
U32 = jnp.uint32
HIGHEST = lax.Precision.HIGHEST

LANES = 128
SUBLANES = 8
VMEM_LIMIT_BYTES = 56 * 1024 * 1024

N_HEADS = 8
HEAD_DIM = 64
CONV_KERNEL = 31
CONV_HALO = 32
N_EXPERTS = 32
TOP_K = 4
SWIGLU_LIMIT = 7.0
SWIGLU_ALPHA = 1.702
RMS_EPS = 1e-5
LN_EPS = 1e-5
N_MOD = 6
N_SPLIT = 3
LOG2_E = 1.4426950408889634
HEADS_PER_STEP = 2

INPROJ_TILE = 512
ATTN_BQ = 4096
ATTN_BK = 512
MERGE_TILE = 1024
TOKEN_TILE = 256
EXPERT_BLOCK = 768
CHUNK = SUBLANES
SLOT_ROWS = 1280
GROUP_TABLE = SUBLANES * LANES
DEST_TILES_PER_STEP = 8
assert SLOT_ROWS >= TOKEN_TILE * TOP_K + N_EXPERTS * (CHUNK - 1) and SLOT_ROWS // CHUNK < GROUP_TABLE


def _cparams(*sem):
    return pltpu.CompilerParams(dimension_semantics=sem, vmem_limit_bytes=VMEM_LIMIT_BYTES)


def _resident(shape):
    nd = len(shape)
    return pl.BlockSpec(shape, lambda *_: (0,) * nd, pipeline_mode=pl.Buffered(1))


def _split3(x):
    p1 = x.astype(BF16)
    r1 = x - p1.astype(F32)
    p2 = r1.astype(BF16)
    p3 = (r1 - p2.astype(F32)).astype(BF16)
    return jnp.concatenate([p1, p2, p3], axis=1)


def _rms(x):
    return x * lax.rsqrt(jnp.mean(x * x, axis=-1, keepdims=True) + RMS_EPS)


def _mod_kernel(ct_ref, w_ref, b_ref, o_ref, *, batch):
    rows = []
    for b in range(batch):
        col = ct_ref[:, b:b + 1]
        rows.append(jnp.sum((col * jax.nn.sigmoid(col)) * w_ref[...], axis=0, keepdims=True) + b_ref[...])
    rows.append(jnp.zeros((SUBLANES - batch, rows[0].shape[1]), F32))
    o_ref[...] = jnp.concatenate(rows, axis=0)


def _mod(c, w_ada, b_ada, layer):
    bsz, d = c.shape
    n = w_ada.shape[2]
    tn = 1024
    out = pl.pallas_call(
        functools.partial(_mod_kernel, batch=bsz),
        grid=(n // tn,),
        in_specs=[pl.BlockSpec((d, bsz), lambda j: (0, 0)),
                  pl.BlockSpec((None, d, tn), lambda j: (layer, 0, j)),
                  pl.BlockSpec((1, tn), lambda j: (0, j))],
        out_specs=pl.BlockSpec((SUBLANES, tn), lambda j: (0, j)),
        out_shape=jax.ShapeDtypeStruct((SUBLANES, n), F32),
        compiler_params=_cparams("arbitrary"),
        name="mod",
    )(c.T, w_ada, b_ada[None])
    mod = out[:bsz].reshape(bsz, N_MOD, d)
    return jnp.pad(mod, ((0, 0), (0, SUBLANES - N_MOD), (0, 0)))


def _conv_module(a, first_of_seq, w_ref, b_ref, g_ref, beta_ref, xs_scr, ys_scr):
    ts = a.shape[0]
    xs_scr[0:CONV_HALO, :] = jnp.where(first_of_seq, 0.0, xs_scr[ts:ts + CONV_HALO, :])
    xs_scr[CONV_HALO:, :] = a
    acc = jnp.zeros(a.shape, F32) + b_ref[...]
    base = CONV_HALO - (CONV_KERNEL - 1)
    for rho in range(SUBLANES):
        offs = [o for o in range(base, base + CONV_KERNEL) if o % SUBLANES == rho]
        if not offs:
            continue
        rows = offs[-1] + ts - rho
        ys_scr[0:rows, :] = xs_scr[rho:rho + rows, :]
        for o in offs:
            lo = o - rho
            acc = acc + w_ref[o - base:o - base + 1, :] * ys_scr[lo:lo + ts, :]
    mu = jnp.mean(acc, axis=-1, keepdims=True)
    cen = acc - mu
    var = jnp.mean(cen * cen, axis=-1, keepdims=True)
    y = cen * lax.rsqrt(var + LN_EPS) * g_ref[...] + beta_ref[...]
    return (y * jax.nn.sigmoid(y)).astype(BF16)


def _inproj_kernel(x_ref, mod_ref, g_ref, wa_ref, wqkv_ref, wf_ref, wg_ref, bf_ref,
                   tri_ref, eq_ref, ek_ref, oneq_ref, onek_ref, cw_ref, cb_ref, cg_ref, cbeta_ref,
                   act_ref, q_ref, k_ref, v_ref, aq_ref, ak_ref, tot_ref, sgc_ref, sga_ref,
                   carry_scr, xs_scr, ys_scr, *, conv_width, d_model, tiles_per_seq):
    first_of_seq = pl.program_id(0) % tiles_per_seq == 0

    @pl.when(pl.program_id(0) == 0)
    def _():
        xs_scr[...] = jnp.zeros(xs_scr.shape, F32)
        carry_scr[...] = jnp.zeros(carry_scr.shape, F32)

    x = x_ref[...]
    h = _rms(x) * g_ref[...] * (1.0 + mod_ref[1:2, :]) + mod_ref[0:1, :]
    hb = h.astype(BF16)

    u = jnp.dot(hb, wa_ref[...], preferred_element_type=F32)
    a = u[:, :conv_width] * jax.nn.sigmoid(u[:, conv_width:])
    act_ref[...] = _conv_module(a, first_of_seq, cw_ref, cb_ref, cg_ref, cbeta_ref, xs_scr, ys_scr)

    fl = jnp.dot(hb, wf_ref[...], preferred_element_type=F32) + bf_ref[...]
    lane = lax.broadcasted_iota(I32, fl.shape, 1)
    log_f = -(jnp.maximum(-fl, 0.0) + jnp.log1p(jnp.exp(-jnp.abs(fl))))
    log_f = jnp.where(lane < N_HEADS, log_f, 0.0)
    cs = jnp.dot(tri_ref[...], _split3(log_f), preferred_element_type=F32)
    cum = (cs[:, :LANES] + cs[:, LANES:2 * LANES] + cs[:, 2 * LANES:]) * LOG2_E
    tm = cum.shape[0]

    pieces = []
    for sb in range(tm // ATTN_BK):
        blk = cum[sb * ATTN_BK:(sb + 1) * ATTN_BK, :]
        if sb > 0:
            blk = blk - cum[sb * ATTN_BK - 1:sb * ATTN_BK, :]
        tot_ref[sb] = blk[ATTN_BK - 1:, :]
        pieces.append(blk)
    cum_k = pieces[0] if len(pieces) == 1 else jnp.concatenate(pieces, axis=0)

    first_of_block = pl.program_id(0) % (ATTN_BQ // tm) == 0
    cum_q = cum + jnp.where(first_of_block, 0.0, carry_scr[0:1, :])
    carry_scr[...] = jnp.broadcast_to(cum_q[tm - 1:, :], carry_scr.shape)

    qkv = jnp.dot(hb, wqkv_ref[...], preferred_element_type=F32)
    aw = qkv.shape[1] // 3
    q_ref[...] = (qkv[:, :aw] * (HEAD_DIM ** -0.5 * LOG2_E)).astype(BF16)
    k_ref[...] = qkv[:, aw:2 * aw].astype(BF16)
    v_ref[...] = qkv[:, 2 * aw:].astype(BF16)
    aq_ref[...] = (jnp.dot(_split3(cum_q), eq_ref[...], preferred_element_type=F32) + oneq_ref[...]).astype(BF16)
    ak_ref[...] = (jnp.dot(_split3(cum_k), ek_ref[...], preferred_element_type=F32) + onek_ref[...]).astype(BF16)

    gts = jnp.dot(hb, wg_ref[...], preferred_element_type=F32)
    sgc_ref[...] = jax.nn.sigmoid(gts[:, :d_model]).astype(BF16)
    sga_ref[...] = jax.nn.sigmoid(gts[:, d_model:]).astype(BF16)


def _aug_constants():
    rows = jnp.arange(N_SPLIT * LANES)
    piece, head = rows // LANES, rows % LANES
    cols = jnp.arange(N_HEADS * HEAD_DIM)
    pair, off = cols // LANES, cols % LANES
    chead = 2 * pair + jnp.where(off >= HEAD_DIM, 0, 1)
    pos = off % HEAD_DIM
    valid = (head[:, None] < N_HEADS) & (head[:, None] == chead[None, :])
    eq = jnp.where(valid & (pos[None, :] == piece[:, None]), 1.0, 0.0).astype(BF16)
    ek = jnp.where(valid & (pos[None, :] == piece[:, None] + N_SPLIT), -1.0, 0.0).astype(BF16)
    oneq = jnp.where((pos >= N_SPLIT) & (pos < 2 * N_SPLIT), 1.0, 0.0).astype(F32)[None]
    onek = jnp.where(pos < N_SPLIT, 1.0, 0.0).astype(F32)[None]
    return eq, ek, oneq, onek


def _inproj(x2, mod8, norm_g, w_in, b_forget, conv_w, conv_b, conv_ln_g, conv_ln_b, seq):
    t, d = x2.shape
    tm = INPROJ_TILE
    nsub = tm // ATTN_BK
    cw = d // 2
    aw = N_HEADS * HEAD_DIM
    o0 = 2 * cw
    wa = w_in[:, :o0].astype(BF16)
    o1 = o0 + 3 * aw
    wqkv = w_in[:, o0:o1].astype(BF16)
    wf = jnp.pad(w_in[:, o1:o1 + N_HEADS], ((0, 0), (0, LANES - N_HEADS))).astype(BF16)
    bfp = jnp.pad(b_forget, (0, LANES - N_HEADS))[None]
    wg = w_in[:, o1 + N_HEADS:].astype(BF16)
    r = jnp.arange(tm)
    tri = (r[None, :] <= r[:, None]).astype(BF16)
    eq, ek, oneq, onek = _aug_constants()
    tiles_per_seq = seq // tm
    row = lambda i: (i, 0)
    outs = pl.pallas_call(
        functools.partial(_inproj_kernel, conv_width=cw, d_model=d, tiles_per_seq=tiles_per_seq),
        grid=(t // tm,),
        in_specs=[pl.BlockSpec((tm, d), row),
                  pl.BlockSpec((None, SUBLANES, d), lambda i: (i // tiles_per_seq, 0, 0)),
                  _resident((1, d)),
                  _resident(wa.shape), _resident(wqkv.shape),
                  _resident(wf.shape), _resident(wg.shape), _resident(bfp.shape),
                  _resident(tri.shape), _resident(eq.shape), _resident(ek.shape),
                  _resident(oneq.shape), _resident(onek.shape),
                  _resident(conv_w.shape), _resident((1, cw)), _resident((1, cw)), _resident((1, cw))],
        out_specs=[pl.BlockSpec((tm, cw), row)] + [pl.BlockSpec((tm, aw), row)] * 5 + [
                   pl.BlockSpec((nsub, 1, LANES), lambda i: (i, 0, 0)),
                   pl.BlockSpec((tm, d), row), pl.BlockSpec((tm, d), row)],
        out_shape=[jax.ShapeDtypeStruct((t, cw), BF16)] + [jax.ShapeDtypeStruct((t, aw), BF16)] * 5 + [
                   jax.ShapeDtypeStruct((t // ATTN_BK, 1, LANES), F32),
                   jax.ShapeDtypeStruct((t, d), BF16), jax.ShapeDtypeStruct((t, d), BF16)],
        scratch_shapes=[pltpu.VMEM((SUBLANES, LANES), F32),
                        pltpu.VMEM((CONV_HALO + tm, cw), F32), pltpu.VMEM((CONV_HALO + tm, cw), F32)],
        compiler_params=_cparams("arbitrary"),
        name="inproj",
    )(x2, mod8, norm_g[None], wa, wqkv, wf, wg, bfp, tri, eq, ek, oneq, onek,
      conv_w, conv_b[None], conv_ln_g[None], conv_ln_b[None])
    return outs


def _attn_kernel(q_ref, aq_ref, k_ref, ak_ref, v_ref, tot_ref, o_ref, qop_scr, m_scr, acc_scr):
    bq = q_ref.shape[0]
    bk = ATTN_BK
    ratio = bq // bk
    qi = pl.program_id(2)
    nt = (((1,), (1,)), ((), ()))
    heads = range(HEADS_PER_STEP)
    zero = jnp.zeros((1, LANES), F32)

    def splice(hh, main, extra):
        first = lax.broadcasted_iota(I32, main.shape, 1) < HEAD_DIM
        return jnp.where(first, main, extra) if hh == 0 else jnp.where(first, extra, main)

    for hh in heads:
        qop_scr[hh] = splice(hh, q_ref[...], aq_ref[...])

    def scores(hh, kb, row0):
        start = pl.multiple_of(kb * bk, bk)
        k = splice(hh, k_ref[pl.ds(start, bk), :], ak_ref[pl.ds(start, bk), :])
        v = v_ref[pl.ds(start, bk), :]
        v = splice(hh, v, jnp.ones(v.shape, v.dtype))
        s = lax.dot_general(qop_scr[hh, row0:, :], k, nt, preferred_element_type=F32)
        return s, v

    def update(hh, s, v, delta, row0):
        m_prev = m_scr[hh, row0:, :]
        m_new = jnp.maximum(m_prev, jnp.max(s, axis=1, keepdims=True) + delta)
        shift = m_new - delta
        p = jnp.exp2((s - shift[:, :1]).astype(BF16))
        alpha = jnp.exp2(m_prev - m_new)
        pv = jnp.dot(p, v, preferred_element_type=F32)
        acc_scr[hh, row0:, :] = alpha * acc_scr[hh, row0:, :] + pv
        m_scr[hh, row0:, :] = m_new

    for hh in heads:
        delta = zero
        for sb in range(ratio):
            kb = qi * ratio + sb
            row0 = sb * bk
            s, v = scores(hh, kb, row0)
            row = lax.broadcasted_iota(I32, s.shape, 0)
            col = lax.broadcasted_iota(I32, s.shape, 1)
            s = jnp.where(col <= row, s, -jnp.inf)
            if sb == 0:
                m0 = jnp.max(s, axis=1, keepdims=True)
                p = jnp.exp2((s - m0).astype(BF16))
                m_scr[hh] = jnp.broadcast_to(m0, m_scr.shape[1:])
                acc_scr[hh] = jnp.dot(p, v, preferred_element_type=F32)
            else:
                update(hh, s, v, delta, row0)
            delta = delta - tot_ref[hh, pl.ds(kb, 1), :]

    def body(step, deltas):
        kb = qi * ratio - 1 - step
        new_deltas = []
        for hh in heads:
            delta = deltas[hh] + tot_ref[hh, pl.ds(kb, 1), :]
            s, v = scores(hh, kb, 0)
            update(hh, s, v, delta, 0)
            new_deltas.append(delta)
        return tuple(new_deltas)

    lax.fori_loop(0, qi * ratio, body, tuple(zero for _ in heads))
    acc0, acc1 = acc_scr[0], acc_scr[1]
    out0 = acc0 / acc0[:, HEAD_DIM:HEAD_DIM + 1]
    out1 = acc1 / acc1[:, 0:1]
    o_ref[...] = splice(0, out0, out1).astype(BF16)


def _attention(q, aq, k, ak, v, tot, bsz, seq):
    t, aw = q.shape
    bq = ATTN_BQ
    nq = seq // bq
    nk = seq // ATTN_BK
    assert HEADS_PER_STEP * HEAD_DIM == LANES
    tot = tot.reshape(bsz, nk, LANES)[:, :, :N_HEADS].transpose(0, 2, 1)
    tot = jnp.broadcast_to(tot[..., None], (bsz, N_HEADS, nk, LANES))
    qblk = pl.BlockSpec((bq, LANES), lambda b, h, i: (b * nq + i, h))
    kblk = pl.BlockSpec((None, seq, LANES), lambda b, h, i: (b, 0, h))
    seq3 = lambda a: a.reshape(bsz, seq, aw)
    return pl.pallas_call(
        _attn_kernel,
        grid=(bsz, N_HEADS // HEADS_PER_STEP, nq),
        in_specs=[qblk, qblk, kblk, kblk, kblk,
                  pl.BlockSpec((None, HEADS_PER_STEP, nk, LANES), lambda b, h, i: (b, h, 0, 0))],
        out_specs=qblk,
        out_shape=jax.ShapeDtypeStruct((t, aw), BF16),
        scratch_shapes=[pltpu.VMEM((HEADS_PER_STEP, bq, LANES), BF16),
                        pltpu.VMEM((HEADS_PER_STEP, bq, LANES), F32),
                        pltpu.VMEM((HEADS_PER_STEP, bq, LANES), F32)],
        compiler_params=_cparams("arbitrary", "arbitrary", "arbitrary"),
        name="attn",
    )(q, aq, seq3(k), seq3(ak), seq3(v), tot)


def _pack_bf16_pair(lo, hi):
    lo_bits = pltpu.bitcast(lo.astype(BF16).astype(F32), U32)
    hi_bits = pltpu.bitcast(hi.astype(BF16).astype(F32), U32)
    return (hi_bits & jnp.uint32(0xFFFF0000)) | (lo_bits >> 16)


def _unpack_bf16_pair(w):
    lo = pltpu.bitcast(w << 16, F32)
    hi = pltpu.bitcast(w & jnp.uint32(0xFFFF0000), F32)
    return jnp.concatenate([lo, hi], axis=1)


def _expert_onehots(ids):
    lane = lax.broadcasted_iota(I32, ids.shape, 1)
    return lane, [lane == ids[:, kk:kk + 1] for kk in range(TOP_K)]


def _route_tile(ids, gates, carry, tri, utri):
    lane, hots = _expert_onehots(ids)
    multi = jnp.zeros(lane.shape, F32)
    for hot in hots:
        multi = multi + jnp.where(hot, 1.0, 0.0)
    before = jnp.dot(tri, multi.astype(BF16), preferred_element_type=F32)
    units = jnp.ceil(jnp.sum(multi, axis=0, keepdims=True) * (1.0 / CHUNK))
    units8 = jnp.broadcast_to(units, (SUBLANES, LANES)).astype(BF16)
    lstart = jnp.dot(units8, utri, preferred_element_type=F32)[0:1, :] * CHUNK
    csize = units * CHUNK
    slot = jnp.zeros(lane.shape, F32)
    for kk, hot in enumerate(hots):
        sl = jnp.sum(jnp.where(hot, lstart + before, 0.0), axis=1, keepdims=True)
        slot = jnp.where(lane == kk, sl, slot)
    row = lax.broadcasted_iota(I32, (SUBLANES, LANES), 0)
    meta = jnp.where(row == 0, csize, jnp.where(row == 1, lstart, jnp.where(row == 2, carry, 0.0)))
    slott = slot.T[0:SUBLANES, :].astype(I32)
    gatest = gates.T[0:SUBLANES, :]
    return slot.astype(I32), slott, gatest, meta, carry + csize


def _merge_kernel(x_ref, act_ref, attn_ref, sgc_ref, sga_ref, mod_ref, g2_ref, wco_ref, wao_ref, wout_ref,
                  wr_ref, br_ref, tri_ref, utri_ref,
                  x1_ref, h2_ref, slot_ref, slott_ref, gatest_ref, meta_ref, cnt_ref, carry_scr):
    @pl.when(pl.program_id(0) == 0)
    def _():
        carry_scr[...] = jnp.zeros(carry_scr.shape, F32)

    conv_out = jnp.dot(act_ref[...], wco_ref[...], preferred_element_type=F32)
    attn_out = jnp.dot(attn_ref[...], wao_ref[...], preferred_element_type=F32)
    mixed = sgc_ref[...].astype(F32) * conv_out + sga_ref[...].astype(F32) * attn_out
    upd = jnp.dot(mixed.astype(BF16), wout_ref[...], preferred_element_type=F32)
    x1 = x_ref[...] + mod_ref[2:3, :] * upd
    x1_ref[...] = x1
    h2 = _rms(x1) * g2_ref[...] * (1.0 + mod_ref[4:5, :]) + mod_ref[3:4, :]
    h2_hi = h2.astype(BF16)
    h2_ref[...] = h2_hi
    h2_lo = (h2 - h2_hi.astype(F32)).astype(BF16)
    h2_cat = jnp.concatenate([h2_hi, h2_lo, h2_hi], axis=1)
    logits = jnp.dot(h2_cat, wr_ref[...], preferred_element_type=F32) + br_ref[...]
    lane = lax.broadcasted_iota(I32, logits.shape, 1)
    work = jnp.where(lane < N_EXPERTS, logits, -jnp.inf)
    ids = jnp.zeros(logits.shape, I32)
    vals = []
    for kk in range(TOP_K):
        mx = jnp.max(work, axis=1, keepdims=True)
        idx = jnp.min(jnp.where(work == mx, lane, LANES), axis=1, keepdims=True)
        ids = jnp.where(lane == kk, idx, ids)
        vals.append(mx)
        work = jnp.where(lane == idx, -jnp.inf, work)
    exps = [jnp.exp(vv - vals[0]) for vv in vals]
    den = exps[0] + exps[1] + exps[2] + exps[3]
    gates = jnp.zeros(logits.shape, F32)
    for kk in range(TOP_K):
        gates = jnp.where(lane == kk, exps[kk] / den, gates)

    carry = carry_scr[0:1, :]
    tt = TOKEN_TILE
    for sub in range(logits.shape[0] // tt):
        rows = slice(sub * tt, (sub + 1) * tt)
        slot, slott, gatest, meta, carry = _route_tile(ids[rows, :], gates[rows, :], carry,
                                                       tri_ref[...], utri_ref[...])
        slot_ref[rows, :] = slot
        slott_ref[sub * SUBLANES:(sub + 1) * SUBLANES, :] = slott
        gatest_ref[sub * SUBLANES:(sub + 1) * SUBLANES, :] = gatest
        meta_ref[sub] = meta
    carry_scr[...] = jnp.broadcast_to(carry, carry_scr.shape)
    cnt_ref[...] = jnp.broadcast_to(carry, cnt_ref.shape)


def _merge(x2, act, attn, sgc, sga, mod8, norm2_g, w_conv_out, w_attn_out, w_out, w_router, b_router, seq):
    t, d = x2.shape
    tm = MERGE_TILE
    tt = TOKEN_TILE
    sub = tm // tt
    nt = t // tt
    cw = act.shape[1]
    sw = attn.shape[1]
    wco = w_conv_out.astype(BF16)
    wao = w_attn_out.astype(BF16)
    wout = w_out.astype(BF16)
    wr = jnp.pad(w_router, ((0, 0), (0, LANES - N_EXPERTS)))
    wr_hi = wr.astype(BF16)
    wr_lo = (wr - wr_hi.astype(F32)).astype(BF16)
    wr = jnp.concatenate([wr_hi, wr_hi, wr_lo], axis=0)
    br = jnp.pad(b_router, (0, LANES - N_EXPERTS))[None]
    r = jnp.arange(tt)
    tri = (r[None, :] < r[:, None]).astype(BF16)
    e = jnp.arange(LANES)
    utri = (e[:, None] < e[None, :]).astype(BF16)
    tiles_per_seq = seq // tm
    row = lambda i: (i, 0)
    return pl.pallas_call(
        _merge_kernel,
        grid=(t // tm,),
        in_specs=[pl.BlockSpec((tm, d), row), pl.BlockSpec((tm, cw), row), pl.BlockSpec((tm, sw), row),
                  pl.BlockSpec((tm, d), row), pl.BlockSpec((tm, d), row),
                  pl.BlockSpec((None, SUBLANES, d), lambda i: (i // tiles_per_seq, 0, 0)),
                  _resident((1, d)), _resident(wco.shape), _resident(wao.shape), _resident(wout.shape),
                  _resident(wr.shape), _resident(br.shape), _resident(tri.shape), _resident(utri.shape)],
        out_specs=[pl.BlockSpec((tm, d), row), pl.BlockSpec((tm, d), row),
                   pl.BlockSpec((tm, LANES), row),
                   pl.BlockSpec((sub * SUBLANES, tt), row), pl.BlockSpec((sub * SUBLANES, tt), row),
                   pl.BlockSpec((sub, SUBLANES, LANES), lambda i: (i, 0, 0)),
                   pl.BlockSpec((SUBLANES, LANES), lambda i: (0, 0))],
        out_shape=[jax.ShapeDtypeStruct((t, d), F32), jax.ShapeDtypeStruct((t, d), BF16),
                   jax.ShapeDtypeStruct((t, LANES), I32),
                   jax.ShapeDtypeStruct((nt * SUBLANES, tt), I32), jax.ShapeDtypeStruct((nt * SUBLANES, tt), F32),
                   jax.ShapeDtypeStruct((nt, SUBLANES, LANES), F32),
                   jax.ShapeDtypeStruct((SUBLANES, LANES), F32)],
        scratch_shapes=[pltpu.VMEM((SUBLANES, LANES), F32)],
        compiler_params=_cparams("arbitrary"),
        name="merge",
    )(x2, act, attn, sgc, sga, mod8, norm2_g[None], wco, wao, wout, wr, br, tri, utri)


def _dest_kernel(cnt_ref, meta_ref, tri_ref, gd_ref, blk_ref):
    cnt = cnt_ref[...]
    padded = jnp.ceil(cnt * (1.0 / EXPERT_BLOCK)) * EXPERT_BLOCK
    pend = jnp.dot(padded, tri_ref[...], preferred_element_type=F32, precision=HIGHEST)
    pstart = (pend - padded)[0:1, :]

    grow = lax.broadcasted_iota(I32, (SUBLANES, LANES), 0)
    glane = lax.broadcasted_iota(I32, (SUBLANES, LANES), 1)
    gpos = ((grow * LANES + glane) * CHUNK).astype(F32)
    last = jnp.logical_and(grow == SUBLANES - 1, glane == LANES - 1)
    for tile in range(meta_ref.shape[0]):
        meta = meta_ref[tile]
        csize, lstart, seg_off = meta[0:1, :], meta[1:2, :], meta[2:3, :]
        lend = lstart + csize
        base = pstart + seg_off - lstart
        owner = jnp.zeros(gpos.shape, I32)
        for e in range(N_EXPERTS):
            owner = owner + jnp.where(gpos >= lend[0:1, e:e + 1], 1, 0)
        gdst = gpos
        for e in range(N_EXPERTS):
            gdst = gdst + jnp.where(owner == e, base[0:1, e:e + 1], 0.0)
        n_groups = lend[0:1, N_EXPERTS - 1:N_EXPERTS] * (1.0 / CHUNK)
        gd_ref[tile] = jnp.where(last, n_groups, gdst).astype(I32)

    @pl.when(pl.program_id(0) == 0)
    def _():
        shape = blk_ref.shape[1:]
        pos = (lax.broadcasted_iota(I32, shape, 0) * LANES + lax.broadcasted_iota(I32, shape, 1))
        pos = pos.astype(F32) * EXPERT_BLOCK
        be = jnp.zeros(shape, I32)
        for e in range(N_EXPERTS):
            be = be + jnp.where(pos >= pend[0:1, e:e + 1], 1, 0)
        blk_ref[0] = jnp.minimum(be, N_EXPERTS - 1)
        nact = pend[0:1, N_EXPERTS - 1:N_EXPERTS] * (1.0 / EXPERT_BLOCK)
        blk_ref[1] = jnp.broadcast_to(nact, shape).astype(I32)
        blk_ref[2] = jnp.broadcast_to(pend[0:1, :], shape).astype(I32)


def _dest(cnt, meta):
    nt = meta.shape[0]
    per_step = DEST_TILES_PER_STEP
    r = jnp.arange(LANES)
    tri = (r[:, None] <= r[None, :]).astype(F32)
    return pl.pallas_call(
        _dest_kernel,
        grid=(nt // per_step,),
        in_specs=[pl.BlockSpec((SUBLANES, LANES), lambda i: (0, 0)),
                  pl.BlockSpec((per_step, SUBLANES, LANES), lambda i: (i, 0, 0)),
                  pl.BlockSpec((LANES, LANES), lambda i: (0, 0))],
        out_specs=[pl.BlockSpec((per_step, SUBLANES, LANES), lambda i: (i, 0, 0)),
                   pl.BlockSpec((3, SUBLANES, LANES), lambda i: (0, 0, 0))],
        out_shape=[jax.ShapeDtypeStruct((nt, SUBLANES, LANES), I32),
                   jax.ShapeDtypeStruct((3, SUBLANES, LANES), I32)],
        compiler_params=_cparams("arbitrary"),
        name="dest",
    )(cnt, meta, tri)


def _row_copy(src, dst, sem):
    return pltpu.make_async_copy(src, dst, sem)


def _dispatch_kernel(pend_ref, na_ref, gd_ref, slott_ref, gatest_ref, h2_ref, xs_ref, zero_ref, sorted_scr, sem):
    tm = h2_ref.shape[0]
    bm = zero_ref.shape[0]
    nb = xs_ref.shape[0] // bm

    @pl.when(pl.program_id(0) == 0)
    def _():
        zero_ref[...] = jnp.zeros(zero_ref.shape, zero_ref.dtype)

        def zero_block(start):
            return _row_copy(zero_ref, xs_ref.at[pl.ds(pl.multiple_of(start, bm), bm)], sem)

        def nonempty(e):
            return pend_ref[e] > (pend_ref[e - 1] if e > 0 else 0)

        for e in range(N_EXPERTS):
            @pl.when(nonempty(e))
            def _():
                zero_block(pend_ref[e] - bm).start()

        def tail(b, carry):
            zero_block(b * bm).start()
            return carry

        lax.fori_loop(na_ref[0], nb, tail, 0)
        for e in range(N_EXPERTS):
            @pl.when(nonempty(e))
            def _():
                zero_block(0).wait()

        def tail_wait(b, carry):
            zero_block(0).wait()
            return carry

        lax.fori_loop(na_ref[0], nb, tail_wait, 0)

    srow = lax.broadcasted_iota(I32, (SLOT_ROWS, tm), 0)
    perm = jnp.zeros((SLOT_ROWS, tm), F32)
    gate = jnp.zeros((SLOT_ROWS, 1), F32)
    for kk in range(TOP_K):
        hit = srow == slott_ref[kk:kk + 1, :]
        perm = perm + jnp.where(hit, 1.0, 0.0)
        gate = gate + jnp.sum(jnp.where(hit, gatest_ref[kk:kk + 1, :], 0.0), axis=1, keepdims=True)
    perm = perm.astype(BF16)
    half = h2_ref.shape[1] // 2
    lo = jnp.dot(perm, h2_ref[:, :half], preferred_element_type=F32)
    hi = jnp.dot(perm, h2_ref[:, half:], preferred_element_type=F32)
    sorted_scr[:, :half] = pltpu.bitcast(hi, U32) | (pltpu.bitcast(lo, U32) >> 16)
    sorted_scr[:, half:] = pltpu.bitcast(jnp.broadcast_to(gate, (SLOT_ROWS, LANES)), U32)

    n_groups = gd_ref[GROUP_TABLE - 1]

    def group_copy(j, d):
        src = sorted_scr.at[pl.ds(pl.multiple_of(j * CHUNK, CHUNK), CHUNK)]
        return _row_copy(src, xs_ref.at[pl.ds(pl.multiple_of(d, CHUNK), CHUNK)], sem)

    def issue(j, carry):
        group_copy(j, gd_ref[j]).start()
        return carry

    def retire(j, carry):
        group_copy(0, 0).wait()
        return carry

    lax.fori_loop(0, n_groups, issue, 0)
    lax.fori_loop(0, n_groups, retire, 0)


def _dispatch(pend, n_active, gd_flat, slott, gatest, h2b, n_rows):
    t, d = h2b.shape
    w = d // 2 + LANES
    tm = TOKEN_TILE
    grid_spec = pltpu.PrefetchScalarGridSpec(
        num_scalar_prefetch=2,
        grid=(t // tm,),
        in_specs=[pl.BlockSpec((GROUP_TABLE,), lambda i, pe, na: (i,), memory_space=pltpu.SMEM),
                  pl.BlockSpec((SUBLANES, tm), lambda i, pe, na: (i, 0)),
                  pl.BlockSpec((SUBLANES, tm), lambda i, pe, na: (i, 0)),
                  pl.BlockSpec((tm, d), lambda i, pe, na: (i, 0))],
        out_specs=pl.BlockSpec(memory_space=pl.ANY),
        scratch_shapes=[pltpu.VMEM((EXPERT_BLOCK, w), U32), pltpu.VMEM((SLOT_ROWS, w), U32),
                        pltpu.SemaphoreType.DMA(())],
    )
    return pl.pallas_call(
        _dispatch_kernel,
        grid_spec=grid_spec,
        out_shape=jax.ShapeDtypeStruct((n_rows, w), U32),
        compiler_params=_cparams("arbitrary"),
        name="dispatch",
    )(pend, n_active, gd_flat, slott, gatest, h2b)


def _expert_kernel(be_ref, na_ref, xs_ref, w1_ref, b1_ref, w2_ref, b2_ref, y_ref, w1b_ref, w2b_ref, *, d_expert):
    i = pl.program_id(0)
    active = i < na_ref[0]
    prev = be_ref[jnp.maximum(i - 1, 0)]
    new_expert = jnp.logical_or(i == 0, be_ref[i] != prev)
    half = y_ref.shape[1]

    @pl.when(jnp.logical_and(active, new_expert))
    def _():
        w1b_ref[...] = w1_ref[...].astype(BF16)
        w2b_ref[...] = w2_ref[...].astype(BF16)

    @pl.when(active)
    def _():
        x = _unpack_bf16_pair(xs_ref[:, :half]).astype(BF16)
        gate = pltpu.bitcast(xs_ref[:, half:], F32)[:, 0:1]
        hc = jnp.dot(x, w1b_ref[...], preferred_element_type=F32) + b1_ref[...]
        g = jnp.minimum(hc[:, :d_expert], SWIGLU_LIMIT)
        u = jnp.clip(hc[:, d_expert:], -SWIGLU_LIMIT, SWIGLU_LIMIT)
        hm = (u + 1.0) * (g * jax.nn.sigmoid(SWIGLU_ALPHA * g))
        y = (jnp.dot(hm.astype(BF16), w2b_ref[...], preferred_element_type=F32) + b2_ref[...]) * gate
        y_ref[...] = _pack_bf16_pair(y[:, :half], y[:, half:])

    @pl.when(jnp.logical_not(active))
    def _():
        y_ref[...] = jnp.zeros(y_ref.shape, U32)


def _experts(block_e, n_active, xs, w1, b1, w2, b2):
    n_rows, w = xs.shape
    n_exp, d, two_de = w1.shape
    de = two_de // 2
    bm = EXPERT_BLOCK
    nb = n_rows // bm

    def blk(i, be, na):
        return jnp.maximum(jnp.minimum(i, na[0] - 1), 0)

    grid_spec = pltpu.PrefetchScalarGridSpec(
        num_scalar_prefetch=2,
        grid=(nb,),
        in_specs=[pl.BlockSpec((bm, w), lambda i, be, na: (blk(i, be, na), 0)),
                  pl.BlockSpec((None, d, two_de), lambda i, be, na: (be[blk(i, be, na)], 0, 0)),
                  pl.BlockSpec((None, 1, two_de), lambda i, be, na: (be[blk(i, be, na)], 0, 0)),
                  pl.BlockSpec((None, de, d), lambda i, be, na: (be[blk(i, be, na)], 0, 0)),
                  pl.BlockSpec((None, 1, d), lambda i, be, na: (be[blk(i, be, na)], 0, 0))],
        out_specs=pl.BlockSpec((bm, d // 2), lambda i, be, na: (i, 0)),
        scratch_shapes=[pltpu.VMEM((d, two_de), BF16), pltpu.VMEM((de, d), BF16)],
    )
    return pl.pallas_call(
        functools.partial(_expert_kernel, d_expert=de),
        grid_spec=grid_spec,
        out_shape=jax.ShapeDtypeStruct((n_rows, d // 2), U32),
        compiler_params=_cparams("arbitrary"),
        name="expert",
    )(block_e, n_active, xs, w1, b1[:, None, :], w2, b2[:, None, :])


def _combine_kernel(gd_ref, gd_next_ref, slot_ref, x1_ref, mod_ref, fg_ref, yb_ref, o_ref, sorted_scr, sem):
    tm = x1_ref.shape[0]
    step = pl.program_id(0)
    cur = step % 2

    def group_copy(buf, j, d):
        dst = sorted_scr.at[buf, pl.ds(pl.multiple_of(j * CHUNK, CHUNK), CHUNK)]
        return _row_copy(yb_ref.at[pl.ds(pl.multiple_of(d, CHUNK), CHUNK)], dst, sem.at[buf])

    def fetch(table_ref, buf):
        def issue(j, carry):
            group_copy(buf, j, table_ref[j]).start()
            return carry
        lax.fori_loop(0, table_ref[GROUP_TABLE - 1], issue, 0)

    @pl.when(step == 0)
    def _():
        sorted_scr[...] = jnp.zeros(sorted_scr.shape, sorted_scr.dtype)
        fetch(gd_ref, 0)

    @pl.when(step + 1 < pl.num_programs(0))
    def _():
        fetch(gd_next_ref, 1 - cur)

    def retire(j, carry):
        group_copy(cur, 0, 0).wait()
        return carry

    lax.fori_loop(0, gd_ref[GROUP_TABLE - 1], retire, 0)

    slot = slot_ref[...]
    scol = lax.broadcasted_iota(I32, (tm, SLOT_ROWS), 1)
    pick = jnp.zeros((tm, SLOT_ROWS), F32)
    for kk in range(TOP_K):
        pick = pick + jnp.where(scol == slot[:, kk:kk + 1], 1.0, 0.0)
    pick = pick.astype(BF16)
    words = sorted_scr[cur]
    lo = pltpu.bitcast(words << 16, F32).astype(BF16)
    hi = pltpu.bitcast(words & jnp.uint32(0xFFFF0000), F32).astype(BF16)
    y = jnp.concatenate([jnp.dot(pick, lo, preferred_element_type=F32),
                         jnp.dot(pick, hi, preferred_element_type=F32)], axis=1)
    x2 = x1_ref[...] + mod_ref[5:6, :] * y
    o_ref[...] = _rms(x2) * fg_ref[...]


def _combine(gd_flat, slot, x1, mod8, final_g, yb, seq):
    t, d = x1.shape
    tm = TOKEN_TILE
    nt = t // tm
    tiles_per_seq = seq // tm
    return pl.pallas_call(
        _combine_kernel,
        grid=(nt,),
        in_specs=[pl.BlockSpec((GROUP_TABLE,), lambda i: (i,), memory_space=pltpu.SMEM),
                  pl.BlockSpec((GROUP_TABLE,), lambda i: (jnp.minimum(i + 1, nt - 1),), memory_space=pltpu.SMEM),
                  pl.BlockSpec((tm, LANES), lambda i: (i, 0)),
                  pl.BlockSpec((tm, d), lambda i: (i, 0)),
                  pl.BlockSpec((None, SUBLANES, d), lambda i: (i // tiles_per_seq, 0, 0)),
                  pl.BlockSpec((1, d), lambda i: (0, 0)),
                  pl.BlockSpec(memory_space=pl.ANY)],
        out_specs=pl.BlockSpec((tm, d), lambda i: (i, 0)),
        out_shape=jax.ShapeDtypeStruct((t, d), F32),
        scratch_shapes=[pltpu.VMEM((2, SLOT_ROWS, d // 2), U32), pltpu.SemaphoreType.DMA((2,))],
        compiler_params=_cparams("arbitrary"),
        name="combine",
    )(gd_flat, gd_flat, slot, x1, mod8, final_g[None], yb)


def kernel(x, c, w_ada, b_ada, norm1_g, w_in, b_forget, conv_w, conv_b, conv_ln_g, conv_ln_b, w_conv_out,
           w_attn_out, w_out, norm2_g, w_router, b_router, w_exp_in, b_exp_in, w_exp_out, b_exp_out, final_g):
    bsz, seq, d = x.shape
    t = bsz * seq
    depth = w_ada.shape[0]
    xf = x.reshape(t, d)
    n_rows = t * TOP_K + (t // TOKEN_TILE) * N_EXPERTS * (CHUNK - 1) + N_EXPERTS * (EXPERT_BLOCK - CHUNK)
    n_rows = -(-n_rows // EXPERT_BLOCK) * EXPERT_BLOCK
    n_blocks = n_rows // EXPERT_BLOCK
    for l in range(depth):
        mod8 = _mod(c, w_ada, b_ada[l], l)
        act, q, k, v, aq, ak, tot, sgc, sga = _inproj(xf, mod8, norm1_g[l], w_in[l], b_forget[l], conv_w[l],
                                                      conv_b[l], conv_ln_g[l], conv_ln_b[l], seq)
        attn = _attention(q, aq, k, ak, v, tot, bsz, seq)
        x1, h2b, slot, slott, gatest, meta, cnt = _merge(xf, act, attn, sgc, sga, mod8, norm2_g[l], w_conv_out[l],
                                                         w_attn_out[l], w_out[l], w_router[l], b_router[l], seq)
        gd, blk = _dest(cnt, meta)
        gd_flat = gd.reshape(-1)
        block_e = blk[0].reshape(-1)[:n_blocks]
        n_active = blk[1, 0, :1]
        pend = blk[2, 0, :N_EXPERTS]
        xs = _dispatch(pend, n_active, gd_flat, slott, gatest, h2b, n_rows)
        yb = _experts(block_e, n_active, xs, w_exp_in[l], b_exp_in[l], w_exp_out[l], b_exp_out[l])
        assert depth == 1
        xf = _combine(gd_flat, slot, x1, mod8, final_g, yb, seq)
    return xf.reshape(bsz, seq, d)
```

```python
import functools

import jax
import jax.numpy as jnp
from jax import lax
from jax.experimental import pallas as pl
from jax.experimental.pallas import tpu as pltpu

F32 = jnp.float32
BF16 = jnp.bfloat16
I32 = jnp.int32
U32 = jnp.uint32
HIGHEST = lax.Precision.HIGHEST

LANES = 128
SUBLANES = 8
VMEM_LIMIT_BYTES = 56 * 1024 * 1024

N_HEADS = 8
HEAD_DIM = 64
CONV_KERNEL = 31
CONV_HALO = 32
N_EXPERTS = 32
TOP_K = 4
SWIGLU_LIMIT = 7.0
SWIGLU_ALPHA = 1.702
RMS_EPS = 1e-5
LN_EPS = 1e-5
N_MOD = 6
N_SPLIT = 3
LOG2_E = 1.4426950408889634
HIGH_HALF = 0xFFFF0000
HEADS_PER_STEP = 2

MOD_COLS_PER_STEP = 1024
INPROJ_TILE = 512
ATTN_BQ = 4096
ATTN_BK = 512
MERGE_TILE = 1024
TOKEN_TILE = 256
EXPERT_BLOCK = 768
CHUNK = SUBLANES
SLOT_ROWS = 1280
GROUP_TABLE = SUBLANES * LANES
DEST_TILES_PER_STEP = 8
assert SLOT_ROWS >= TOKEN_TILE * TOP_K + N_EXPERTS * (CHUNK - 1) and SLOT_ROWS // CHUNK < GROUP_TABLE


def _cparams(*sem):
    return pltpu.CompilerParams(dimension_semantics=sem, vmem_limit_bytes=VMEM_LIMIT_BYTES)


def _resident(shape):
    nd = len(shape)
    return pl.BlockSpec(shape, lambda *_: (0,) * nd, pipeline_mode=pl.Buffered(1))


def _split3(x):
    p1 = x.astype(BF16)
    r1 = x - p1.astype(F32)
    p2 = r1.astype(BF16)
    p3 = (r1 - p2.astype(F32)).astype(BF16)
    return jnp.concatenate([p1, p2, p3], axis=1)


def _rms(x):
    return x * lax.rsqrt(jnp.mean(x * x, axis=-1, keepdims=True) + RMS_EPS)


def _mod_kernel(ct_ref, w_ref, b_ref, o_ref, *, batch):
    rows = []
    for b in range(batch):
        col = ct_ref[:, b:b + 1]
        rows.append(jnp.sum((col * jax.nn.sigmoid(col)) * w_ref[...], axis=0, keepdims=True) + b_ref[...])
    rows.append(jnp.zeros((SUBLANES - batch, rows[0].shape[1]), F32))
    o_ref[...] = jnp.concatenate(rows, axis=0)


def _mod(c, w_ada, b_ada, layer):
    bsz, d = c.shape
    n = w_ada.shape[2]
    tn = MOD_COLS_PER_STEP
    out = pl.pallas_call(
        functools.partial(_mod_kernel, batch=bsz),
        grid=(n // tn,),
        in_specs=[pl.BlockSpec((d, bsz), lambda j: (0, 0)),
                  pl.BlockSpec((None, d, tn), lambda j: (layer, 0, j)),
                  pl.BlockSpec((1, tn), lambda j: (0, j))],
        out_specs=pl.BlockSpec((SUBLANES, tn), lambda j: (0, j)),
        out_shape=jax.ShapeDtypeStruct((SUBLANES, n), F32),
        compiler_params=_cparams("arbitrary"),
        name="mod",
    )(c.T, w_ada, b_ada[None])
    mod = out[:bsz].reshape(bsz, N_MOD, d)
    return jnp.pad(mod, ((0, 0), (0, SUBLANES - N_MOD), (0, 0)))


def _conv_module(a, first_of_seq, w_ref, b_ref, g_ref, beta_ref, xs_scr, ys_scr):
    ts = a.shape[0]
    xs_scr[0:CONV_HALO, :] = jnp.where(first_of_seq, 0.0, xs_scr[ts:ts + CONV_HALO, :])
    xs_scr[CONV_HALO:, :] = a
    acc = jnp.zeros(a.shape, F32) + b_ref[...]
    base = CONV_HALO - (CONV_KERNEL - 1)
    for rho in range(SUBLANES):
        offs = [o for o in range(base, base + CONV_KERNEL) if o % SUBLANES == rho]
        if not offs:
            continue
        rows = offs[-1] + ts - rho
        ys_scr[0:rows, :] = xs_scr[rho:rho + rows, :]
        for o in offs:
            lo = o - rho
            acc = acc + w_ref[o - base:o - base + 1, :] * ys_scr[lo:lo + ts, :]
    mu = jnp.mean(acc, axis=-1, keepdims=True)
    cen = acc - mu
    var = jnp.mean(cen * cen, axis=-1, keepdims=True)
    y = cen * lax.rsqrt(var + LN_EPS) * g_ref[...] + beta_ref[...]
    return (y * jax.nn.sigmoid(y)).astype(BF16)


def _inproj_kernel(x_ref, mod_ref, g_ref, wa_ref, wqkv_ref, wf_ref, wg_ref, bf_ref,
                   tri_ref, eq_ref, ek_ref, oneq_ref, onek_ref, cw_ref, cb_ref, cg_ref, cbeta_ref,
                   act_ref, q_ref, k_ref, v_ref, aq_ref, ak_ref, tot_ref, sgc_ref, sga_ref,
                   carry_scr, xs_scr, ys_scr, *, conv_width, d_model, tiles_per_seq):
    first_of_seq = pl.program_id(0) % tiles_per_seq == 0

    @pl.when(pl.program_id(0) == 0)
    def _():
        xs_scr[...] = jnp.zeros(xs_scr.shape, F32)
        carry_scr[...] = jnp.zeros(carry_scr.shape, F32)

    x = x_ref[...]
    h = _rms(x) * g_ref[...] * (1.0 + mod_ref[1:2, :]) + mod_ref[0:1, :]
    hb = h.astype(BF16)

    u = jnp.dot(hb, wa_ref[...], preferred_element_type=F32)
    a = u[:, :conv_width] * jax.nn.sigmoid(u[:, conv_width:])
    act_ref[...] = _conv_module(a, first_of_seq, cw_ref, cb_ref, cg_ref, cbeta_ref, xs_scr, ys_scr)

    fl = jnp.dot(hb, wf_ref[...], preferred_element_type=F32) + bf_ref[...]
    lane = lax.broadcasted_iota(I32, fl.shape, 1)
    log_f = -(jnp.maximum(-fl, 0.0) + jnp.log1p(jnp.exp(-jnp.abs(fl))))
    log_f = jnp.where(lane < N_HEADS, log_f, 0.0)
    cs = jnp.dot(tri_ref[...], _split3(log_f), preferred_element_type=F32)
    cum = (cs[:, :LANES] + cs[:, LANES:2 * LANES] + cs[:, 2 * LANES:]) * LOG2_E
    tm = cum.shape[0]

    pieces = []
    for sb in range(tm // ATTN_BK):
        blk = cum[sb * ATTN_BK:(sb + 1) * ATTN_BK, :]
        if sb > 0:
            blk = blk - cum[sb * ATTN_BK - 1:sb * ATTN_BK, :]
        tot_ref[sb] = blk[ATTN_BK - 1:, :]
        pieces.append(blk)
    cum_k = pieces[0] if len(pieces) == 1 else jnp.concatenate(pieces, axis=0)

    first_of_block = pl.program_id(0) % (ATTN_BQ // tm) == 0
    cum_q = cum + jnp.where(first_of_block, 0.0, carry_scr[0:1, :])
    carry_scr[...] = jnp.broadcast_to(cum_q[tm - 1:, :], carry_scr.shape)

    qkv = jnp.dot(hb, wqkv_ref[...], preferred_element_type=F32)
    aw = qkv.shape[1] // 3
    q_ref[...] = (qkv[:, :aw] * (HEAD_DIM ** -0.5 * LOG2_E)).astype(BF16)
    k_ref[...] = qkv[:, aw:2 * aw].astype(BF16)
    v_ref[...] = qkv[:, 2 * aw:].astype(BF16)
    aq_ref[...] = (jnp.dot(_split3(cum_q), eq_ref[...], preferred_element_type=F32) + oneq_ref[...]).astype(BF16)
    ak_ref[...] = (jnp.dot(_split3(cum_k), ek_ref[...], preferred_element_type=F32) + onek_ref[...]).astype(BF16)

    gts = jnp.dot(hb, wg_ref[...], preferred_element_type=F32)
    sgc_ref[...] = jax.nn.sigmoid(gts[:, :d_model]).astype(BF16)
    sga_ref[...] = jax.nn.sigmoid(gts[:, d_model:]).astype(BF16)


def _aug_constants():
    rows = jnp.arange(N_SPLIT * LANES)
    piece, head = rows // LANES, rows % LANES
    cols = jnp.arange(N_HEADS * HEAD_DIM)
    pair, off = cols // LANES, cols % LANES
    chead = 2 * pair + jnp.where(off >= HEAD_DIM, 0, 1)
    pos = off % HEAD_DIM
    valid = (head[:, None] < N_HEADS) & (head[:, None] == chead[None, :])
    eq = jnp.where(valid & (pos[None, :] == piece[:, None]), 1.0, 0.0).astype(BF16)
    ek = jnp.where(valid & (pos[None, :] == piece[:, None] + N_SPLIT), -1.0, 0.0).astype(BF16)
    oneq = jnp.where((pos >= N_SPLIT) & (pos < 2 * N_SPLIT), 1.0, 0.0).astype(F32)[None]
    onek = jnp.where(pos < N_SPLIT, 1.0, 0.0).astype(F32)[None]
    return eq, ek, oneq, onek


def _inproj(x2, mod8, norm_g, w_in, b_forget, conv_w, conv_b, conv_ln_g, conv_ln_b, seq):
    t, d = x2.shape
    tm = INPROJ_TILE
    nsub = tm // ATTN_BK
    cw = d // 2
    aw = N_HEADS * HEAD_DIM
    o0 = 2 * cw
    wa = w_in[:, :o0].astype(BF16)
    o1 = o0 + 3 * aw
    wqkv = w_in[:, o0:o1].astype(BF16)
    wf = jnp.pad(w_in[:, o1:o1 + N_HEADS], ((0, 0), (0, LANES - N_HEADS))).astype(BF16)
    bfp = jnp.pad(b_forget, (0, LANES - N_HEADS))[None]
    wg = w_in[:, o1 + N_HEADS:].astype(BF16)
    r = jnp.arange(tm)
    tri = (r[None, :] <= r[:, None]).astype(BF16)
    eq, ek, oneq, onek = _aug_constants()
    tiles_per_seq = seq // tm
    row = lambda i: (i, 0)
    outs = pl.pallas_call(
        functools.partial(_inproj_kernel, conv_width=cw, d_model=d, tiles_per_seq=tiles_per_seq),
        grid=(t // tm,),
        in_specs=[pl.BlockSpec((tm, d), row),
                  pl.BlockSpec((None, SUBLANES, d), lambda i: (i // tiles_per_seq, 0, 0)),
                  _resident((1, d)),
                  _resident(wa.shape), _resident(wqkv.shape),
                  _resident(wf.shape), _resident(wg.shape), _resident(bfp.shape),
                  _resident(tri.shape), _resident(eq.shape), _resident(ek.shape),
                  _resident(oneq.shape), _resident(onek.shape),
                  _resident(conv_w.shape), _resident((1, cw)), _resident((1, cw)), _resident((1, cw))],
        out_specs=[pl.BlockSpec((tm, cw), row)] + [pl.BlockSpec((tm, aw), row)] * 5 + [
                   pl.BlockSpec((nsub, 1, LANES), lambda i: (i, 0, 0)),
                   pl.BlockSpec((tm, d), row), pl.BlockSpec((tm, d), row)],
        out_shape=[jax.ShapeDtypeStruct((t, cw), BF16)] + [jax.ShapeDtypeStruct((t, aw), BF16)] * 5 + [
                   jax.ShapeDtypeStruct((t // ATTN_BK, 1, LANES), F32),
                   jax.ShapeDtypeStruct((t, d), BF16), jax.ShapeDtypeStruct((t, d), BF16)],
        scratch_shapes=[pltpu.VMEM((SUBLANES, LANES), F32),
                        pltpu.VMEM((CONV_HALO + tm, cw), F32), pltpu.VMEM((CONV_HALO + tm, cw), F32)],
        compiler_params=_cparams("arbitrary"),
        name="inproj",
    )(x2, mod8, norm_g[None], wa, wqkv, wf, wg, bfp, tri, eq, ek, oneq, onek,
      conv_w, conv_b[None], conv_ln_g[None], conv_ln_b[None])
    return outs


def _attn_kernel(q_ref, aq_ref, k_ref, ak_ref, v_ref, tot_ref, o_ref, qop_scr, m_scr, acc_scr):
    bq = q_ref.shape[0]
    bk = ATTN_BK
    ratio = bq // bk
    qi = pl.program_id(2)
    nt = (((1,), (1,)), ((), ()))
    heads = range(HEADS_PER_STEP)
    zero = jnp.zeros((1, LANES), F32)

    def splice(hh, main, extra):
        first = lax.broadcasted_iota(I32, main.shape, 1) < HEAD_DIM
        return jnp.where(first, main, extra) if hh == 0 else jnp.where(first, extra, main)

    for hh in heads:
        qop_scr[hh] = splice(hh, q_ref[...], aq_ref[...])

    def scores(hh, kb, row0):
        start = pl.multiple_of(kb * bk, bk)
        k = splice(hh, k_ref[pl.ds(start, bk), :], ak_ref[pl.ds(start, bk), :])
        v = v_ref[pl.ds(start, bk), :]
        v = splice(hh, v, jnp.ones(v.shape, v.dtype))
        s = lax.dot_general(qop_scr[hh, row0:, :], k, nt, preferred_element_type=F32)
        return s, v

    def update(hh, s, v, delta, row0):
        m_prev = m_scr[hh, row0:, :]
        m_new = jnp.maximum(m_prev, jnp.max(s, axis=1, keepdims=True) + delta)
        shift = m_new - delta
        p = jnp.exp2((s - shift[:, :1]).astype(BF16))
        alpha = jnp.exp2(m_prev - m_new)
        pv = jnp.dot(p, v, preferred_element_type=F32)
        acc_scr[hh, row0:, :] = alpha * acc_scr[hh, row0:, :] + pv
        m_scr[hh, row0:, :] = m_new

    for hh in heads:
        delta = zero
        for sb in range(ratio):
            kb = qi * ratio + sb
            row0 = sb * bk
            s, v = scores(hh, kb, row0)
            row = lax.broadcasted_iota(I32, s.shape, 0)
            col = lax.broadcasted_iota(I32, s.shape, 1)
            s = jnp.where(col <= row, s, -jnp.inf)
            if sb == 0:
                m0 = jnp.max(s, axis=1, keepdims=True)
                p = jnp.exp2((s - m0).astype(BF16))
                m_scr[hh] = jnp.broadcast_to(m0, m_scr.shape[1:])
                acc_scr[hh] = jnp.dot(p, v, preferred_element_type=F32)
            else:
                update(hh, s, v, delta, row0)
            delta = delta - tot_ref[hh, pl.ds(kb, 1), :]

    def body(step, deltas):
        kb = qi * ratio - 1 - step
        new_deltas = []
        for hh in heads:
            delta = deltas[hh] + tot_ref[hh, pl.ds(kb, 1), :]
            s, v = scores(hh, kb, 0)
            update(hh, s, v, delta, 0)
            new_deltas.append(delta)
        return tuple(new_deltas)

    lax.fori_loop(0, qi * ratio, body, tuple(zero for _ in heads))
    acc0, acc1 = acc_scr[0], acc_scr[1]
    out0 = acc0 / acc0[:, HEAD_DIM:HEAD_DIM + 1]
    out1 = acc1 / acc1[:, 0:1]
    o_ref[...] = splice(0, out0, out1).astype(BF16)


def _attention(q, aq, k, ak, v, tot, bsz, seq):
    t, aw = q.shape
    bq = ATTN_BQ
    nq = seq // bq
    nk = seq // ATTN_BK
    assert HEADS_PER_STEP * HEAD_DIM == LANES
    tot = tot.reshape(bsz, nk, LANES)[:, :, :N_HEADS].transpose(0, 2, 1)
    tot = jnp.broadcast_to(tot[..., None], (bsz, N_HEADS, nk, LANES))
    qblk = pl.BlockSpec((bq, LANES), lambda b, h, i: (b * nq + i, h))
    kblk = pl.BlockSpec((None, seq, LANES), lambda b, h, i: (b, 0, h))
    seq3 = lambda a: a.reshape(bsz, seq, aw)
    return pl.pallas_call(
        _attn_kernel,
        grid=(bsz, N_HEADS // HEADS_PER_STEP, nq),
        in_specs=[qblk, qblk, kblk, kblk, kblk,
                  pl.BlockSpec((None, HEADS_PER_STEP, nk, LANES), lambda b, h, i: (b, h, 0, 0))],
        out_specs=qblk,
        out_shape=jax.ShapeDtypeStruct((t, aw), BF16),
        scratch_shapes=[pltpu.VMEM((HEADS_PER_STEP, bq, LANES), BF16),
                        pltpu.VMEM((HEADS_PER_STEP, bq, LANES), F32),
                        pltpu.VMEM((HEADS_PER_STEP, bq, LANES), F32)],
        compiler_params=_cparams("arbitrary", "arbitrary", "arbitrary"),
        name="attn",
    )(q, aq, seq3(k), seq3(ak), seq3(v), tot)


def _pack_bf16_pair(lo, hi):
    lo_bits = pltpu.bitcast(lo.astype(BF16).astype(F32), U32)
    hi_bits = pltpu.bitcast(hi.astype(BF16).astype(F32), U32)
    return (hi_bits & jnp.uint32(HIGH_HALF)) | (lo_bits >> 16)


def _unpack_bf16_pair(w):
    lo = pltpu.bitcast(w << 16, F32)
    hi = pltpu.bitcast(w & jnp.uint32(HIGH_HALF), F32)
    return jnp.concatenate([lo, hi], axis=1)


def _expert_onehots(ids):
    lane = lax.broadcasted_iota(I32, ids.shape, 1)
    return lane, [lane == ids[:, kk:kk + 1] for kk in range(TOP_K)]


def _route_tile(ids, gates, carry, tri, utri):
    lane, hots = _expert_onehots(ids)
    multi = jnp.zeros(lane.shape, F32)
    for hot in hots:
        multi = multi + jnp.where(hot, 1.0, 0.0)
    before = jnp.dot(tri, multi.astype(BF16), preferred_element_type=F32)
    units = jnp.ceil(jnp.sum(multi, axis=0, keepdims=True) * (1.0 / CHUNK))
    units8 = jnp.broadcast_to(units, (SUBLANES, LANES)).astype(BF16)
    lstart = jnp.dot(units8, utri, preferred_element_type=F32)[0:1, :] * CHUNK
    csize = units * CHUNK
    slot = jnp.zeros(lane.shape, F32)
    for kk, hot in enumerate(hots):
        sl = jnp.sum(jnp.where(hot, lstart + before, 0.0), axis=1, keepdims=True)
        slot = jnp.where(lane == kk, sl, slot)
    row = lax.broadcasted_iota(I32, (SUBLANES, LANES), 0)
    meta = jnp.where(row == 0, csize, jnp.where(row == 1, lstart, jnp.where(row == 2, carry, 0.0)))
    slott = slot.T[0:SUBLANES, :].astype(I32)
    gatest = gates.T[0:SUBLANES, :]
    return slot.astype(I32), slott, gatest, meta, carry + csize


def _merge_kernel(x_ref, act_ref, attn_ref, sgc_ref, sga_ref, mod_ref, g2_ref, wco_ref, wao_ref, wout_ref,
                  wr_ref, br_ref, tri_ref, utri_ref,
                  x1_ref, h2_ref, slot_ref, slott_ref, gatest_ref, meta_ref, cnt_ref, carry_scr):
    @pl.when(pl.program_id(0) == 0)
    def _():
        carry_scr[...] = jnp.zeros(carry_scr.shape, F32)

    conv_out = jnp.dot(act_ref[...], wco_ref[...], preferred_element_type=F32)
    attn_out = jnp.dot(attn_ref[...], wao_ref[...], preferred_element_type=F32)
    mixed = sgc_ref[...].astype(F32) * conv_out + sga_ref[...].astype(F32) * attn_out
    upd = jnp.dot(mixed.astype(BF16), wout_ref[...], preferred_element_type=F32)
    x1 = x_ref[...] + mod_ref[2:3, :] * upd
    x1_ref[...] = x1
    h2 = _rms(x1) * g2_ref[...] * (1.0 + mod_ref[4:5, :]) + mod_ref[3:4, :]
    h2_hi = h2.astype(BF16)
    h2_ref[...] = h2_hi
    h2_lo = (h2 - h2_hi.astype(F32)).astype(BF16)
    h2_cat = jnp.concatenate([h2_hi, h2_lo, h2_hi], axis=1)
    logits = jnp.dot(h2_cat, wr_ref[...], preferred_element_type=F32) + br_ref[...]
    lane = lax.broadcasted_iota(I32, logits.shape, 1)
    work = jnp.where(lane < N_EXPERTS, logits, -jnp.inf)
    ids = jnp.zeros(logits.shape, I32)
    vals = []
    for kk in range(TOP_K):
        mx = jnp.max(work, axis=1, keepdims=True)
        idx = jnp.min(jnp.where(work == mx, lane, LANES), axis=1, keepdims=True)
        ids = jnp.where(lane == kk, idx, ids)
        vals.append(mx)
        work = jnp.where(lane == idx, -jnp.inf, work)
    exps = [jnp.exp(vv - vals[0]) for vv in vals]
    den = exps[0] + exps[1] + exps[2] + exps[3]
    gates = jnp.zeros(logits.shape, F32)
    for kk in range(TOP_K):
        gates = jnp.where(lane == kk, exps[kk] / den, gates)

    carry = carry_scr[0:1, :]
    tt = TOKEN_TILE
    for sub in range(logits.shape[0] // tt):
        rows = slice(sub * tt, (sub + 1) * tt)
        slot, slott, gatest, meta, carry = _route_tile(ids[rows, :], gates[rows, :], carry,
                                                       tri_ref[...], utri_ref[...])
        slot_ref[rows, :] = slot
        slott_ref[sub * SUBLANES:(sub + 1) * SUBLANES, :] = slott
        gatest_ref[sub * SUBLANES:(sub + 1) * SUBLANES, :] = gatest
        meta_ref[sub] = meta
    carry_scr[...] = jnp.broadcast_to(carry, carry_scr.shape)
    cnt_ref[...] = jnp.broadcast_to(carry, cnt_ref.shape)


def _merge(x2, act, attn, sgc, sga, mod8, norm2_g, w_conv_out, w_attn_out, w_out, w_router, b_router, seq):
    t, d = x2.shape
    tm = MERGE_TILE
    tt = TOKEN_TILE
    sub = tm // tt
    nt = t // tt
    cw = act.shape[1]
    sw = attn.shape[1]
    wco = w_conv_out.astype(BF16)
    wao = w_attn_out.astype(BF16)
    wout = w_out.astype(BF16)
    wr = jnp.pad(w_router, ((0, 0), (0, LANES - N_EXPERTS)))
    wr_hi = wr.astype(BF16)
    wr_lo = (wr - wr_hi.astype(F32)).astype(BF16)
    wr = jnp.concatenate([wr_hi, wr_hi, wr_lo], axis=0)
    br = jnp.pad(b_router, (0, LANES - N_EXPERTS))[None]
    r = jnp.arange(tt)
    tri = (r[None, :] < r[:, None]).astype(BF16)
    e = jnp.arange(LANES)
    utri = (e[:, None] < e[None, :]).astype(BF16)
    tiles_per_seq = seq // tm
    row = lambda i: (i, 0)
    return pl.pallas_call(
        _merge_kernel,
        grid=(t // tm,),
        in_specs=[pl.BlockSpec((tm, d), row), pl.BlockSpec((tm, cw), row), pl.BlockSpec((tm, sw), row),
                  pl.BlockSpec((tm, d), row), pl.BlockSpec((tm, d), row),
                  pl.BlockSpec((None, SUBLANES, d), lambda i: (i // tiles_per_seq, 0, 0)),
                  _resident((1, d)), _resident(wco.shape), _resident(wao.shape), _resident(wout.shape),
                  _resident(wr.shape), _resident(br.shape), _resident(tri.shape), _resident(utri.shape)],
        out_specs=[pl.BlockSpec((tm, d), row), pl.BlockSpec((tm, d), row),
                   pl.BlockSpec((tm, LANES), row),
                   pl.BlockSpec((sub * SUBLANES, tt), row), pl.BlockSpec((sub * SUBLANES, tt), row),
                   pl.BlockSpec((sub, SUBLANES, LANES), lambda i: (i, 0, 0)),
                   pl.BlockSpec((SUBLANES, LANES), lambda i: (0, 0))],
        out_shape=[jax.ShapeDtypeStruct((t, d), F32), jax.ShapeDtypeStruct((t, d), BF16),
                   jax.ShapeDtypeStruct((t, LANES), I32),
                   jax.ShapeDtypeStruct((nt * SUBLANES, tt), I32), jax.ShapeDtypeStruct((nt * SUBLANES, tt), F32),
                   jax.ShapeDtypeStruct((nt, SUBLANES, LANES), F32),
                   jax.ShapeDtypeStruct((SUBLANES, LANES), F32)],
        scratch_shapes=[pltpu.VMEM((SUBLANES, LANES), F32)],
        compiler_params=_cparams("arbitrary"),
        name="merge",
    )(x2, act, attn, sgc, sga, mod8, norm2_g[None], wco, wao, wout, wr, br, tri, utri)


def _dest_kernel(cnt_ref, meta_ref, tri_ref, gd_ref, blk_ref):
    cnt = cnt_ref[...]
    padded = jnp.ceil(cnt * (1.0 / EXPERT_BLOCK)) * EXPERT_BLOCK
    pend = jnp.dot(padded, tri_ref[...], preferred_element_type=F32, precision=HIGHEST)
    pstart = (pend - padded)[0:1, :]

    grow = lax.broadcasted_iota(I32, (SUBLANES, LANES), 0)
    glane = lax.broadcasted_iota(I32, (SUBLANES, LANES), 1)
    gpos = ((grow * LANES + glane) * CHUNK).astype(F32)
    last = jnp.logical_and(grow == SUBLANES - 1, glane == LANES - 1)
    for tile in range(meta_ref.shape[0]):
        meta = meta_ref[tile]
        csize, lstart, seg_off = meta[0:1, :], meta[1:2, :], meta[2:3, :]
        lend = lstart + csize
        base = pstart + seg_off - lstart
        owner = jnp.zeros(gpos.shape, I32)
        for e in range(N_EXPERTS):
            owner = owner + jnp.where(gpos >= lend[0:1, e:e + 1], 1, 0)
        gdst = gpos
        for e in range(N_EXPERTS):
            gdst = gdst + jnp.where(owner == e, base[0:1, e:e + 1], 0.0)
        n_groups = lend[0:1, N_EXPERTS - 1:N_EXPERTS] * (1.0 / CHUNK)
        gd_ref[tile] = jnp.where(last, n_groups, gdst).astype(I32)

    @pl.when(pl.program_id(0) == 0)
    def _():
        shape = blk_ref.shape[1:]
        pos = (lax.broadcasted_iota(I32, shape, 0) * LANES + lax.broadcasted_iota(I32, shape, 1))
        pos = pos.astype(F32) * EXPERT_BLOCK
        be = jnp.zeros(shape, I32)
        for e in range(N_EXPERTS):
            be = be + jnp.where(pos >= pend[0:1, e:e + 1], 1, 0)
        blk_ref[0] = jnp.minimum(be, N_EXPERTS - 1)
        nact = pend[0:1, N_EXPERTS - 1:N_EXPERTS] * (1.0 / EXPERT_BLOCK)
        blk_ref[1] = jnp.broadcast_to(nact, shape).astype(I32)
        blk_ref[2] = jnp.broadcast_to(pend[0:1, :], shape).astype(I32)


def _dest(cnt, meta):
    nt = meta.shape[0]
    per_step = DEST_TILES_PER_STEP
    r = jnp.arange(LANES)
    tri = (r[:, None] <= r[None, :]).astype(F32)
    return pl.pallas_call(
        _dest_kernel,
        grid=(nt // per_step,),
        in_specs=[pl.BlockSpec((SUBLANES, LANES), lambda i: (0, 0)),
                  pl.BlockSpec((per_step, SUBLANES, LANES), lambda i: (i, 0, 0)),
                  pl.BlockSpec((LANES, LANES), lambda i: (0, 0))],
        out_specs=[pl.BlockSpec((per_step, SUBLANES, LANES), lambda i: (i, 0, 0)),
                   pl.BlockSpec((3, SUBLANES, LANES), lambda i: (0, 0, 0))],
        out_shape=[jax.ShapeDtypeStruct((nt, SUBLANES, LANES), I32),
                   jax.ShapeDtypeStruct((3, SUBLANES, LANES), I32)],
        compiler_params=_cparams("arbitrary"),
        name="dest",
    )(cnt, meta, tri)


def _row_copy(src, dst, sem):
    return pltpu.make_async_copy(src, dst, sem)


def _dispatch_kernel(pend_ref, na_ref, gd_ref, slott_ref, gatest_ref, h2_ref, xs_ref, zero_ref, sorted_scr, sem):
    tm = h2_ref.shape[0]
    bm = zero_ref.shape[0]
    nb = xs_ref.shape[0] // bm

    @pl.when(pl.program_id(0) == 0)
    def _():
        zero_ref[...] = jnp.zeros(zero_ref.shape, zero_ref.dtype)

        def zero_block(start):
            return _row_copy(zero_ref, xs_ref.at[pl.ds(pl.multiple_of(start, bm), bm)], sem)

        def nonempty(e):
            return pend_ref[e] > (pend_ref[e - 1] if e > 0 else 0)

        for e in range(N_EXPERTS):
            @pl.when(nonempty(e))
            def _():
                zero_block(pend_ref[e] - bm).start()

        def tail(b, carry):
            zero_block(b * bm).start()
            return carry

        lax.fori_loop(na_ref[0], nb, tail, 0)
        for e in range(N_EXPERTS):
            @pl.when(nonempty(e))
            def _():
                zero_block(0).wait()

        def tail_wait(b, carry):
            zero_block(0).wait()
            return carry

        lax.fori_loop(na_ref[0], nb, tail_wait, 0)

    srow = lax.broadcasted_iota(I32, (SLOT_ROWS, tm), 0)
    perm = jnp.zeros((SLOT_ROWS, tm), F32)
    gate = jnp.zeros((SLOT_ROWS, 1), F32)
    for kk in range(TOP_K):
        hit = srow == slott_ref[kk:kk + 1, :]
        perm = perm + jnp.where(hit, 1.0, 0.0)
        gate = gate + jnp.sum(jnp.where(hit, gatest_ref[kk:kk + 1, :], 0.0), axis=1, keepdims=True)
    perm = perm.astype(BF16)
    half = h2_ref.shape[1] // 2
    lo = jnp.dot(perm, h2_ref[:, :half], preferred_element_type=F32)
    hi = jnp.dot(perm, h2_ref[:, half:], preferred_element_type=F32)
    sorted_scr[:, :half] = pltpu.bitcast(hi, U32) | (pltpu.bitcast(lo, U32) >> 16)
    sorted_scr[:, half:] = pltpu.bitcast(jnp.broadcast_to(gate, (SLOT_ROWS, LANES)), U32)

    n_groups = gd_ref[GROUP_TABLE - 1]

    def group_copy(j, d):
        src = sorted_scr.at[pl.ds(pl.multiple_of(j * CHUNK, CHUNK), CHUNK)]
        return _row_copy(src, xs_ref.at[pl.ds(pl.multiple_of(d, CHUNK), CHUNK)], sem)

    def issue(j, carry):
        group_copy(j, gd_ref[j]).start()
        return carry

    def retire(j, carry):
        group_copy(0, 0).wait()
        return carry

    lax.fori_loop(0, n_groups, issue, 0)
    lax.fori_loop(0, n_groups, retire, 0)


def _dispatch(pend, n_active, gd_flat, slott, gatest, h2b, n_rows):
    t, d = h2b.shape
    w = d // 2 + LANES
    tm = TOKEN_TILE
    grid_spec = pltpu.PrefetchScalarGridSpec(
        num_scalar_prefetch=2,
        grid=(t // tm,),
        in_specs=[pl.BlockSpec((GROUP_TABLE,), lambda i, pe, na: (i,), memory_space=pltpu.SMEM),
                  pl.BlockSpec((SUBLANES, tm), lambda i, pe, na: (i, 0)),
                  pl.BlockSpec((SUBLANES, tm), lambda i, pe, na: (i, 0)),
                  pl.BlockSpec((tm, d), lambda i, pe, na: (i, 0))],
        out_specs=pl.BlockSpec(memory_space=pl.ANY),
        scratch_shapes=[pltpu.VMEM((EXPERT_BLOCK, w), U32), pltpu.VMEM((SLOT_ROWS, w), U32),
                        pltpu.SemaphoreType.DMA(())],
    )
    return pl.pallas_call(
        _dispatch_kernel,
        grid_spec=grid_spec,
        out_shape=jax.ShapeDtypeStruct((n_rows, w), U32),
        compiler_params=_cparams("arbitrary"),
        name="dispatch",
    )(pend, n_active, gd_flat, slott, gatest, h2b)


def _expert_kernel(be_ref, na_ref, xs_ref, w1_ref, b1_ref, w2_ref, b2_ref, y_ref, w1b_ref, w2b_ref, *, d_expert):
    i = pl.program_id(0)
    active = i < na_ref[0]
    prev = be_ref[jnp.maximum(i - 1, 0)]
    new_expert = jnp.logical_or(i == 0, be_ref[i] != prev)
    half = y_ref.shape[1]

    @pl.when(jnp.logical_and(active, new_expert))
    def _():
        w1b_ref[...] = w1_ref[...].astype(BF16)
        w2b_ref[...] = w2_ref[...].astype(BF16)

    @pl.when(active)
    def _():
        x = _unpack_bf16_pair(xs_ref[:, :half]).astype(BF16)
        gate = pltpu.bitcast(xs_ref[:, half:], F32)[:, 0:1]
        hc = jnp.dot(x, w1b_ref[...], preferred_element_type=F32) + b1_ref[...]
        g = jnp.minimum(hc[:, :d_expert], SWIGLU_LIMIT)
        u = jnp.clip(hc[:, d_expert:], -SWIGLU_LIMIT, SWIGLU_LIMIT)
        hm = (u + 1.0) * (g * jax.nn.sigmoid(SWIGLU_ALPHA * g))
        y = (jnp.dot(hm.astype(BF16), w2b_ref[...], preferred_element_type=F32) + b2_ref[...]) * gate
        y_ref[...] = _pack_bf16_pair(y[:, :half], y[:, half:])

    @pl.when(jnp.logical_not(active))
    def _():
        y_ref[...] = jnp.zeros(y_ref.shape, U32)


def _experts(block_e, n_active, xs, w1, b1, w2, b2):
    n_rows, w = xs.shape
    n_exp, d, two_de = w1.shape
    de = two_de // 2
    bm = EXPERT_BLOCK
    nb = n_rows // bm

    def blk(i, be, na):
        return jnp.maximum(jnp.minimum(i, na[0] - 1), 0)

    grid_spec = pltpu.PrefetchScalarGridSpec(
        num_scalar_prefetch=2,
        grid=(nb,),
        in_specs=[pl.BlockSpec((bm, w), lambda i, be, na: (blk(i, be, na), 0)),
                  pl.BlockSpec((None, d, two_de), lambda i, be, na: (be[blk(i, be, na)], 0, 0)),
                  pl.BlockSpec((None, 1, two_de), lambda i, be, na: (be[blk(i, be, na)], 0, 0)),
                  pl.BlockSpec((None, de, d), lambda i, be, na: (be[blk(i, be, na)], 0, 0)),
                  pl.BlockSpec((None, 1, d), lambda i, be, na: (be[blk(i, be, na)], 0, 0))],
        out_specs=pl.BlockSpec((bm, d // 2), lambda i, be, na: (i, 0)),
        scratch_shapes=[pltpu.VMEM((d, two_de), BF16), pltpu.VMEM((de, d), BF16)],
    )
    return pl.pallas_call(
        functools.partial(_expert_kernel, d_expert=de),
        grid_spec=grid_spec,
        out_shape=jax.ShapeDtypeStruct((n_rows, d // 2), U32),
        compiler_params=_cparams("arbitrary"),
        name="expert",
    )(block_e, n_active, xs, w1, b1[:, None, :], w2, b2[:, None, :])


def _combine_kernel(gd_ref, gd_next_ref, slot_ref, x1_ref, mod_ref, fg_ref, yb_ref, o_ref, sorted_scr, sem):
    tm = x1_ref.shape[0]
    step = pl.program_id(0)
    cur = step % 2

    def group_copy(buf, j, d):
        dst = sorted_scr.at[buf, pl.ds(pl.multiple_of(j * CHUNK, CHUNK), CHUNK)]
        return _row_copy(yb_ref.at[pl.ds(pl.multiple_of(d, CHUNK), CHUNK)], dst, sem.at[buf])

    def fetch(table_ref, buf):
        def issue(j, carry):
            group_copy(buf, j, table_ref[j]).start()
            return carry
        lax.fori_loop(0, table_ref[GROUP_TABLE - 1], issue, 0)

    @pl.when(step == 0)
    def _():
        sorted_scr[...] = jnp.zeros(sorted_scr.shape, sorted_scr.dtype)
        fetch(gd_ref, 0)

    @pl.when(step + 1 < pl.num_programs(0))
    def _():
        fetch(gd_next_ref, 1 - cur)

    def retire(j, carry):
        group_copy(cur, 0, 0).wait()
        return carry

    lax.fori_loop(0, gd_ref[GROUP_TABLE - 1], retire, 0)

    slot = slot_ref[...]
    scol = lax.broadcasted_iota(I32, (tm, SLOT_ROWS), 1)
    pick = jnp.zeros((tm, SLOT_ROWS), F32)
    for kk in range(TOP_K):
        pick = pick + jnp.where(scol == slot[:, kk:kk + 1], 1.0, 0.0)
    pick = pick.astype(BF16)
    words = sorted_scr[cur]
    lo = pltpu.bitcast(words << 16, F32).astype(BF16)
    hi = pltpu.bitcast(words & jnp.uint32(HIGH_HALF), F32).astype(BF16)
    y = jnp.concatenate([jnp.dot(pick, lo, preferred_element_type=F32),
                         jnp.dot(pick, hi, preferred_element_type=F32)], axis=1)
    x2 = x1_ref[...] + mod_ref[5:6, :] * y
    o_ref[...] = _rms(x2) * fg_ref[...]


def _combine(gd_flat, slot, x1, mod8, final_g, yb, seq):
    t, d = x1.shape
    tm = TOKEN_TILE
    nt = t // tm
    tiles_per_seq = seq // tm
    return pl.pallas_call(
        _combine_kernel,
        grid=(nt,),
        in_specs=[pl.BlockSpec((GROUP_TABLE,), lambda i: (i,), memory_space=pltpu.SMEM),
                  pl.BlockSpec((GROUP_TABLE,), lambda i: (jnp.minimum(i + 1, nt - 1),), memory_space=pltpu.SMEM),
                  pl.BlockSpec((tm, LANES), lambda i: (i, 0)),
                  pl.BlockSpec((tm, d), lambda i: (i, 0)),
                  pl.BlockSpec((None, SUBLANES, d), lambda i: (i // tiles_per_seq, 0, 0)),
                  pl.BlockSpec((1, d), lambda i: (0, 0)),
                  pl.BlockSpec(memory_space=pl.ANY)],
        out_specs=pl.BlockSpec((tm, d), lambda i: (i, 0)),
        out_shape=jax.ShapeDtypeStruct((t, d), F32),
        scratch_shapes=[pltpu.VMEM((2, SLOT_ROWS, d // 2), U32), pltpu.SemaphoreType.DMA((2,))],
        compiler_params=_cparams("arbitrary"),
        name="combine",
    )(gd_flat, gd_flat, slot, x1, mod8, final_g[None], yb)


def kernel(x, c, w_ada, b_ada, norm1_g, w_in, b_forget, conv_w, conv_b, conv_ln_g, conv_ln_b, w_conv_out,
           w_attn_out, w_out, norm2_g, w_router, b_router, w_exp_in, b_exp_in, w_exp_out, b_exp_out, final_g):
    bsz, seq, d = x.shape
    t = bsz * seq
    depth = w_ada.shape[0]
    xf = x.reshape(t, d)
    n_rows = t * TOP_K + (t // TOKEN_TILE) * N_EXPERTS * (CHUNK - 1) + N_EXPERTS * (EXPERT_BLOCK - CHUNK)
    n_rows = -(-n_rows // EXPERT_BLOCK) * EXPERT_BLOCK
    n_blocks = n_rows // EXPERT_BLOCK
    for l in range(depth):
        mod8 = _mod(c, w_ada, b_ada[l], l)
        act, q, k, v, aq, ak, tot, sgc, sga = _inproj(xf, mod8, norm1_g[l], w_in[l], b_forget[l], conv_w[l],
                                                      conv_b[l], conv_ln_g[l], conv_ln_b[l], seq)
        attn = _attention(q, aq, k, ak, v, tot, bsz, seq)
        x1, h2b, slot, slott, gatest, meta, cnt = _merge(xf, act, attn, sgc, sga, mod8, norm2_g[l], w_conv_out[l],
                                                         w_attn_out[l], w_out[l], w_router[l], b_router[l], seq)
        gd, blk = _dest(cnt, meta)
        gd_flat = gd.reshape(-1)
        block_e = blk[0].reshape(-1)[:n_blocks]
        n_active = blk[1, 0, :1]
        pend = blk[2, 0, :N_EXPERTS]
        xs = _dispatch(pend, n_active, gd_flat, slott, gatest, h2b, n_rows)
        yb = _experts(block_e, n_active, xs, w_exp_in[l], b_exp_in[l], w_exp_out[l], b_exp_out[l])
        assert depth == 1
        xf = _combine(gd_flat, slot, x1, mod8, final_g, yb, seq)
    return xf.reshape(bsz, seq, d)
```

```python
import functools

import jax
import jax.numpy as jnp
from jax import lax
from jax.experimental import pallas as pl
from jax.experimental.pallas import tpu as pltpu

F32 = jnp.float32
BF16 = jnp.bfloat16
I32 = jnp.int32
U32 = jnp.uint32
HIGHEST = lax.Precision.HIGHEST

LANES = 128
SUBLANES = 8
VMEM_LIMIT_BYTES = 56 * 1024 * 1024

N_HEADS = 8
HEAD_DIM = 64
CONV_KERNEL = 31
CONV_HALO = 32
N_EXPERTS = 32
TOP_K = 4
SWIGLU_LIMIT = 7.0
SWIGLU_ALPHA = 1.702
RMS_EPS = 1e-5
LN_EPS = 1e-5
N_MOD = 6
N_SPLIT = 3
LOG2_E = 1.4426950408889634
HIGH_HALF = 0xFFFF0000
HEADS_PER_STEP = 2

MOD_COLS_PER_STEP = 1024
INPROJ_TILE = 512
ATTN_BQ = 4096
ATTN_BK = 512
MERGE_TILE = 1024
TOKEN_TILE = 256
EXPERT_BLOCK = 768
CHUNK = SUBLANES
SLOT_ROWS = 1280
GROUP_TABLE = SUBLANES * LANES
DEST_TILES_PER_STEP = 8
assert SLOT_ROWS >= TOKEN_TILE * TOP_K + N_EXPERTS * (CHUNK - 1) and SLOT_ROWS // CHUNK < GROUP_TABLE


def _cparams(*sem):
    return pltpu.CompilerParams(dimension_semantics=sem, vmem_limit_bytes=VMEM_LIMIT_BYTES)


def _resident(shape):
    nd = len(shape)
    return pl.BlockSpec(shape, lambda *_: (0,) * nd, pipeline_mode=pl.Buffered(1))


def _split3(x):
    p1 = x.astype(BF16)
    r1 = x - p1.astype(F32)
    p2 = r1.astype(BF16)
    p3 = (r1 - p2.astype(F32)).astype(BF16)
    return jnp.concatenate([p1, p2, p3], axis=1)


def _rms(x):
    return x * lax.rsqrt(jnp.mean(x * x, axis=-1, keepdims=True) + RMS_EPS)


def _mod_kernel(ct_ref, w_ref, b_ref, o_ref, *, batch):
    rows = []
    for b in range(batch):
        col = ct_ref[:, b:b + 1]
        rows.append(jnp.sum((col * jax.nn.sigmoid(col)) * w_ref[...], axis=0, keepdims=True) + b_ref[...])
    rows.append(jnp.zeros((SUBLANES - batch, rows[0].shape[1]), F32))
    o_ref[...] = jnp.concatenate(rows, axis=0)


def _mod(c, w_ada, b_ada, layer):
    bsz, d = c.shape
    n = w_ada.shape[2]
    tn = MOD_COLS_PER_STEP
    out = pl.pallas_call(
        functools.partial(_mod_kernel, batch=bsz),
        grid=(n // tn,),
        in_specs=[pl.BlockSpec((d, bsz), lambda j: (0, 0)),
                  pl.BlockSpec((None, d, tn), lambda j: (layer, 0, j)),
                  pl.BlockSpec((1, tn), lambda j: (0, j))],
        out_specs=pl.BlockSpec((SUBLANES, tn), lambda j: (0, j)),
        out_shape=jax.ShapeDtypeStruct((SUBLANES, n), F32),
        compiler_params=_cparams("arbitrary"),
        name="mod",
    )(c.T, w_ada, b_ada[None])
    mod = out[:bsz].reshape(bsz, N_MOD, d)
    return jnp.pad(mod, ((0, 0), (0, SUBLANES - N_MOD), (0, 0)))


def _conv_module(a, first_of_seq, w_ref, b_ref, g_ref, beta_ref, xs_scr, ys_scr):
    ts = a.shape[0]
    xs_scr[0:CONV_HALO, :] = jnp.where(first_of_seq, 0.0, xs_scr[ts:ts + CONV_HALO, :])
    xs_scr[CONV_HALO:, :] = a
    acc = jnp.zeros(a.shape, F32) + b_ref[...]
    base = CONV_HALO - (CONV_KERNEL - 1)
    for rho in range(SUBLANES):
        offs = [o for o in range(base, base + CONV_KERNEL) if o % SUBLANES == rho]
        if not offs:
            continue
        rows = offs[-1] + ts - rho
        ys_scr[0:rows, :] = xs_scr[rho:rho + rows, :]
        for o in offs:
            lo = o - rho
            acc = acc + w_ref[o - base:o - base + 1, :] * ys_scr[lo:lo + ts, :]
    mu = jnp.mean(acc, axis=-1, keepdims=True)
    cen = acc - mu
    var = jnp.mean(cen * cen, axis=-1, keepdims=True)
    y = cen * lax.rsqrt(var + LN_EPS) * g_ref[...] + beta_ref[...]
    return (y * jax.nn.sigmoid(y)).astype(BF16)


def _inproj_kernel(x_ref, mod_ref, g_ref, wa_ref, wqkv_ref, wf_ref, wg_ref, bf_ref,
                   tri_ref, eq_ref, ek_ref, oneq_ref, onek_ref, cw_ref, cb_ref, cg_ref, cbeta_ref,
                   act_ref, q_ref, k_ref, v_ref, aq_ref, ak_ref, tot_ref, sgc_ref, sga_ref,
                   carry_scr, xs_scr, ys_scr, *, conv_width, d_model, tiles_per_seq):
    first_of_seq = pl.program_id(0) % tiles_per_seq == 0

    @pl.when(pl.program_id(0) == 0)
    def _():
        xs_scr[...] = jnp.zeros(xs_scr.shape, F32)
        carry_scr[...] = jnp.zeros(carry_scr.shape, F32)

    x = x_ref[...]
    h = _rms(x) * g_ref[...] * (1.0 + mod_ref[1:2, :]) + mod_ref[0:1, :]
    hb = h.astype(BF16)

    u = jnp.dot(hb, wa_ref[...], preferred_element_type=F32)
    a = u[:, :conv_width] * jax.nn.sigmoid(u[:, conv_width:])
    act_ref[...] = _conv_module(a, first_of_seq, cw_ref, cb_ref, cg_ref, cbeta_ref, xs_scr, ys_scr)

    fl = jnp.dot(hb, wf_ref[...], preferred_element_type=F32) + bf_ref[...]
    lane = lax.broadcasted_iota(I32, fl.shape, 1)
    log_f = -(jnp.maximum(-fl, 0.0) + jnp.log1p(jnp.exp(-jnp.abs(fl))))
    log_f = jnp.where(lane < N_HEADS, log_f, 0.0)
    cs = jnp.dot(tri_ref[...], _split3(log_f), preferred_element_type=F32)
    cum = (cs[:, :LANES] + cs[:, LANES:2 * LANES] + cs[:, 2 * LANES:]) * LOG2_E
    tm = cum.shape[0]

    pieces = []
    for sb in range(tm // ATTN_BK):
        blk = cum[sb * ATTN_BK:(sb + 1) * ATTN_BK, :]
        if sb > 0:
            blk = blk - cum[sb * ATTN_BK - 1:sb * ATTN_BK, :]
        tot_ref[sb] = blk[ATTN_BK - 1:, :]
        pieces.append(blk)
    cum_k = pieces[0] if len(pieces) == 1 else jnp.concatenate(pieces, axis=0)

    first_of_block = pl.program_id(0) % (ATTN_BQ // tm) == 0
    cum_q = cum + jnp.where(first_of_block, 0.0, carry_scr[0:1, :])
    carry_scr[...] = jnp.broadcast_to(cum_q[tm - 1:, :], carry_scr.shape)

    qkv = jnp.dot(hb, wqkv_ref[...], preferred_element_type=F32)
    aw = qkv.shape[1] // 3
    q_ref[...] = (qkv[:, :aw] * (HEAD_DIM ** -0.5 * LOG2_E)).astype(BF16)
    k_ref[...] = qkv[:, aw:2 * aw].astype(BF16)
    v_ref[...] = qkv[:, 2 * aw:].astype(BF16)
    aq_ref[...] = (jnp.dot(_split3(cum_q), eq_ref[...], preferred_element_type=F32) + oneq_ref[...]).astype(BF16)
    ak_ref[...] = (jnp.dot(_split3(cum_k), ek_ref[...], preferred_element_type=F32) + onek_ref[...]).astype(BF16)

    gts = jnp.dot(hb, wg_ref[...], preferred_element_type=F32)
    sgc_ref[...] = jax.nn.sigmoid(gts[:, :d_model]).astype(BF16)
    sga_ref[...] = jax.nn.sigmoid(gts[:, d_model:]).astype(BF16)


def _aug_constants():
    rows = jnp.arange(N_SPLIT * LANES)
    piece, head = rows // LANES, rows % LANES
    cols = jnp.arange(N_HEADS * HEAD_DIM)
    pair, off = cols // LANES, cols % LANES
    chead = 2 * pair + jnp.where(off >= HEAD_DIM, 0, 1)
    pos = off % HEAD_DIM
    valid = (head[:, None] < N_HEADS) & (head[:, None] == chead[None, :])
    eq = jnp.where(valid & (pos[None, :] == piece[:, None]), 1.0, 0.0).astype(BF16)
    ek = jnp.where(valid & (pos[None, :] == piece[:, None] + N_SPLIT), -1.0, 0.0).astype(BF16)
    oneq = jnp.where((pos >= N_SPLIT) & (pos < 2 * N_SPLIT), 1.0, 0.0).astype(F32)[None]
    onek = jnp.where(pos < N_SPLIT, 1.0, 0.0).astype(F32)[None]
    return eq, ek, oneq, onek


def _inproj(x2, mod8, norm_g, w_in, b_forget, conv_w, conv_b, conv_ln_g, conv_ln_b, seq):
    t, d = x2.shape
    tm = INPROJ_TILE
    nsub = tm // ATTN_BK
    cw = d // 2
    aw = N_HEADS * HEAD_DIM
    o0 = 2 * cw
    wa = w_in[:, :o0].astype(BF16)
    o1 = o0 + 3 * aw
    wqkv = w_in[:, o0:o1].astype(BF16)
    wf = jnp.pad(w_in[:, o1:o1 + N_HEADS], ((0, 0), (0, LANES - N_HEADS))).astype(BF16)
    bfp = jnp.pad(b_forget, (0, LANES - N_HEADS))[None]
    wg = w_in[:, o1 + N_HEADS:].astype(BF16)
    r = jnp.arange(tm)
    tri = (r[None, :] <= r[:, None]).astype(BF16)
    eq, ek, oneq, onek = _aug_constants()
    tiles_per_seq = seq // tm
    row = lambda i: (i, 0)
    outs = pl.pallas_call(
        functools.partial(_inproj_kernel, conv_width=cw, d_model=d, tiles_per_seq=tiles_per_seq),
        grid=(t // tm,),
        in_specs=[pl.BlockSpec((tm, d), row),
                  pl.BlockSpec((None, SUBLANES, d), lambda i: (i // tiles_per_seq, 0, 0)),
                  _resident((1, d)),
                  _resident(wa.shape), _resident(wqkv.shape),
                  _resident(wf.shape), _resident(wg.shape), _resident(bfp.shape),
                  _resident(tri.shape), _resident(eq.shape), _resident(ek.shape),
                  _resident(oneq.shape), _resident(onek.shape),
                  _resident(conv_w.shape), _resident((1, cw)), _resident((1, cw)), _resident((1, cw))],
        out_specs=[pl.BlockSpec((tm, cw), row)] + [pl.BlockSpec((tm, aw), row)] * 5 + [
                   pl.BlockSpec((nsub, 1, LANES), lambda i: (i, 0, 0)),
                   pl.BlockSpec((tm, d), row), pl.BlockSpec((tm, d), row)],
        out_shape=[jax.ShapeDtypeStruct((t, cw), BF16)] + [jax.ShapeDtypeStruct((t, aw), BF16)] * 5 + [
                   jax.ShapeDtypeStruct((t // ATTN_BK, 1, LANES), F32),
                   jax.ShapeDtypeStruct((t, d), BF16), jax.ShapeDtypeStruct((t, d), BF16)],
        scratch_shapes=[pltpu.VMEM((SUBLANES, LANES), F32),
                        pltpu.VMEM((CONV_HALO + tm, cw), F32), pltpu.VMEM((CONV_HALO + tm, cw), F32)],
        compiler_params=_cparams("arbitrary"),
        name="inproj",
    )(x2, mod8, norm_g[None], wa, wqkv, wf, wg, bfp, tri, eq, ek, oneq, onek,
      conv_w, conv_b[None], conv_ln_g[None], conv_ln_b[None])
    return outs


def _attn_kernel(q_ref, aq_ref, k_ref, ak_ref, v_ref, tot_ref, o_ref, qop_scr, m_scr, acc_scr):
    bq = q_ref.shape[0]
    bk = ATTN_BK
    ratio = bq // bk
    qi = pl.program_id(2)
    nt = (((1,), (1,)), ((), ()))
    heads = range(HEADS_PER_STEP)
    zero = jnp.zeros((1, LANES), F32)

    def splice(hh, main, extra):
        first = lax.broadcasted_iota(I32, main.shape, 1) < HEAD_DIM
        return jnp.where(first, main, extra) if hh == 0 else jnp.where(first, extra, main)

    for hh in heads:
        qop_scr[hh] = splice(hh, q_ref[...], aq_ref[...])

    def scores(hh, kb, row0):
        start = pl.multiple_of(kb * bk, bk)
        k = splice(hh, k_ref[pl.ds(start, bk), :], ak_ref[pl.ds(start, bk), :])
        v = v_ref[pl.ds(start, bk), :]
        v = splice(hh, v, jnp.ones(v.shape, v.dtype))
        s = lax.dot_general(qop_scr[hh, row0:, :], k, nt, preferred_element_type=F32)
        return s, v

    def update(hh, s, v, delta, row0):
        m_prev = m_scr[hh, row0:, :]
        m_new = jnp.maximum(m_prev, jnp.max(s, axis=1, keepdims=True) + delta)
        shift = m_new - delta
        p = jnp.exp2((s - shift[:, :1]).astype(BF16))
        alpha = jnp.exp2(m_prev - m_new)
        pv = jnp.dot(p, v, preferred_element_type=F32)
        acc_scr[hh, row0:, :] = alpha * acc_scr[hh, row0:, :] + pv
        m_scr[hh, row0:, :] = m_new

    for hh in heads:
        delta = zero
        for sb in range(ratio):
            kb = qi * ratio + sb
            row0 = sb * bk
            s, v = scores(hh, kb, row0)
            row = lax.broadcasted_iota(I32, s.shape, 0)
            col = lax.broadcasted_iota(I32, s.shape, 1)
            s = jnp.where(col <= row, s, -jnp.inf)
            if sb == 0:
                m0 = jnp.max(s, axis=1, keepdims=True)
                p = jnp.exp2((s - m0).astype(BF16))
                m_scr[hh] = jnp.broadcast_to(m0, m_scr.shape[1:])
                acc_scr[hh] = jnp.dot(p, v, preferred_element_type=F32)
            else:
                update(hh, s, v, delta, row0)
            delta = delta - tot_ref[hh, pl.ds(kb, 1), :]

    def body(step, deltas):
        kb = qi * ratio - 1 - step
        new_deltas = []
        for hh in heads:
            delta = deltas[hh] + tot_ref[hh, pl.ds(kb, 1), :]
            s, v = scores(hh, kb, 0)
            update(hh, s, v, delta, 0)
            new_deltas.append(delta)
        return tuple(new_deltas)

    lax.fori_loop(0, qi * ratio, body, tuple(zero for _ in heads))
    acc0, acc1 = acc_scr[0], acc_scr[1]
    out0 = acc0 / acc0[:, HEAD_DIM:HEAD_DIM + 1]
    out1 = acc1 / acc1[:, 0:1]
    o_ref[...] = splice(0, out0, out1).astype(BF16)


def _attention(q, aq, k, ak, v, tot, bsz, seq):
    t, aw = q.shape
    bq = ATTN_BQ
    nq = seq // bq
    nk = seq // ATTN_BK
    assert HEADS_PER_STEP * HEAD_DIM == LANES
    tot = tot.reshape(bsz, nk, LANES)[:, :, :N_HEADS].transpose(0, 2, 1)
    tot = jnp.broadcast_to(tot[..., None], (bsz, N_HEADS, nk, LANES))
    qblk = pl.BlockSpec((bq, LANES), lambda b, h, i: (b * nq + i, h))
    kblk = pl.BlockSpec((None, seq, LANES), lambda b, h, i: (b, 0, h))
    seq3 = lambda a: a.reshape(bsz, seq, aw)
    return pl.pallas_call(
        _attn_kernel,
        grid=(bsz, N_HEADS // HEADS_PER_STEP, nq),
        in_specs=[qblk, qblk, kblk, kblk, kblk,
                  pl.BlockSpec((None, HEADS_PER_STEP, nk, LANES), lambda b, h, i: (b, h, 0, 0))],
        out_specs=qblk,
        out_shape=jax.ShapeDtypeStruct((t, aw), BF16),
        scratch_shapes=[pltpu.VMEM((HEADS_PER_STEP, bq, LANES), BF16),
                        pltpu.VMEM((HEADS_PER_STEP, bq, LANES), F32),
                        pltpu.VMEM((HEADS_PER_STEP, bq, LANES), F32)],
        compiler_params=_cparams("arbitrary", "arbitrary", "arbitrary"),
        name="attn",
    )(q, aq, seq3(k), seq3(ak), seq3(v), tot)


def _pack_bf16_pair(lo, hi):
    lo_bits = pltpu.bitcast(lo.astype(BF16).astype(F32), U32)
    hi_bits = pltpu.bitcast(hi.astype(BF16).astype(F32), U32)
    return (hi_bits & jnp.uint32(HIGH_HALF)) | (lo_bits >> 16)


def _unpack_bf16_pair(w):
    lo = pltpu.bitcast(w << 16, F32)
    hi = pltpu.bitcast(w & jnp.uint32(HIGH_HALF), F32)
    return jnp.concatenate([lo, hi], axis=1)


def _expert_onehots(ids):
    lane = lax.broadcasted_iota(I32, ids.shape, 1)
    return lane, [lane == ids[:, kk:kk + 1] for kk in range(TOP_K)]


def _route_tile(ids, gates, carry, tri, utri):
    lane, hots = _expert_onehots(ids)
    multi = jnp.zeros(lane.shape, F32)
    for hot in hots:
        multi = multi + jnp.where(hot, 1.0, 0.0)
    before = jnp.dot(tri, multi.astype(BF16), preferred_element_type=F32)
    units = jnp.ceil(jnp.sum(multi, axis=0, keepdims=True) * (1.0 / CHUNK))
    units8 = jnp.broadcast_to(units, (SUBLANES, LANES)).astype(BF16)
    lstart = jnp.dot(units8, utri, preferred_element_type=F32)[0:1, :] * CHUNK
    csize = units * CHUNK
    slot = jnp.zeros(lane.shape, F32)
    for kk, hot in enumerate(hots):
        sl = jnp.sum(jnp.where(hot, lstart + before, 0.0), axis=1, keepdims=True)
        slot = jnp.where(lane == kk, sl, slot)
    row = lax.broadcasted_iota(I32, (SUBLANES, LANES), 0)
    meta = jnp.where(row == 0, csize, jnp.where(row == 1, lstart, jnp.where(row == 2, carry, 0.0)))
    slott = slot.T[0:SUBLANES, :].astype(I32)
    gatest = gates.T[0:SUBLANES, :]
    return slot.astype(I32), slott, gatest, meta, carry + csize


def _merge_kernel(x_ref, act_ref, attn_ref, sgc_ref, sga_ref, mod_ref, g2_ref, wco_ref, wao_ref, wout_ref,
                  wr_ref, br_ref, tri_ref, utri_ref,
                  x1_ref, h2_ref, slot_ref, slott_ref, gatest_ref, meta_ref, cnt_ref, carry_scr):
    @pl.when(pl.program_id(0) == 0)
    def _():
        carry_scr[...] = jnp.zeros(carry_scr.shape, F32)

    conv_out = jnp.dot(act_ref[...], wco_ref[...], preferred_element_type=F32)
    attn_out = jnp.dot(attn_ref[...], wao_ref[...], preferred_element_type=F32)
    mixed = sgc_ref[...] * conv_out.astype(BF16) + sga_ref[...] * attn_out.astype(BF16)
    upd = jnp.dot(mixed, wout_ref[...], preferred_element_type=F32)
    x1 = x_ref[...] + mod_ref[2:3, :] * upd
    x1_ref[...] = x1
    h2 = _rms(x1) * g2_ref[...] * (1.0 + mod_ref[4:5, :]) + mod_ref[3:4, :]
    h2_hi = h2.astype(BF16)
    h2_ref[...] = h2_hi
    h2_lo = (h2 - h2_hi.astype(F32)).astype(BF16)
    h2_cat = jnp.concatenate([h2_hi, h2_lo, h2_hi], axis=1)
    logits = jnp.dot(h2_cat, wr_ref[...], preferred_element_type=F32) + br_ref[...]
    lane = lax.broadcasted_iota(I32, logits.shape, 1)
    work = jnp.where(lane < N_EXPERTS, logits, -jnp.inf)
    ids = jnp.zeros(logits.shape, I32)
    vals = []
    for kk in range(TOP_K):
        mx = jnp.max(work, axis=1, keepdims=True)
        idx = jnp.min(jnp.where(work == mx, lane, LANES), axis=1, keepdims=True)
        ids = jnp.where(lane == kk, idx, ids)
        vals.append(mx)
        work = jnp.where(lane == idx, -jnp.inf, work)
    exps = [jnp.exp(vv - vals[0]) for vv in vals]
    den = exps[0] + exps[1] + exps[2] + exps[3]
    gates = jnp.zeros(logits.shape, F32)
    for kk in range(TOP_K):
        gates = jnp.where(lane == kk, exps[kk] / den, gates)

    carry = carry_scr[0:1, :]
    tt = TOKEN_TILE
    for sub in range(logits.shape[0] // tt):
        rows = slice(sub * tt, (sub + 1) * tt)
        slot, slott, gatest, meta, carry = _route_tile(ids[rows, :], gates[rows, :], carry,
                                                       tri_ref[...], utri_ref[...])
        slot_ref[rows, :] = slot
        slott_ref[sub * SUBLANES:(sub + 1) * SUBLANES, :] = slott
        gatest_ref[sub * SUBLANES:(sub + 1) * SUBLANES, :] = gatest
        meta_ref[sub] = meta
    carry_scr[...] = jnp.broadcast_to(carry, carry_scr.shape)
    cnt_ref[...] = jnp.broadcast_to(carry, cnt_ref.shape)


def _merge(x2, act, attn, sgc, sga, mod8, norm2_g, w_conv_out, w_attn_out, w_out, w_router, b_router, seq):
    t, d = x2.shape
    tm = MERGE_TILE
    tt = TOKEN_TILE
    sub = tm // tt
    nt = t // tt
    cw = act.shape[1]
    sw = attn.shape[1]
    wco = w_conv_out.astype(BF16)
    wao = w_attn_out.astype(BF16)
    wout = w_out.astype(BF16)
    wr = jnp.pad(w_router, ((0, 0), (0, LANES - N_EXPERTS)))
    wr_hi = wr.astype(BF16)
    wr_lo = (wr - wr_hi.astype(F32)).astype(BF16)
    wr = jnp.concatenate([wr_hi, wr_hi, wr_lo], axis=0)
    br = jnp.pad(b_router, (0, LANES - N_EXPERTS))[None]
    r = jnp.arange(tt)
    tri = (r[None, :] < r[:, None]).astype(BF16)
    e = jnp.arange(LANES)
    utri = (e[:, None] < e[None, :]).astype(BF16)
    tiles_per_seq = seq // tm
    row = lambda i: (i, 0)
    return pl.pallas_call(
        _merge_kernel,
        grid=(t // tm,),
        in_specs=[pl.BlockSpec((tm, d), row), pl.BlockSpec((tm, cw), row), pl.BlockSpec((tm, sw), row),
                  pl.BlockSpec((tm, d), row), pl.BlockSpec((tm, d), row),
                  pl.BlockSpec((None, SUBLANES, d), lambda i: (i // tiles_per_seq, 0, 0)),
                  _resident((1, d)), _resident(wco.shape), _resident(wao.shape), _resident(wout.shape),
                  _resident(wr.shape), _resident(br.shape), _resident(tri.shape), _resident(utri.shape)],
        out_specs=[pl.BlockSpec((tm, d), row), pl.BlockSpec((tm, d), row),
                   pl.BlockSpec((tm, LANES), row),
                   pl.BlockSpec((sub * SUBLANES, tt), row), pl.BlockSpec((sub * SUBLANES, tt), row),
                   pl.BlockSpec((sub, SUBLANES, LANES), lambda i: (i, 0, 0)),
                   pl.BlockSpec((SUBLANES, LANES), lambda i: (0, 0))],
        out_shape=[jax.ShapeDtypeStruct((t, d), F32), jax.ShapeDtypeStruct((t, d), BF16),
                   jax.ShapeDtypeStruct((t, LANES), I32),
                   jax.ShapeDtypeStruct((nt * SUBLANES, tt), I32), jax.ShapeDtypeStruct((nt * SUBLANES, tt), F32),
                   jax.ShapeDtypeStruct((nt, SUBLANES, LANES), F32),
                   jax.ShapeDtypeStruct((SUBLANES, LANES), F32)],
        scratch_shapes=[pltpu.VMEM((SUBLANES, LANES), F32)],
        compiler_params=_cparams("arbitrary"),
        name="merge",
    )(x2, act, attn, sgc, sga, mod8, norm2_g[None], wco, wao, wout, wr, br, tri, utri)


def _dest_kernel(cnt_ref, meta_ref, tri_ref, gd_ref, blk_ref):
    cnt = cnt_ref[...]
    padded = jnp.ceil(cnt * (1.0 / EXPERT_BLOCK)) * EXPERT_BLOCK
    pend = jnp.dot(padded, tri_ref[...], preferred_element_type=F32, precision=HIGHEST)
    pstart = (pend - padded)[0:1, :]

    grow = lax.broadcasted_iota(I32, (SUBLANES, LANES), 0)
    glane = lax.broadcasted_iota(I32, (SUBLANES, LANES), 1)
    gpos = ((grow * LANES + glane) * CHUNK).astype(F32)
    last = jnp.logical_and(grow == SUBLANES - 1, glane == LANES - 1)
    for tile in range(meta_ref.shape[0]):
        meta = meta_ref[tile]
        csize, lstart, seg_off = meta[0:1, :], meta[1:2, :], meta[2:3, :]
        lend = lstart + csize
        base = pstart + seg_off - lstart
        owner = jnp.zeros(gpos.shape, I32)
        for e in range(N_EXPERTS):
            owner = owner + jnp.where(gpos >= lend[0:1, e:e + 1], 1, 0)
        gdst = gpos
        for e in range(N_EXPERTS):
            gdst = gdst + jnp.where(owner == e, base[0:1, e:e + 1], 0.0)
        n_groups = lend[0:1, N_EXPERTS - 1:N_EXPERTS] * (1.0 / CHUNK)
        gd_ref[tile] = jnp.where(last, n_groups, gdst).astype(I32)

    @pl.when(pl.program_id(0) == 0)
    def _():
        shape = blk_ref.shape[1:]
        pos = (lax.broadcasted_iota(I32, shape, 0) * LANES + lax.broadcasted_iota(I32, shape, 1))
        pos = pos.astype(F32) * EXPERT_BLOCK
        be = jnp.zeros(shape, I32)
        for e in range(N_EXPERTS):
            be = be + jnp.where(pos >= pend[0:1, e:e + 1], 1, 0)
        blk_ref[0] = jnp.minimum(be, N_EXPERTS - 1)
        nact = pend[0:1, N_EXPERTS - 1:N_EXPERTS] * (1.0 / EXPERT_BLOCK)
        blk_ref[1] = jnp.broadcast_to(nact, shape).astype(I32)
        blk_ref[2] = jnp.broadcast_to(pend[0:1, :], shape).astype(I32)


def _dest(cnt, meta):
    nt = meta.shape[0]
    per_step = DEST_TILES_PER_STEP
    r = jnp.arange(LANES)
    tri = (r[:, None] <= r[None, :]).astype(F32)
    return pl.pallas_call(
        _dest_kernel,
        grid=(nt // per_step,),
        in_specs=[pl.BlockSpec((SUBLANES, LANES), lambda i: (0, 0)),
                  pl.BlockSpec((per_step, SUBLANES, LANES), lambda i: (i, 0, 0)),
                  pl.BlockSpec((LANES, LANES), lambda i: (0, 0))],
        out_specs=[pl.BlockSpec((per_step, SUBLANES, LANES), lambda i: (i, 0, 0)),
                   pl.BlockSpec((3, SUBLANES, LANES), lambda i: (0, 0, 0))],
        out_shape=[jax.ShapeDtypeStruct((nt, SUBLANES, LANES), I32),
                   jax.ShapeDtypeStruct((3, SUBLANES, LANES), I32)],
        compiler_params=_cparams("arbitrary"),
        name="dest",
    )(cnt, meta, tri)


def _row_copy(src, dst, sem):
    return pltpu.make_async_copy(src, dst, sem)


def _dispatch_kernel(pend_ref, na_ref, gd_ref, slott_ref, gatest_ref, h2_ref, xs_ref, zero_ref, sorted_scr, sem):
    tm = h2_ref.shape[0]
    bm = zero_ref.shape[0]
    nb = xs_ref.shape[0] // bm

    @pl.when(pl.program_id(0) == 0)
    def _():
        zero_ref[...] = jnp.zeros(zero_ref.shape, zero_ref.dtype)

        def zero_block(start):
            return _row_copy(zero_ref, xs_ref.at[pl.ds(pl.multiple_of(start, bm), bm)], sem)

        def nonempty(e):
            return pend_ref[e] > (pend_ref[e - 1] if e > 0 else 0)

        for e in range(N_EXPERTS):
            @pl.when(nonempty(e))
            def _():
                zero_block(pend_ref[e] - bm).start()

        def tail(b, carry):
            zero_block(b * bm).start()
            return carry

        lax.fori_loop(na_ref[0], nb, tail, 0)
        for e in range(N_EXPERTS):
            @pl.when(nonempty(e))
            def _():
                zero_block(0).wait()

        def tail_wait(b, carry):
            zero_block(0).wait()
            return carry

        lax.fori_loop(na_ref[0], nb, tail_wait, 0)

    srow = lax.broadcasted_iota(I32, (SLOT_ROWS, tm), 0)
    perm = jnp.zeros((SLOT_ROWS, tm), F32)
    gate = jnp.zeros((SLOT_ROWS, 1), F32)
    for kk in range(TOP_K):
        hit = srow == slott_ref[kk:kk + 1, :]
        perm = perm + jnp.where(hit, 1.0, 0.0)
        gate = gate + jnp.sum(jnp.where(hit, gatest_ref[kk:kk + 1, :], 0.0), axis=1, keepdims=True)
    perm = perm.astype(BF16)
    half = h2_ref.shape[1] // 2
    lo = jnp.dot(perm, h2_ref[:, :half], preferred_element_type=F32)
    hi = jnp.dot(perm, h2_ref[:, half:], preferred_element_type=F32)
    sorted_scr[:, :half] = pltpu.bitcast(hi, U32) | (pltpu.bitcast(lo, U32) >> 16)
    sorted_scr[:, half:] = pltpu.bitcast(jnp.broadcast_to(gate, (SLOT_ROWS, LANES)), U32)

    n_groups = gd_ref[GROUP_TABLE - 1]

    def group_copy(j, d):
        src = sorted_scr.at[pl.ds(pl.multiple_of(j * CHUNK, CHUNK), CHUNK)]
        return _row_copy(src, xs_ref.at[pl.ds(pl.multiple_of(d, CHUNK), CHUNK)], sem)

    def issue(j, carry):
        group_copy(j, gd_ref[j]).start()
        return carry

    def retire(j, carry):
        group_copy(0, 0).wait()
        return carry

    lax.fori_loop(0, n_groups, issue, 0)
    lax.fori_loop(0, n_groups, retire, 0)


def _dispatch(pend, n_active, gd_flat, slott, gatest, h2b, n_rows):
    t, d = h2b.shape
    w = d // 2 + LANES
    tm = TOKEN_TILE
    grid_spec = pltpu.PrefetchScalarGridSpec(
        num_scalar_prefetch=2,
        grid=(t // tm,),
        in_specs=[pl.BlockSpec((GROUP_TABLE,), lambda i, pe, na: (i,), memory_space=pltpu.SMEM),
                  pl.BlockSpec((SUBLANES, tm), lambda i, pe, na: (i, 0)),
                  pl.BlockSpec((SUBLANES, tm), lambda i, pe, na: (i, 0)),
                  pl.BlockSpec((tm, d), lambda i, pe, na: (i, 0))],
        out_specs=pl.BlockSpec(memory_space=pl.ANY),
        scratch_shapes=[pltpu.VMEM((EXPERT_BLOCK, w), U32), pltpu.VMEM((SLOT_ROWS, w), U32),
                        pltpu.SemaphoreType.DMA(())],
    )
    return pl.pallas_call(
        _dispatch_kernel,
        grid_spec=grid_spec,
        out_shape=jax.ShapeDtypeStruct((n_rows, w), U32),
        compiler_params=_cparams("arbitrary"),
        name="dispatch",
    )(pend, n_active, gd_flat, slott, gatest, h2b)


def _expert_kernel(be_ref, na_ref, xs_ref, w1_ref, b1_ref, w2_ref, b2_ref, y_ref, w1b_ref, w2b_ref, *, d_expert):
    i = pl.program_id(0)
    active = i < na_ref[0]
    prev = be_ref[jnp.maximum(i - 1, 0)]
    new_expert = jnp.logical_or(i == 0, be_ref[i] != prev)
    half = y_ref.shape[1]

    @pl.when(jnp.logical_and(active, new_expert))
    def _():
        w1b_ref[...] = w1_ref[...].astype(BF16)
        w2b_ref[...] = w2_ref[...].astype(BF16)

    @pl.when(active)
    def _():
        x = _unpack_bf16_pair(xs_ref[:, :half]).astype(BF16)
        gate = pltpu.bitcast(xs_ref[:, half:], F32)[:, 0:1]
        hc = jnp.dot(x, w1b_ref[...], preferred_element_type=F32) + b1_ref[...]
        g = jnp.minimum(hc[:, :d_expert], SWIGLU_LIMIT)
        u = jnp.clip(hc[:, d_expert:], -SWIGLU_LIMIT, SWIGLU_LIMIT)
        hm = (u + 1.0) * (g * jax.nn.sigmoid(SWIGLU_ALPHA * g))
        y = (jnp.dot(hm.astype(BF16), w2b_ref[...], preferred_element_type=F32) + b2_ref[...]) * gate
        y_ref[...] = _pack_bf16_pair(y[:, :half], y[:, half:])

    @pl.when(jnp.logical_not(active))
    def _():
        y_ref[...] = jnp.zeros(y_ref.shape, U32)


def _experts(block_e, n_active, xs, w1, b1, w2, b2):
    n_rows, w = xs.shape
    n_exp, d, two_de = w1.shape
    de = two_de // 2
    bm = EXPERT_BLOCK
    nb = n_rows // bm

    def blk(i, be, na):
        return jnp.maximum(jnp.minimum(i, na[0] - 1), 0)

    grid_spec = pltpu.PrefetchScalarGridSpec(
        num_scalar_prefetch=2,
        grid=(nb,),
        in_specs=[pl.BlockSpec((bm, w), lambda i, be, na: (blk(i, be, na), 0)),
                  pl.BlockSpec((None, d, two_de), lambda i, be, na: (be[blk(i, be, na)], 0, 0)),
                  pl.BlockSpec((None, 1, two_de), lambda i, be, na: (be[blk(i, be, na)], 0, 0)),
                  pl.BlockSpec((None, de, d), lambda i, be, na: (be[blk(i, be, na)], 0, 0)),
                  pl.BlockSpec((None, 1, d), lambda i, be, na: (be[blk(i, be, na)], 0, 0))],
        out_specs=pl.BlockSpec((bm, d // 2), lambda i, be, na: (i, 0)),
        scratch_shapes=[pltpu.VMEM((d, two_de), BF16), pltpu.VMEM((de, d), BF16)],
    )
    return pl.pallas_call(
        functools.partial(_expert_kernel, d_expert=de),
        grid_spec=grid_spec,
        out_shape=jax.ShapeDtypeStruct((n_rows, d // 2), U32),
        compiler_params=_cparams("arbitrary"),
        name="expert",
    )(block_e, n_active, xs, w1, b1[:, None, :], w2, b2[:, None, :])


def _combine_kernel(gd_ref, gd_next_ref, slot_ref, x1_ref, mod_ref, fg_ref, yb_ref, o_ref, sorted_scr, sem):
    tm = x1_ref.shape[0]
    step = pl.program_id(0)
    cur = step % 2

    def group_copy(buf, j, d):
        dst = sorted_scr.at[buf, pl.ds(pl.multiple_of(j * CHUNK, CHUNK), CHUNK)]
        return _row_copy(yb_ref.at[pl.ds(pl.multiple_of(d, CHUNK), CHUNK)], dst, sem.at[buf])

    def fetch(table_ref, buf):
        def issue(j, carry):
            group_copy(buf, j, table_ref[j]).start()
            return carry
        lax.fori_loop(0, table_ref[GROUP_TABLE - 1], issue, 0)

    @pl.when(step == 0)
    def _():
        sorted_scr[...] = jnp.zeros(sorted_scr.shape, sorted_scr.dtype)
        fetch(gd_ref, 0)

    @pl.when(step + 1 < pl.num_programs(0))
    def _():
        fetch(gd_next_ref, 1 - cur)

    def retire(j, carry):
        group_copy(cur, 0, 0).wait()
        return carry

    lax.fori_loop(0, gd_ref[GROUP_TABLE - 1], retire, 0)

    slot = slot_ref[...]
    scol = lax.broadcasted_iota(I32, (tm, SLOT_ROWS), 1)
    pick = jnp.zeros((tm, SLOT_ROWS), F32)
    for kk in range(TOP_K):
        pick = pick + jnp.where(scol == slot[:, kk:kk + 1], 1.0, 0.0)
    pick = pick.astype(BF16)
    words = sorted_scr[cur]
    lo = pltpu.bitcast(words << 16, F32).astype(BF16)
    hi = pltpu.bitcast(words & jnp.uint32(HIGH_HALF), F32).astype(BF16)
    y = jnp.concatenate([jnp.dot(pick, lo, preferred_element_type=F32),
                         jnp.dot(pick, hi, preferred_element_type=F32)], axis=1)
    x2 = x1_ref[...] + mod_ref[5:6, :] * y
    o_ref[...] = _rms(x2) * fg_ref[...]


def _combine(gd_flat, slot, x1, mod8, final_g, yb, seq):
    t, d = x1.shape
    tm = TOKEN_TILE
    nt = t // tm
    tiles_per_seq = seq // tm
    return pl.pallas_call(
        _combine_kernel,
        grid=(nt,),
        in_specs=[pl.BlockSpec((GROUP_TABLE,), lambda i: (i,), memory_space=pltpu.SMEM),
                  pl.BlockSpec((GROUP_TABLE,), lambda i: (jnp.minimum(i + 1, nt - 1),), memory_space=pltpu.SMEM),
                  pl.BlockSpec((tm, LANES), lambda i: (i, 0)),
                  pl.BlockSpec((tm, d), lambda i: (i, 0)),
                  pl.BlockSpec((None, SUBLANES, d), lambda i: (i // tiles_per_seq, 0, 0)),
                  pl.BlockSpec((1, d), lambda i: (0, 0)),
                  pl.BlockSpec(memory_space=pl.ANY)],
        out_specs=pl.BlockSpec((tm, d), lambda i: (i, 0)),
        out_shape=jax.ShapeDtypeStruct((t, d), F32),
        scratch_shapes=[pltpu.VMEM((2, SLOT_ROWS, d // 2), U32), pltpu.SemaphoreType.DMA((2,))],
        compiler_params=_cparams("arbitrary"),
        name="combine",
    )(gd_flat, gd_flat, slot, x1, mod8, final_g[None], yb)


def kernel(x, c, w_ada, b_ada, norm1_g, w_in, b_forget, conv_w, conv_b, conv_ln_g, conv_ln_b, w_conv_out,
           w_attn_out, w_out, norm2_g, w_router, b_router, w_exp_in, b_exp_in, w_exp_out, b_exp_out, final_g):
    bsz, seq, d = x.shape
    t = bsz * seq
    depth = w_ada.shape[0]
    xf = x.reshape(t, d)
    n_rows = t * TOP_K + (t // TOKEN_TILE) * N_EXPERTS * (CHUNK - 1) + N_EXPERTS * (EXPERT_BLOCK - CHUNK)
    n_rows = -(-n_rows // EXPERT_BLOCK) * EXPERT_BLOCK
    n_blocks = n_rows // EXPERT_BLOCK
    for l in range(depth):
        mod8 = _mod(c, w_ada, b_ada[l], l)
        act, q, k, v, aq, ak, tot, sgc, sga = _inproj(xf, mod8, norm1_g[l], w_in[l], b_forget[l], conv_w[l],
                                                      conv_b[l], conv_ln_g[l], conv_ln_b[l], seq)
        attn = _attention(q, aq, k, ak, v, tot, bsz, seq)
        x1, h2b, slot, slott, gatest, meta, cnt = _merge(xf, act, attn, sgc, sga, mod8, norm2_g[l], w_conv_out[l],
                                                         w_attn_out[l], w_out[l], w_router[l], b_router[l], seq)
        gd, blk = _dest(cnt, meta)
        gd_flat = gd.reshape(-1)
        block_e = blk[0].reshape(-1)[:n_blocks]
        n_active = blk[1, 0, :1]
        pend = blk[2, 0, :N_EXPERTS]
        xs = _dispatch(pend, n_active, gd_flat, slott, gatest, h2b, n_rows)
        yb = _experts(block_e, n_active, xs, w_exp_in[l], b_exp_in[l], w_exp_out[l], b_exp_out[l])
        assert depth == 1
        xf = _combine(gd_flat, slot, x1, mod8, final_g, yb, seq)
    return xf.reshape(bsz, seq, d)
```

```python
import functools

import jax
import jax.numpy as jnp
from jax import lax
from jax.experimental import pallas as pl
from jax.experimental.pallas import tpu as pltpu

F32 = jnp.float32
BF16 = jnp.bfloat16
I32 = jnp.int32
U32 = jnp.uint32
HIGHEST = lax.Precision.HIGHEST

LANES = 128
SUBLANES = 8
VMEM_LIMIT_BYTES = 56 * 1024 * 1024

N_HEADS = 8
HEAD_DIM = 64
CONV_KERNEL = 31
CONV_HALO = 32
N_EXPERTS = 32
TOP_K = 4
SWIGLU_LIMIT = 7.0
SWIGLU_ALPHA = 1.702
RMS_EPS = 1e-5
LN_EPS = 1e-5
N_MOD = 6
N_SPLIT = 3
LOG2_E = 1.4426950408889634
HIGH_HALF = 0xFFFF0000
HEADS_PER_STEP = 2

MOD_COLS_PER_STEP = 1024
INPROJ_TILE = 512
ATTN_BQ = 4096
ATTN_BK = 512
MERGE_TILE = 1024
TOKEN_TILE = 256
EXPERT_BLOCK = 768
CHUNK = SUBLANES
SLOT_ROWS = 1280
GROUP_TABLE = SUBLANES * LANES
DEST_TILES_PER_STEP = 8
assert SLOT_ROWS >= TOKEN_TILE * TOP_K + N_EXPERTS * (CHUNK - 1) and SLOT_ROWS // CHUNK < GROUP_TABLE


def _cparams(*sem):
    return pltpu.CompilerParams(dimension_semantics=sem, vmem_limit_bytes=VMEM_LIMIT_BYTES)


def _resident(shape):
    nd = len(shape)
    return pl.BlockSpec(shape, lambda *_: (0,) * nd, pipeline_mode=pl.Buffered(1))


def _split3(x):
    p1 = x.astype(BF16)
    r1 = x - p1.astype(F32)
    p2 = r1.astype(BF16)
    p3 = (r1 - p2.astype(F32)).astype(BF16)
    return jnp.concatenate([p1, p2, p3], axis=1)


def _rms(x):
    return x * lax.rsqrt(jnp.mean(x * x, axis=-1, keepdims=True) + RMS_EPS)


def _mod_kernel(ct_ref, w_ref, b_ref, o_ref, *, batch):
    rows = []
    for b in range(batch):
        col = ct_ref[:, b:b + 1]
        rows.append(jnp.sum((col * jax.nn.sigmoid(col)) * w_ref[...], axis=0, keepdims=True) + b_ref[...])
    rows.append(jnp.zeros((SUBLANES - batch, rows[0].shape[1]), F32))
    o_ref[...] = jnp.concatenate(rows, axis=0)


def _mod(c, w_ada, b_ada, layer):
    bsz, d = c.shape
    n = w_ada.shape[2]
    tn = MOD_COLS_PER_STEP
    out = pl.pallas_call(
        functools.partial(_mod_kernel, batch=bsz),
        grid=(n // tn,),
        in_specs=[pl.BlockSpec((d, bsz), lambda j: (0, 0)),
                  pl.BlockSpec((None, d, tn), lambda j: (layer, 0, j)),
                  pl.BlockSpec((1, tn), lambda j: (0, j))],
        out_specs=pl.BlockSpec((SUBLANES, tn), lambda j: (0, j)),
        out_shape=jax.ShapeDtypeStruct((SUBLANES, n), F32),
        compiler_params=_cparams("arbitrary"),
        name="mod",
    )(c.T, w_ada, b_ada[None])
    mod = out[:bsz].reshape(bsz, N_MOD, d)
    return jnp.pad(mod, ((0, 0), (0, SUBLANES - N_MOD), (0, 0)))


def _conv_module(a, first_of_seq, w_ref, b_ref, g_ref, beta_ref, xs_scr, ys_scr):
    ts = a.shape[0]
    xs_scr[0:CONV_HALO, :] = jnp.where(first_of_seq, 0.0, xs_scr[ts:ts + CONV_HALO, :])
    xs_scr[CONV_HALO:, :] = a
    acc = jnp.zeros(a.shape, F32) + b_ref[...]
    base = CONV_HALO - (CONV_KERNEL - 1)
    for rho in range(SUBLANES):
        offs = [o for o in range(base, base + CONV_KERNEL) if o % SUBLANES == rho]
        if not offs:
            continue
        rows = offs[-1] + ts - rho
        ys_scr[0:rows, :] = xs_scr[rho:rho + rows, :]
        for o in offs:
            lo = o - rho
            acc = acc + w_ref[o - base:o - base + 1, :] * ys_scr[lo:lo + ts, :]
    mu = jnp.mean(acc, axis=-1, keepdims=True)
    cen = acc - mu
    var = jnp.mean(cen * cen, axis=-1, keepdims=True)
    y = cen * lax.rsqrt(var + LN_EPS) * g_ref[...] + beta_ref[...]
    return (y * jax.nn.sigmoid(y)).astype(BF16)


def _inproj_kernel(x_ref, mod_ref, g_ref, wa_ref, wqkv_ref, wf_ref, wg_ref, bf_ref,
                   tri_ref, eq_ref, ek_ref, oneq_ref, onek_ref, cw_ref, cb_ref, cg_ref, cbeta_ref,
                   act_ref, q_ref, k_ref, v_ref, aq_ref, ak_ref, tot_ref, sgc_ref, sga_ref,
                   carry_scr, xs_scr, ys_scr, *, conv_width, d_model, tiles_per_seq):
    first_of_seq = pl.program_id(0) % tiles_per_seq == 0

    @pl.when(pl.program_id(0) == 0)
    def _():
        xs_scr[...] = jnp.zeros(xs_scr.shape, F32)
        carry_scr[...] = jnp.zeros(carry_scr.shape, F32)

    x = x_ref[...]
    h = _rms(x) * g_ref[...] * (1.0 + mod_ref[1:2, :]) + mod_ref[0:1, :]
    hb = h.astype(BF16)

    u = jnp.dot(hb, wa_ref[...], preferred_element_type=F32)
    a = u[:, :conv_width] * jax.nn.sigmoid(u[:, conv_width:])
    act_ref[...] = _conv_module(a, first_of_seq, cw_ref, cb_ref, cg_ref, cbeta_ref, xs_scr, ys_scr)

    fl = jnp.dot(hb, wf_ref[...], preferred_element_type=F32) + bf_ref[...]
    lane = lax.broadcasted_iota(I32, fl.shape, 1)
    log_f = -(jnp.maximum(-fl, 0.0) + jnp.log1p(jnp.exp(-jnp.abs(fl))))
    log_f = jnp.where(lane < N_HEADS, log_f, 0.0)
    cs = jnp.dot(tri_ref[...], _split3(log_f), preferred_element_type=F32)
    cum = (cs[:, :LANES] + cs[:, LANES:2 * LANES] + cs[:, 2 * LANES:]) * LOG2_E
    tm = cum.shape[0]

    pieces = []
    for sb in range(tm // ATTN_BK):
        blk = cum[sb * ATTN_BK:(sb + 1) * ATTN_BK, :]
        if sb > 0:
            blk = blk - cum[sb * ATTN_BK - 1:sb * ATTN_BK, :]
        tot_ref[sb] = blk[ATTN_BK - 1:, :]
        pieces.append(blk)
    cum_k = pieces[0] if len(pieces) == 1 else jnp.concatenate(pieces, axis=0)

    first_of_block = pl.program_id(0) % (ATTN_BQ // tm) == 0
    cum_q = cum + jnp.where(first_of_block, 0.0, carry_scr[0:1, :])
    carry_scr[...] = jnp.broadcast_to(cum_q[tm - 1:, :], carry_scr.shape)

    qkv = jnp.dot(hb, wqkv_ref[...], preferred_element_type=F32)
    aw = qkv.shape[1] // 3
    q_ref[...] = (qkv[:, :aw] * (HEAD_DIM ** -0.5 * LOG2_E)).astype(BF16)
    k_ref[...] = qkv[:, aw:2 * aw].astype(BF16)
    v_ref[...] = qkv[:, 2 * aw:].astype(BF16)
    aq_ref[...] = (jnp.dot(_split3(cum_q), eq_ref[...], preferred_element_type=F32) + oneq_ref[...]).astype(BF16)
    ak_ref[...] = (jnp.dot(_split3(cum_k), ek_ref[...], preferred_element_type=F32) + onek_ref[...]).astype(BF16)

    sgc_ref[...] = jax.nn.sigmoid(jnp.dot(hb, wg_ref[:, :d_model], preferred_element_type=F32)).astype(BF16)
    sga_ref[...] = jax.nn.sigmoid(jnp.dot(hb, wg_ref[:, d_model:], preferred_element_type=F32)).astype(BF16)


def _aug_constants():
    rows = jnp.arange(N_SPLIT * LANES)
    piece, head = rows // LANES, rows % LANES
    cols = jnp.arange(N_HEADS * HEAD_DIM)
    pair, off = cols // LANES, cols % LANES
    chead = 2 * pair + jnp.where(off >= HEAD_DIM, 0, 1)
    pos = off % HEAD_DIM
    valid = (head[:, None] < N_HEADS) & (head[:, None] == chead[None, :])
    eq = jnp.where(valid & (pos[None, :] == piece[:, None]), 1.0, 0.0).astype(BF16)
    ek = jnp.where(valid & (pos[None, :] == piece[:, None] + N_SPLIT), -1.0, 0.0).astype(BF16)
    oneq = jnp.where((pos >= N_SPLIT) & (pos < 2 * N_SPLIT), 1.0, 0.0).astype(F32)[None]
    onek = jnp.where(pos < N_SPLIT, 1.0, 0.0).astype(F32)[None]
    return eq, ek, oneq, onek


def _inproj(x2, mod8, norm_g, w_in, b_forget, conv_w, conv_b, conv_ln_g, conv_ln_b, seq):
    t, d = x2.shape
    tm = INPROJ_TILE
    nsub = tm // ATTN_BK
    cw = d // 2
    aw = N_HEADS * HEAD_DIM
    o0 = 2 * cw
    wa = w_in[:, :o0].astype(BF16)
    o1 = o0 + 3 * aw
    wqkv = w_in[:, o0:o1].astype(BF16)
    wf = jnp.pad(w_in[:, o1:o1 + N_HEADS], ((0, 0), (0, LANES - N_HEADS))).astype(BF16)
    bfp = jnp.pad(b_forget, (0, LANES - N_HEADS))[None]
    wg = w_in[:, o1 + N_HEADS:].astype(BF16)
    r = jnp.arange(tm)
    tri = (r[None, :] <= r[:, None]).astype(BF16)
    eq, ek, oneq, onek = _aug_constants()
    tiles_per_seq = seq // tm
    row = lambda i: (i, 0)
    outs = pl.pallas_call(
        functools.partial(_inproj_kernel, conv_width=cw, d_model=d, tiles_per_seq=tiles_per_seq),
        grid=(t // tm,),
        in_specs=[pl.BlockSpec((tm, d), row),
                  pl.BlockSpec((None, SUBLANES, d), lambda i: (i // tiles_per_seq, 0, 0)),
                  _resident((1, d)),
                  _resident(wa.shape), _resident(wqkv.shape),
                  _resident(wf.shape), _resident(wg.shape), _resident(bfp.shape),
                  _resident(tri.shape), _resident(eq.shape), _resident(ek.shape),
                  _resident(oneq.shape), _resident(onek.shape),
                  _resident(conv_w.shape), _resident((1, cw)), _resident((1, cw)), _resident((1, cw))],
        out_specs=[pl.BlockSpec((tm, cw), row)] + [pl.BlockSpec((tm, aw), row)] * 5 + [
                   pl.BlockSpec((nsub, 1, LANES), lambda i: (i, 0, 0)),
                   pl.BlockSpec((tm, d), row), pl.BlockSpec((tm, d), row)],
        out_shape=[jax.ShapeDtypeStruct((t, cw), BF16)] + [jax.ShapeDtypeStruct((t, aw), BF16)] * 5 + [
                   jax.ShapeDtypeStruct((t // ATTN_BK, 1, LANES), F32),
                   jax.ShapeDtypeStruct((t, d), BF16), jax.ShapeDtypeStruct((t, d), BF16)],
        scratch_shapes=[pltpu.VMEM((SUBLANES, LANES), F32),
                        pltpu.VMEM((CONV_HALO + tm, cw), F32), pltpu.VMEM((CONV_HALO + tm, cw), F32)],
        compiler_params=_cparams("arbitrary"),
        name="inproj",
    )(x2, mod8, norm_g[None], wa, wqkv, wf, wg, bfp, tri, eq, ek, oneq, onek,
      conv_w, conv_b[None], conv_ln_g[None], conv_ln_b[None])
    return outs


def _attn_kernel(q_ref, aq_ref, k_ref, ak_ref, v_ref, tot_ref, o_ref, qop_scr, m_scr, acc_scr):
    bq = q_ref.shape[0]
    bk = ATTN_BK
    ratio = bq // bk
    qi = pl.program_id(2)
    nt = (((1,), (1,)), ((), ()))
    heads = range(HEADS_PER_STEP)
    zero = jnp.zeros((1, LANES), F32)

    def splice(hh, main, extra):
        first = lax.broadcasted_iota(I32, main.shape, 1) < HEAD_DIM
        return jnp.where(first, main, extra) if hh == 0 else jnp.where(first, extra, main)

    for hh in heads:
        qop_scr[hh] = splice(hh, q_ref[...], aq_ref[...])

    def scores(hh, kb, row0):
        start = pl.multiple_of(kb * bk, bk)
        k = splice(hh, k_ref[pl.ds(start, bk), :], ak_ref[pl.ds(start, bk), :])
        v = v_ref[pl.ds(start, bk), :]
        v = splice(hh, v, jnp.ones(v.shape, v.dtype))
        s = lax.dot_general(qop_scr[hh, row0:, :], k, nt, preferred_element_type=F32)
        return s, v

    def update(hh, s, v, delta, row0):
        m_prev = m_scr[hh, row0:, :]
        m_new = jnp.maximum(m_prev, jnp.max(s, axis=1, keepdims=True) + delta)
        shift = m_new - delta
        p = jnp.exp2((s - shift[:, :1]).astype(BF16))
        alpha = jnp.exp2(m_prev - m_new)
        pv = jnp.dot(p, v, preferred_element_type=F32)
        acc_scr[hh, row0:, :] = alpha * acc_scr[hh, row0:, :] + pv
        m_scr[hh, row0:, :] = m_new

    for hh in heads:
        delta = zero
        for sb in range(ratio):
            kb = qi * ratio + sb
            row0 = sb * bk
            s, v = scores(hh, kb, row0)
            row = lax.broadcasted_iota(I32, s.shape, 0)
            col = lax.broadcasted_iota(I32, s.shape, 1)
            s = jnp.where(col <= row, s, -jnp.inf)
            if sb == 0:
                m0 = jnp.max(s, axis=1, keepdims=True)
                p = jnp.exp2((s - m0).astype(BF16))
                m_scr[hh] = jnp.broadcast_to(m0, m_scr.shape[1:])
                acc_scr[hh] = jnp.dot(p, v, preferred_element_type=F32)
            else:
                update(hh, s, v, delta, row0)
            delta = delta - tot_ref[hh, pl.ds(kb, 1), :]

    def body(step, deltas):
        kb = qi * ratio - 1 - step
        new_deltas = []
        for hh in heads:
            delta = deltas[hh] + tot_ref[hh, pl.ds(kb, 1), :]
            s, v = scores(hh, kb, 0)
            update(hh, s, v, delta, 0)
            new_deltas.append(delta)
        return tuple(new_deltas)

    lax.fori_loop(0, qi * ratio, body, tuple(zero for _ in heads))
    acc0, acc1 = acc_scr[0], acc_scr[1]
    out0 = acc0 / acc0[:, HEAD_DIM:HEAD_DIM + 1]
    out1 = acc1 / acc1[:, 0:1]
    o_ref[...] = splice(0, out0, out1).astype(BF16)


def _attention(q, aq, k, ak, v, tot, bsz, seq):
    t, aw = q.shape
    bq = ATTN_BQ
    nq = seq // bq
    nk = seq // ATTN_BK
    assert HEADS_PER_STEP * HEAD_DIM == LANES
    tot = tot.reshape(bsz, nk, LANES)[:, :, :N_HEADS].transpose(0, 2, 1)
    tot = jnp.broadcast_to(tot[..., None], (bsz, N_HEADS, nk, LANES))
    qblk = pl.BlockSpec((bq, LANES), lambda b, h, i: (b * nq + i, h))
    kblk = pl.BlockSpec((None, seq, LANES), lambda b, h, i: (b, 0, h))
    seq3 = lambda a: a.reshape(bsz, seq, aw)
    return pl.pallas_call(
        _attn_kernel,
        grid=(bsz, N_HEADS // HEADS_PER_STEP, nq),
        in_specs=[qblk, qblk, kblk, kblk, kblk,
                  pl.BlockSpec((None, HEADS_PER_STEP, nk, LANES), lambda b, h, i: (b, h, 0, 0))],
        out_specs=qblk,
        out_shape=jax.ShapeDtypeStruct((t, aw), BF16),
        scratch_shapes=[pltpu.VMEM((HEADS_PER_STEP, bq, LANES), BF16),
                        pltpu.VMEM((HEADS_PER_STEP, bq, LANES), F32),
                        pltpu.VMEM((HEADS_PER_STEP, bq, LANES), F32)],
        compiler_params=_cparams("arbitrary", "arbitrary", "arbitrary"),
        name="attn",
    )(q, aq, seq3(k), seq3(ak), seq3(v), tot)


def _pack_bf16_pair(lo, hi):
    lo_bits = pltpu.bitcast(lo.astype(BF16).astype(F32), U32)
    hi_bits = pltpu.bitcast(hi.astype(BF16).astype(F32), U32)
    return (hi_bits & jnp.uint32(HIGH_HALF)) | (lo_bits >> 16)


def _unpack_bf16_pair(w):
    lo = pltpu.bitcast(w << 16, F32)
    hi = pltpu.bitcast(w & jnp.uint32(HIGH_HALF), F32)
    return jnp.concatenate([lo, hi], axis=1)


def _expert_onehots(ids):
    lane = lax.broadcasted_iota(I32, ids.shape, 1)
    return lane, [lane == ids[:, kk:kk + 1] for kk in range(TOP_K)]


def _route_tile(ids, gates, carry, tri, utri):
    lane, hots = _expert_onehots(ids)
    multi = jnp.zeros(lane.shape, F32)
    for hot in hots:
        multi = multi + jnp.where(hot, 1.0, 0.0)
    before = jnp.dot(tri, multi.astype(BF16), preferred_element_type=F32)
    units = jnp.ceil(jnp.sum(multi, axis=0, keepdims=True) * (1.0 / CHUNK))
    units8 = jnp.broadcast_to(units, (SUBLANES, LANES)).astype(BF16)
    lstart = jnp.dot(units8, utri, preferred_element_type=F32)[0:1, :] * CHUNK
    csize = units * CHUNK
    slot = jnp.zeros(lane.shape, F32)
    for kk, hot in enumerate(hots):
        sl = jnp.sum(jnp.where(hot, lstart + before, 0.0), axis=1, keepdims=True)
        slot = jnp.where(lane == kk, sl, slot)
    row = lax.broadcasted_iota(I32, (SUBLANES, LANES), 0)
    meta = jnp.where(row == 0, csize, jnp.where(row == 1, lstart, jnp.where(row == 2, carry, 0.0)))
    slott = slot.T[0:SUBLANES, :].astype(I32)
    gatest = gates.T[0:SUBLANES, :]
    return slot.astype(I32), slott, gatest, meta, carry + csize


def _merge_kernel(x_ref, act_ref, attn_ref, sgc_ref, sga_ref, mod_ref, g2_ref, wco_ref, wao_ref, wout_ref,
                  wr_ref, br_ref, tri_ref, utri_ref,
                  x1_ref, h2_ref, slot_ref, slott_ref, gatest_ref, meta_ref, cnt_ref, carry_scr):
    @pl.when(pl.program_id(0) == 0)
    def _():
        carry_scr[...] = jnp.zeros(carry_scr.shape, F32)

    conv_out = jnp.dot(act_ref[...], wco_ref[...], preferred_element_type=F32)
    attn_out = jnp.dot(attn_ref[...], wao_ref[...], preferred_element_type=F32)
    mixed = sgc_ref[...].astype(F32) * conv_out + sga_ref[...].astype(F32) * attn_out
    upd = jnp.dot(mixed.astype(BF16), wout_ref[...], preferred_element_type=F32)
    x1 = x_ref[...] + mod_ref[2:3, :] * upd
    x1_ref[...] = x1
    h2 = _rms(x1) * g2_ref[...] * (1.0 + mod_ref[4:5, :]) + mod_ref[3:4, :]
    h2_hi = h2.astype(BF16)
    h2_ref[...] = h2_hi
    h2_lo = (h2 - h2_hi.astype(F32)).astype(BF16)
    h2_cat = jnp.concatenate([h2_hi, h2_lo, h2_hi], axis=1)
    logits = jnp.dot(h2_cat, wr_ref[...], preferred_element_type=F32) + br_ref[...]
    lane = lax.broadcasted_iota(I32, logits.shape, 1)
    work = jnp.where(lane < N_EXPERTS, logits, -jnp.inf)
    ids = jnp.zeros(logits.shape, I32)
    vals = []
    for kk in range(TOP_K):
        mx = jnp.max(work, axis=1, keepdims=True)
        idx = jnp.min(jnp.where(work == mx, lane, LANES), axis=1, keepdims=True)
        ids = jnp.where(lane == kk, idx, ids)
        vals.append(mx)
        work = jnp.where(lane == idx, -jnp.inf, work)
    exps = [jnp.exp(vv - vals[0]) for vv in vals]
    den = exps[0] + exps[1] + exps[2] + exps[3]
    gates = jnp.zeros(logits.shape, F32)
    for kk in range(TOP_K):
        gates = jnp.where(lane == kk, exps[kk] / den, gates)

    carry = carry_scr[0:1, :]
    tt = TOKEN_TILE
    for sub in range(logits.shape[0] // tt):
        rows = slice(sub * tt, (sub + 1) * tt)
        slot, slott, gatest, meta, carry = _route_tile(ids[rows, :], gates[rows, :], carry,
                                                       tri_ref[...], utri_ref[...])
        slot_ref[rows, :] = slot
        slott_ref[sub * SUBLANES:(sub + 1) * SUBLANES, :] = slott
        gatest_ref[sub * SUBLANES:(sub + 1) * SUBLANES, :] = gatest
        meta_ref[sub] = meta
    carry_scr[...] = jnp.broadcast_to(carry, carry_scr.shape)
    cnt_ref[...] = jnp.broadcast_to(carry, cnt_ref.shape)


def _merge(x2, act, attn, sgc, sga, mod8, norm2_g, w_conv_out, w_attn_out, w_out, w_router, b_router, seq):
    t, d = x2.shape
    tm = MERGE_TILE
    tt = TOKEN_TILE
    sub = tm // tt
    nt = t // tt
    cw = act.shape[1]
    sw = attn.shape[1]
    wco = w_conv_out.astype(BF16)
    wao = w_attn_out.astype(BF16)
    wout = w_out.astype(BF16)
    wr = jnp.pad(w_router, ((0, 0), (0, LANES - N_EXPERTS)))
    wr_hi = wr.astype(BF16)
    wr_lo = (wr - wr_hi.astype(F32)).astype(BF16)
    wr = jnp.concatenate([wr_hi, wr_hi, wr_lo], axis=0)
    br = jnp.pad(b_router, (0, LANES - N_EXPERTS))[None]
    r = jnp.arange(tt)
    tri = (r[None, :] < r[:, None]).astype(BF16)
    e = jnp.arange(LANES)
    utri = (e[:, None] < e[None, :]).astype(BF16)
    tiles_per_seq = seq // tm
    row = lambda i: (i, 0)
    return pl.pallas_call(
        _merge_kernel,
        grid=(t // tm,),
        in_specs=[pl.BlockSpec((tm, d), row), pl.BlockSpec((tm, cw), row), pl.BlockSpec((tm, sw), row),
                  pl.BlockSpec((tm, d), row), pl.BlockSpec((tm, d), row),
                  pl.BlockSpec((None, SUBLANES, d), lambda i: (i // tiles_per_seq, 0, 0)),
                  _resident((1, d)), _resident(wco.shape), _resident(wao.shape), _resident(wout.shape),
                  _resident(wr.shape), _resident(br.shape), _resident(tri.shape), _resident(utri.shape)],
        out_specs=[pl.BlockSpec((tm, d), row), pl.BlockSpec((tm, d), row),
                   pl.BlockSpec((tm, LANES), row),
                   pl.BlockSpec((sub * SUBLANES, tt), row), pl.BlockSpec((sub * SUBLANES, tt), row),
                   pl.BlockSpec((sub, SUBLANES, LANES), lambda i: (i, 0, 0)),
                   pl.BlockSpec((SUBLANES, LANES), lambda i: (0, 0))],
        out_shape=[jax.ShapeDtypeStruct((t, d), F32), jax.ShapeDtypeStruct((t, d), BF16),
                   jax.ShapeDtypeStruct((t, LANES), I32),
                   jax.ShapeDtypeStruct((nt * SUBLANES, tt), I32), jax.ShapeDtypeStruct((nt * SUBLANES, tt), F32),
                   jax.ShapeDtypeStruct((nt, SUBLANES, LANES), F32),
                   jax.ShapeDtypeStruct((SUBLANES, LANES), F32)],
        scratch_shapes=[pltpu.VMEM((SUBLANES, LANES), F32)],
        compiler_params=_cparams("arbitrary"),
        name="merge",
    )(x2, act, attn, sgc, sga, mod8, norm2_g[None], wco, wao, wout, wr, br, tri, utri)


def _dest_kernel(cnt_ref, meta_ref, tri_ref, gd_ref, blk_ref):
    cnt = cnt_ref[...]
    padded = jnp.ceil(cnt * (1.0 / EXPERT_BLOCK)) * EXPERT_BLOCK
    pend = jnp.dot(padded, tri_ref[...], preferred_element_type=F32, precision=HIGHEST)
    pstart = (pend - padded)[0:1, :]

    grow = lax.broadcasted_iota(I32, (SUBLANES, LANES), 0)
    glane = lax.broadcasted_iota(I32, (SUBLANES, LANES), 1)
    gpos = ((grow * LANES + glane) * CHUNK).astype(F32)
    last = jnp.logical_and(grow == SUBLANES - 1, glane == LANES - 1)
    for tile in range(meta_ref.shape[0]):
        meta = meta_ref[tile]
        csize, lstart, seg_off = meta[0:1, :], meta[1:2, :], meta[2:3, :]
        lend = lstart + csize
        base = pstart + seg_off - lstart
        owner = jnp.zeros(gpos.shape, I32)
        for e in range(N_EXPERTS):
            owner = owner + jnp.where(gpos >= lend[0:1, e:e + 1], 1, 0)
        gdst = gpos
        for e in range(N_EXPERTS):
            gdst = gdst + jnp.where(owner == e, base[0:1, e:e + 1], 0.0)
        n_groups = lend[0:1, N_EXPERTS - 1:N_EXPERTS] * (1.0 / CHUNK)
        gd_ref[tile] = jnp.where(last, n_groups, gdst).astype(I32)

    @pl.when(pl.program_id(0) == 0)
    def _():
        shape = blk_ref.shape[1:]
        pos = (lax.broadcasted_iota(I32, shape, 0) * LANES + lax.broadcasted_iota(I32, shape, 1))
        pos = pos.astype(F32) * EXPERT_BLOCK
        be = jnp.zeros(shape, I32)
        for e in range(N_EXPERTS):
            be = be + jnp.where(pos >= pend[0:1, e:e + 1], 1, 0)
        blk_ref[0] = jnp.minimum(be, N_EXPERTS - 1)
        nact = pend[0:1, N_EXPERTS - 1:N_EXPERTS] * (1.0 / EXPERT_BLOCK)
        blk_ref[1] = jnp.broadcast_to(nact, shape).astype(I32)
        blk_ref[2] = jnp.broadcast_to(pend[0:1, :], shape).astype(I32)


def _dest(cnt, meta):
    nt = meta.shape[0]
    per_step = DEST_TILES_PER_STEP
    r = jnp.arange(LANES)
    tri = (r[:, None] <= r[None, :]).astype(F32)
    return pl.pallas_call(
        _dest_kernel,
        grid=(nt // per_step,),
        in_specs=[pl.BlockSpec((SUBLANES, LANES), lambda i: (0, 0)),
                  pl.BlockSpec((per_step, SUBLANES, LANES), lambda i: (i, 0, 0)),
                  pl.BlockSpec((LANES, LANES), lambda i: (0, 0))],
        out_specs=[pl.BlockSpec((per_step, SUBLANES, LANES), lambda i: (i, 0, 0)),
                   pl.BlockSpec((3, SUBLANES, LANES), lambda i: (0, 0, 0))],
        out_shape=[jax.ShapeDtypeStruct((nt, SUBLANES, LANES), I32),
                   jax.ShapeDtypeStruct((3, SUBLANES, LANES), I32)],
        compiler_params=_cparams("arbitrary"),
        name="dest",
    )(cnt, meta, tri)


def _row_copy(src, dst, sem):
    return pltpu.make_async_copy(src, dst, sem)


def _dispatch_kernel(pend_ref, na_ref, gd_ref, slott_ref, gatest_ref, h2_ref, xs_ref, zero_ref, sorted_scr, sem):
    tm = h2_ref.shape[0]
    bm = zero_ref.shape[0]
    nb = xs_ref.shape[0] // bm

    @pl.when(pl.program_id(0) == 0)
    def _():
        zero_ref[...] = jnp.zeros(zero_ref.shape, zero_ref.dtype)

        def zero_block(start):
            return _row_copy(zero_ref, xs_ref.at[pl.ds(pl.multiple_of(start, bm), bm)], sem)

        def nonempty(e):
            return pend_ref[e] > (pend_ref[e - 1] if e > 0 else 0)

        for e in range(N_EXPERTS):
            @pl.when(nonempty(e))
            def _():
                zero_block(pend_ref[e] - bm).start()

        def tail(b, carry):
            zero_block(b * bm).start()
            return carry

        lax.fori_loop(na_ref[0], nb, tail, 0)
        for e in range(N_EXPERTS):
            @pl.when(nonempty(e))
            def _():
                zero_block(0).wait()

        def tail_wait(b, carry):
            zero_block(0).wait()
            return carry

        lax.fori_loop(na_ref[0], nb, tail_wait, 0)

    srow = lax.broadcasted_iota(I32, (SLOT_ROWS, tm), 0)
    perm = jnp.zeros((SLOT_ROWS, tm), F32)
    gate = jnp.zeros((SLOT_ROWS, 1), F32)
    for kk in range(TOP_K):
        hit = srow == slott_ref[kk:kk + 1, :]
        perm = perm + jnp.where(hit, 1.0, 0.0)
        gate = gate + jnp.sum(jnp.where(hit, gatest_ref[kk:kk + 1, :], 0.0), axis=1, keepdims=True)
    perm = perm.astype(BF16)
    half = h2_ref.shape[1] // 2
    lo = jnp.dot(perm, h2_ref[:, :half], preferred_element_type=F32)
    hi = jnp.dot(perm, h2_ref[:, half:], preferred_element_type=F32)
    sorted_scr[:, :half] = pltpu.bitcast(hi, U32) | (pltpu.bitcast(lo, U32) >> 16)
    sorted_scr[:, half:] = pltpu.bitcast(jnp.broadcast_to(gate, (SLOT_ROWS, LANES)), U32)

    n_groups = gd_ref[GROUP_TABLE - 1]

    def group_copy(j, d):
        src = sorted_scr.at[pl.ds(pl.multiple_of(j * CHUNK, CHUNK), CHUNK)]
        return _row_copy(src, xs_ref.at[pl.ds(pl.multiple_of(d, CHUNK), CHUNK)], sem)

    def issue(j, carry):
        group_copy(j, gd_ref[j]).start()
        return carry

    def retire(j, carry):
        group_copy(0, 0).wait()
        return carry

    lax.fori_loop(0, n_groups, issue, 0)
    lax.fori_loop(0, n_groups, retire, 0)


def _dispatch(pend, n_active, gd_flat, slott, gatest, h2b, n_rows):
    t, d = h2b.shape
    w = d // 2 + LANES
    tm = TOKEN_TILE
    grid_spec = pltpu.PrefetchScalarGridSpec(
        num_scalar_prefetch=2,
        grid=(t // tm,),
        in_specs=[pl.BlockSpec((GROUP_TABLE,), lambda i, pe, na: (i,), memory_space=pltpu.SMEM),
                  pl.BlockSpec((SUBLANES, tm), lambda i, pe, na: (i, 0)),
                  pl.BlockSpec((SUBLANES, tm), lambda i, pe, na: (i, 0)),
                  pl.BlockSpec((tm, d), lambda i, pe, na: (i, 0))],
        out_specs=pl.BlockSpec(memory_space=pl.ANY),
        scratch_shapes=[pltpu.VMEM((EXPERT_BLOCK, w), U32), pltpu.VMEM((SLOT_ROWS, w), U32),
                        pltpu.SemaphoreType.DMA(())],
    )
    return pl.pallas_call(
        _dispatch_kernel,
        grid_spec=grid_spec,
        out_shape=jax.ShapeDtypeStruct((n_rows, w), U32),
        compiler_params=_cparams("arbitrary"),
        name="dispatch",
    )(pend, n_active, gd_flat, slott, gatest, h2b)


def _expert_kernel(be_ref, na_ref, xs_ref, w1_ref, b1_ref, w2_ref, b2_ref, y_ref, w1b_ref, w2b_ref, *, d_expert):
    i = pl.program_id(0)
    active = i < na_ref[0]
    prev = be_ref[jnp.maximum(i - 1, 0)]
    new_expert = jnp.logical_or(i == 0, be_ref[i] != prev)
    half = y_ref.shape[1]

    @pl.when(jnp.logical_and(active, new_expert))
    def _():
        w1b_ref[...] = w1_ref[...].astype(BF16)
        w2b_ref[...] = w2_ref[...].astype(BF16)

    @pl.when(active)
    def _():
        x = _unpack_bf16_pair(xs_ref[:, :half]).astype(BF16)
        gate = pltpu.bitcast(xs_ref[:, half:], F32)[:, 0:1]
        hc = jnp.dot(x, w1b_ref[...], preferred_element_type=F32) + b1_ref[...]
        g = jnp.minimum(hc[:, :d_expert], SWIGLU_LIMIT)
        u = jnp.clip(hc[:, d_expert:], -SWIGLU_LIMIT, SWIGLU_LIMIT)
        hm = (u + 1.0) * (g * jax.nn.sigmoid(SWIGLU_ALPHA * g))
        y = (jnp.dot(hm.astype(BF16), w2b_ref[...], preferred_element_type=F32) + b2_ref[...]) * gate
        y_ref[...] = _pack_bf16_pair(y[:, :half], y[:, half:])

    @pl.when(jnp.logical_not(active))
    def _():
        y_ref[...] = jnp.zeros(y_ref.shape, U32)


def _experts(block_e, n_active, xs, w1, b1, w2, b2):
    n_rows, w = xs.shape
    n_exp, d, two_de = w1.shape
    de = two_de // 2
    bm = EXPERT_BLOCK
    nb = n_rows // bm

    def blk(i, be, na):
        return jnp.maximum(jnp.minimum(i, na[0] - 1), 0)

    grid_spec = pltpu.PrefetchScalarGridSpec(
        num_scalar_prefetch=2,
        grid=(nb,),
        in_specs=[pl.BlockSpec((bm, w), lambda i, be, na: (blk(i, be, na), 0)),
                  pl.BlockSpec((None, d, two_de), lambda i, be, na: (be[blk(i, be, na)], 0, 0)),
                  pl.BlockSpec((None, 1, two_de), lambda i, be, na: (be[blk(i, be, na)], 0, 0)),
                  pl.BlockSpec((None, de, d), lambda i, be, na: (be[blk(i, be, na)], 0, 0)),
                  pl.BlockSpec((None, 1, d), lambda i, be, na: (be[blk(i, be, na)], 0, 0))],
        out_specs=pl.BlockSpec((bm, d // 2), lambda i, be, na: (i, 0)),
        scratch_shapes=[pltpu.VMEM((d, two_de), BF16), pltpu.VMEM((de, d), BF16)],
    )
    return pl.pallas_call(
        functools.partial(_expert_kernel, d_expert=de),
        grid_spec=grid_spec,
        out_shape=jax.ShapeDtypeStruct((n_rows, d // 2), U32),
        compiler_params=_cparams("arbitrary"),
        name="expert",
    )(block_e, n_active, xs, w1, b1[:, None, :], w2, b2[:, None, :])


def _combine_kernel(gd_ref, gd_next_ref, slot_ref, x1_ref, mod_ref, fg_ref, yb_ref, o_ref, sorted_scr, sem):
    tm = x1_ref.shape[0]
    step = pl.program_id(0)
    cur = step % 2

    def group_copy(buf, j, d):
        dst = sorted_scr.at[buf, pl.ds(pl.multiple_of(j * CHUNK, CHUNK), CHUNK)]
        return _row_copy(yb_ref.at[pl.ds(pl.multiple_of(d, CHUNK), CHUNK)], dst, sem.at[buf])

    def fetch(table_ref, buf):
        def issue(j, carry):
            group_copy(buf, j, table_ref[j]).start()
            return carry
        lax.fori_loop(0, table_ref[GROUP_TABLE - 1], issue, 0)

    @pl.when(step == 0)
    def _():
        sorted_scr[...] = jnp.zeros(sorted_scr.shape, sorted_scr.dtype)
        fetch(gd_ref, 0)

    @pl.when(step + 1 < pl.num_programs(0))
    def _():
        fetch(gd_next_ref, 1 - cur)

    def retire(j, carry):
        group_copy(cur, 0, 0).wait()
        return carry

    lax.fori_loop(0, gd_ref[GROUP_TABLE - 1], retire, 0)

    slot = slot_ref[...]
    scol = lax.broadcasted_iota(I32, (tm, SLOT_ROWS), 1)
    pick = jnp.zeros((tm, SLOT_ROWS), F32)
    for kk in range(TOP_K):
        pick = pick + jnp.where(scol == slot[:, kk:kk + 1], 1.0, 0.0)
    pick = pick.astype(BF16)
    words = sorted_scr[cur]
    lo = pltpu.bitcast(words << 16, F32).astype(BF16)
    hi = pltpu.bitcast(words & jnp.uint32(HIGH_HALF), F32).astype(BF16)
    y = jnp.concatenate([jnp.dot(pick, lo, preferred_element_type=F32),
                         jnp.dot(pick, hi, preferred_element_type=F32)], axis=1)
    x2 = x1_ref[...] + mod_ref[5:6, :] * y
    o_ref[...] = _rms(x2) * fg_ref[...]


def _combine(gd_flat, slot, x1, mod8, final_g, yb, seq):
    t, d = x1.shape
    tm = TOKEN_TILE
    nt = t // tm
    tiles_per_seq = seq // tm
    return pl.pallas_call(
        _combine_kernel,
        grid=(nt,),
        in_specs=[pl.BlockSpec((GROUP_TABLE,), lambda i: (i,), memory_space=pltpu.SMEM),
                  pl.BlockSpec((GROUP_TABLE,), lambda i: (jnp.minimum(i + 1, nt - 1),), memory_space=pltpu.SMEM),
                  pl.BlockSpec((tm, LANES), lambda i: (i, 0)),
                  pl.BlockSpec((tm, d), lambda i: (i, 0)),
                  pl.BlockSpec((None, SUBLANES, d), lambda i: (i // tiles_per_seq, 0, 0)),
                  pl.BlockSpec((1, d), lambda i: (0, 0)),
                  pl.BlockSpec(memory_space=pl.ANY)],
        out_specs=pl.BlockSpec((tm, d), lambda i: (i, 0)),
        out_shape=jax.ShapeDtypeStruct((t, d), F32),
        scratch_shapes=[pltpu.VMEM((2, SLOT_ROWS, d // 2), U32), pltpu.SemaphoreType.DMA((2,))],
        compiler_params=_cparams("arbitrary"),
        name="combine",
    )(gd_flat, gd_flat, slot, x1, mod8, final_g[None], yb)


def kernel(x, c, w_ada, b_ada, norm1_g, w_in, b_forget, conv_w, conv_b, conv_ln_g, conv_ln_b, w_conv_out,
           w_attn_out, w_out, norm2_g, w_router, b_router, w_exp_in, b_exp_in, w_exp_out, b_exp_out, final_g):
    bsz, seq, d = x.shape
    t = bsz * seq
    depth = w_ada.shape[0]
    xf = x.reshape(t, d)
    n_rows = t * TOP_K + (t // TOKEN_TILE) * N_EXPERTS * (CHUNK - 1) + N_EXPERTS * (EXPERT_BLOCK - CHUNK)
    n_rows = -(-n_rows // EXPERT_BLOCK) * EXPERT_BLOCK
    n_blocks = n_rows // EXPERT_BLOCK
    for l in range(depth):
        mod8 = _mod(c, w_ada, b_ada[l], l)
        act, q, k, v, aq, ak, tot, sgc, sga = _inproj(xf, mod8, norm1_g[l], w_in[l], b_forget[l], conv_w[l],
                                                      conv_b[l], conv_ln_g[l], conv_ln_b[l], seq)
        attn = _attention(q, aq, k, ak, v, tot, bsz, seq)
        x1, h2b, slot, slott, gatest, meta, cnt = _merge(xf, act, attn, sgc, sga, mod8, norm2_g[l], w_conv_out[l],
                                                         w_attn_out[l], w_out[l], w_router[l], b_router[l], seq)
        gd, blk = _dest(cnt, meta)
        gd_flat = gd.reshape(-1)
        block_e = blk[0].reshape(-1)[:n_blocks]
        n_active = blk[1, 0, :1]
        pend = blk[2, 0, :N_EXPERTS]
        xs = _dispatch(pend, n_active, gd_flat, slott, gatest, h2b, n_rows)
        yb = _experts(block_e, n_active, xs, w_exp_in[l], b_exp_in[l], w_exp_out[l], b_exp_out[l])
        assert depth == 1
        xf = _combine(gd_flat, slot, x1, mod8, final_g, yb, seq)
    return xf.reshape(bsz, seq, d)
```
